```python
import math
import jax
import jax.numpy as jnp
from jax import lax
import numpy as np

D_MODEL = 1024
BATCH = 8
SEQ = 2048
DEPTH = 1
DEC_BATCH = 32
DEC_SEQ = 4
PAST_LEN = 16384
PAGE_SIZE = 128

N_META = 16
EPS = 1e-6
GDN_HEADS = 8
GDN_DK = 64
GDN_DV = 64
CONV_W = 4
GDN_CHUNK = 64
MLA_HEADS = 8
MLA_D_NOPE = 64
MLA_D_ROPE = 32
MLA_D_V = 64
Q_LORA = 384
KV_LORA = 256
ROPE_THETA = 10000.0
Q_BLOCK = 128
D_FF = 4 * D_MODEL
GDN_QK_W = GDN_HEADS * GDN_DK
GDN_V_W = GDN_HEADS * GDN_DV
CONV_CH = 2 * GDN_QK_W + GDN_V_W
MLA_V_W = MLA_HEADS * MLA_D_V
MIX_WIDTH = GDN_V_W + MLA_V_W
IN_SIZES = (CONV_CH, GDN_V_W, GDN_HEADS, GDN_HEADS, Q_LORA, KV_LORA, MLA_D_ROPE)
IN_COLS = CONV_CH + GDN_V_W + 2 * GDN_HEADS + Q_LORA + KV_LORA + MLA_D_ROPE
MLA_SCALE = (MLA_D_NOPE + MLA_D_ROPE) ** -0.5

kernel_name = 'hymba_gdn_mla_step'


def split_cols(x, sizes):
    idx, s = [], 0
    for n in sizes[:-1]:
        s += n
        idx.append(s)
    return jnp.split(x, idx, axis=-1)


def rmsnorm(x, g):
    xf = x.astype(jnp.float32)
    y = xf * lax.rsqrt(jnp.mean(xf * xf, axis=-1, keepdims=True) + EPS)
    return (y * g.astype(jnp.float32)).astype(x.dtype)


def l2norm(x):
    xf = x.astype(jnp.float32)
    return xf * lax.rsqrt(jnp.sum(xf * xf, axis=-1, keepdims=True) + EPS)


def rope(x, pos):
    half = MLA_D_ROPE // 2
    inv = ROPE_THETA ** (-jnp.arange(half, dtype=jnp.float32) / half)
    ang = pos.astype(jnp.float32)[:, None] * inv[None, :]
    shape = (pos.shape[0],) + (1,) * (x.ndim - 3) + (MLA_D_ROPE,)
    cos = jnp.concatenate([jnp.cos(ang), jnp.cos(ang)], -1).reshape(shape)
    sin = jnp.concatenate([jnp.sin(ang), jnp.sin(ang)], -1).reshape(shape)
    xf = x.astype(jnp.float32)
    x1, x2 = xf[..., :half], xf[..., half:]
    rot = jnp.concatenate([-x2, x1], -1)
    return (xf * cos + rot * sin).astype(x.dtype)


def causal_conv(x, buf, w):
    T = x.shape[1]
    xp = jnp.concatenate([buf.astype(x.dtype), x], axis=1)
    y = xp[:, 0:T] * w[0]
    for j in range(1, CONV_W):
        y = y + xp[:, j:j + T] * w[j]
    return jax.nn.silu(y), xp[:, -(CONV_W - 1):]


def mixer_projections(hn, w_in, w_q_b, q_norm_g, kv_norm_g, w_kv_b, pos):
    p = hn @ w_in
    qkv, z, b, a, c_q, c_kv, k_pe = split_cols(p, IN_SIZES)
    q = jnp.einsum('btc,chd->bthd', rmsnorm(c_q, q_norm_g), w_q_b)
    q_nope, q_pe = q[..., :MLA_D_NOPE], q[..., MLA_D_NOPE:]
    q_lat = jnp.einsum('bthn,chn->bthc', q_nope, w_kv_b[..., :MLA_D_NOPE])
    q_pe = rope(q_pe, pos)
    latent = rmsnorm(c_kv, kv_norm_g)
    k_pe = rope(k_pe, pos)
    return qkv, z, b, a, q_lat, q_pe, latent, k_pe


def gdn_inputs(qkv_c, b, a, a_log, dt_bias):
    B, T, _ = qkv_c.shape
    q, k, v = split_cols(qkv_c, (GDN_QK_W, GDN_QK_W, GDN_V_W))
    q = l2norm(q.reshape(B, T, GDN_HEADS, GDN_DK)) * (GDN_DK ** -0.5)
    k = l2norm(k.reshape(B, T, GDN_HEADS, GDN_DK))
    v = v.reshape(B, T, GDN_HEADS, GDN_DV).astype(jnp.float32)
    beta = jax.nn.sigmoid(b.astype(jnp.float32))
    g = -jnp.exp(a_log.astype(jnp.float32)) * jax.nn.softplus(a.astype(jnp.float32) + dt_bias.astype(jnp.float32))
    return q, k, v, g, beta


def gdn_chunked(q, k, v, g, beta, s0, chunk):
    B, T, H, DK = q.shape
    DV = v.shape[-1]
    n = T // chunk

    def to_chunks(x):
        return x.reshape((B, n, chunk, H) + x.shape[3:]).swapaxes(2, 3)

    q, k, v, g, beta = [to_chunks(x) for x in (q, k, v, g, beta)]
    g = jnp.cumsum(g, axis=-1)
    idx = jnp.arange(chunk)
    lower = idx[:, None] >= idx[None, :]
    strict = idx[:, None] > idx[None, :]
    diff = g[..., :, None] - g[..., None, :]
    decay = jnp.where(lower, jnp.exp(jnp.where(lower, diff, 0.0)), 0.0)
    k_beta = k * beta[..., None]
    v_beta = v * beta[..., None]
    lmat = jnp.where(strict, jnp.einsum('bnhid,bnhjd->bnhij', k_beta, k) * decay, 0.0)
    amat = lmat + jnp.eye(chunk, dtype=jnp.float32)
    rhs = jnp.concatenate([v_beta, k_beta * jnp.exp(g)[..., None]], axis=-1)
    sol = lax.linalg.triangular_solve(amat, rhs, left_side=True, lower=True, unit_diagonal=True)
    u, w = sol[..., :DV], sol[..., DV:]
    qk = jnp.where(lower, jnp.einsum('bnhid,bnhjd->bnhij', q, k) * decay, 0.0)
    g_last = g[..., -1]

    def step(s, xs):
        q_c, k_c, u_c, w_c, qk_c, g_c, gl = xs
        v_new = u_c - jnp.einsum('bhcd,bhde->bhce', w_c, s)
        o = jnp.einsum('bhcd,bhde->bhce', q_c * jnp.exp(g_c)[..., None], s) + jnp.einsum('bhij,bhje->bhie', qk_c, v_new)
        k_dec = k_c * jnp.exp(gl[..., None] - g_c)[..., None]
        s = s * jnp.exp(gl)[..., None, None] + jnp.einsum('bhcd,bhce->bhde', k_dec, v_new)
        return s, o

    xs = tuple(x.swapaxes(0, 1) for x in (q, k, u, w, qk, g, g_last))
    s_fin, o = lax.scan(step, s0, xs)
    o = jnp.transpose(o, (1, 0, 3, 2, 4)).reshape(B, T, H, DV)
    return o, s_fin


def gdn_output(o, z, norm_g, dtype):
    B, T = o.shape[:2]
    gate = jax.nn.silu(z.astype(jnp.float32)).reshape(B, T, GDN_HEADS, GDN_DV)
    return (rmsnorm(o, norm_g) * gate).reshape(B, T, GDN_V_W).astype(dtype)


def mla_output(out_lat, w_kv_b):
    B, T = out_lat.shape[:2]
    o = jnp.einsum('bthc,chv->bthv', out_lat, w_kv_b[..., MLA_D_NOPE:])
    return o.reshape(B, T, MLA_V_W)


def latent_scores(q_lat, q_pe, c, k_pe):
    s = jnp.einsum('bqhc,bkc->bhqk', q_lat, c) + jnp.einsum('bqhr,bkr->bhqk', q_pe, k_pe)
    return s.astype(jnp.float32) * MLA_SCALE


def masked_latent_attention(q_lat, q_pe, c, k_pe, q_pos, k_pos):
    s = latent_scores(q_lat, q_pe, c, k_pe)
    s = jnp.where((k_pos[None, :] <= q_pos[:, None])[None, None], s, -jnp.inf)
    p = jax.nn.softmax(s, axis=-1).astype(c.dtype)
    return jnp.einsum('bhqk,bkc->bqhc', p, c)


def prompt_attention(q_lat, q_pe, c, k_pe):
    B, T_all = c.shape[:2]
    pos = jnp.arange(T_all)
    meta_out = masked_latent_attention(q_lat[:, :N_META], q_pe[:, :N_META], c[:, :N_META], k_pe[:, :N_META], pos[:N_META], pos[:N_META])
    n_real = T_all - N_META
    n_blk = n_real // Q_BLOCK
    ql = q_lat[:, N_META:].reshape(B, n_blk, Q_BLOCK, MLA_HEADS, KV_LORA).swapaxes(0, 1)
    qp = q_pe[:, N_META:].reshape(B, n_blk, Q_BLOCK, MLA_HEADS, MLA_D_ROPE).swapaxes(0, 1)

    def block(args):
        ql_b, qp_b, i = args
        q_pos = N_META + i * Q_BLOCK + jnp.arange(Q_BLOCK)
        return masked_latent_attention(ql_b, qp_b, c, k_pe, q_pos, pos)

    out = lax.map(block, (ql, qp, jnp.arange(n_blk)))
    out = out.swapaxes(0, 1).reshape(B, n_real, MLA_HEADS, KV_LORA)
    return jnp.concatenate([meta_out, out], axis=1)


def sample_attention(q_lat, q_pe, c_new, kpe_new, c_past, kpe_past):
    T = q_lat.shape[1]
    P = c_past.shape[1]
    s_past = latent_scores(q_lat, q_pe, c_past, kpe_past)
    idx = jnp.arange(T)
    s_new = jnp.where((idx[None, :] <= idx[:, None])[None, None], latent_scores(q_lat, q_pe, c_new, kpe_new), -jnp.inf)
    p = jax.nn.softmax(jnp.concatenate([s_past, s_new], axis=-1), axis=-1).astype(c_new.dtype)
    return jnp.einsum('bhqk,bkc->bqhc', p[..., :P], c_past) + jnp.einsum('bhqk,bkc->bqhc', p[..., P:], c_new)


def sq_relu_mlp(hn, w_up, w_down):
    return jnp.square(jax.nn.relu(hn @ w_up)) @ w_down


def setup_inputs(seed: int = 0) -> dict:
    key = jax.random.key(seed)
    ks = jax.random.split(key, 24)
    f32 = jnp.float32
    n_pages = PAST_LEN // PAGE_SIZE
    n_used = DEC_BATCH * n_pages
    n_pool = (5 * n_used) // 4

    def nrm(k, shape, scale):
        return jax.random.normal(k, shape, f32) * scale

    def gain(k, shape):
        return 1.0 + 0.1 * jax.random.normal(k, shape, f32)

    page_table = jax.random.permutation(ks[6], n_pool)[:n_used].reshape(DEC_BATCH, n_pages).astype(jnp.int32)
    dt = jnp.exp(jax.random.uniform(ks[10], (DEPTH, GDN_HEADS), f32, math.log(1e-3), math.log(1e-1)))
    dt_bias = dt + jnp.log(-jnp.expm1(-dt))
    a_log = jnp.log(jax.random.uniform(ks[9], (DEPTH, GDN_HEADS), f32, 1.0, 16.0))
    return {
        'x_prompt': nrm(ks[0], (BATCH, SEQ, D_MODEL), 1.0),
        'x_sample': nrm(ks[1], (DEC_BATCH, DEC_SEQ, D_MODEL), 1.0),
        'cache_latent': nrm(ks[2], (DEPTH, n_pool, PAGE_SIZE, KV_LORA), 1.0),
        'cache_krope': nrm(ks[3], (DEPTH, n_pool, PAGE_SIZE, MLA_D_ROPE), 1.0),
        'state_conv': nrm(ks[4], (DEPTH, DEC_BATCH, CONV_W - 1, CONV_CH), 1.0),
        'state_ssm': nrm(ks[5], (DEPTH, DEC_BATCH, GDN_HEADS, GDN_DK, GDN_DV), GDN_DK ** -0.5),
        'page_table': page_table,
        'meta_tokens': nrm(ks[7], (N_META, D_MODEL), 1.0),
        'norm_mix_g': gain(ks[8], (DEPTH, D_MODEL)),
        'w_in': nrm(ks[11], (DEPTH, D_MODEL, IN_COLS), D_MODEL ** -0.5),
        'conv_w': nrm(ks[12], (DEPTH, CONV_W, CONV_CH), CONV_W ** -0.5),
        'a_log': a_log,
        'dt_bias': dt_bias,
        'gdn_norm_g': gain(ks[13], (DEPTH, GDN_DV)),
        'q_norm_g': gain(ks[14], (DEPTH, Q_LORA)),
        'w_q_b': nrm(ks[15], (DEPTH, Q_LORA, MLA_HEADS, MLA_D_NOPE + MLA_D_ROPE), Q_LORA ** -0.5),
        'kv_norm_g': gain(ks[16], (DEPTH, KV_LORA)),
        'w_kv_b': nrm(ks[17], (DEPTH, KV_LORA, MLA_HEADS, MLA_D_NOPE + MLA_D_V), KV_LORA ** -0.5),
        'w_out': nrm(ks[18], (DEPTH, MIX_WIDTH, D_MODEL), MIX_WIDTH ** -0.5),
        'norm_mlp_g': gain(ks[19], (DEPTH, D_MODEL)),
        'w_up': nrm(ks[20], (DEPTH, D_MODEL, D_FF), D_MODEL ** -0.5),
        'w_down': nrm(ks[21], (DEPTH, D_FF, D_MODEL), D_FF ** -0.5),
        'final_norm_g': gain(ks[22], (D_MODEL,)),
    }


def reference(x_prompt, x_sample, cache_latent, cache_krope, state_conv, state_ssm, page_table,
              meta_tokens, norm_mix_g, w_in, conv_w, a_log, dt_bias, gdn_norm_g, q_norm_g, w_q_b,
              kv_norm_g, w_kv_b, w_out, norm_mlp_g, w_up, w_down, final_norm_g):
    dt = x_prompt.dtype
    B = x_prompt.shape[0]
    Bs, Ts = x_sample.shape[:2]
    h_p = jnp.concatenate([jnp.broadcast_to(meta_tokens.astype(dt)[None], (B, N_META, D_MODEL)), x_prompt], axis=1)
    h_s = x_sample
    pos_p = jnp.arange(h_p.shape[1])
    pos_s = PAST_LEN + jnp.arange(Ts)
    lat_p, krope_p, conv_p, ssm_p = [], [], [], []
    lat_s, krope_s, conv_s, ssm_s = [], [], [], []
    for l in range(DEPTH):
        hn = rmsnorm(h_p, norm_mix_g[l])
        qkv, z, b, a, q_lat, q_pe, lat, kpe = mixer_projections(hn, w_in[l], w_q_b[l], q_norm_g[l], kv_norm_g[l], w_kv_b[l], pos_p)
        qkv_c, cbuf = causal_conv(qkv, jnp.zeros((B, CONV_W - 1, CONV_CH), dt), conv_w[l])
        q, k, v, g, beta = gdn_inputs(qkv_c, b, a, a_log[l], dt_bias[l])
        s0 = jnp.zeros((B, GDN_HEADS, GDN_DK, GDN_DV), jnp.float32)
        o_m, s1 = gdn_chunked(q[:, :N_META], k[:, :N_META], v[:, :N_META], g[:, :N_META], beta[:, :N_META], s0, N_META)
        o_r, s2 = gdn_chunked(q[:, N_META:], k[:, N_META:], v[:, N_META:], g[:, N_META:], beta[:, N_META:], s1, GDN_CHUNK)
        o_gdn = gdn_output(jnp.concatenate([o_m, o_r], axis=1), z, gdn_norm_g[l], dt)
        o_mla = mla_output(prompt_attention(q_lat, q_pe, lat, kpe), w_kv_b[l])
        h_p = h_p + jnp.concatenate([o_gdn, o_mla], axis=-1) @ w_out[l]
        h_p = h_p + sq_relu_mlp(rmsnorm(h_p, norm_mlp_g[l]), w_up[l], w_down[l])
        lat_p.append(lat)
        krope_p.append(kpe)
        conv_p.append(cbuf)
        ssm_p.append(s2.astype(dt))
        hn_s = rmsnorm(h_s, norm_mix_g[l])
        qkv_s, z_s, b_s, a_s, q_lat_s, q_pe_s, lat_new, kpe_new = mixer_projections(hn_s, w_in[l], w_q_b[l], q_norm_g[l], kv_norm_g[l], w_kv_b[l], pos_s)
        qkv_cs, cbuf_s = causal_conv(qkv_s, state_conv[l], conv_w[l])
        qs, ks_, vs, gs, betas = gdn_inputs(qkv_cs, b_s, a_s, a_log[l], dt_bias[l])
        o_s, s_new = gdn_chunked(qs, ks_, vs, gs, betas, state_ssm[l].astype(jnp.float32), Ts)
        o_gdn_s = gdn_output(o_s, z_s, gdn_norm_g[l], dt)
        c_past = cache_latent[l][page_table].reshape(Bs, -1, KV_LORA)
        kpe_past = cache_krope[l][page_table].reshape(Bs, -1, MLA_D_ROPE)
        o_mla_s = mla_output(sample_attention(q_lat_s, q_pe_s, lat_new, kpe_new, c_past.astype(dt), kpe_past.astype(dt)), w_kv_b[l])
        h_s = h_s + jnp.concatenate([o_gdn_s, o_mla_s], axis=-1) @ w_out[l]
        h_s = h_s + sq_relu_mlp(rmsnorm(h_s, norm_mlp_g[l]), w_up[l], w_down[l])
        lat_s.append(lat_new)
        krope_s.append(kpe_new)
        conv_s.append(cbuf_s)
        ssm_s.append(s_new.astype(dt))
    y_prompt = rmsnorm(h_p[:, N_META:], final_norm_g)
    y_sample = rmsnorm(h_s, final_norm_g)
    return (y_prompt, y_sample,
            jnp.stack(lat_p), jnp.stack(krope_p), jnp.stack(conv_p), jnp.stack(ssm_p),
            jnp.stack(lat_s), jnp.stack(krope_s), jnp.stack(conv_s), jnp.stack(ssm_s))
```

```python
import functools
import math

import jax
import jax.numpy as jnp
from jax import lax
from jax.experimental import pallas as pl
from jax.experimental.pallas import tpu as pltpu

F32 = jnp.float32
BF16 = jnp.bfloat16

N_META = 16
EPS = 1e-6
GDN_HEADS = 8
GDN_DK = 64
GDN_DV = 64
CONV_W = 4
GDN_CHUNK = 64
MLA_HEADS = 8
MLA_D_NOPE = 64
MLA_D_ROPE = 32
MLA_D_V = 64
ROPE_THETA = 10000.0
MLA_SCALE = (MLA_D_NOPE + MLA_D_ROPE) ** -0.5
GDN_QK_W = GDN_HEADS * GDN_DK
GDN_V_W = GDN_HEADS * GDN_DV
CONV_CH = 2 * GDN_QK_W + GDN_V_W

LANES = 128
SUBLANES = 8
HEAD_PAD = 128
SAMPLE_T_PAD = 8
VMEM_LIMIT = 56 * 1024 * 1024


def _dot(a, b):
    return jnp.dot(a, b, preferred_element_type=F32)


def _dot_nt(a, b):
    return lax.dot_general(a, b, (((1,), (1,)), ((), ())), preferred_element_type=F32)


def _dot_tn(a, b):
    return lax.dot_general(a, b, (((0,), (0,)), ((), ())), preferred_element_type=F32)


def _rms(x, g):
    return x * lax.rsqrt(jnp.mean(x * x, axis=-1, keepdims=True) + EPS) * g


def _sigmoid(x):
    return 1.0 / (1.0 + jnp.exp(-x))


def _const_spec(shape):
    nd = len(shape)
    return pl.BlockSpec(shape, lambda *_: (0,) * nd)


def _proj_common(x_ref, gmix_ref, w1_ref, wba_ref, wcq_ref, wckv_ref, wkpe_ref, wkpesw_ref,
                 qg_ref, kvg_ref, kcos, ksin, qkv_ref, z_ref, ba_ref, lat_ref, kpe_ref):
    hn = _rms(x_ref[...], gmix_ref[...]).astype(BF16)
    qkvz = _dot(hn, w1_ref[...])
    qkv_ref[...] = qkvz[:, :CONV_CH]
    z_ref[...] = qkvz[:, CONV_CH:]
    ba_ref[...] = _dot(hn, wba_ref[...])
    cq = _dot(hn, wcq_ref[...])
    ckv = _dot(hn, wckv_ref[...])
    kpe = _dot(hn, wkpe_ref[...]) * kcos + _dot(hn, wkpesw_ref[...]) * ksin
    kpe_ref[...] = kpe[:, :MLA_D_ROPE]
    lat = _rms(ckv, kvg_ref[...])
    lat_ref[...] = lat
    cqn = _rms(cq, qg_ref[...]).astype(BF16)
    return cqn, lat.astype(BF16), kpe.astype(BF16)


def _proj_prompt_kernel(x_ref, tab_ref, gmix_ref, w1_ref, wba_ref, wcq_ref, wckv_ref, wkpe_ref,
                        wkpesw_ref, qg_ref, kvg_ref, wq_ref, wqsw_ref, wkb_ref, wvb_ref, esel_ref,
                        qkv_ref, z_ref, ba_ref, lat_ref, kpe_ref, qcat_ref, kcat_ref, vsp_ref):
    qcos = tab_ref[:, 0 * LANES:1 * LANES]
    qsin = tab_ref[:, 1 * LANES:2 * LANES]
    kcos = tab_ref[:, 2 * LANES:3 * LANES]
    ksin = tab_ref[:, 3 * LANES:4 * LANES]
    cqn, latb, kpeb = _proj_common(x_ref, gmix_ref, w1_ref, wba_ref, wcq_ref, wckv_ref, wkpe_ref,
                                   wkpesw_ref, qg_ref, kvg_ref, kcos, ksin,
                                   qkv_ref, z_ref, ba_ref, lat_ref, kpe_ref)
    qa = _dot(cqn, wq_ref[...])
    qb = _dot(cqn, wqsw_ref[...])
    for h in range(MLA_HEADS):
        sl = slice(h * HEAD_PAD, (h + 1) * HEAD_PAD)
        qcat_ref[:, sl] = (qa[:, sl] * qcos + qb[:, sl] * qsin).astype(BF16)
    kcat_ref[...] = (_dot(latb, wkb_ref[...]) + _dot(kpeb, esel_ref[...])).astype(BF16)
    vsp_ref[...] = _dot(latb, wvb_ref[...]).astype(BF16)


def _proj_sample_kernel(x_ref, tab_ref, gmix_ref, w1_ref, wba_ref, wcq_ref, wckv_ref, wkpe_ref,
                        wkpesw_ref, qg_ref, kvg_ref, wqn_ref, wqpe_ref, wqpesw_ref, wabs_ref,
                        qkv_ref, z_ref, ba_ref, lat_ref, kpe_ref, qlat_ref, qpe_ref):
    pe_w = MLA_HEADS * MLA_D_ROPE
    qcos = tab_ref[:, 0:pe_w]
    qsin = tab_ref[:, pe_w:2 * pe_w]
    kcos = tab_ref[:, 2 * pe_w:2 * pe_w + LANES]
    ksin = tab_ref[:, 2 * pe_w + LANES:2 * pe_w + 2 * LANES]
    cqn, _, _ = _proj_common(x_ref, gmix_ref, w1_ref, wba_ref, wcq_ref, wckv_ref, wkpe_ref,
                             wkpesw_ref, qg_ref, kvg_ref, kcos, ksin,
                             qkv_ref, z_ref, ba_ref, lat_ref, kpe_ref)
    qn = _dot(cqn, wqn_ref[...]).astype(BF16)
    qlat_ref[...] = (_dot(qn, wabs_ref[...]) * MLA_SCALE).astype(BF16)
    qpe_ref[...] = (_dot(cqn, wqpe_ref[...]) * qcos + _dot(cqn, wqpesw_ref[...]) * qsin).astype(BF16)


def _proj_call(kernel_fn, x2d, tab, consts, out_widths, out_dtypes, tm, name):
    n_rows, d_model = x2d.shape
    assert n_rows % tm == 0
    row_spec = lambda w: pl.BlockSpec((tm, w), lambda i: (i, 0))
    assert tab.shape[0] % tm == 0 and n_rows % tab.shape[0] == 0
    tab_blocks = tab.shape[0] // tm
    tab_spec = pl.BlockSpec((tm, tab.shape[1]), lambda i: (i % tab_blocks, 0))
    in_specs = [row_spec(d_model), tab_spec] + [_const_spec(c.shape) for c in consts]
    return pl.pallas_call(
        kernel_fn,
        grid=(n_rows // tm,),
        in_specs=in_specs,
        out_specs=[row_spec(w) for w in out_widths],
        out_shape=[jax.ShapeDtypeStruct((n_rows, w), dt) for w, dt in zip(out_widths, out_dtypes)],
        compiler_params=pltpu.CompilerParams(dimension_semantics=("arbitrary",),
                                             vmem_limit_bytes=VMEM_LIMIT),
        name=name,
    )(x2d, tab, *consts)


def _gdn_kernel(qkv_ref, z_ref, ba_ref, cinit_ref, sinit_ref, cw_ref, alog_ref, dtb_ref, ng_ref,
                o_ref, sfin_ref, xbuf, s_scr, *, chunk, valid_len, n_chunks):
    c = pl.program_id(1)
    C = chunk
    tail0 = SUBLANES - (CONV_W - 1)

    @pl.when(c == 0)
    def _():
        xbuf[0:SUBLANES, :] = cinit_ref[0]
        s_scr[...] = sinit_ref[0]

    xbuf[SUBLANES:SUBLANES + C, :] = qkv_ref[0]
    cw = cw_ref[...]
    y = xbuf[tail0:tail0 + C, :] * cw[0:1, :]
    for j in range(1, CONV_W):
        y = y + xbuf[tail0 + j:tail0 + j + C, :] * cw[j:j + 1, :]
    xbuf[tail0:SUBLANES, :] = xbuf[SUBLANES + C - (CONV_W - 1):SUBLANES + C, :]
    y = y * _sigmoid(y)

    ba = ba_ref[0]
    beta_all = _sigmoid(ba)
    sp_in = ba + dtb_ref[...]
    softplus = jnp.maximum(sp_in, 0.0) + jnp.log1p(jnp.exp(-jnp.abs(sp_in)))
    g_all = -jnp.exp(alog_ref[...]) * softplus
    if valid_len < n_chunks * C:
        row = c * C + lax.broadcasted_iota(jnp.int32, (C, LANES), 0)
        beta_all = jnp.where(row < valid_len, beta_all, 0.0)
        g_all = jnp.where(row < valid_len, g_all, 0.0)

    ri = lax.broadcasted_iota(jnp.int32, (C, C), 0)
    ci = lax.broadcasted_iota(jnp.int32, (C, C), 1)
    lower = ri >= ci
    strict = ri > ci
    gc_all = lax.dot_general(lower.astype(F32), g_all, (((1,), (0,)), ((), ())),
                             precision=lax.Precision.HIGHEST, preferred_element_type=F32)
    sel = (lax.broadcasted_iota(jnp.int32, (SUBLANES, LANES), 1)
           == lax.broadcasted_iota(jnp.int32, (SUBLANES, LANES), 0) + GDN_HEADS).astype(F32)
    gc_t = lax.dot_general(sel, gc_all, (((1,), (1,)), ((), ())),
                           precision=lax.Precision.HIGHEST, preferred_element_type=F32)

    z = z_ref[0]
    ng = ng_ref[...]
    n_iter = max(1, int(math.ceil(math.log2(C))))
    for h in range(GDN_HEADS):
        hs = slice(h * GDN_DK, (h + 1) * GDN_DK)
        qh = y[:, hs]
        kh = y[:, GDN_QK_W + h * GDN_DK:GDN_QK_W + (h + 1) * GDN_DK]
        vh = y[:, 2 * GDN_QK_W + h * GDN_DV:2 * GDN_QK_W + (h + 1) * GDN_DV]
        qh = qh * lax.rsqrt(jnp.sum(qh * qh, axis=-1, keepdims=True) + EPS) * (GDN_DK ** -0.5)
        kh = kh * lax.rsqrt(jnp.sum(kh * kh, axis=-1, keepdims=True) + EPS)
        beta = beta_all[:, h:h + 1]
        gcol = gc_all[:, GDN_HEADS + h:GDN_HEADS + h + 1]
        grow = gc_t[h:h + 1, :]
        diff = gcol - grow
        decay = jnp.where(lower, jnp.exp(jnp.where(lower, diff, 0.0)), 0.0)
        kb = kh * beta
        vb = vh * beta
        egc = jnp.exp(gcol)
        khb = kh.astype(BF16)
        lmat = jnp.where(strict, _dot_nt(kb.astype(BF16), khb) * decay, 0.0)
        qk = jnp.where(lower, _dot_nt(qh.astype(BF16), khb) * decay, 0.0)
        x = jnp.concatenate([vb, kb * egc], axis=1)
        p = -lmat
        for it in range(n_iter):
            pb = p.astype(BF16)
            x = x + _dot(pb, x.astype(BF16))
            if it + 1 < n_iter:
                p = _dot(pb, pb)
        u = x[:, :GDN_DV]
        w = x[:, GDN_DV:]
        s = s_scr[h]
        sb = s.astype(BF16)
        ws = _dot(jnp.concatenate([w, qh * egc], axis=0).astype(BF16), sb)
        v_new = u - ws[:C]
        v_newb = v_new.astype(BF16)
        o = ws[C:] + _dot(qk.astype(BF16), v_newb)
        gl = gcol[C - 1:C, :]
        k_dec = kh * jnp.exp(gl - gcol)
        s_scr[h] = s * jnp.exp(gl) + _dot_tn(k_dec.astype(BF16), v_newb)
        zh = z[:, hs]
        o_n = o * lax.rsqrt(jnp.mean(o * o, axis=-1, keepdims=True) + EPS) * ng
        o_ref[0, :, hs] = (o_n * (zh * _sigmoid(zh))).astype(o_ref.dtype)

    @pl.when(c == n_chunks - 1)
    def _():
        sfin_ref[0] = s_scr[...]


def _gdn_call(qkv, z, ba, conv_init, s_init, cw, alog, dtb, ng, chunk, valid_len, name):
    B, T, _ = qkv.shape
    assert T % chunk == 0
    n_chunks = T // chunk
    shared_c = conv_init.shape[0] == 1
    shared_s = s_init.shape[0] == 1
    kern = functools.partial(_gdn_kernel, chunk=chunk, valid_len=valid_len, n_chunks=n_chunks)
    tok = lambda w: pl.BlockSpec((1, chunk, w), lambda b, c: (b, c, 0))
    return pl.pallas_call(
        kern,
        grid=(B, n_chunks),
        in_specs=[
            tok(CONV_CH), tok(GDN_V_W), tok(LANES),
            pl.BlockSpec((1, SUBLANES, CONV_CH), (lambda b, c: (0, 0, 0)) if shared_c else (lambda b, c: (b, 0, 0))),
            pl.BlockSpec((1, GDN_HEADS, GDN_DK, GDN_DV),
                         (lambda b, c: (0, 0, 0, 0)) if shared_s else (lambda b, c: (b, 0, 0, 0))),
            _const_spec(cw.shape), _const_spec(alog.shape), _const_spec(dtb.shape), _const_spec(ng.shape),
        ],
        out_specs=[
            tok(GDN_V_W),
            pl.BlockSpec((1, GDN_HEADS, GDN_DK, GDN_DV), lambda b, c: (b, 0, 0, 0)),
        ],
        out_shape=[
            jax.ShapeDtypeStruct((B, T, GDN_V_W), BF16),
            jax.ShapeDtypeStruct((B, GDN_HEADS, GDN_DK, GDN_DV), F32),
        ],
        scratch_shapes=[
            pltpu.VMEM((SUBLANES + chunk, CONV_CH), F32),
            pltpu.VMEM((GDN_HEADS, GDN_DK, GDN_DV), F32),
        ],
        compiler_params=pltpu.CompilerParams(dimension_semantics=("arbitrary", "arbitrary"),
                                             vmem_limit_bytes=VMEM_LIMIT),
        name=name,
    )(qkv, z, ba, conv_init, s_init, cw, alog, dtb, ng)


def _attn_kernel(q_ref, k_ref, v_ref, km_ref, vm_ref, o_ref, *, tq, n_meta):
    qi = pl.program_id(2)
    neg = -jnp.inf
    col_m = lax.broadcasted_iota(jnp.int32, (tq, km_ref.shape[0]), 1)
    row_d = lax.broadcasted_iota(jnp.int32, (tq, tq), 0)
    col_d = lax.broadcasted_iota(jnp.int32, (tq, tq), 1)
    out = None
    for hh in range(2):
        ls = slice(hh * HEAD_PAD, (hh + 1) * HEAD_PAD)
        q = q_ref[0, :, ls]
        s = jnp.where(col_m < n_meta, _dot_nt(q, km_ref[:, ls]), neg)
        m = jnp.max(s, axis=-1, keepdims=True)
        p = jnp.exp(s - m)
        l = jnp.sum(p, axis=-1, keepdims=True)
        acc = _dot(p.astype(BF16), vm_ref[:, ls])

        def step(s, m, l, acc, v):
            m_new = jnp.maximum(m, jnp.max(s, axis=-1, keepdims=True))
            alpha = jnp.exp(m - m_new)
            p = jnp.exp(s - m_new)
            l = alpha * l + jnp.sum(p, axis=-1, keepdims=True)
            acc = alpha * acc + _dot(p.astype(BF16), v)
            return m_new, l, acc

        def body(kb, carry, q=q, ls=ls):
            m, l, acc = carry
            start = pl.multiple_of(kb * tq, tq)
            k = k_ref[0, pl.ds(start, tq), ls]
            v = v_ref[0, pl.ds(start, tq), ls]
            return step(_dot_nt(q, k), m, l, acc, v)

        m, l, acc = lax.fori_loop(0, qi, body, (m, l, acc))
        start = pl.multiple_of(qi * tq, tq)
        k = k_ref[0, pl.ds(start, tq), ls]
        v = v_ref[0, pl.ds(start, tq), ls]
        s = jnp.where(col_d <= row_d, _dot_nt(q, k), neg)
        m, l, acc = step(s, m, l, acc, v)
        contrib = acc / l
        out = contrib if out is None else out + contrib
    o_ref[0] = out.astype(o_ref.dtype)


def _attn_call(qcat, kcat, vsp, k_meta, v_meta, tq):
    B, T, _ = qcat.shape
    n_pairs = MLA_HEADS // 2
    pw = 2 * HEAD_PAD
    kern = functools.partial(_attn_kernel, tq=tq, n_meta=N_META)
    return pl.pallas_call(
        kern,
        grid=(B, n_pairs, T // tq),
        in_specs=[
            pl.BlockSpec((1, tq, pw), lambda b, hp, qi: (b, qi, hp)),
            pl.BlockSpec((1, T, pw), lambda b, hp, qi: (b, 0, hp)),
            pl.BlockSpec((1, T, pw), lambda b, hp, qi: (b, 0, hp)),
            pl.BlockSpec((k_meta.shape[0], pw), lambda b, hp, qi: (0, hp)),
            pl.BlockSpec((v_meta.shape[0], pw), lambda b, hp, qi: (0, hp)),
        ],
        out_specs=pl.BlockSpec((1, tq, LANES), lambda b, hp, qi: (b, qi, hp)),
        out_shape=jax.ShapeDtypeStruct((B, T, n_pairs * LANES), BF16),
        compiler_params=pltpu.CompilerParams(dimension_semantics=("arbitrary", "arbitrary", "arbitrary"),
                                             vmem_limit_bytes=VMEM_LIMIT),
        name="attn",
    )(qcat, kcat, vsp, k_meta, v_meta)


def _dattn_kernel(pt_ref, *refs, pages_per_step, t_new, n_steps):
    del pt_ref
    P = pages_per_step
    qlat_ref, qpe_ref, latn_ref, kpen_ref = refs[:4]
    lat_pages = refs[4:4 + P]
    kpe_pages = refs[4 + P:4 + 2 * P]
    o_ref = refs[4 + 2 * P]
    m_scr, l_scr, acc_scr = refs[4 + 2 * P + 1:]
    j = pl.program_id(1)
    neg = -jnp.inf

    @pl.when(j == 0)
    def _():
        m_scr[...] = jnp.full(m_scr.shape, neg, F32)
        l_scr[...] = jnp.zeros(l_scr.shape, F32)
        acc_scr[...] = jnp.zeros(acc_scr.shape, F32)

    ql = qlat_ref[0]
    qp = qpe_ref[0]

    def update(s, vals):
        m = m_scr[...]
        m_new = jnp.maximum(m, jnp.max(s, axis=-1, keepdims=True))
        alpha = jnp.exp(m - m_new)
        p = jnp.exp(s - m_new)
        l_scr[...] = alpha * l_scr[...] + jnp.sum(p, axis=-1, keepdims=True)
        acc_scr[...] = alpha * acc_scr[...] + _dot(p.astype(BF16), vals)
        m_scr[...] = m_new

    cb = jnp.concatenate([r[0, 0] for r in lat_pages], axis=0).astype(BF16)
    kb = jnp.concatenate([r[0, 0] for r in kpe_pages], axis=0).astype(BF16)
    update(_dot_nt(ql, cb) + _dot_nt(qp, kb), cb)

    @pl.when(j == n_steps - 1)
    def _():
        R = ql.shape[0]
        pad = 2 * SUBLANES - latn_ref.shape[1]
        cn = jnp.concatenate([latn_ref[0], jnp.zeros((pad, latn_ref.shape[2]), F32)], axis=0).astype(BF16)
        kn = jnp.concatenate([kpen_ref[0], jnp.zeros((pad, kpen_ref.shape[2]), F32)], axis=0).astype(BF16)
        s = _dot_nt(ql, cn) + _dot_nt(qp, kn)
        tok = lax.shift_right_logical(lax.broadcasted_iota(jnp.int32, s.shape, 0), int(math.log2(MLA_HEADS)))
        key = lax.broadcasted_iota(jnp.int32, s.shape, 1)
        s = jnp.where((key <= tok) & (key < t_new), s, neg)
        update(s, cn)
        o_ref[0] = (acc_scr[...] / l_scr[...]).astype(o_ref.dtype)


def _dattn_call(page_table, qlat, qpe, lat_new, kpe_new, cache_latent, cache_krope, t_new, pages_per_step):
    Bs, R, kv_lora = qlat.shape
    n_pages = page_table.shape[1]
    P = pages_per_step
    assert n_pages % P == 0
    n_steps = n_pages // P
    page = cache_latent.shape[2]
    kern = functools.partial(_dattn_kernel, pages_per_step=P, t_new=t_new, n_steps=n_steps)
    per_b = lambda shape: pl.BlockSpec((1,) + shape, lambda b, j, pt: (b, 0, 0))

    def page_spec(width, k):
        return pl.BlockSpec((1, 1, page, width), lambda b, j, pt: (0, pt[b, j * P + k], 0, 0))

    grid_spec = pltpu.PrefetchScalarGridSpec(
        num_scalar_prefetch=1,
        grid=(Bs, n_steps),
        in_specs=[per_b((R, kv_lora)), per_b((R, MLA_D_ROPE)),
                  per_b(lat_new.shape[1:]), per_b(kpe_new.shape[1:])]
                 + [page_spec(kv_lora, k) for k in range(P)]
                 + [page_spec(MLA_D_ROPE, k) for k in range(P)],
        out_specs=per_b((R, kv_lora)),
        scratch_shapes=[pltpu.VMEM((R, 1), F32), pltpu.VMEM((R, 1), F32), pltpu.VMEM((R, kv_lora), F32)],
    )
    return pl.pallas_call(
        kern,
        grid_spec=grid_spec,
        out_shape=jax.ShapeDtypeStruct((Bs, R, kv_lora), BF16),
        compiler_params=pltpu.CompilerParams(dimension_semantics=("arbitrary", "arbitrary"),
                                             vmem_limit_bytes=VMEM_LIMIT),
        name="dattn",
    )(page_table, qlat, qpe, lat_new, kpe_new, *([cache_latent] * P), *([cache_krope] * P))


def _post_kernel(*refs, absorbed_values):
    if absorbed_values:
        h_ref, og_ref, om_ref, wvbd_ref, wo1_ref, wo2_ref, g2_ref, wup_ref, wdn_ref, gf_ref, y_ref = refs
        om = _dot(om_ref[...], wvbd_ref[...]).astype(BF16)
    else:
        h_ref, og_ref, om_ref, wo1_ref, wo2_ref, g2_ref, wup_ref, wdn_ref, gf_ref, y_ref = refs
        om = om_ref[...]
    h2 = h_ref[...] + _dot(og_ref[...], wo1_ref[...]) + _dot(om, wo2_ref[...])
    hn = _rms(h2, g2_ref[...]).astype(BF16)
    u = jnp.maximum(_dot(hn, wup_ref[...]), 0.0)
    h3 = h2 + _dot((u * u).astype(BF16), wdn_ref[...])
    y_ref[...] = _rms(h3, gf_ref[...])


def _post_call(h2d, og, om, consts, tm, absorbed_values, name):
    n_rows, d_model = h2d.shape
    assert n_rows % tm == 0
    row_spec = lambda w: pl.BlockSpec((tm, w), lambda i: (i, 0))
    kern = functools.partial(_post_kernel, absorbed_values=absorbed_values)
    return pl.pallas_call(
        kern,
        grid=(n_rows // tm,),
        in_specs=[row_spec(d_model), row_spec(og.shape[1]), row_spec(om.shape[1])]
                 + [_const_spec(c.shape) for c in consts],
        out_specs=row_spec(d_model),
        out_shape=jax.ShapeDtypeStruct((n_rows, d_model), F32),
        compiler_params=pltpu.CompilerParams(dimension_semantics=("arbitrary",),
                                             vmem_limit_bytes=VMEM_LIMIT),
        name=name,
    )(h2d, og, om, *consts)


def _rope_parts(pos):
    half = MLA_D_ROPE // 2
    inv = ROPE_THETA ** (-jnp.arange(half, dtype=F32) / half)
    ang = pos.astype(F32)[:, None] * inv[None, :]
    cos = jnp.concatenate([jnp.cos(ang), jnp.cos(ang)], -1)
    sin_signed = jnp.concatenate([-jnp.sin(ang), jnp.sin(ang)], -1)
    return cos, sin_signed


def _pad_lanes(x, width):
    return jnp.pad(x, ((0, 0), (0, width - x.shape[1])))


def _key_tables(cos, sin_signed):
    return _pad_lanes(cos, LANES), _pad_lanes(sin_signed, LANES)


def _prompt_tables(pos):
    cos, sin_signed = _rope_parts(pos)
    n = pos.shape[0]
    qcos = jnp.concatenate([jnp.full((n, MLA_D_NOPE), MLA_SCALE, F32), cos * MLA_SCALE], -1)
    qsin = jnp.concatenate([jnp.zeros((n, MLA_D_NOPE), F32), sin_signed * MLA_SCALE], -1)
    kcos, ksin = _key_tables(cos, sin_signed)
    return jnp.concatenate([_pad_lanes(qcos, LANES), _pad_lanes(qsin, LANES), kcos, ksin], -1)


def _sample_tables(pos):
    cos, sin_signed = _rope_parts(pos)
    kcos, ksin = _key_tables(cos, sin_signed)
    return jnp.concatenate([jnp.tile(cos * MLA_SCALE, (1, MLA_HEADS)),
                            jnp.tile(sin_signed * MLA_SCALE, (1, MLA_HEADS)), kcos, ksin], -1)


def _swap_halves(w):
    half = MLA_D_ROPE // 2
    return jnp.concatenate([w[..., half:], w[..., :half]], -1)


def kernel(x_prompt, x_sample, cache_latent, cache_krope, state_conv, state_ssm, page_table,
           meta_tokens, norm_mix_g, w_in, conv_w, a_log, dt_bias, gdn_norm_g, q_norm_g, w_q_b,
           kv_norm_g, w_kv_b, w_out, norm_mlp_g, w_up, w_down, final_norm_g):
    assert w_in.shape[0] == 1, "single-layer problem"
    B, T, D = x_prompt.shape
    Bs, Ts, _ = x_sample.shape
    assert CONV_W - 1 <= Ts <= SAMPLE_T_PAD
    past_len = page_table.shape[1] * cache_latent.shape[2]
    q_lora = q_norm_g.shape[1]
    kv_lora = kv_norm_g.shape[1]
    H = MLA_HEADS

    wi = w_in[0]
    c0 = CONV_CH + GDN_V_W
    w1 = wi[:, :c0].astype(BF16)
    wba = _pad_lanes(wi[:, c0:c0 + 2 * GDN_HEADS], LANES).astype(BF16)
    c1 = c0 + 2 * GDN_HEADS
    wcq = wi[:, c1:c1 + q_lora].astype(BF16)
    wckv = wi[:, c1 + q_lora:c1 + q_lora + kv_lora].astype(BF16)
    wkpe_raw = wi[:, c1 + q_lora + kv_lora:]
    wkpe = _pad_lanes(wkpe_raw, LANES).astype(BF16)
    wkpesw = _pad_lanes(_swap_halves(wkpe_raw), LANES).astype(BF16)
    gmix = norm_mix_g[0][None, :]
    qg = q_norm_g[0][None, :]
    kvg = kv_norm_g[0][None, :]
    wq = w_q_b[0]
    wq_nope, wq_pe = wq[..., :MLA_D_NOPE], wq[..., MLA_D_NOPE:]
    wq_pesw = _swap_halves(wq_pe)
    zq = lambda n: jnp.zeros((q_lora, H, n), F32)
    wqcat = jnp.concatenate([wq_nope, wq_pe, zq(HEAD_PAD - MLA_D_NOPE - MLA_D_ROPE)], -1)
    wqsw = jnp.concatenate([zq(MLA_D_NOPE), wq_pesw, zq(HEAD_PAD - MLA_D_NOPE - MLA_D_ROPE)], -1)
    wqcat = wqcat.reshape(q_lora, H * HEAD_PAD).astype(BF16)
    wqsw = wqsw.reshape(q_lora, H * HEAD_PAD).astype(BF16)
    wkv = w_kv_b[0]
    wkb, wvb = wkv[..., :MLA_D_NOPE], wkv[..., MLA_D_NOPE:]
    zk = lambda n: jnp.zeros((kv_lora, H, n), F32)
    wkb_sp = jnp.concatenate([wkb, zk(HEAD_PAD - MLA_D_NOPE)], -1).reshape(kv_lora, H * HEAD_PAD).astype(BF16)
    odd = (jnp.arange(H) % 2 == 1)[None, :, None]
    wvb_sp = jnp.where(odd, jnp.concatenate([zk(HEAD_PAD - MLA_D_V), wvb], -1),
                       jnp.concatenate([wvb, zk(HEAD_PAD - MLA_D_V)], -1))
    wvb_sp = wvb_sp.reshape(kv_lora, H * HEAD_PAD).astype(BF16)
    jj = jnp.arange(LANES)[:, None]
    ll = jnp.arange(H * HEAD_PAD)[None, :]
    esel = ((ll % HEAD_PAD == jj + MLA_D_NOPE) & (jj < MLA_D_ROPE)).astype(BF16)
    eye_h = jnp.eye(H, dtype=bool)
    wabs = jnp.where(eye_h[:, None, :, None], wkb.transpose(1, 2, 0)[:, :, None, :], 0.0)
    wabs = wabs.reshape(H * MLA_D_NOPE, H * kv_lora).astype(BF16)
    wvbd = jnp.where(eye_h[:, None, :, None], wvb.transpose(1, 0, 2)[:, :, None, :], 0.0)
    wvbd = wvbd.reshape(H * kv_lora, H * MLA_D_V).astype(BF16)
    wo1 = w_out[0][:GDN_V_W].astype(BF16)
    wo2 = w_out[0][GDN_V_W:].astype(BF16)
    g2 = norm_mlp_g[0][None, :]
    wup = w_up[0].astype(BF16)
    wdn = w_down[0].astype(BF16)
    gf = final_norm_g[None, :]
    cw = jnp.pad(conv_w[0], ((0, SUBLANES - CONV_W), (0, 0)))
    lane_row = lambda v: jnp.pad(v[None, :], ((0, 0), (GDN_HEADS, LANES - 2 * GDN_HEADS)))
    alog = lane_row(a_log[0])
    dtb = lane_row(dt_bias[0])
    ng = gdn_norm_g[0][None, :]

    common = (gmix, w1, wba, wcq, wckv, wkpe, wkpesw, qg, kvg)
    prompt_consts = common + (wqcat, wqsw, wkb_sp, wvb_sp, esel)
    sample_consts = common + (wq_nope.reshape(q_lora, -1).astype(BF16), wq_pe.reshape(q_lora, -1).astype(BF16),
                              wq_pesw.reshape(q_lora, -1).astype(BF16), wabs)
    pw = [CONV_CH, GDN_V_W, LANES, kv_lora, MLA_D_ROPE, H * HEAD_PAD, H * HEAD_PAD, H * HEAD_PAD]
    pd = [F32, F32, F32, F32, F32, BF16, BF16, BF16]

    tab_m = _prompt_tables(jnp.arange(N_META))
    qkv_m, z_m, ba_m, lat_m, kpe_m, _, kcat_m, vsp_m = _proj_call(
        _proj_prompt_kernel, meta_tokens, tab_m, prompt_consts, pw, pd, N_META, "proj_meta")
    zero_conv = jnp.zeros((1, SUBLANES, CONV_CH), F32)
    zero_s = jnp.zeros((1, GDN_HEADS, GDN_DK, GDN_DV), F32)
    _, s1 = _gdn_call(qkv_m[None], z_m[None], ba_m[None], zero_conv, zero_s, cw, alog, dtb, ng,
                      N_META, N_META, "gdn_meta")
    conv1 = jnp.pad(qkv_m[N_META - (CONV_W - 1):], ((SUBLANES - (CONV_W - 1), 0), (0, 0)))[None]
    k_meta = jnp.pad(kcat_m, ((0, LANES - N_META), (0, 0)))
    v_meta = jnp.pad(vsp_m, ((0, LANES - N_META), (0, 0)))

    tab_p = _prompt_tables(N_META + jnp.arange(T))
    xp2d = x_prompt.reshape(B * T, D)
    qkv_p, z_p, ba_p, lat_p, kpe_p, qcat_p, kcat_p, vsp_p = _proj_call(
        _proj_prompt_kernel, xp2d, tab_p, prompt_consts, pw, pd, 256, "proj_prompt")
    r3 = lambda a: a.reshape(B, T, a.shape[-1])
    qkv_p3 = r3(qkv_p)
    og_p, s2 = _gdn_call(qkv_p3, r3(z_p), r3(ba_p), conv1, s1, cw, alog, dtb, ng, GDN_CHUNK, T, "gdn_prompt")
    om_p = _attn_call(r3(qcat_p), r3(kcat_p), r3(vsp_p), k_meta, v_meta, 256)
    post_consts = (wo1, wo2, g2, wup, wdn, gf)
    y_p = _post_call(xp2d, og_p.reshape(B * T, -1), om_p.reshape(B * T, -1), post_consts, 512, False, "post_prompt")

    tp = SAMPLE_T_PAD
    xs = jnp.pad(x_sample, ((0, 0), (0, tp - Ts), (0, 0))).reshape(Bs * tp, D)
    tab_s = jnp.tile(_sample_tables(past_len + jnp.arange(tp)), (Bs, 1))
    sw = [CONV_CH, GDN_V_W, LANES, kv_lora, MLA_D_ROPE, H * kv_lora, H * MLA_D_ROPE]
    sd = [F32, F32, F32, F32, F32, BF16, BF16]
    qkv_s, z_s, ba_s, lat_s, kpe_s, qlat_s, qpe_s = _proj_call(
        _proj_sample_kernel, xs, tab_s, sample_consts, sw, sd, Bs * tp, "proj_sample")
    s3 = lambda a: a.reshape(Bs, tp, a.shape[-1])
    qkv_s3 = s3(qkv_s)
    conv_in_s = jnp.pad(state_conv[0], ((0, 0), (SUBLANES - (CONV_W - 1), 0), (0, 0)))
    og_s, s_new = _gdn_call(qkv_s3, s3(z_s), s3(ba_s), conv_in_s, state_ssm[0], cw, alog, dtb, ng, tp, Ts, "gdn_sample")
    out_lat = _dattn_call(page_table, qlat_s.reshape(Bs, tp * H, kv_lora), qpe_s.reshape(Bs, tp * H, MLA_D_ROPE),
                          s3(lat_s), s3(kpe_s), cache_latent, cache_krope, Ts, 8)
    sample_post_consts = (wvbd,) + post_consts
    y_s = _post_call(xs, og_s.reshape(Bs * tp, -1), out_lat.reshape(Bs * tp, H * kv_lora),
                     sample_post_consts, Bs * tp, True, "post_sample")

    bc = lambda a: jnp.broadcast_to(a[None], (B,) + a.shape)
    lat_po = jnp.concatenate([bc(lat_m), r3(lat_p)], axis=1)[None]
    kpe_po = jnp.concatenate([bc(kpe_m), r3(kpe_p)], axis=1)[None]
    conv_po = qkv_p3[:, T - (CONV_W - 1):][None]
    conv_so = qkv_s3[:, Ts - (CONV_W - 1):Ts][None]
    return (y_p.reshape(B, T, D), y_s.reshape(Bs, tp, D)[:, :Ts],
            lat_po, kpe_po, conv_po, s2[None],
            s3(lat_s)[:, :Ts][None], s3(kpe_s)[:, :Ts][None], conv_so, s_new[None])
```

```python
import functools
import math

import jax
import jax.numpy as jnp
from jax import lax
from jax.experimental import pallas as pl
from jax.experimental.pallas import tpu as pltpu

F32 = jnp.float32
BF16 = jnp.bfloat16

N_META = 16
EPS = 1e-6
GDN_HEADS = 8
GDN_DK = 64
GDN_DV = 64
CONV_W = 4
GDN_CHUNK = 64
MLA_HEADS = 8
MLA_D_NOPE = 64
MLA_D_ROPE = 32
MLA_D_V = 64
ROPE_THETA = 10000.0
MLA_SCALE = (MLA_D_NOPE + MLA_D_ROPE) ** -0.5
GDN_QK_W = GDN_HEADS * GDN_DK
GDN_V_W = GDN_HEADS * GDN_DV
CONV_CH = 2 * GDN_QK_W + GDN_V_W

LANES = 128
SUBLANES = 8
HEAD_PAD = 128
SAMPLE_T_PAD = 8
VMEM_LIMIT = 56 * 1024 * 1024


def _dot(a, b):
    return jnp.dot(a, b, preferred_element_type=F32)


def _dot_nt(a, b):
    return lax.dot_general(a, b, (((1,), (1,)), ((), ())), preferred_element_type=F32)


def _dot_tn(a, b):
    return lax.dot_general(a, b, (((0,), (0,)), ((), ())), preferred_element_type=F32)


def _rms(x, g):
    return x * lax.rsqrt(jnp.mean(x * x, axis=-1, keepdims=True) + EPS) * g


def _sigmoid(x):
    return 1.0 / (1.0 + jnp.exp(-x))


def _const_spec(shape):
    nd = len(shape)
    return pl.BlockSpec(shape, lambda *_: (0,) * nd)


def _proj_common(x_ref, gmix_ref, w1_ref, wba_ref, wcq_ref, wckv_ref, wkpe_ref, wkpesw_ref,
                 qg_ref, kvg_ref, kcos, ksin, qkv_ref, z_ref, ba_ref, lat_ref, kpe_ref):
    hn = _rms(x_ref[...], gmix_ref[...]).astype(BF16)
    qkvz = _dot(hn, w1_ref[...])
    qkv_ref[...] = qkvz[:, :CONV_CH]
    z_ref[...] = qkvz[:, CONV_CH:]
    ba_ref[...] = _dot(hn, wba_ref[...])
    cq = _dot(hn, wcq_ref[...])
    ckv = _dot(hn, wckv_ref[...])
    kpe = _dot(hn, wkpe_ref[...]) * kcos + _dot(hn, wkpesw_ref[...]) * ksin
    kpe_ref[...] = kpe[:, :MLA_D_ROPE]
    lat = _rms(ckv, kvg_ref[...])
    lat_ref[...] = lat
    cqn = _rms(cq, qg_ref[...]).astype(BF16)
    return cqn, lat.astype(BF16), kpe.astype(BF16)


def _proj_prompt_kernel(x_ref, tab_ref, gmix_ref, w1_ref, wba_ref, wcq_ref, wckv_ref, wkpe_ref,
                        wkpesw_ref, qg_ref, kvg_ref, wq_ref, wqsw_ref, wkb_ref, wvb_ref, esel_ref,
                        qkv_ref, z_ref, ba_ref, lat_ref, kpe_ref, qcat_ref, kcat_ref, vsp_ref):
    qcos = tab_ref[:, 0 * LANES:1 * LANES]
    qsin = tab_ref[:, 1 * LANES:2 * LANES]
    kcos = tab_ref[:, 2 * LANES:3 * LANES]
    ksin = tab_ref[:, 3 * LANES:4 * LANES]
    cqn, latb, kpeb = _proj_common(x_ref, gmix_ref, w1_ref, wba_ref, wcq_ref, wckv_ref, wkpe_ref,
                                   wkpesw_ref, qg_ref, kvg_ref, kcos, ksin,
                                   qkv_ref, z_ref, ba_ref, lat_ref, kpe_ref)
    qa = _dot(cqn, wq_ref[...])
    qb = _dot(cqn, wqsw_ref[...])
    for h in range(MLA_HEADS):
        sl = slice(h * HEAD_PAD, (h + 1) * HEAD_PAD)
        qcat_ref[:, sl] = (qa[:, sl] * qcos + qb[:, sl] * qsin).astype(BF16)
    kcat_ref[...] = (_dot(latb, wkb_ref[...]) + _dot(kpeb, esel_ref[...])).astype(BF16)
    vsp_ref[...] = _dot(latb, wvb_ref[...]).astype(BF16)


def _proj_sample_kernel(x_ref, tab_ref, gmix_ref, w1_ref, wba_ref, wcq_ref, wckv_ref, wkpe_ref,
                        wkpesw_ref, qg_ref, kvg_ref, wqn_ref, wqpe_ref, wqpesw_ref, wabs_ref,
                        qkv_ref, z_ref, ba_ref, lat_ref, kpe_ref, qlat_ref, qpe_ref):
    pe_w = MLA_HEADS * MLA_D_ROPE
    qcos = tab_ref[:, 0:pe_w]
    qsin = tab_ref[:, pe_w:2 * pe_w]
    kcos = tab_ref[:, 2 * pe_w:2 * pe_w + LANES]
    ksin = tab_ref[:, 2 * pe_w + LANES:2 * pe_w + 2 * LANES]
    cqn, _, _ = _proj_common(x_ref, gmix_ref, w1_ref, wba_ref, wcq_ref, wckv_ref, wkpe_ref,
                             wkpesw_ref, qg_ref, kvg_ref, kcos, ksin,
                             qkv_ref, z_ref, ba_ref, lat_ref, kpe_ref)
    qn = _dot(cqn, wqn_ref[...]).astype(BF16)
    qlat_ref[...] = (_dot(qn, wabs_ref[...]) * MLA_SCALE).astype(BF16)
    qpe_ref[...] = (_dot(cqn, wqpe_ref[...]) * qcos + _dot(cqn, wqpesw_ref[...]) * qsin).astype(BF16)


def _proj_call(kernel_fn, x2d, tab, consts, out_widths, out_dtypes, tm, name):
    n_rows, d_model = x2d.shape
    assert n_rows % tm == 0
    row_spec = lambda w: pl.BlockSpec((tm, w), lambda i: (i, 0))
    assert tab.shape[0] % tm == 0 and n_rows % tab.shape[0] == 0
    tab_blocks = tab.shape[0] // tm
    tab_spec = pl.BlockSpec((tm, tab.shape[1]), lambda i: (i % tab_blocks, 0))
    in_specs = [row_spec(d_model), tab_spec] + [_const_spec(c.shape) for c in consts]
    return pl.pallas_call(
        kernel_fn,
        grid=(n_rows // tm,),
        in_specs=in_specs,
        out_specs=[row_spec(w) for w in out_widths],
        out_shape=[jax.ShapeDtypeStruct((n_rows, w), dt) for w, dt in zip(out_widths, out_dtypes)],
        compiler_params=pltpu.CompilerParams(dimension_semantics=("arbitrary",),
                                             vmem_limit_bytes=VMEM_LIMIT),
        name=name,
    )(x2d, tab, *consts)


def _gdn_intra_kernel(qkv_ref, prev_ref, ba_ref, cinit_ref, cw_ref, alog_ref, dtb_ref,
                      t_ref, qk_ref, kq_ref, kdt_ref, vb_ref, egl_ref, xbuf, *, chunk, valid_len, n_chunks):
    c = pl.program_id(1)
    C = chunk
    H = GDN_HEADS
    tail0 = SUBLANES - (CONV_W - 1)

    xbuf[0:SUBLANES, :] = jnp.where(c == 0, cinit_ref[0], prev_ref[0])
    xbuf[SUBLANES:SUBLANES + C, :] = qkv_ref[0]
    cw = cw_ref[...]
    y = xbuf[tail0:tail0 + C, :] * cw[0:1, :]
    for j in range(1, CONV_W):
        y = y + xbuf[tail0 + j:tail0 + j + C, :] * cw[j:j + 1, :]
    y = y * _sigmoid(y)

    ba = ba_ref[0]
    beta_all = _sigmoid(ba)
    sp_in = ba + dtb_ref[...]
    softplus = jnp.maximum(sp_in, 0.0) + jnp.log1p(jnp.exp(-jnp.abs(sp_in)))
    g_all = -jnp.exp(alog_ref[...]) * softplus
    if valid_len < n_chunks * C:
        row = c * C + lax.broadcasted_iota(jnp.int32, (C, LANES), 0)
        beta_all = jnp.where(row < valid_len, beta_all, 0.0)
        g_all = jnp.where(row < valid_len, g_all, 0.0)

    ri = lax.broadcasted_iota(jnp.int32, (C, C), 0)
    ci = lax.broadcasted_iota(jnp.int32, (C, C), 1)
    lower = ri >= ci
    strict = ri > ci
    gc_all = lax.dot_general(lower.astype(F32), g_all, (((1,), (0,)), ((), ())),
                             precision=lax.Precision.HIGHEST, preferred_element_type=F32)
    sel = (lax.broadcasted_iota(jnp.int32, (SUBLANES, LANES), 1)
           == lax.broadcasted_iota(jnp.int32, (SUBLANES, LANES), 0) + GDN_HEADS).astype(F32)
    gc_t = lax.dot_general(sel, gc_all, (((1,), (1,)), ((), ())),
                           precision=lax.Precision.HIGHEST, preferred_element_type=F32)

    egl_ref[0, 0] = jnp.broadcast_to(jnp.exp(gc_t[:, C - 1:C]), (SUBLANES, LANES))

    heads = range(H)
    eye_c = (ri == ci).astype(F32)
    rd = lax.broadcasted_iota(jnp.int32, (GDN_DK, GDN_DK), 0)
    cd = lax.broadcasted_iota(jnp.int32, (GDN_DK, GDN_DK), 1)
    eye_dk = (rd == cd).astype(BF16)
    qs, ks, decays, egcs, kbs = [], [], [], [], []
    for h in heads:
        qh = y[:, h * GDN_DK:(h + 1) * GDN_DK]
        kh = y[:, GDN_QK_W + h * GDN_DK:GDN_QK_W + (h + 1) * GDN_DK]
        vh = y[:, 2 * GDN_QK_W + h * GDN_DV:2 * GDN_QK_W + (h + 1) * GDN_DV]
        qh = qh * lax.rsqrt(jnp.sum(qh * qh, axis=-1, keepdims=True) + EPS) * (GDN_DK ** -0.5)
        kh = kh * lax.rsqrt(jnp.sum(kh * kh, axis=-1, keepdims=True) + EPS)
        beta = beta_all[:, h:h + 1]
        gcol = gc_all[:, H + h:H + h + 1]
        grow = gc_t[h:h + 1, :]
        diff = gcol - grow
        decays.append(jnp.where(lower, jnp.exp(jnp.where(lower, diff, 0.0)), 0.0))
        egc = jnp.exp(gcol)
        kb = kh * beta
        vb_ref[0, 0, h] = vh * beta
        kq_ref[0, 0, h] = jnp.concatenate([kb * egc, qh * egc], axis=0).astype(BF16)
        k_dec = (kh * jnp.exp(gcol[C - 1:C, :] - gcol)).astype(BF16)
        kdt_ref[0, 0, h] = _dot_nt(eye_dk, k_dec).astype(BF16)
        qs.append(qh.astype(BF16))
        ks.append(kh.astype(BF16))
        kbs.append(kb.astype(BF16))
    kk = [_dot_nt(kbs[h], ks[h]) for h in heads]
    qk = [_dot_nt(qs[h], ks[h]) for h in heads]
    for h in heads:
        qk_ref[0, 0, h] = jnp.where(lower, qk[h] * decays[h], 0.0).astype(BF16)
    p = [jnp.where(strict, -(kk[h] * decays[h]), 0.0) for h in heads]
    t = [eye_c + p[h] for h in heads]
    n_iter = max(1, int(math.ceil(math.log2(C))))
    for _ in range(1, n_iter):
        pb = [p[h].astype(BF16) for h in heads]
        p = [_dot(pb[h], pb[h]) for h in heads]
        t = [t[h] + _dot(t[h].astype(BF16), p[h].astype(BF16)) for h in heads]
    for h in heads:
        t_ref[0, 0, h] = t[h].astype(BF16)


def _gdn_scan_kernel(t_ref, qk_ref, kq_ref, kdt_ref, vb_ref, egl_ref, z_ref, sinit_ref, ng_ref,
                     o_ref, sfin_ref, s_scr, *, chunk, n_chunks):
    c = pl.program_id(1)
    C = chunk
    heads = range(GDN_HEADS)

    @pl.when(c == 0)
    def _():
        s_scr[...] = sinit_ref[0]

    s = [s_scr[h] for h in heads]
    ws = [_dot(kq_ref[0, 0, h], s[h].astype(BF16)) for h in heads]
    r = [(vb_ref[0, 0, h] - ws[h][:C]).astype(BF16) for h in heads]
    v_new = [_dot(t_ref[0, 0, h], r[h]).astype(BF16) for h in heads]
    o = [ws[h][C:] + _dot(qk_ref[0, 0, h], v_new[h]) for h in heads]
    egl = egl_ref[0, 0]
    for h in heads:
        s_scr[h] = s[h] * egl[h:h + 1, 0:1] + _dot(kdt_ref[0, 0, h], v_new[h])
    z = z_ref[0]
    ng = ng_ref[...]
    for h in heads:
        hs = slice(h * GDN_DV, (h + 1) * GDN_DV)
        zh = z[:, hs]
        o_n = o[h] * lax.rsqrt(jnp.mean(o[h] * o[h], axis=-1, keepdims=True) + EPS) * ng
        o_ref[0, :, hs] = (o_n * (zh * _sigmoid(zh))).astype(o_ref.dtype)

    @pl.when(c == n_chunks - 1)
    def _():
        sfin_ref[0] = s_scr[...]


def _gdn_call(qkv, z, ba, conv_init, s_init, cw, alog, dtb, ng, chunk, valid_len, name):
    B, T, _ = qkv.shape
    assert T % chunk == 0 and chunk % SUBLANES == 0
    n_chunks = T // chunk
    H, C = GDN_HEADS, chunk
    shared_c = conv_init.shape[0] == 1
    shared_s = s_init.shape[0] == 1
    params = pltpu.CompilerParams(dimension_semantics=("arbitrary", "arbitrary"), vmem_limit_bytes=VMEM_LIMIT)
    tok = lambda w: pl.BlockSpec((1, chunk, w), lambda b, c: (b, c, 0))
    per_chunk = lambda *dims: pl.BlockSpec((1, 1) + dims, lambda b, c: (b, c) + (0,) * len(dims))
    prev_rows = pl.BlockSpec((1, SUBLANES, CONV_CH),
                             lambda b, c: (b, jnp.maximum(c * (C // SUBLANES) - 1, 0), 0))
    inter_dims = [(H, C, C), (H, C, C), (H, 2 * C, GDN_DK), (H, GDN_DK, C), (H, C, GDN_DV), (SUBLANES, LANES)]
    inter_dtypes = [BF16, BF16, BF16, BF16, F32, F32]
    inter = pl.pallas_call(
        functools.partial(_gdn_intra_kernel, chunk=chunk, valid_len=valid_len, n_chunks=n_chunks),
        grid=(B, n_chunks),
        in_specs=[
            tok(CONV_CH), prev_rows, tok(LANES),
            pl.BlockSpec((1, SUBLANES, CONV_CH), (lambda b, c: (0, 0, 0)) if shared_c else (lambda b, c: (b, 0, 0))),
            _const_spec(cw.shape), _const_spec(alog.shape), _const_spec(dtb.shape),
        ],
        out_specs=[per_chunk(*d) for d in inter_dims],
        out_shape=[jax.ShapeDtypeStruct((B, n_chunks) + d, dt) for d, dt in zip(inter_dims, inter_dtypes)],
        scratch_shapes=[pltpu.VMEM((SUBLANES + chunk, CONV_CH), F32)],
        compiler_params=params,
        name=name + "_intra",
    )(qkv, qkv, ba, conv_init, cw, alog, dtb)
    return pl.pallas_call(
        functools.partial(_gdn_scan_kernel, chunk=chunk, n_chunks=n_chunks),
        grid=(B, n_chunks),
        in_specs=[per_chunk(*d) for d in inter_dims] + [
            tok(GDN_V_W),
            pl.BlockSpec((1, H, GDN_DK, GDN_DV), (lambda b, c: (0, 0, 0, 0)) if shared_s else (lambda b, c: (b, 0, 0, 0))),
            _const_spec(ng.shape),
        ],
        out_specs=[
            tok(GDN_V_W),
            pl.BlockSpec((1, H, GDN_DK, GDN_DV), lambda b, c: (b, 0, 0, 0)),
        ],
        out_shape=[
            jax.ShapeDtypeStruct((B, T, GDN_V_W), BF16),
            jax.ShapeDtypeStruct((B, H, GDN_DK, GDN_DV), F32),
        ],
        scratch_shapes=[pltpu.VMEM((H, GDN_DK, GDN_DV), F32)],
        compiler_params=params,
        name=name + "_scan",
    )(*inter, z, s_init, ng)


def _attn_kernel(q_ref, k_ref, v_ref, km_ref, vm_ref, o_ref, *, tq, n_meta):
    qi = pl.program_id(2)
    neg = -jnp.inf
    col_m = lax.broadcasted_iota(jnp.int32, (tq, km_ref.shape[0]), 1)
    row_d = lax.broadcasted_iota(jnp.int32, (tq, tq), 0)
    col_d = lax.broadcasted_iota(jnp.int32, (tq, tq), 1)
    out = None
    for hh in range(2):
        ls = slice(hh * HEAD_PAD, (hh + 1) * HEAD_PAD)
        q = q_ref[0, :, ls]
        s = jnp.where(col_m < n_meta, _dot_nt(q, km_ref[:, ls]), neg)
        m = jnp.max(s, axis=-1, keepdims=True)
        p = jnp.exp(s - m)
        l = jnp.sum(p, axis=-1, keepdims=True)
        acc = _dot(p.astype(BF16), vm_ref[:, ls])

        def step(s, m, l, acc, v):
            m_new = jnp.maximum(m, jnp.max(s, axis=-1, keepdims=True))
            alpha = jnp.exp(m - m_new)
            p = jnp.exp(s - m_new)
            l = alpha * l + jnp.sum(p, axis=-1, keepdims=True)
            acc = alpha * acc + _dot(p.astype(BF16), v)
            return m_new, l, acc

        def body(kb, carry, q=q, ls=ls):
            m, l, acc = carry
            start = pl.multiple_of(kb * tq, tq)
            k = k_ref[0, pl.ds(start, tq), ls]
            v = v_ref[0, pl.ds(start, tq), ls]
            return step(_dot_nt(q, k), m, l, acc, v)

        m, l, acc = lax.fori_loop(0, qi, body, (m, l, acc))
        start = pl.multiple_of(qi * tq, tq)
        k = k_ref[0, pl.ds(start, tq), ls]
        v = v_ref[0, pl.ds(start, tq), ls]
        s = jnp.where(col_d <= row_d, _dot_nt(q, k), neg)
        m, l, acc = step(s, m, l, acc, v)
        contrib = acc / l
        out = contrib if out is None else out + contrib
    o_ref[0] = out.astype(o_ref.dtype)


def _attn_call(qcat, kcat, vsp, k_meta, v_meta, tq):
    B, T, _ = qcat.shape
    n_pairs = MLA_HEADS // 2
    pw = 2 * HEAD_PAD
    kern = functools.partial(_attn_kernel, tq=tq, n_meta=N_META)
    return pl.pallas_call(
        kern,
        grid=(B, n_pairs, T // tq),
        in_specs=[
            pl.BlockSpec((1, tq, pw), lambda b, hp, qi: (b, qi, hp)),
            pl.BlockSpec((1, T, pw), lambda b, hp, qi: (b, 0, hp)),
            pl.BlockSpec((1, T, pw), lambda b, hp, qi: (b, 0, hp)),
            pl.BlockSpec((k_meta.shape[0], pw), lambda b, hp, qi: (0, hp)),
            pl.BlockSpec((v_meta.shape[0], pw), lambda b, hp, qi: (0, hp)),
        ],
        out_specs=pl.BlockSpec((1, tq, LANES), lambda b, hp, qi: (b, qi, hp)),
        out_shape=jax.ShapeDtypeStruct((B, T, n_pairs * LANES), BF16),
        compiler_params=pltpu.CompilerParams(dimension_semantics=("arbitrary", "arbitrary", "arbitrary"),
                                             vmem_limit_bytes=VMEM_LIMIT),
        name="attn",
    )(qcat, kcat, vsp, k_meta, v_meta)


def _dattn_kernel(pt_ref, *refs, pages_per_step, t_new, n_steps):
    del pt_ref
    P = pages_per_step
    qlat_ref, qpe_ref, latn_ref, kpen_ref = refs[:4]
    lat_pages = refs[4:4 + P]
    kpe_pages = refs[4 + P:4 + 2 * P]
    o_ref = refs[4 + 2 * P]
    m_scr, l_scr, acc_scr = refs[4 + 2 * P + 1:]
    j = pl.program_id(1)
    neg = -jnp.inf

    @pl.when(j == 0)
    def _():
        m_scr[...] = jnp.full(m_scr.shape, neg, F32)
        l_scr[...] = jnp.zeros(l_scr.shape, F32)
        acc_scr[...] = jnp.zeros(acc_scr.shape, F32)

    ql = qlat_ref[0]
    qp = qpe_ref[0]

    def update(s, vals):
        m = m_scr[...]
        m_new = jnp.maximum(m, jnp.max(s, axis=-1, keepdims=True))
        alpha = jnp.exp(m - m_new)
        p = jnp.exp(s - m_new)
        l_scr[...] = alpha * l_scr[...] + jnp.sum(p, axis=-1, keepdims=True)
        acc_scr[...] = alpha * acc_scr[...] + _dot(p.astype(BF16), vals)
        m_scr[...] = m_new

    cb = jnp.concatenate([r[0, 0] for r in lat_pages], axis=0).astype(BF16)
    kb = jnp.concatenate([r[0, 0] for r in kpe_pages], axis=0).astype(BF16)
    update(_dot_nt(ql, cb) + _dot_nt(qp, kb), cb)

    @pl.when(j == n_steps - 1)
    def _():
        R = ql.shape[0]
        pad = 2 * SUBLANES - latn_ref.shape[1]
        cn = jnp.concatenate([latn_ref[0], jnp.zeros((pad, latn_ref.shape[2]), F32)], axis=0).astype(BF16)
        kn = jnp.concatenate([kpen_ref[0], jnp.zeros((pad, kpen_ref.shape[2]), F32)], axis=0).astype(BF16)
        s = _dot_nt(ql, cn) + _dot_nt(qp, kn)
        tok = lax.shift_right_logical(lax.broadcasted_iota(jnp.int32, s.shape, 0), int(math.log2(MLA_HEADS)))
        key = lax.broadcasted_iota(jnp.int32, s.shape, 1)
        s = jnp.where((key <= tok) & (key < t_new), s, neg)
        update(s, cn)
        o_ref[0] = (acc_scr[...] / l_scr[...]).astype(o_ref.dtype)


def _dattn_call(page_table, qlat, qpe, lat_new, kpe_new, cache_latent, cache_krope, t_new, pages_per_step):
    Bs, R, kv_lora = qlat.shape
    n_pages = page_table.shape[1]
    P = pages_per_step
    assert n_pages % P == 0
    n_steps = n_pages // P
    page = cache_latent.shape[2]
    kern = functools.partial(_dattn_kernel, pages_per_step=P, t_new=t_new, n_steps=n_steps)
    per_b = lambda shape: pl.BlockSpec((1,) + shape, lambda b, j, pt: (b, 0, 0))

    def page_spec(width, k):
        return pl.BlockSpec((1, 1, page, width), lambda b, j, pt: (0, pt[b, j * P + k], 0, 0))

    grid_spec = pltpu.PrefetchScalarGridSpec(
        num_scalar_prefetch=1,
        grid=(Bs, n_steps),
        in_specs=[per_b((R, kv_lora)), per_b((R, MLA_D_ROPE)),
                  per_b(lat_new.shape[1:]), per_b(kpe_new.shape[1:])]
                 + [page_spec(kv_lora, k) for k in range(P)]
                 + [page_spec(MLA_D_ROPE, k) for k in range(P)],
        out_specs=per_b((R, kv_lora)),
        scratch_shapes=[pltpu.VMEM((R, 1), F32), pltpu.VMEM((R, 1), F32), pltpu.VMEM((R, kv_lora), F32)],
    )
    return pl.pallas_call(
        kern,
        grid_spec=grid_spec,
        out_shape=jax.ShapeDtypeStruct((Bs, R, kv_lora), BF16),
        compiler_params=pltpu.CompilerParams(dimension_semantics=("arbitrary", "arbitrary"),
                                             vmem_limit_bytes=VMEM_LIMIT),
        name="dattn",
    )(page_table, qlat, qpe, lat_new, kpe_new, *([cache_latent] * P), *([cache_krope] * P))


def _post_kernel(*refs, absorbed_values):
    if absorbed_values:
        h_ref, og_ref, om_ref, wvbd_ref, wo1_ref, wo2_ref, g2_ref, wup_ref, wdn_ref, gf_ref, y_ref = refs
        om = _dot(om_ref[...], wvbd_ref[...]).astype(BF16)
    else:
        h_ref, og_ref, om_ref, wo1_ref, wo2_ref, g2_ref, wup_ref, wdn_ref, gf_ref, y_ref = refs
        om = om_ref[...]
    h2 = h_ref[...] + _dot(og_ref[...], wo1_ref[...]) + _dot(om, wo2_ref[...])
    hn = _rms(h2, g2_ref[...]).astype(BF16)
    u = jnp.maximum(_dot(hn, wup_ref[...]), 0.0)
    h3 = h2 + _dot((u * u).astype(BF16), wdn_ref[...])
    y_ref[...] = _rms(h3, gf_ref[...])


def _post_call(h2d, og, om, consts, tm, absorbed_values, name):
    n_rows, d_model = h2d.shape
    assert n_rows % tm == 0
    row_spec = lambda w: pl.BlockSpec((tm, w), lambda i: (i, 0))
    kern = functools.partial(_post_kernel, absorbed_values=absorbed_values)
    return pl.pallas_call(
        kern,
        grid=(n_rows // tm,),
        in_specs=[row_spec(d_model), row_spec(og.shape[1]), row_spec(om.shape[1])]
                 + [_const_spec(c.shape) for c in consts],
        out_specs=row_spec(d_model),
        out_shape=jax.ShapeDtypeStruct((n_rows, d_model), F32),
        compiler_params=pltpu.CompilerParams(dimension_semantics=("arbitrary",),
                                             vmem_limit_bytes=VMEM_LIMIT),
        name=name,
    )(h2d, og, om, *consts)


def _rope_parts(pos):
    half = MLA_D_ROPE // 2
    inv = ROPE_THETA ** (-jnp.arange(half, dtype=F32) / half)
    ang = pos.astype(F32)[:, None] * inv[None, :]
    cos = jnp.concatenate([jnp.cos(ang), jnp.cos(ang)], -1)
    sin_signed = jnp.concatenate([-jnp.sin(ang), jnp.sin(ang)], -1)
    return cos, sin_signed


def _pad_lanes(x, width):
    return jnp.pad(x, ((0, 0), (0, width - x.shape[1])))


def _key_tables(cos, sin_signed):
    return _pad_lanes(cos, LANES), _pad_lanes(sin_signed, LANES)


def _prompt_tables(pos):
    cos, sin_signed = _rope_parts(pos)
    n = pos.shape[0]
    qcos = jnp.concatenate([jnp.full((n, MLA_D_NOPE), MLA_SCALE, F32), cos * MLA_SCALE], -1)
    qsin = jnp.concatenate([jnp.zeros((n, MLA_D_NOPE), F32), sin_signed * MLA_SCALE], -1)
    kcos, ksin = _key_tables(cos, sin_signed)
    return jnp.concatenate([_pad_lanes(qcos, LANES), _pad_lanes(qsin, LANES), kcos, ksin], -1)


def _sample_tables(pos):
    cos, sin_signed = _rope_parts(pos)
    kcos, ksin = _key_tables(cos, sin_signed)
    return jnp.concatenate([jnp.tile(cos * MLA_SCALE, (1, MLA_HEADS)),
                            jnp.tile(sin_signed * MLA_SCALE, (1, MLA_HEADS)), kcos, ksin], -1)


def _swap_halves(w):
    half = MLA_D_ROPE // 2
    return jnp.concatenate([w[..., half:], w[..., :half]], -1)


def kernel(x_prompt, x_sample, cache_latent, cache_krope, state_conv, state_ssm, page_table,
           meta_tokens, norm_mix_g, w_in, conv_w, a_log, dt_bias, gdn_norm_g, q_norm_g, w_q_b,
           kv_norm_g, w_kv_b, w_out, norm_mlp_g, w_up, w_down, final_norm_g):
    assert w_in.shape[0] == 1, "single-layer problem"
    B, T, D = x_prompt.shape
    Bs, Ts, _ = x_sample.shape
    assert CONV_W - 1 <= Ts <= SAMPLE_T_PAD
    past_len = page_table.shape[1] * cache_latent.shape[2]
    q_lora = q_norm_g.shape[1]
    kv_lora = kv_norm_g.shape[1]
    H = MLA_HEADS

    wi = w_in[0]
    c0 = CONV_CH + GDN_V_W
    w1 = wi[:, :c0].astype(BF16)
    wba = _pad_lanes(wi[:, c0:c0 + 2 * GDN_HEADS], LANES).astype(BF16)
    c1 = c0 + 2 * GDN_HEADS
    wcq = wi[:, c1:c1 + q_lora].astype(BF16)
    wckv = wi[:, c1 + q_lora:c1 + q_lora + kv_lora].astype(BF16)
    wkpe_raw = wi[:, c1 + q_lora + kv_lora:]
    wkpe = _pad_lanes(wkpe_raw, LANES).astype(BF16)
    wkpesw = _pad_lanes(_swap_halves(wkpe_raw), LANES).astype(BF16)
    gmix = norm_mix_g[0][None, :]
    qg = q_norm_g[0][None, :]
    kvg = kv_norm_g[0][None, :]
    wq = w_q_b[0]
    wq_nope, wq_pe = wq[..., :MLA_D_NOPE], wq[..., MLA_D_NOPE:]
    wq_pesw = _swap_halves(wq_pe)
    zq = lambda n: jnp.zeros((q_lora, H, n), F32)
    wqcat = jnp.concatenate([wq_nope, wq_pe, zq(HEAD_PAD - MLA_D_NOPE - MLA_D_ROPE)], -1)
    wqsw = jnp.concatenate([zq(MLA_D_NOPE), wq_pesw, zq(HEAD_PAD - MLA_D_NOPE - MLA_D_ROPE)], -1)
    wqcat = wqcat.reshape(q_lora, H * HEAD_PAD).astype(BF16)
    wqsw = wqsw.reshape(q_lora, H * HEAD_PAD).astype(BF16)
    wkv = w_kv_b[0]
    wkb, wvb = wkv[..., :MLA_D_NOPE], wkv[..., MLA_D_NOPE:]
    zk = lambda n: jnp.zeros((kv_lora, H, n), F32)
    wkb_sp = jnp.concatenate([wkb, zk(HEAD_PAD - MLA_D_NOPE)], -1).reshape(kv_lora, H * HEAD_PAD).astype(BF16)
    odd = (jnp.arange(H) % 2 == 1)[None, :, None]
    wvb_sp = jnp.where(odd, jnp.concatenate([zk(HEAD_PAD - MLA_D_V), wvb], -1),
                       jnp.concatenate([wvb, zk(HEAD_PAD - MLA_D_V)], -1))
    wvb_sp = wvb_sp.reshape(kv_lora, H * HEAD_PAD).astype(BF16)
    jj = jnp.arange(LANES)[:, None]
    ll = jnp.arange(H * HEAD_PAD)[None, :]
    esel = ((ll % HEAD_PAD == jj + MLA_D_NOPE) & (jj < MLA_D_ROPE)).astype(BF16)
    eye_h = jnp.eye(H, dtype=bool)
    wabs = jnp.where(eye_h[:, None, :, None], wkb.transpose(1, 2, 0)[:, :, None, :], 0.0)
    wabs = wabs.reshape(H * MLA_D_NOPE, H * kv_lora).astype(BF16)
    wvbd = jnp.where(eye_h[:, None, :, None], wvb.transpose(1, 0, 2)[:, :, None, :], 0.0)
    wvbd = wvbd.reshape(H * kv_lora, H * MLA_D_V).astype(BF16)
    wo1 = w_out[0][:GDN_V_W].astype(BF16)
    wo2 = w_out[0][GDN_V_W:].astype(BF16)
    g2 = norm_mlp_g[0][None, :]
    wup = w_up[0].astype(BF16)
    wdn = w_down[0].astype(BF16)
    gf = final_norm_g[None, :]
    cw = jnp.pad(conv_w[0], ((0, SUBLANES - CONV_W), (0, 0)))
    lane_row = lambda v: jnp.pad(v[None, :], ((0, 0), (GDN_HEADS, LANES - 2 * GDN_HEADS)))
    alog = lane_row(a_log[0])
    dtb = lane_row(dt_bias[0])
    ng = gdn_norm_g[0][None, :]

    common = (gmix, w1, wba, wcq, wckv, wkpe, wkpesw, qg, kvg)
    prompt_consts = common + (wqcat, wqsw, wkb_sp, wvb_sp, esel)
    sample_consts = common + (wq_nope.reshape(q_lora, -1).astype(BF16), wq_pe.reshape(q_lora, -1).astype(BF16),
                              wq_pesw.reshape(q_lora, -1).astype(BF16), wabs)
    pw = [CONV_CH, GDN_V_W, LANES, kv_lora, MLA_D_ROPE, H * HEAD_PAD, H * HEAD_PAD, H * HEAD_PAD]
    pd = [F32, F32, F32, F32, F32, BF16, BF16, BF16]

    tab_m = _prompt_tables(jnp.arange(N_META))
    qkv_m, z_m, ba_m, lat_m, kpe_m, _, kcat_m, vsp_m = _proj_call(
        _proj_prompt_kernel, meta_tokens, tab_m, prompt_consts, pw, pd, N_META, "proj_meta")
    zero_conv = jnp.zeros((1, SUBLANES, CONV_CH), F32)
    zero_s = jnp.zeros((1, GDN_HEADS, GDN_DK, GDN_DV), F32)
    _, s1 = _gdn_call(qkv_m[None], z_m[None], ba_m[None], zero_conv, zero_s, cw, alog, dtb, ng,
                      N_META, N_META, "gdn_meta")
    conv1 = jnp.pad(qkv_m[N_META - (CONV_W - 1):], ((SUBLANES - (CONV_W - 1), 0), (0, 0)))[None]
    k_meta = jnp.pad(kcat_m, ((0, LANES - N_META), (0, 0)))
    v_meta = jnp.pad(vsp_m, ((0, LANES - N_META), (0, 0)))

    tab_p = _prompt_tables(N_META + jnp.arange(T))
    xp2d = x_prompt.reshape(B * T, D)
    qkv_p, z_p, ba_p, lat_p, kpe_p, qcat_p, kcat_p, vsp_p = _proj_call(
        _proj_prompt_kernel, xp2d, tab_p, prompt_consts, pw, pd, 256, "proj_prompt")
    r3 = lambda a: a.reshape(B, T, a.shape[-1])
    qkv_p3 = r3(qkv_p)
    og_p, s2 = _gdn_call(qkv_p3, r3(z_p), r3(ba_p), conv1, s1, cw, alog, dtb, ng, GDN_CHUNK, T, "gdn_prompt")
    om_p = _attn_call(r3(qcat_p), r3(kcat_p), r3(vsp_p), k_meta, v_meta, 256)
    post_consts = (wo1, wo2, g2, wup, wdn, gf)
    y_p = _post_call(xp2d, og_p.reshape(B * T, -1), om_p.reshape(B * T, -1), post_consts, 512, False, "post_prompt")

    tp = SAMPLE_T_PAD
    xs = jnp.pad(x_sample, ((0, 0), (0, tp - Ts), (0, 0))).reshape(Bs * tp, D)
    tab_s = jnp.tile(_sample_tables(past_len + jnp.arange(tp)), (Bs, 1))
    sw = [CONV_CH, GDN_V_W, LANES, kv_lora, MLA_D_ROPE, H * kv_lora, H * MLA_D_ROPE]
    sd = [F32, F32, F32, F32, F32, BF16, BF16]
    qkv_s, z_s, ba_s, lat_s, kpe_s, qlat_s, qpe_s = _proj_call(
        _proj_sample_kernel, xs, tab_s, sample_consts, sw, sd, Bs * tp, "proj_sample")
    s3 = lambda a: a.reshape(Bs, tp, a.shape[-1])
    qkv_s3 = s3(qkv_s)
    conv_in_s = jnp.pad(state_conv[0], ((0, 0), (SUBLANES - (CONV_W - 1), 0), (0, 0)))
    og_s, s_new = _gdn_call(qkv_s3, s3(z_s), s3(ba_s), conv_in_s, state_ssm[0], cw, alog, dtb, ng, tp, Ts, "gdn_sample")
    out_lat = _dattn_call(page_table, qlat_s.reshape(Bs, tp * H, kv_lora), qpe_s.reshape(Bs, tp * H, MLA_D_ROPE),
                          s3(lat_s), s3(kpe_s), cache_latent, cache_krope, Ts, 8)
    sample_post_consts = (wvbd,) + post_consts
    y_s = _post_call(xs, og_s.reshape(Bs * tp, -1), out_lat.reshape(Bs * tp, H * kv_lora),
                     sample_post_consts, Bs * tp, True, "post_sample")

    bc = lambda a: jnp.broadcast_to(a[None], (B,) + a.shape)
    lat_po = jnp.concatenate([bc(lat_m), r3(lat_p)], axis=1)[None]
    kpe_po = jnp.concatenate([bc(kpe_m), r3(kpe_p)], axis=1)[None]
    conv_po = qkv_p3[:, T - (CONV_W - 1):][None]
    conv_so = qkv_s3[:, Ts - (CONV_W - 1):Ts][None]
    return (y_p.reshape(B, T, D), y_s.reshape(Bs, tp, D)[:, :Ts],
            lat_po, kpe_po, conv_po, s2[None],
            s3(lat_s)[:, :Ts][None], s3(kpe_s)[:, :Ts][None], conv_so, s_new[None])
```

```python
import functools
import math

import jax
import jax.numpy as jnp
from jax import lax
from jax.experimental import pallas as pl
from jax.experimental.pallas import tpu as pltpu

F32 = jnp.float32
BF16 = jnp.bfloat16

N_META = 16
EPS = 1e-6
GDN_HEADS = 8
GDN_DK = 64
GDN_DV = 64
CONV_W = 4
GDN_CHUNK = 64
MLA_HEADS = 8
MLA_D_NOPE = 64
MLA_D_ROPE = 32
MLA_D_V = 64
ROPE_THETA = 10000.0
MLA_SCALE = (MLA_D_NOPE + MLA_D_ROPE) ** -0.5
GDN_QK_W = GDN_HEADS * GDN_DK
GDN_V_W = GDN_HEADS * GDN_DV
CONV_CH = 2 * GDN_QK_W + GDN_V_W

LANES = 128
SUBLANES = 8
HEAD_PAD = 128
SAMPLE_T_PAD = 8
VMEM_LIMIT = 56 * 1024 * 1024


def _dot(a, b):
    return jnp.dot(a, b, preferred_element_type=F32)


def _dot_nt(a, b):
    return lax.dot_general(a, b, (((1,), (1,)), ((), ())), preferred_element_type=F32)


def _dot_tn(a, b):
    return lax.dot_general(a, b, (((0,), (0,)), ((), ())), preferred_element_type=F32)


def _rms(x, g):
    return x * lax.rsqrt(jnp.mean(x * x, axis=-1, keepdims=True) + EPS) * g


def _sigmoid(x):
    return 1.0 / (1.0 + jnp.exp(-x))


def _const_spec(shape):
    nd = len(shape)
    return pl.BlockSpec(shape, lambda *_: (0,) * nd)


def _proj_common(x_ref, gmix_ref, w1_ref, wba_ref, wcq_ref, wckv_ref, wkpe_ref, wkpesw_ref,
                 qg_ref, kvg_ref, kcos, ksin, qkv_ref, z_ref, ba_ref, lat_ref, kpe_ref):
    hn = _rms(x_ref[...], gmix_ref[...]).astype(BF16)
    qkvz = _dot(hn, w1_ref[...])
    qkv_ref[...] = qkvz[:, :CONV_CH]
    z_ref[...] = qkvz[:, CONV_CH:]
    ba_ref[...] = _dot(hn, wba_ref[...])
    cq = _dot(hn, wcq_ref[...])
    ckv = _dot(hn, wckv_ref[...])
    kpe = _dot(hn, wkpe_ref[...]) * kcos + _dot(hn, wkpesw_ref[...]) * ksin
    kpe_ref[...] = kpe[:, :MLA_D_ROPE]
    lat = _rms(ckv, kvg_ref[...])
    lat_ref[...] = lat
    cqn = _rms(cq, qg_ref[...]).astype(BF16)
    return cqn, lat.astype(BF16), kpe.astype(BF16)


def _proj_prompt_kernel(x_ref, tab_ref, gmix_ref, w1_ref, wba_ref, wcq_ref, wckv_ref, wkpe_ref,
                        wkpesw_ref, qg_ref, kvg_ref, wq_ref, wqsw_ref, wkb_ref, wvb_ref, esel_ref,
                        qkv_ref, z_ref, ba_ref, lat_ref, kpe_ref, qcat_ref, kcat_ref, vt_ref):
    qcos = tab_ref[:, 0 * LANES:1 * LANES]
    qsin = tab_ref[:, 1 * LANES:2 * LANES]
    kcos = tab_ref[:, 2 * LANES:3 * LANES]
    ksin = tab_ref[:, 3 * LANES:4 * LANES]
    cqn, latb, kpeb = _proj_common(x_ref, gmix_ref, w1_ref, wba_ref, wcq_ref, wckv_ref, wkpe_ref,
                                   wkpesw_ref, qg_ref, kvg_ref, kcos, ksin,
                                   qkv_ref, z_ref, ba_ref, lat_ref, kpe_ref)
    qa = _dot(cqn, wq_ref[...])
    qb = _dot(cqn, wqsw_ref[...])
    for h in range(MLA_HEADS):
        sl = slice(h * HEAD_PAD, (h + 1) * HEAD_PAD)
        qcat_ref[:, sl] = (qa[:, sl] * qcos + qb[:, sl] * qsin).astype(BF16)
    kcat_ref[...] = (_dot(latb, wkb_ref[...]) + _dot(kpeb, esel_ref[...])).astype(BF16)
    vt_ref[...] = _dot_nt(wvb_ref[...], latb).astype(BF16)


def _proj_sample_kernel(x_ref, tab_ref, gmix_ref, w1_ref, wba_ref, wcq_ref, wckv_ref, wkpe_ref,
                        wkpesw_ref, qg_ref, kvg_ref, wqn_ref, wqpe_ref, wqpesw_ref, wabs_ref,
                        qkv_ref, z_ref, ba_ref, lat_ref, kpe_ref, qlat_ref, qpe_ref):
    pe_w = MLA_HEADS * MLA_D_ROPE
    qcos = tab_ref[:, 0:pe_w]
    qsin = tab_ref[:, pe_w:2 * pe_w]
    kcos = tab_ref[:, 2 * pe_w:2 * pe_w + LANES]
    ksin = tab_ref[:, 2 * pe_w + LANES:2 * pe_w + 2 * LANES]
    cqn, _, _ = _proj_common(x_ref, gmix_ref, w1_ref, wba_ref, wcq_ref, wckv_ref, wkpe_ref,
                             wkpesw_ref, qg_ref, kvg_ref, kcos, ksin,
                             qkv_ref, z_ref, ba_ref, lat_ref, kpe_ref)
    qn = _dot(cqn, wqn_ref[...]).astype(BF16)
    qlat_ref[...] = (_dot(qn, wabs_ref[...]) * MLA_SCALE).astype(BF16)
    qpe_ref[...] = (_dot(cqn, wqpe_ref[...]) * qcos + _dot(cqn, wqpesw_ref[...]) * qsin).astype(BF16)


def _proj_call(kernel_fn, x2d, tab, consts, out_widths, out_dtypes, tm, name, transposed_outs=()):
    n_rows, d_model = x2d.shape
    assert n_rows % tm == 0
    row_spec = lambda w: pl.BlockSpec((tm, w), lambda i: (i, 0))
    col_spec = lambda w: pl.BlockSpec((w, tm), lambda i: (0, i))
    is_t = [k in transposed_outs for k in range(len(out_widths))]
    assert tab.shape[0] % tm == 0 and n_rows % tab.shape[0] == 0
    tab_blocks = tab.shape[0] // tm
    tab_spec = pl.BlockSpec((tm, tab.shape[1]), lambda i: (i % tab_blocks, 0))
    in_specs = [row_spec(d_model), tab_spec] + [_const_spec(c.shape) for c in consts]
    return pl.pallas_call(
        kernel_fn,
        grid=(n_rows // tm,),
        in_specs=in_specs,
        out_specs=[col_spec(w) if t else row_spec(w) for w, t in zip(out_widths, is_t)],
        out_shape=[jax.ShapeDtypeStruct((w, n_rows) if t else (n_rows, w), dt)
                   for w, dt, t in zip(out_widths, out_dtypes, is_t)],
        compiler_params=pltpu.CompilerParams(dimension_semantics=("arbitrary",),
                                             vmem_limit_bytes=VMEM_LIMIT),
        name=name,
    )(x2d, tab, *consts)


def _gdn_intra_kernel(qkv_ref, prev_ref, ba_ref, cinit_ref, cw_ref, alog_ref, dtb_ref,
                      t_ref, qk_ref, kq_ref, kdt_ref, vb_ref, egl_ref, xbuf, *, chunk, valid_len, n_chunks):
    c = pl.program_id(1)
    C = chunk
    H = GDN_HEADS
    tail0 = SUBLANES - (CONV_W - 1)

    xbuf[0:SUBLANES, :] = jnp.where(c == 0, cinit_ref[0], prev_ref[0])
    xbuf[SUBLANES:SUBLANES + C, :] = qkv_ref[0]
    cw = cw_ref[...]
    y = xbuf[tail0:tail0 + C, :] * cw[0:1, :]
    for j in range(1, CONV_W):
        y = y + xbuf[tail0 + j:tail0 + j + C, :] * cw[j:j + 1, :]
    y = y * _sigmoid(y)

    ba = ba_ref[0]
    beta_all = _sigmoid(ba)
    sp_in = ba + dtb_ref[...]
    softplus = jnp.maximum(sp_in, 0.0) + jnp.log1p(jnp.exp(-jnp.abs(sp_in)))
    g_all = -jnp.exp(alog_ref[...]) * softplus
    if valid_len < n_chunks * C:
        row = c * C + lax.broadcasted_iota(jnp.int32, (C, LANES), 0)
        beta_all = jnp.where(row < valid_len, beta_all, 0.0)
        g_all = jnp.where(row < valid_len, g_all, 0.0)

    ri = lax.broadcasted_iota(jnp.int32, (C, C), 0)
    ci = lax.broadcasted_iota(jnp.int32, (C, C), 1)
    lower = ri >= ci
    strict = ri > ci
    gc_all = lax.dot_general(lower.astype(F32), g_all, (((1,), (0,)), ((), ())),
                             precision=lax.Precision.HIGHEST, preferred_element_type=F32)
    sel = (lax.broadcasted_iota(jnp.int32, (SUBLANES, LANES), 1)
           == lax.broadcasted_iota(jnp.int32, (SUBLANES, LANES), 0) + GDN_HEADS).astype(F32)
    gc_t = lax.dot_general(sel, gc_all, (((1,), (1,)), ((), ())),
                           precision=lax.Precision.HIGHEST, preferred_element_type=F32)

    egl_ref[0, 0] = jnp.broadcast_to(jnp.exp(gc_t[:, C - 1:C]), (SUBLANES, LANES))

    heads = range(H)
    eye_c = (ri == ci).astype(F32)
    rd = lax.broadcasted_iota(jnp.int32, (GDN_DK, GDN_DK), 0)
    cd = lax.broadcasted_iota(jnp.int32, (GDN_DK, GDN_DK), 1)
    eye_dk = (rd == cd).astype(BF16)
    qs, ks, decays, egcs, kbs = [], [], [], [], []
    for h in heads:
        qh = y[:, h * GDN_DK:(h + 1) * GDN_DK]
        kh = y[:, GDN_QK_W + h * GDN_DK:GDN_QK_W + (h + 1) * GDN_DK]
        vh = y[:, 2 * GDN_QK_W + h * GDN_DV:2 * GDN_QK_W + (h + 1) * GDN_DV]
        qh = qh * lax.rsqrt(jnp.sum(qh * qh, axis=-1, keepdims=True) + EPS) * (GDN_DK ** -0.5)
        kh = kh * lax.rsqrt(jnp.sum(kh * kh, axis=-1, keepdims=True) + EPS)
        beta = beta_all[:, h:h + 1]
        gcol = gc_all[:, H + h:H + h + 1]
        grow = gc_t[h:h + 1, :]
        diff = gcol - grow
        decays.append(jnp.where(lower, jnp.exp(jnp.where(lower, diff, 0.0)), 0.0))
        egc = jnp.exp(gcol)
        kb = kh * beta
        vb_ref[0, 0, h] = vh * beta
        kq_ref[0, 0, h] = jnp.concatenate([kb * egc, qh * egc], axis=0).astype(BF16)
        k_dec = (kh * jnp.exp(gcol[C - 1:C, :] - gcol)).astype(BF16)
        kdt_ref[0, 0, h] = _dot_nt(eye_dk, k_dec).astype(BF16)
        qs.append(qh.astype(BF16))
        ks.append(kh.astype(BF16))
        kbs.append(kb.astype(BF16))
    kk = [_dot_nt(kbs[h], ks[h]) for h in heads]
    qk = [_dot_nt(qs[h], ks[h]) for h in heads]
    for h in heads:
        qk_ref[0, 0, h] = jnp.where(lower, qk[h] * decays[h], 0.0).astype(BF16)
    p = [jnp.where(strict, -(kk[h] * decays[h]), 0.0) for h in heads]
    t = [eye_c + p[h] for h in heads]
    n_iter = max(1, int(math.ceil(math.log2(C))))
    for _ in range(1, n_iter):
        pb = [p[h].astype(BF16) for h in heads]
        p = [_dot(pb[h], pb[h]) for h in heads]
        t = [t[h] + _dot(t[h].astype(BF16), p[h].astype(BF16)) for h in heads]
    for h in heads:
        t_ref[0, 0, h] = t[h].astype(BF16)


def _gdn_scan_kernel(t_ref, qk_ref, kq_ref, kdt_ref, vb_ref, egl_ref, z_ref, sinit_ref, ng_ref,
                     o_ref, sfin_ref, s_scr, *, chunk, n_chunks):
    c = pl.program_id(1)
    C = chunk
    heads = range(GDN_HEADS)

    @pl.when(c == 0)
    def _():
        s_scr[...] = sinit_ref[0]

    s = [s_scr[h] for h in heads]
    ws = [_dot(kq_ref[0, 0, h], s[h].astype(BF16)) for h in heads]
    r = [(vb_ref[0, 0, h] - ws[h][:C]).astype(BF16) for h in heads]
    v_new = [_dot(t_ref[0, 0, h], r[h]).astype(BF16) for h in heads]
    o = [ws[h][C:] + _dot(qk_ref[0, 0, h], v_new[h]) for h in heads]
    egl = egl_ref[0, 0]
    for h in heads:
        s_scr[h] = s[h] * egl[h:h + 1, 0:1] + _dot(kdt_ref[0, 0, h], v_new[h])
    z = z_ref[0]
    ng = ng_ref[...]
    for h in heads:
        hs = slice(h * GDN_DV, (h + 1) * GDN_DV)
        zh = z[:, hs]
        o_n = o[h] * lax.rsqrt(jnp.mean(o[h] * o[h], axis=-1, keepdims=True) + EPS) * ng
        o_ref[0, :, hs] = (o_n * (zh * _sigmoid(zh))).astype(o_ref.dtype)

    @pl.when(c == n_chunks - 1)
    def _():
        sfin_ref[0] = s_scr[...]


def _gdn_call(qkv, z, ba, conv_init, s_init, cw, alog, dtb, ng, chunk, valid_len, name):
    B, T, _ = qkv.shape
    assert T % chunk == 0 and chunk % SUBLANES == 0
    n_chunks = T // chunk
    H, C = GDN_HEADS, chunk
    shared_c = conv_init.shape[0] == 1
    shared_s = s_init.shape[0] == 1
    params = pltpu.CompilerParams(dimension_semantics=("arbitrary", "arbitrary"), vmem_limit_bytes=VMEM_LIMIT)
    tok = lambda w: pl.BlockSpec((1, chunk, w), lambda b, c: (b, c, 0))
    per_chunk = lambda *dims: pl.BlockSpec((1, 1) + dims, lambda b, c: (b, c) + (0,) * len(dims))
    prev_rows = pl.BlockSpec((1, SUBLANES, CONV_CH),
                             lambda b, c: (b, jnp.maximum(c * (C // SUBLANES) - 1, 0), 0))
    inter_dims = [(H, C, C), (H, C, C), (H, 2 * C, GDN_DK), (H, GDN_DK, C), (H, C, GDN_DV), (SUBLANES, LANES)]
    inter_dtypes = [BF16, BF16, BF16, BF16, F32, F32]
    inter = pl.pallas_call(
        functools.partial(_gdn_intra_kernel, chunk=chunk, valid_len=valid_len, n_chunks=n_chunks),
        grid=(B, n_chunks),
        in_specs=[
            tok(CONV_CH), prev_rows, tok(LANES),
            pl.BlockSpec((1, SUBLANES, CONV_CH), (lambda b, c: (0, 0, 0)) if shared_c else (lambda b, c: (b, 0, 0))),
            _const_spec(cw.shape), _const_spec(alog.shape), _const_spec(dtb.shape),
        ],
        out_specs=[per_chunk(*d) for d in inter_dims],
        out_shape=[jax.ShapeDtypeStruct((B, n_chunks) + d, dt) for d, dt in zip(inter_dims, inter_dtypes)],
        scratch_shapes=[pltpu.VMEM((SUBLANES + chunk, CONV_CH), F32)],
        compiler_params=params,
        name=name + "_intra",
    )(qkv, qkv, ba, conv_init, cw, alog, dtb)
    return pl.pallas_call(
        functools.partial(_gdn_scan_kernel, chunk=chunk, n_chunks=n_chunks),
        grid=(B, n_chunks),
        in_specs=[per_chunk(*d) for d in inter_dims] + [
            tok(GDN_V_W),
            pl.BlockSpec((1, H, GDN_DK, GDN_DV), (lambda b, c: (0, 0, 0, 0)) if shared_s else (lambda b, c: (b, 0, 0, 0))),
            _const_spec(ng.shape),
        ],
        out_specs=[
            tok(GDN_V_W),
            pl.BlockSpec((1, H, GDN_DK, GDN_DV), lambda b, c: (b, 0, 0, 0)),
        ],
        out_shape=[
            jax.ShapeDtypeStruct((B, T, GDN_V_W), BF16),
            jax.ShapeDtypeStruct((B, H, GDN_DK, GDN_DV), F32),
        ],
        scratch_shapes=[pltpu.VMEM((H, GDN_DK, GDN_DV), F32)],
        compiler_params=params,
        name=name + "_scan",
    )(*inter, z, s_init, ng)


def _attn_kernel(q_ref, k_ref, vt_ref, km_ref, vmt_ref, o_ref, m_scr, l_scr, acc_scr, *, tq, n_meta, hps):
    qi = pl.program_id(2)
    neg = -jnp.inf
    key_m = lax.broadcasted_iota(jnp.int32, (km_ref.shape[0], tq), 0)
    key_d = lax.broadcasted_iota(jnp.int32, (tq, tq), 0)
    qry_d = lax.broadcasted_iota(jnp.int32, (tq, tq), 1)
    heads = range(hps)
    ls = [slice(h * HEAD_PAD, (h + 1) * HEAD_PAD) for h in heads]

    s = [jnp.where(key_m < n_meta, _dot_nt(km_ref[:, ls[h]], q_ref[0, :, ls[h]]), neg) for h in heads]
    m = [jnp.max(s[h], axis=0, keepdims=True) for h in heads]
    p = [jnp.exp(s[h] - m[h]) for h in heads]
    for h in heads:
        m_scr[h] = m[h]
        l_scr[h] = jnp.sum(p[h], axis=0, keepdims=True)
        acc_scr[h] = _dot(vmt_ref[ls[h], :], p[h].astype(BF16))

    def block(start, diagonal):
        s = [_dot_nt(k_ref[0, pl.ds(start, tq), ls[h]], q_ref[0, :, ls[h]]) for h in heads]
        if diagonal:
            s = [jnp.where(key_d <= qry_d, s[h], neg) for h in heads]
        m_old = [m_scr[h] for h in heads]
        m_new = [jnp.maximum(m_old[h], jnp.max(s[h], axis=0, keepdims=True)) for h in heads]
        alpha = [jnp.exp(m_old[h] - m_new[h]) for h in heads]
        p = [jnp.exp(s[h] - m_new[h]) for h in heads]
        pv = [_dot(vt_ref[ls[h], pl.ds(start, tq)], p[h].astype(BF16)) for h in heads]
        for h in heads:
            l_scr[h] = alpha[h] * l_scr[h] + jnp.sum(p[h], axis=0, keepdims=True)
            acc_scr[h] = alpha[h] * acc_scr[h] + pv[h]
            m_scr[h] = m_new[h]

    def body(kb, carry):
        block(pl.multiple_of(kb * tq, tq), False)
        return carry

    lax.fori_loop(0, qi, body, 0)
    block(pl.multiple_of(qi * tq, tq), True)
    for j in range(hps // 2):
        pair = acc_scr[2 * j] / l_scr[2 * j] + acc_scr[2 * j + 1] / l_scr[2 * j + 1]
        o_ref[0, :, j * LANES:(j + 1) * LANES] = pair.T.astype(o_ref.dtype)


def _attn_call(qcat, kcat, vt, k_meta, vt_meta, tq, hps):
    B, T, _ = qcat.shape
    assert MLA_HEADS % hps == 0 and hps % 2 == 0
    n_groups = MLA_HEADS // hps
    gw = hps * HEAD_PAD
    ow = (hps // 2) * LANES
    kern = functools.partial(_attn_kernel, tq=tq, n_meta=N_META, hps=hps)
    return pl.pallas_call(
        kern,
        grid=(B, n_groups, T // tq),
        in_specs=[
            pl.BlockSpec((1, tq, gw), lambda b, g, qi: (b, qi, g)),
            pl.BlockSpec((1, T, gw), lambda b, g, qi: (b, 0, g)),
            pl.BlockSpec((gw, T), lambda b, g, qi: (g, b)),
            pl.BlockSpec((k_meta.shape[0], gw), lambda b, g, qi: (0, g)),
            pl.BlockSpec((gw, vt_meta.shape[1]), lambda b, g, qi: (g, 0)),
        ],
        out_specs=pl.BlockSpec((1, tq, ow), lambda b, g, qi: (b, qi, g)),
        out_shape=jax.ShapeDtypeStruct((B, T, n_groups * ow), BF16),
        scratch_shapes=[pltpu.VMEM((hps, 1, tq), F32), pltpu.VMEM((hps, 1, tq), F32),
                        pltpu.VMEM((hps, HEAD_PAD, tq), F32)],
        compiler_params=pltpu.CompilerParams(dimension_semantics=("arbitrary", "arbitrary", "arbitrary"),
                                             vmem_limit_bytes=VMEM_LIMIT),
        name="attn",
    )(qcat, kcat, vt, k_meta, vt_meta)


def _dattn_kernel(pt_ref, *refs, pages_per_step, t_new, n_steps):
    del pt_ref
    P = pages_per_step
    qlat_ref, qpe_ref, latn_ref, kpen_ref = refs[:4]
    lat_pages = refs[4:4 + P]
    kpe_pages = refs[4 + P:4 + 2 * P]
    o_ref = refs[4 + 2 * P]
    m_scr, l_scr, acc_scr = refs[4 + 2 * P + 1:]
    j = pl.program_id(1)
    neg = -jnp.inf

    @pl.when(j == 0)
    def _():
        m_scr[...] = jnp.full(m_scr.shape, neg, F32)
        l_scr[...] = jnp.zeros(l_scr.shape, F32)
        acc_scr[...] = jnp.zeros(acc_scr.shape, F32)

    ql = qlat_ref[0]
    qp = qpe_ref[0]

    def update(s, vals):
        m = m_scr[...]
        m_new = jnp.maximum(m, jnp.max(s, axis=-1, keepdims=True))
        alpha = jnp.exp(m - m_new)
        p = jnp.exp(s - m_new)
        l_scr[...] = alpha * l_scr[...] + jnp.sum(p, axis=-1, keepdims=True)
        acc_scr[...] = alpha * acc_scr[...] + _dot(p.astype(BF16), vals)
        m_scr[...] = m_new

    cb = jnp.concatenate([r[0, 0] for r in lat_pages], axis=0).astype(BF16)
    kbt = jnp.concatenate([r[0, 0] for r in kpe_pages], axis=1).astype(BF16)
    update(_dot_nt(ql, cb) + _dot(qp, kbt), cb)

    @pl.when(j == n_steps - 1)
    def _():
        R = ql.shape[0]
        pad = 2 * SUBLANES - latn_ref.shape[1]
        cn = jnp.concatenate([latn_ref[0], jnp.zeros((pad, latn_ref.shape[2]), F32)], axis=0).astype(BF16)
        kn = jnp.concatenate([kpen_ref[0], jnp.zeros((pad, kpen_ref.shape[2]), F32)], axis=0).astype(BF16)
        s = _dot_nt(ql, cn) + _dot_nt(qp, kn)
        tok = lax.shift_right_logical(lax.broadcasted_iota(jnp.int32, s.shape, 0), int(math.log2(MLA_HEADS)))
        key = lax.broadcasted_iota(jnp.int32, s.shape, 1)
        s = jnp.where((key <= tok) & (key < t_new), s, neg)
        update(s, cn)
        o_ref[0] = (acc_scr[...] / l_scr[...]).astype(o_ref.dtype)


def _dattn_call(page_table, qlat, qpe, lat_new, kpe_new, cache_latent, cache_krope_t, t_new, pages_per_step):
    Bs, R, kv_lora = qlat.shape
    n_pages = page_table.shape[1]
    P = pages_per_step
    assert n_pages % P == 0
    n_steps = n_pages // P
    page = cache_latent.shape[2]
    kern = functools.partial(_dattn_kernel, pages_per_step=P, t_new=t_new, n_steps=n_steps)
    per_b = lambda shape: pl.BlockSpec((1,) + shape, lambda b, j, pt: (b, 0, 0))

    def page_spec(rows, width, k):
        return pl.BlockSpec((1, 1, rows, width), lambda b, j, pt: (0, pt[b, j * P + k], 0, 0))

    grid_spec = pltpu.PrefetchScalarGridSpec(
        num_scalar_prefetch=1,
        grid=(Bs, n_steps),
        in_specs=[per_b((R, kv_lora)), per_b((R, MLA_D_ROPE)),
                  per_b(lat_new.shape[1:]), per_b(kpe_new.shape[1:])]
                 + [page_spec(page, kv_lora, k) for k in range(P)]
                 + [page_spec(MLA_D_ROPE, page, k) for k in range(P)],
        out_specs=per_b((R, kv_lora)),
        scratch_shapes=[pltpu.VMEM((R, 1), F32), pltpu.VMEM((R, 1), F32), pltpu.VMEM((R, kv_lora), F32)],
    )
    return pl.pallas_call(
        kern,
        grid_spec=grid_spec,
        out_shape=jax.ShapeDtypeStruct((Bs, R, kv_lora), BF16),
        compiler_params=pltpu.CompilerParams(dimension_semantics=("arbitrary", "arbitrary"),
                                             vmem_limit_bytes=VMEM_LIMIT),
        name="dattn",
    )(page_table, qlat, qpe, lat_new, kpe_new, *([cache_latent] * P), *([cache_krope_t] * P))


def _post_kernel(*refs, absorbed_values):
    if absorbed_values:
        h_ref, og_ref, om_ref, wvbd_ref, wo1_ref, wo2_ref, g2_ref, wup_ref, wdn_ref, gf_ref, y_ref = refs
        om = _dot(om_ref[...], wvbd_ref[...]).astype(BF16)
    else:
        h_ref, og_ref, om_ref, wo1_ref, wo2_ref, g2_ref, wup_ref, wdn_ref, gf_ref, y_ref = refs
        om = om_ref[...]
    h2 = h_ref[...] + _dot(og_ref[...], wo1_ref[...]) + _dot(om, wo2_ref[...])
    hn = _rms(h2, g2_ref[...]).astype(BF16)
    u = jnp.maximum(_dot(hn, wup_ref[...]), 0.0)
    h3 = h2 + _dot((u * u).astype(BF16), wdn_ref[...])
    y_ref[...] = _rms(h3, gf_ref[...])


def _post_call(h2d, og, om, consts, tm, absorbed_values, name):
    n_rows, d_model = h2d.shape
    assert n_rows % tm == 0
    row_spec = lambda w: pl.BlockSpec((tm, w), lambda i: (i, 0))
    kern = functools.partial(_post_kernel, absorbed_values=absorbed_values)
    return pl.pallas_call(
        kern,
        grid=(n_rows // tm,),
        in_specs=[row_spec(d_model), row_spec(og.shape[1]), row_spec(om.shape[1])]
                 + [_const_spec(c.shape) for c in consts],
        out_specs=row_spec(d_model),
        out_shape=jax.ShapeDtypeStruct((n_rows, d_model), F32),
        compiler_params=pltpu.CompilerParams(dimension_semantics=("arbitrary",),
                                             vmem_limit_bytes=VMEM_LIMIT),
        name=name,
    )(h2d, og, om, *consts)


def _rope_parts(pos):
    half = MLA_D_ROPE // 2
    inv = ROPE_THETA ** (-jnp.arange(half, dtype=F32) / half)
    ang = pos.astype(F32)[:, None] * inv[None, :]
    cos = jnp.concatenate([jnp.cos(ang), jnp.cos(ang)], -1)
    sin_signed = jnp.concatenate([-jnp.sin(ang), jnp.sin(ang)], -1)
    return cos, sin_signed


def _pad_lanes(x, width):
    return jnp.pad(x, ((0, 0), (0, width - x.shape[1])))


def _key_tables(cos, sin_signed):
    return _pad_lanes(cos, LANES), _pad_lanes(sin_signed, LANES)


def _prompt_tables(pos):
    cos, sin_signed = _rope_parts(pos)
    n = pos.shape[0]
    qcos = jnp.concatenate([jnp.full((n, MLA_D_NOPE), MLA_SCALE, F32), cos * MLA_SCALE], -1)
    qsin = jnp.concatenate([jnp.zeros((n, MLA_D_NOPE), F32), sin_signed * MLA_SCALE], -1)
    kcos, ksin = _key_tables(cos, sin_signed)
    return jnp.concatenate([_pad_lanes(qcos, LANES), _pad_lanes(qsin, LANES), kcos, ksin], -1)


def _sample_tables(pos):
    cos, sin_signed = _rope_parts(pos)
    kcos, ksin = _key_tables(cos, sin_signed)
    return jnp.concatenate([jnp.tile(cos * MLA_SCALE, (1, MLA_HEADS)),
                            jnp.tile(sin_signed * MLA_SCALE, (1, MLA_HEADS)), kcos, ksin], -1)


def _swap_halves(w):
    half = MLA_D_ROPE // 2
    return jnp.concatenate([w[..., half:], w[..., :half]], -1)


def kernel(x_prompt, x_sample, cache_latent, cache_krope, state_conv, state_ssm, page_table,
           meta_tokens, norm_mix_g, w_in, conv_w, a_log, dt_bias, gdn_norm_g, q_norm_g, w_q_b,
           kv_norm_g, w_kv_b, w_out, norm_mlp_g, w_up, w_down, final_norm_g):
    assert w_in.shape[0] == 1, "single-layer problem"
    B, T, D = x_prompt.shape
    Bs, Ts, _ = x_sample.shape
    assert CONV_W - 1 <= Ts <= SAMPLE_T_PAD
    past_len = page_table.shape[1] * cache_latent.shape[2]
    q_lora = q_norm_g.shape[1]
    kv_lora = kv_norm_g.shape[1]
    H = MLA_HEADS

    wi = w_in[0]
    c0 = CONV_CH + GDN_V_W
    w1 = wi[:, :c0].astype(BF16)
    wba = _pad_lanes(wi[:, c0:c0 + 2 * GDN_HEADS], LANES).astype(BF16)
    c1 = c0 + 2 * GDN_HEADS
    wcq = wi[:, c1:c1 + q_lora].astype(BF16)
    wckv = wi[:, c1 + q_lora:c1 + q_lora + kv_lora].astype(BF16)
    wkpe_raw = wi[:, c1 + q_lora + kv_lora:]
    wkpe = _pad_lanes(wkpe_raw, LANES).astype(BF16)
    wkpesw = _pad_lanes(_swap_halves(wkpe_raw), LANES).astype(BF16)
    gmix = norm_mix_g[0][None, :]
    qg = q_norm_g[0][None, :]
    kvg = kv_norm_g[0][None, :]
    wq = w_q_b[0]
    wq_nope, wq_pe = wq[..., :MLA_D_NOPE], wq[..., MLA_D_NOPE:]
    wq_pesw = _swap_halves(wq_pe)
    zq = lambda n: jnp.zeros((q_lora, H, n), F32)
    wqcat = jnp.concatenate([wq_nope, wq_pe, zq(HEAD_PAD - MLA_D_NOPE - MLA_D_ROPE)], -1)
    wqsw = jnp.concatenate([zq(MLA_D_NOPE), wq_pesw, zq(HEAD_PAD - MLA_D_NOPE - MLA_D_ROPE)], -1)
    wqcat = wqcat.reshape(q_lora, H * HEAD_PAD).astype(BF16)
    wqsw = wqsw.reshape(q_lora, H * HEAD_PAD).astype(BF16)
    wkv = w_kv_b[0]
    wkb, wvb = wkv[..., :MLA_D_NOPE], wkv[..., MLA_D_NOPE:]
    zk = lambda n: jnp.zeros((kv_lora, H, n), F32)
    wkb_sp = jnp.concatenate([wkb, zk(HEAD_PAD - MLA_D_NOPE)], -1).reshape(kv_lora, H * HEAD_PAD).astype(BF16)
    odd = (jnp.arange(H) % 2 == 1)[None, :, None]
    wvb_sp = jnp.where(odd, jnp.concatenate([zk(HEAD_PAD - MLA_D_V), wvb], -1),
                       jnp.concatenate([wvb, zk(HEAD_PAD - MLA_D_V)], -1))
    wvb_sp_t = wvb_sp.reshape(kv_lora, H * HEAD_PAD).T.astype(BF16)
    jj =jnp.arange(LANES)[:, None]
    ll = jnp.arange(H * HEAD_PAD)[None, :]
    esel = ((ll % HEAD_PAD == jj + MLA_D_NOPE) & (jj < MLA_D_ROPE)).astype(BF16)
    eye_h = jnp.eye(H, dtype=bool)
    wabs = jnp.where(eye_h[:, None, :, None], wkb.transpose(1, 2, 0)[:, :, None, :], 0.0)
    wabs = wabs.reshape(H * MLA_D_NOPE, H * kv_lora).astype(BF16)
    wvbd = jnp.where(eye_h[:, None, :, None], wvb.transpose(1, 0, 2)[:, :, None, :], 0.0)
    wvbd = wvbd.reshape(H * kv_lora, H * MLA_D_V).astype(BF16)
    wo1 = w_out[0][:GDN_V_W].astype(BF16)
    wo2 = w_out[0][GDN_V_W:].astype(BF16)
    g2 = norm_mlp_g[0][None, :]
    wup = w_up[0].astype(BF16)
    wdn = w_down[0].astype(BF16)
    gf = final_norm_g[None, :]
    cw = jnp.pad(conv_w[0], ((0, SUBLANES - CONV_W), (0, 0)))
    lane_row = lambda v: jnp.pad(v[None, :], ((0, 0), (GDN_HEADS, LANES - 2 * GDN_HEADS)))
    alog = lane_row(a_log[0])
    dtb = lane_row(dt_bias[0])
    ng = gdn_norm_g[0][None, :]

    common = (gmix, w1, wba, wcq, wckv, wkpe, wkpesw, qg, kvg)
    prompt_consts = common + (wqcat, wqsw, wkb_sp, wvb_sp_t, esel)
    sample_consts = common + (wq_nope.reshape(q_lora, -1).astype(BF16), wq_pe.reshape(q_lora, -1).astype(BF16),
                              wq_pesw.reshape(q_lora, -1).astype(BF16), wabs)
    pw = [CONV_CH, GDN_V_W, LANES, kv_lora, MLA_D_ROPE, H * HEAD_PAD, H * HEAD_PAD, H * HEAD_PAD]
    pd = [F32, F32, F32, F32, F32, BF16, BF16, BF16]

    tab_m = _prompt_tables(jnp.arange(N_META))
    qkv_m, z_m, ba_m, lat_m, kpe_m, _, kcat_m, vt_m = _proj_call(
        _proj_prompt_kernel, meta_tokens, tab_m, prompt_consts, pw, pd, N_META, "proj_meta", transposed_outs=(7,))
    zero_conv = jnp.zeros((1, SUBLANES, CONV_CH), F32)
    zero_s = jnp.zeros((1, GDN_HEADS, GDN_DK, GDN_DV), F32)
    _, s1 = _gdn_call(qkv_m[None], z_m[None], ba_m[None], zero_conv, zero_s, cw, alog, dtb, ng,
                      N_META, N_META, "gdn_meta")
    conv1 = jnp.pad(qkv_m[N_META - (CONV_W - 1):], ((SUBLANES - (CONV_W - 1), 0), (0, 0)))[None]
    k_meta = jnp.pad(kcat_m, ((0, LANES - N_META), (0, 0)))
    vt_meta = jnp.pad(vt_m, ((0, 0), (0, LANES - N_META)))

    tab_p = _prompt_tables(N_META + jnp.arange(T))
    xp2d = x_prompt.reshape(B * T, D)
    qkv_p, z_p, ba_p, lat_p, kpe_p, qcat_p, kcat_p, vt_p = _proj_call(
        _proj_prompt_kernel, xp2d, tab_p, prompt_consts, pw, pd, 256, "proj_prompt", transposed_outs=(7,))
    r3 = lambda a: a.reshape(B, T, a.shape[-1])
    qkv_p3 = r3(qkv_p)
    og_p, s2 = _gdn_call(qkv_p3, r3(z_p), r3(ba_p), conv1, s1, cw, alog, dtb, ng, GDN_CHUNK, T, "gdn_prompt")
    om_p = _attn_call(r3(qcat_p), r3(kcat_p), vt_p, k_meta, vt_meta, 256, 4)
    post_consts = (wo1, wo2, g2, wup, wdn, gf)
    y_p = _post_call(xp2d, og_p.reshape(B * T, -1), om_p.reshape(B * T, -1), post_consts, 512, False, "post_prompt")

    tp = SAMPLE_T_PAD
    xs = jnp.pad(x_sample, ((0, 0), (0, tp - Ts), (0, 0))).reshape(Bs * tp, D)
    tab_s = jnp.tile(_sample_tables(past_len + jnp.arange(tp)), (Bs, 1))
    sw = [CONV_CH, GDN_V_W, LANES, kv_lora, MLA_D_ROPE, H * kv_lora, H * MLA_D_ROPE]
    sd = [F32, F32, F32, F32, F32, BF16, BF16]
    qkv_s, z_s, ba_s, lat_s, kpe_s, qlat_s, qpe_s = _proj_call(
        _proj_sample_kernel, xs, tab_s, sample_consts, sw, sd, Bs * tp, "proj_sample")
    s3 = lambda a: a.reshape(Bs, tp, a.shape[-1])
    qkv_s3 = s3(qkv_s)
    conv_in_s = jnp.pad(state_conv[0], ((0, 0), (SUBLANES - (CONV_W - 1), 0), (0, 0)))
    og_s, s_new = _gdn_call(qkv_s3, s3(z_s), s3(ba_s), conv_in_s, state_ssm[0], cw, alog, dtb, ng, tp, Ts, "gdn_sample")
    out_lat = _dattn_call(page_table, qlat_s.reshape(Bs, tp * H, kv_lora), qpe_s.reshape(Bs, tp * H, MLA_D_ROPE),
                          s3(lat_s), s3(kpe_s), cache_latent, jnp.swapaxes(cache_krope, 2, 3), Ts, 16)
    sample_post_consts = (wvbd,) + post_consts
    y_s = _post_call(xs, og_s.reshape(Bs * tp, -1), out_lat.reshape(Bs * tp, H * kv_lora),
                     sample_post_consts, Bs * tp, True, "post_sample")

    bc = lambda a: jnp.broadcast_to(a[None], (B,) + a.shape)
    lat_po = jnp.concatenate([bc(lat_m), r3(lat_p)], axis=1)[None]
    kpe_po = jnp.concatenate([bc(kpe_m), r3(kpe_p)], axis=1)[None]
    conv_po = qkv_p3[:, T - (CONV_W - 1):][None]
    conv_so = qkv_s3[:, Ts - (CONV_W - 1):Ts][None]
    return (y_p.reshape(B, T, D), y_s.reshape(Bs, tp, D)[:, :Ts],
            lat_po, kpe_po, conv_po, s2[None],
            s3(lat_s)[:, :Ts][None], s3(kpe_s)[:, :Ts][None], conv_so, s_new[None])
```

```python
import functools
import math

import jax
import jax.numpy as jnp
from jax import lax
from jax.experimental import pallas as pl
from jax.experimental.pallas import tpu as pltpu

F32 = jnp.float32
BF16 = jnp.bfloat16

N_META = 16
EPS = 1e-6
GDN_HEADS = 8
GDN_DK = 64
GDN_DV = 64
CONV_W = 4
GDN_CHUNK = 64
MLA_HEADS = 8
MLA_D_NOPE = 64
MLA_D_ROPE = 32
MLA_D_V = 64
ROPE_THETA = 10000.0
MLA_SCALE = (MLA_D_NOPE + MLA_D_ROPE) ** -0.5
GDN_QK_W = GDN_HEADS * GDN_DK
GDN_V_W = GDN_HEADS * GDN_DV
CONV_CH = 2 * GDN_QK_W + GDN_V_W

LANES = 128
SUBLANES = 8
HEAD_PAD = 128
SAMPLE_T_PAD = 8
VMEM_LIMIT = 56 * 1024 * 1024


def _dot(a, b):
    return jnp.dot(a, b, preferred_element_type=F32)


def _dot_nt(a, b):
    return lax.dot_general(a, b, (((1,), (1,)), ((), ())), preferred_element_type=F32)


def _rms(x, g):
    return x * lax.rsqrt(jnp.mean(x * x, axis=-1, keepdims=True) + EPS) * g


def _sigmoid(x):
    return 1.0 / (1.0 + jnp.exp(-x))


def _const_spec(shape):
    nd = len(shape)
    return pl.BlockSpec(shape, lambda *_: (0,) * nd)


def _proj_common(x_ref, gmix_ref, w1_ref, wba_ref, wcq_ref, wckv_ref, wkpe_ref, wkpesw_ref,
                 qg_ref, kvg_ref, kcos, ksin, qkv_ref, z_ref, ba_ref, lat_ref, kpe_ref):
    hn = _rms(x_ref[...], gmix_ref[...]).astype(BF16)
    qkvz = _dot(hn, w1_ref[...])
    qkv_ref[...] = qkvz[:, :CONV_CH]
    z_ref[...] = qkvz[:, CONV_CH:]
    ba_ref[...] = _dot(hn, wba_ref[...])
    cq = _dot(hn, wcq_ref[...])
    ckv = _dot(hn, wckv_ref[...])
    kpe = _dot(hn, wkpe_ref[...]) * kcos + _dot(hn, wkpesw_ref[...]) * ksin
    kpe_ref[...] = kpe[:, :MLA_D_ROPE]
    lat = _rms(ckv, kvg_ref[...])
    lat_ref[...] = lat
    cqn = _rms(cq, qg_ref[...]).astype(BF16)
    return cqn, lat.astype(BF16), kpe.astype(BF16)


def _proj_prompt_kernel(x_ref, tab_ref, gmix_ref, w1_ref, wba_ref, wcq_ref, wckv_ref, wkpe_ref,
                        wkpesw_ref, qg_ref, kvg_ref, wq_ref, wqsw_ref, wkb_ref, wvb_ref, esel_ref,
                        qkv_ref, z_ref, ba_ref, lat_ref, kpe_ref, qcat_ref, kcat_ref, vt_ref):
    qcos = tab_ref[:, 0 * LANES:1 * LANES]
    qsin = tab_ref[:, 1 * LANES:2 * LANES]
    kcos = tab_ref[:, 2 * LANES:3 * LANES]
    ksin = tab_ref[:, 3 * LANES:4 * LANES]
    cqn, latb, kpeb = _proj_common(x_ref, gmix_ref, w1_ref, wba_ref, wcq_ref, wckv_ref, wkpe_ref,
                                   wkpesw_ref, qg_ref, kvg_ref, kcos, ksin,
                                   qkv_ref, z_ref, ba_ref, lat_ref, kpe_ref)
    qa = _dot(cqn, wq_ref[...])
    qb = _dot(cqn, wqsw_ref[...])
    for h in range(MLA_HEADS):
        sl = slice(h * HEAD_PAD, (h + 1) * HEAD_PAD)
        qcat_ref[:, sl] = (qa[:, sl] * qcos + qb[:, sl] * qsin).astype(BF16)
    kcat_ref[...] = (_dot(latb, wkb_ref[...]) + _dot(kpeb, esel_ref[...])).astype(BF16)
    vt_ref[...] = _dot_nt(wvb_ref[...], latb).astype(BF16)


def _proj_sample_kernel(x_ref, tab_ref, gmix_ref, w1_ref, wba_ref, wcq_ref, wckv_ref, wkpe_ref,
                        wkpesw_ref, qg_ref, kvg_ref, wqn_ref, wqpe_ref, wqpesw_ref, wabs_ref,
                        qkv_ref, z_ref, ba_ref, lat_ref, kpe_ref, qlat_ref, qpe_ref):
    pe_w = MLA_HEADS * MLA_D_ROPE
    qcos = tab_ref[:, 0:pe_w]
    qsin = tab_ref[:, pe_w:2 * pe_w]
    kcos = tab_ref[:, 2 * pe_w:2 * pe_w + LANES]
    ksin = tab_ref[:, 2 * pe_w + LANES:2 * pe_w + 2 * LANES]
    cqn, _, _ = _proj_common(x_ref, gmix_ref, w1_ref, wba_ref, wcq_ref, wckv_ref, wkpe_ref,
                             wkpesw_ref, qg_ref, kvg_ref, kcos, ksin,
                             qkv_ref, z_ref, ba_ref, lat_ref, kpe_ref)
    qn = _dot(cqn, wqn_ref[...]).astype(BF16)
    qlat_ref[...] = (_dot(qn, wabs_ref[...]) * MLA_SCALE).astype(BF16)
    qpe_ref[...] = (_dot(cqn, wqpe_ref[...]) * qcos + _dot(cqn, wqpesw_ref[...]) * qsin).astype(BF16)


def _proj_call(kernel_fn, x2d, tab, consts, out_widths, out_dtypes, tm, name, transposed_outs=()):
    n_rows, d_model = x2d.shape
    assert n_rows % tm == 0
    row_spec = lambda w: pl.BlockSpec((tm, w), lambda i: (i, 0))
    col_spec = lambda w: pl.BlockSpec((w, tm), lambda i: (0, i))
    is_t = [k in transposed_outs for k in range(len(out_widths))]
    assert tab.shape[0] % tm == 0 and n_rows % tab.shape[0] == 0
    tab_blocks = tab.shape[0] // tm
    tab_spec = pl.BlockSpec((tm, tab.shape[1]), lambda i: (i % tab_blocks, 0))
    in_specs = [row_spec(d_model), tab_spec] + [_const_spec(c.shape) for c in consts]
    return pl.pallas_call(
        kernel_fn,
        grid=(n_rows // tm,),
        in_specs=in_specs,
        out_specs=[col_spec(w) if t else row_spec(w) for w, t in zip(out_widths, is_t)],
        out_shape=[jax.ShapeDtypeStruct((w, n_rows) if t else (n_rows, w), dt)
                   for w, dt, t in zip(out_widths, out_dtypes, is_t)],
        compiler_params=pltpu.CompilerParams(dimension_semantics=("arbitrary",),
                                             vmem_limit_bytes=VMEM_LIMIT),
        name=name,
    )(x2d, tab, *consts)


def _gdn_intra_kernel(qkv_ref, prev_ref, ba_ref, cinit_ref, cw_ref, alog_ref, dtb_ref,
                      t_ref, qk_ref, kq_ref, kdt_ref, vb_ref, egl_ref, xbuf, *, chunk, valid_len, n_chunks):
    c = pl.program_id(1)
    C = chunk
    H = GDN_HEADS
    tail0 = SUBLANES - (CONV_W - 1)

    xbuf[0:SUBLANES, :] = jnp.where(c == 0, cinit_ref[0], prev_ref[0])
    xbuf[SUBLANES:SUBLANES + C, :] = qkv_ref[0]
    cw = cw_ref[...]
    y = xbuf[tail0:tail0 + C, :] * cw[0:1, :]
    for j in range(1, CONV_W):
        y = y + xbuf[tail0 + j:tail0 + j + C, :] * cw[j:j + 1, :]
    y = y * _sigmoid(y)

    ba = ba_ref[0]
    beta_all = _sigmoid(ba)
    sp_in = ba + dtb_ref[...]
    softplus = jnp.maximum(sp_in, 0.0) + jnp.log1p(jnp.exp(-jnp.abs(sp_in)))
    g_all = -jnp.exp(alog_ref[...]) * softplus
    if valid_len < n_chunks * C:
        row = c * C + lax.broadcasted_iota(jnp.int32, (C, LANES), 0)
        beta_all = jnp.where(row < valid_len, beta_all, 0.0)
        g_all = jnp.where(row < valid_len, g_all, 0.0)

    ri = lax.broadcasted_iota(jnp.int32, (C, C), 0)
    ci = lax.broadcasted_iota(jnp.int32, (C, C), 1)
    lower = ri >= ci
    strict = ri > ci
    gc_all = lax.dot_general(lower.astype(F32), g_all, (((1,), (0,)), ((), ())),
                             precision=lax.Precision.HIGHEST, preferred_element_type=F32)
    sel = (lax.broadcasted_iota(jnp.int32, (SUBLANES, LANES), 1)
           == lax.broadcasted_iota(jnp.int32, (SUBLANES, LANES), 0) + GDN_HEADS).astype(F32)
    gc_t = lax.dot_general(sel, gc_all, (((1,), (1,)), ((), ())),
                           precision=lax.Precision.HIGHEST, preferred_element_type=F32)

    egl_ref[0, 0] = jnp.broadcast_to(jnp.exp(gc_t[:, C - 1:C]), (SUBLANES, LANES))

    heads = range(H)
    eye_c = (ri == ci).astype(F32)
    rd = lax.broadcasted_iota(jnp.int32, (GDN_DK, GDN_DK), 0)
    cd = lax.broadcasted_iota(jnp.int32, (GDN_DK, GDN_DK), 1)
    eye_dk = (rd == cd).astype(BF16)
    qs, ks, decays, egcs, kbs = [], [], [], [], []
    for h in heads:
        qh = y[:, h * GDN_DK:(h + 1) * GDN_DK]
        kh = y[:, GDN_QK_W + h * GDN_DK:GDN_QK_W + (h + 1) * GDN_DK]
        vh = y[:, 2 * GDN_QK_W + h * GDN_DV:2 * GDN_QK_W + (h + 1) * GDN_DV]
        qh = qh * lax.rsqrt(jnp.sum(qh * qh, axis=-1, keepdims=True) + EPS) * (GDN_DK ** -0.5)
        kh = kh * lax.rsqrt(jnp.sum(kh * kh, axis=-1, keepdims=True) + EPS)
        beta = beta_all[:, h:h + 1]
        gcol = gc_all[:, H + h:H + h + 1]
        grow = gc_t[h:h + 1, :]
        diff = gcol - grow
        decays.append(jnp.where(lower, jnp.exp(jnp.where(lower, diff, 0.0)), 0.0))
        egc = jnp.exp(gcol)
        kb = kh * beta
        vb_ref[0, 0, h] = vh * beta
        kq_ref[0, 0, h] = jnp.concatenate([kb * egc, qh * egc], axis=0).astype(BF16)
        k_dec = (kh * jnp.exp(gcol[C - 1:C, :] - gcol)).astype(BF16)
        kdt_ref[0, 0, h] = _dot_nt(eye_dk, k_dec).astype(BF16)
        qs.append(qh.astype(BF16))
        ks.append(kh.astype(BF16))
        kbs.append(kb.astype(BF16))
    kk = [_dot_nt(kbs[h], ks[h]) for h in heads]
    qk = [_dot_nt(qs[h], ks[h]) for h in heads]
    for h in heads:
        qk_ref[0, 0, h] = jnp.where(lower, qk[h] * decays[h], 0.0).astype(BF16)
    p = [jnp.where(strict, -(kk[h] * decays[h]), 0.0) for h in heads]
    t = [eye_c + p[h] for h in heads]
    n_iter = max(1, int(math.ceil(math.log2(C))))
    for _ in range(1, n_iter):
        pb = [p[h].astype(BF16) for h in heads]
        p = [_dot(pb[h], pb[h]) for h in heads]
        t = [t[h] + _dot(t[h].astype(BF16), p[h].astype(BF16)) for h in heads]
    for h in heads:
        t_ref[0, 0, h] = t[h].astype(BF16)


def _gdn_scan_kernel(t_ref, qk_ref, kq_ref, kdt_ref, vb_ref, egl_ref, z_ref, sinit_ref, ng_ref,
                     o_ref, sfin_ref, s_scr, *, chunk, n_chunks):
    c = pl.program_id(1)
    C = chunk
    heads = range(GDN_HEADS)

    @pl.when(c == 0)
    def _():
        s_scr[...] = sinit_ref[0]

    s = [s_scr[h] for h in heads]
    ws = [_dot(kq_ref[0, 0, h], s[h].astype(BF16)) for h in heads]
    r = [(vb_ref[0, 0, h] - ws[h][:C]).astype(BF16) for h in heads]
    v_new = [_dot(t_ref[0, 0, h], r[h]).astype(BF16) for h in heads]
    o = [ws[h][C:] + _dot(qk_ref[0, 0, h], v_new[h]) for h in heads]
    egl = egl_ref[0, 0]
    for h in heads:
        s_scr[h] = s[h] * egl[h:h + 1, 0:1] + _dot(kdt_ref[0, 0, h], v_new[h])
    z = z_ref[0]
    ng = ng_ref[...]
    for h in heads:
        hs = slice(h * GDN_DV, (h + 1) * GDN_DV)
        zh = z[:, hs]
        o_n = o[h] * lax.rsqrt(jnp.mean(o[h] * o[h], axis=-1, keepdims=True) + EPS) * ng
        o_ref[0, :, hs] = (o_n * (zh * _sigmoid(zh))).astype(o_ref.dtype)

    @pl.when(c == n_chunks - 1)
    def _():
        sfin_ref[0] = s_scr[...]


def _gdn_call(qkv, z, ba, conv_init, s_init, cw, alog, dtb, ng, chunk, valid_len, name):
    B, T, _ = qkv.shape
    assert T % chunk == 0 and chunk % SUBLANES == 0
    n_chunks = T // chunk
    H, C = GDN_HEADS, chunk
    shared_c = conv_init.shape[0] == 1
    shared_s = s_init.shape[0] == 1
    params = pltpu.CompilerParams(dimension_semantics=("arbitrary", "arbitrary"), vmem_limit_bytes=VMEM_LIMIT)
    tok = lambda w: pl.BlockSpec((1, chunk, w), lambda b, c: (b, c, 0))
    per_chunk = lambda *dims: pl.BlockSpec((1, 1) + dims, lambda b, c: (b, c) + (0,) * len(dims))
    prev_rows = pl.BlockSpec((1, SUBLANES, CONV_CH),
                             lambda b, c: (b, jnp.maximum(c * (C // SUBLANES) - 1, 0), 0))
    inter_dims = [(H, C, C), (H, C, C), (H, 2 * C, GDN_DK), (H, GDN_DK, C), (H, C, GDN_DV), (SUBLANES, LANES)]
    inter_dtypes = [BF16, BF16, BF16, BF16, F32, F32]
    inter = pl.pallas_call(
        functools.partial(_gdn_intra_kernel, chunk=chunk, valid_len=valid_len, n_chunks=n_chunks),
        grid=(B, n_chunks),
        in_specs=[
            tok(CONV_CH), prev_rows, tok(LANES),
            pl.BlockSpec((1, SUBLANES, CONV_CH), (lambda b, c: (0, 0, 0)) if shared_c else (lambda b, c: (b, 0, 0))),
            _const_spec(cw.shape), _const_spec(alog.shape), _const_spec(dtb.shape),
        ],
        out_specs=[per_chunk(*d) for d in inter_dims],
        out_shape=[jax.ShapeDtypeStruct((B, n_chunks) + d, dt) for d, dt in zip(inter_dims, inter_dtypes)],
        scratch_shapes=[pltpu.VMEM((SUBLANES + chunk, CONV_CH), F32)],
        compiler_params=params,
        name=name + "_intra",
    )(qkv, qkv, ba, conv_init, cw, alog, dtb)
    return pl.pallas_call(
        functools.partial(_gdn_scan_kernel, chunk=chunk, n_chunks=n_chunks),
        grid=(B, n_chunks),
        in_specs=[per_chunk(*d) for d in inter_dims] + [
            tok(GDN_V_W),
            pl.BlockSpec((1, H, GDN_DK, GDN_DV), (lambda b, c: (0, 0, 0, 0)) if shared_s else (lambda b, c: (b, 0, 0, 0))),
            _const_spec(ng.shape),
        ],
        out_specs=[
            tok(GDN_V_W),
            pl.BlockSpec((1, H, GDN_DK, GDN_DV), lambda b, c: (b, 0, 0, 0)),
        ],
        out_shape=[
            jax.ShapeDtypeStruct((B, T, GDN_V_W), BF16),
            jax.ShapeDtypeStruct((B, H, GDN_DK, GDN_DV), F32),
        ],
        scratch_shapes=[pltpu.VMEM((H, GDN_DK, GDN_DV), F32)],
        compiler_params=params,
        name=name + "_scan",
    )(*inter, z, s_init, ng)


def _attn_kernel(q_ref, k_ref, vt_ref, km_ref, vmt_ref, o_ref, m_scr, l_scr, acc_scr, *, tq, n_meta, hps):
    qi = pl.program_id(2)
    neg = -jnp.inf
    key_m = lax.broadcasted_iota(jnp.int32, (km_ref.shape[0], tq), 0)
    key_d = lax.broadcasted_iota(jnp.int32, (tq, tq), 0)
    qry_d = lax.broadcasted_iota(jnp.int32, (tq, tq), 1)
    heads = range(hps)
    ls = [slice(h * HEAD_PAD, (h + 1) * HEAD_PAD) for h in heads]

    s = [jnp.where(key_m < n_meta, _dot_nt(km_ref[:, ls[h]], q_ref[0, :, ls[h]]), neg) for h in heads]
    m = [jnp.max(s[h], axis=0, keepdims=True) for h in heads]
    p = [jnp.exp(s[h] - m[h]) for h in heads]
    for h in heads:
        m_scr[h] = m[h]
        l_scr[h] = jnp.sum(p[h], axis=0, keepdims=True)
        acc_scr[h] = _dot(vmt_ref[ls[h], :], p[h].astype(BF16))

    def block(start, diagonal):
        s = [_dot_nt(k_ref[0, pl.ds(start, tq), ls[h]], q_ref[0, :, ls[h]]) for h in heads]
        if diagonal:
            s = [jnp.where(key_d <= qry_d, s[h], neg) for h in heads]
        m_old = [m_scr[h] for h in heads]
        m_new = [jnp.maximum(m_old[h], jnp.max(s[h], axis=0, keepdims=True)) for h in heads]
        alpha = [jnp.exp(m_old[h] - m_new[h]) for h in heads]
        p = [jnp.exp(s[h] - m_new[h]) for h in heads]
        pv = [_dot(vt_ref[ls[h], pl.ds(start, tq)], p[h].astype(BF16)) for h in heads]
        for h in heads:
            l_scr[h] = alpha[h] * l_scr[h] + jnp.sum(p[h], axis=0, keepdims=True)
            acc_scr[h] = alpha[h] * acc_scr[h] + pv[h]
            m_scr[h] = m_new[h]

    def body(kb, carry):
        block(pl.multiple_of(kb * tq, tq), False)
        return carry

    lax.fori_loop(0, qi, body, 0)
    block(pl.multiple_of(qi * tq, tq), True)
    for j in range(hps // 2):
        pair = acc_scr[2 * j] / l_scr[2 * j] + acc_scr[2 * j + 1] / l_scr[2 * j + 1]
        o_ref[0, :, j * LANES:(j + 1) * LANES] = pair.T.astype(o_ref.dtype)


def _attn_call(qcat, kcat, vt, k_meta, vt_meta, tq, hps):
    B, T, _ = qcat.shape
    assert MLA_HEADS % hps == 0 and hps % 2 == 0
    n_groups = MLA_HEADS // hps
    gw = hps * HEAD_PAD
    ow = (hps // 2) * LANES
    kern = functools.partial(_attn_kernel, tq=tq, n_meta=N_META, hps=hps)
    return pl.pallas_call(
        kern,
        grid=(B, n_groups, T // tq),
        in_specs=[
            pl.BlockSpec((1, tq, gw), lambda b, g, qi: (b, qi, g)),
            pl.BlockSpec((1, T, gw), lambda b, g, qi: (b, 0, g)),
            pl.BlockSpec((gw, T), lambda b, g, qi: (g, b)),
            pl.BlockSpec((k_meta.shape[0], gw), lambda b, g, qi: (0, g)),
            pl.BlockSpec((gw, vt_meta.shape[1]), lambda b, g, qi: (g, 0)),
        ],
        out_specs=pl.BlockSpec((1, tq, ow), lambda b, g, qi: (b, qi, g)),
        out_shape=jax.ShapeDtypeStruct((B, T, n_groups * ow), BF16),
        scratch_shapes=[pltpu.VMEM((hps, 1, tq), F32), pltpu.VMEM((hps, 1, tq), F32),
                        pltpu.VMEM((hps, HEAD_PAD, tq), F32)],
        compiler_params=pltpu.CompilerParams(dimension_semantics=("arbitrary", "arbitrary", "arbitrary"),
                                             vmem_limit_bytes=VMEM_LIMIT),
        name="attn",
    )(qcat, kcat, vt, k_meta, vt_meta)


def _dattn_kernel(pt_ref, qlat_ref, qpe_ref, latn_ref, kpen_ref, lat_hbm, kpe_hbm, o_ref,
                  lat_buf, kpe_buf, sem_lat, sem_kpe, m_scr, l_scr, acc_scr, *, group, n_groups, t_new, page):
    b = pl.program_id(0)
    n_seq = pl.num_programs(0)
    neg = -jnp.inf

    def page_copies(seq, g, k, slot):
        pid = pt_ref[seq, g * group + k]
        rows = pl.ds(pl.multiple_of(k * page, page), page)
        return (pltpu.make_async_copy(lat_hbm.at[0, pid], lat_buf.at[slot, rows, :], sem_lat.at[slot]),
                pltpu.make_async_copy(kpe_hbm.at[0, pid], kpe_buf.at[slot, :, rows], sem_kpe.at[slot]))

    def start_group(seq, g, slot):
        def body(k, carry):
            for cp in page_copies(seq, g, k, slot):
                cp.start()
            return carry
        lax.fori_loop(0, group, body, 0)

    def wait_group(seq, g, slot):
        def body(k, carry):
            for cp in page_copies(seq, g, k, slot):
                cp.wait()
            return carry
        lax.fori_loop(0, group, body, 0)

    @pl.when(b == 0)
    def _():
        start_group(0, 0, 0)

    m_scr[...] = jnp.full(m_scr.shape, neg, F32)
    l_scr[...] = jnp.zeros(l_scr.shape, F32)
    acc_scr[...] = jnp.zeros(acc_scr.shape, F32)

    ql = qlat_ref[0]
    qp = qpe_ref[0]

    def update(s, vals):
        m = m_scr[...]
        m_new = jnp.maximum(m, jnp.max(s, axis=-1, keepdims=True))
        alpha = jnp.exp(m - m_new)
        p = jnp.exp(s - m_new)
        l_scr[...] = alpha * l_scr[...] + jnp.sum(p, axis=-1, keepdims=True)
        acc_scr[...] = alpha * acc_scr[...] + _dot(p.astype(BF16), vals)
        m_scr[...] = m_new

    for g in range(n_groups):
        slot = g % 2
        if g + 1 < n_groups:
            start_group(b, g + 1, 1 - slot)
        else:
            @pl.when(b + 1 < n_seq)
            def _():
                start_group(b + 1, 0, 1 - slot)
        wait_group(b, g, slot)
        cb = lat_buf[slot].astype(BF16)
        kbt = kpe_buf[slot].astype(BF16)
        update(_dot_nt(ql, cb) + _dot(qp, kbt), cb)

    pad = 2 * SUBLANES - latn_ref.shape[1]
    cn = jnp.concatenate([latn_ref[0], jnp.zeros((pad, latn_ref.shape[2]), F32)], axis=0).astype(BF16)
    kn = jnp.concatenate([kpen_ref[0], jnp.zeros((pad, kpen_ref.shape[2]), F32)], axis=0).astype(BF16)
    s = _dot_nt(ql, cn) + _dot_nt(qp, kn)
    tok = lax.shift_right_logical(lax.broadcasted_iota(jnp.int32, s.shape, 0), int(math.log2(MLA_HEADS)))
    key = lax.broadcasted_iota(jnp.int32, s.shape, 1)
    s = jnp.where((key <= tok) & (key < t_new), s, neg)
    update(s, cn)
    o_ref[0] = (acc_scr[...] / l_scr[...]).astype(o_ref.dtype)


def _dattn_call(page_table, qlat, qpe, lat_new, kpe_new, cache_latent, cache_krope_t, t_new, group):
    Bs, R, kv_lora = qlat.shape
    n_pages = page_table.shape[1]
    assert n_pages % (2 * group) == 0, "the two buffer slots alternate, so a sequence needs an even group count"
    n_groups = n_pages // group
    page = cache_latent.shape[2]
    kern = functools.partial(_dattn_kernel, group=group, n_groups=n_groups, t_new=t_new, page=page)
    per_b = lambda shape: pl.BlockSpec((1,) + shape, lambda b, pt: (b, 0, 0))
    grid_spec = pltpu.PrefetchScalarGridSpec(
        num_scalar_prefetch=1,
        grid=(Bs,),
        in_specs=[per_b((R, kv_lora)), per_b((R, MLA_D_ROPE)),
                  per_b(lat_new.shape[1:]), per_b(kpe_new.shape[1:]),
                  pl.BlockSpec(memory_space=pl.ANY), pl.BlockSpec(memory_space=pl.ANY)],
        out_specs=per_b((R, kv_lora)),
        scratch_shapes=[pltpu.VMEM((2, group * page, kv_lora), F32),
                        pltpu.VMEM((2, MLA_D_ROPE, group * page), F32),
                        pltpu.SemaphoreType.DMA((2,)), pltpu.SemaphoreType.DMA((2,)),
                        pltpu.VMEM((R, 1), F32), pltpu.VMEM((R, 1), F32), pltpu.VMEM((R, kv_lora), F32)],
    )
    return pl.pallas_call(
        kern,
        grid_spec=grid_spec,
        out_shape=jax.ShapeDtypeStruct((Bs, R, kv_lora), BF16),
        compiler_params=pltpu.CompilerParams(dimension_semantics=("arbitrary",),
                                             vmem_limit_bytes=VMEM_LIMIT),
        name="dattn",
    )(page_table, qlat, qpe, lat_new, kpe_new, cache_latent, cache_krope_t)


def _post_kernel(*refs, absorbed_values):
    if absorbed_values:
        h_ref, og_ref, om_ref, wvbd_ref, wo1_ref, wo2_ref, g2_ref, wup_ref, wdn_ref, gf_ref, y_ref = refs
        om = _dot(om_ref[...], wvbd_ref[...]).astype(BF16)
    else:
        h_ref, og_ref, om_ref, wo1_ref, wo2_ref, g2_ref, wup_ref, wdn_ref, gf_ref, y_ref = refs
        om = om_ref[...]
    h2 = h_ref[...] + _dot(og_ref[...], wo1_ref[...]) + _dot(om, wo2_ref[...])
    hn = _rms(h2, g2_ref[...]).astype(BF16)
    u = jnp.maximum(_dot(hn, wup_ref[...]), 0.0)
    h3 = h2 + _dot((u * u).astype(BF16), wdn_ref[...])
    y_ref[...] = _rms(h3, gf_ref[...])


def _post_call(h2d, og, om, consts, tm, absorbed_values, name):
    n_rows, d_model = h2d.shape
    assert n_rows % tm == 0
    row_spec = lambda w: pl.BlockSpec((tm, w), lambda i: (i, 0))
    kern = functools.partial(_post_kernel, absorbed_values=absorbed_values)
    return pl.pallas_call(
        kern,
        grid=(n_rows // tm,),
        in_specs=[row_spec(d_model), row_spec(og.shape[1]), row_spec(om.shape[1])]
                 + [_const_spec(c.shape) for c in consts],
        out_specs=row_spec(d_model),
        out_shape=jax.ShapeDtypeStruct((n_rows, d_model), F32),
        compiler_params=pltpu.CompilerParams(dimension_semantics=("arbitrary",),
                                             vmem_limit_bytes=VMEM_LIMIT),
        name=name,
    )(h2d, og, om, *consts)


def _rope_parts(pos):
    half = MLA_D_ROPE // 2
    inv = ROPE_THETA ** (-jnp.arange(half, dtype=F32) / half)
    ang = pos.astype(F32)[:, None] * inv[None, :]
    cos = jnp.concatenate([jnp.cos(ang), jnp.cos(ang)], -1)
    sin_signed = jnp.concatenate([-jnp.sin(ang), jnp.sin(ang)], -1)
    return cos, sin_signed


def _pad_lanes(x, width):
    return jnp.pad(x, ((0, 0), (0, width - x.shape[1])))


def _key_tables(cos, sin_signed):
    return _pad_lanes(cos, LANES), _pad_lanes(sin_signed, LANES)


def _prompt_tables(pos):
    cos, sin_signed = _rope_parts(pos)
    n = pos.shape[0]
    qcos = jnp.concatenate([jnp.full((n, MLA_D_NOPE), MLA_SCALE, F32), cos * MLA_SCALE], -1)
    qsin = jnp.concatenate([jnp.zeros((n, MLA_D_NOPE), F32), sin_signed * MLA_SCALE], -1)
    kcos, ksin = _key_tables(cos, sin_signed)
    return jnp.concatenate([_pad_lanes(qcos, LANES), _pad_lanes(qsin, LANES), kcos, ksin], -1)


def _sample_tables(pos):
    cos, sin_signed = _rope_parts(pos)
    kcos, ksin = _key_tables(cos, sin_signed)
    return jnp.concatenate([jnp.tile(cos * MLA_SCALE, (1, MLA_HEADS)),
                            jnp.tile(sin_signed * MLA_SCALE, (1, MLA_HEADS)), kcos, ksin], -1)


def _swap_halves(w):
    half = MLA_D_ROPE // 2
    return jnp.concatenate([w[..., half:], w[..., :half]], -1)


def kernel(x_prompt, x_sample, cache_latent, cache_krope, state_conv, state_ssm, page_table,
           meta_tokens, norm_mix_g, w_in, conv_w, a_log, dt_bias, gdn_norm_g, q_norm_g, w_q_b,
           kv_norm_g, w_kv_b, w_out, norm_mlp_g, w_up, w_down, final_norm_g):
    assert w_in.shape[0] == 1, "single-layer problem"
    B, T, D = x_prompt.shape
    Bs, Ts, _ = x_sample.shape
    assert CONV_W - 1 <= Ts <= SAMPLE_T_PAD
    past_len = page_table.shape[1] * cache_latent.shape[2]
    q_lora = q_norm_g.shape[1]
    kv_lora = kv_norm_g.shape[1]
    H = MLA_HEADS

    wi = w_in[0]
    c0 = CONV_CH + GDN_V_W
    w1 = wi[:, :c0].astype(BF16)
    wba = _pad_lanes(wi[:, c0:c0 + 2 * GDN_HEADS], LANES).astype(BF16)
    c1 = c0 + 2 * GDN_HEADS
    wcq = wi[:, c1:c1 + q_lora].astype(BF16)
    wckv = wi[:, c1 + q_lora:c1 + q_lora + kv_lora].astype(BF16)
    wkpe_raw = wi[:, c1 + q_lora + kv_lora:]
    wkpe = _pad_lanes(wkpe_raw, LANES).astype(BF16)
    wkpesw = _pad_lanes(_swap_halves(wkpe_raw), LANES).astype(BF16)
    gmix = norm_mix_g[0][None, :]
    qg = q_norm_g[0][None, :]
    kvg = kv_norm_g[0][None, :]
    wq = w_q_b[0]
    wq_nope, wq_pe = wq[..., :MLA_D_NOPE], wq[..., MLA_D_NOPE:]
    wq_pesw = _swap_halves(wq_pe)
    zq = lambda n: jnp.zeros((q_lora, H, n), F32)
    wqcat = jnp.concatenate([wq_nope, wq_pe, zq(HEAD_PAD - MLA_D_NOPE - MLA_D_ROPE)], -1)
    wqsw = jnp.concatenate([zq(MLA_D_NOPE), wq_pesw, zq(HEAD_PAD - MLA_D_NOPE - MLA_D_ROPE)], -1)
    wqcat = wqcat.reshape(q_lora, H * HEAD_PAD).astype(BF16)
    wqsw = wqsw.reshape(q_lora, H * HEAD_PAD).astype(BF16)
    wkv = w_kv_b[0]
    wkb, wvb = wkv[..., :MLA_D_NOPE], wkv[..., MLA_D_NOPE:]
    zk = lambda n: jnp.zeros((kv_lora, H, n), F32)
    wkb_sp = jnp.concatenate([wkb, zk(HEAD_PAD - MLA_D_NOPE)], -1).reshape(kv_lora, H * HEAD_PAD).astype(BF16)
    odd = (jnp.arange(H) % 2 == 1)[None, :, None]
    wvb_sp = jnp.where(odd, jnp.concatenate([zk(HEAD_PAD - MLA_D_V), wvb], -1),
                       jnp.concatenate([wvb, zk(HEAD_PAD - MLA_D_V)], -1))
    wvb_sp_t = wvb_sp.reshape(kv_lora, H * HEAD_PAD).T.astype(BF16)
    jj =jnp.arange(LANES)[:, None]
    ll = jnp.arange(H * HEAD_PAD)[None, :]
    esel = ((ll % HEAD_PAD == jj + MLA_D_NOPE) & (jj < MLA_D_ROPE)).astype(BF16)
    eye_h = jnp.eye(H, dtype=bool)
    wabs = jnp.where(eye_h[:, None, :, None], wkb.transpose(1, 2, 0)[:, :, None, :], 0.0)
    wabs = wabs.reshape(H * MLA_D_NOPE, H * kv_lora).astype(BF16)
    wvbd = jnp.where(eye_h[:, None, :, None], wvb.transpose(1, 0, 2)[:, :, None, :], 0.0)
    wvbd = wvbd.reshape(H * kv_lora, H * MLA_D_V).astype(BF16)
    wo1 = w_out[0][:GDN_V_W].astype(BF16)
    wo2 = w_out[0][GDN_V_W:].astype(BF16)
    g2 = norm_mlp_g[0][None, :]
    wup = w_up[0].astype(BF16)
    wdn = w_down[0].astype(BF16)
    gf = final_norm_g[None, :]
    cw = jnp.pad(conv_w[0], ((0, SUBLANES - CONV_W), (0, 0)))
    lane_row = lambda v: jnp.pad(v[None, :], ((0, 0), (GDN_HEADS, LANES - 2 * GDN_HEADS)))
    alog = lane_row(a_log[0])
    dtb = lane_row(dt_bias[0])
    ng = gdn_norm_g[0][None, :]

    common = (gmix, w1, wba, wcq, wckv, wkpe, wkpesw, qg, kvg)
    prompt_consts = common + (wqcat, wqsw, wkb_sp, wvb_sp_t, esel)
    sample_consts = common + (wq_nope.reshape(q_lora, -1).astype(BF16), wq_pe.reshape(q_lora, -1).astype(BF16),
                              wq_pesw.reshape(q_lora, -1).astype(BF16), wabs)
    pw = [CONV_CH, GDN_V_W, LANES, kv_lora, MLA_D_ROPE, H * HEAD_PAD, H * HEAD_PAD, H * HEAD_PAD]
    pd = [F32, F32, F32, F32, F32, BF16, BF16, BF16]

    tab_m = _prompt_tables(jnp.arange(N_META))
    qkv_m, z_m, ba_m, lat_m, kpe_m, _, kcat_m, vt_m = _proj_call(
        _proj_prompt_kernel, meta_tokens, tab_m, prompt_consts, pw, pd, N_META, "proj_meta", transposed_outs=(7,))
    zero_conv = jnp.zeros((1, SUBLANES, CONV_CH), F32)
    zero_s = jnp.zeros((1, GDN_HEADS, GDN_DK, GDN_DV), F32)
    _, s1 = _gdn_call(qkv_m[None], z_m[None], ba_m[None], zero_conv, zero_s, cw, alog, dtb, ng,
                      N_META, N_META, "gdn_meta")
    conv1 = jnp.pad(qkv_m[N_META - (CONV_W - 1):], ((SUBLANES - (CONV_W - 1), 0), (0, 0)))[None]
    k_meta = jnp.pad(kcat_m, ((0, LANES - N_META), (0, 0)))
    vt_meta = jnp.pad(vt_m, ((0, 0), (0, LANES - N_META)))

    tab_p = _prompt_tables(N_META + jnp.arange(T))
    xp2d = x_prompt.reshape(B * T, D)
    qkv_p, z_p, ba_p, lat_p, kpe_p, qcat_p, kcat_p, vt_p = _proj_call(
        _proj_prompt_kernel, xp2d, tab_p, prompt_consts, pw, pd, 256, "proj_prompt", transposed_outs=(7,))
    r3 = lambda a: a.reshape(B, T, a.shape[-1])
    qkv_p3 = r3(qkv_p)
    og_p, s2 = _gdn_call(qkv_p3, r3(z_p), r3(ba_p), conv1, s1, cw, alog, dtb, ng, GDN_CHUNK, T, "gdn_prompt")
    om_p = _attn_call(r3(qcat_p), r3(kcat_p), vt_p, k_meta, vt_meta, 256, 8)
    post_consts = (wo1, wo2, g2, wup, wdn, gf)
    y_p = _post_call(xp2d, og_p.reshape(B * T, -1), om_p.reshape(B * T, -1), post_consts, 512, False, "post_prompt")

    tp = SAMPLE_T_PAD
    xs = jnp.pad(x_sample, ((0, 0), (0, tp - Ts), (0, 0))).reshape(Bs * tp, D)
    tab_s = jnp.tile(_sample_tables(past_len + jnp.arange(tp)), (Bs, 1))
    sw = [CONV_CH, GDN_V_W, LANES, kv_lora, MLA_D_ROPE, H * kv_lora, H * MLA_D_ROPE]
    sd = [F32, F32, F32, F32, F32, BF16, BF16]
    qkv_s, z_s, ba_s, lat_s, kpe_s, qlat_s, qpe_s = _proj_call(
        _proj_sample_kernel, xs, tab_s, sample_consts, sw, sd, Bs * tp, "proj_sample")
    s3 = lambda a: a.reshape(Bs, tp, a.shape[-1])
    qkv_s3 = s3(qkv_s)
    conv_in_s = jnp.pad(state_conv[0], ((0, 0), (SUBLANES - (CONV_W - 1), 0), (0, 0)))
    og_s, s_new = _gdn_call(qkv_s3, s3(z_s), s3(ba_s), conv_in_s, state_ssm[0], cw, alog, dtb, ng, tp, Ts, "gdn_sample")
    out_lat = _dattn_call(page_table, qlat_s.reshape(Bs, tp * H, kv_lora), qpe_s.reshape(Bs, tp * H, MLA_D_ROPE),
                          s3(lat_s), s3(kpe_s), cache_latent, jnp.swapaxes(cache_krope, 2, 3), Ts, 32)
    sample_post_consts = (wvbd,) + post_consts
    y_s = _post_call(xs, og_s.reshape(Bs * tp, -1), out_lat.reshape(Bs * tp, H * kv_lora),
                     sample_post_consts, Bs * tp, True, "post_sample")

    bc = lambda a: jnp.broadcast_to(a[None], (B,) + a.shape)
    lat_po = jnp.concatenate([bc(lat_m), r3(lat_p)], axis=1)[None]
    kpe_po = jnp.concatenate([bc(kpe_m), r3(kpe_p)], axis=1)[None]
    conv_po = qkv_p3[:, T - (CONV_W - 1):][None]
    conv_so = qkv_s3[:, Ts - (CONV_W - 1):Ts][None]
    return (y_p.reshape(B, T, D), y_s.reshape(Bs, tp, D)[:, :Ts],
            lat_po, kpe_po, conv_po, s2[None],
            s3(lat_s)[:, :Ts][None], s3(kpe_s)[:, :Ts][None], conv_so, s_new[None])
```

```python
import functools
import math

import jax
import jax.numpy as jnp
from jax import lax
from jax.experimental import pallas as pl
from jax.experimental.pallas import tpu as pltpu

F32 = jnp.float32
BF16 = jnp.bfloat16

N_META = 16
EPS = 1e-6
GDN_HEADS = 8
GDN_DK = 64
GDN_DV = 64
CONV_W = 4
GDN_CHUNK = 64
MLA_HEADS = 8
MLA_D_NOPE = 64
MLA_D_ROPE = 32
MLA_D_V = 64
ROPE_THETA = 10000.0
MLA_SCALE = (MLA_D_NOPE + MLA_D_ROPE) ** -0.5
GDN_QK_W = GDN_HEADS * GDN_DK
GDN_V_W = GDN_HEADS * GDN_DV
CONV_CH = 2 * GDN_QK_W + GDN_V_W

LANES = 128
SUBLANES = 8
HEAD_PAD = 128
SAMPLE_T_PAD = 8
VMEM_LIMIT = 56 * 1024 * 1024


def _dot(a, b):
    return jnp.dot(a, b, preferred_element_type=F32)


def _dot_nt(a, b):
    return lax.dot_general(a, b, (((1,), (1,)), ((), ())), preferred_element_type=F32)


def _rms(x, g):
    return x * lax.rsqrt(jnp.mean(x * x, axis=-1, keepdims=True) + EPS) * g


def _sigmoid(x):
    return 1.0 / (1.0 + jnp.exp(-x))


def _const_spec(shape):
    nd = len(shape)
    return pl.BlockSpec(shape, lambda *_: (0,) * nd)


def _proj_common(x_ref, gmix_ref, w1_ref, wba_ref, wcq_ref, wckv_ref, wkpe_ref, wkpesw_ref,
                 qg_ref, kvg_ref, kcos, ksin, qkv_ref, z_ref, ba_ref, lat_ref, kpe_ref):
    hn = _rms(x_ref[...], gmix_ref[...]).astype(BF16)
    qkvz = _dot(hn, w1_ref[...])
    qkv_ref[...] = qkvz[:, :CONV_CH]
    z_ref[...] = qkvz[:, CONV_CH:]
    ba_ref[...] = _dot(hn, wba_ref[...])
    cq = _dot(hn, wcq_ref[...])
    ckv = _dot(hn, wckv_ref[...])
    kpe = _dot(hn, wkpe_ref[...]) * kcos + _dot(hn, wkpesw_ref[...]) * ksin
    kpe_ref[...] = kpe[:, :MLA_D_ROPE]
    lat = _rms(ckv, kvg_ref[...])
    lat_ref[...] = lat
    cqn = _rms(cq, qg_ref[...]).astype(BF16)
    return cqn, lat.astype(BF16), kpe.astype(BF16)


def _proj_prompt_kernel(x_ref, tab_ref, gmix_ref, w1_ref, wba_ref, wcq_ref, wckv_ref, wkpe_ref,
                        wkpesw_ref, qg_ref, kvg_ref, wq_ref, wqsw_ref, wkb_ref, wvb_ref, esel_ref,
                        qkv_ref, z_ref, ba_ref, lat_ref, kpe_ref, qcat_ref, kcat_ref, vt_ref):
    qcos = tab_ref[:, 0 * LANES:1 * LANES]
    qsin = tab_ref[:, 1 * LANES:2 * LANES]
    kcos = tab_ref[:, 2 * LANES:3 * LANES]
    ksin = tab_ref[:, 3 * LANES:4 * LANES]
    cqn, latb, kpeb = _proj_common(x_ref, gmix_ref, w1_ref, wba_ref, wcq_ref, wckv_ref, wkpe_ref,
                                   wkpesw_ref, qg_ref, kvg_ref, kcos, ksin,
                                   qkv_ref, z_ref, ba_ref, lat_ref, kpe_ref)
    qa = _dot(cqn, wq_ref[...])
    qb = _dot(cqn, wqsw_ref[...])
    for h in range(MLA_HEADS):
        sl = slice(h * HEAD_PAD, (h + 1) * HEAD_PAD)
        qcat_ref[:, sl] = (qa[:, sl] * qcos + qb[:, sl] * qsin).astype(BF16)
    kcat_ref[...] = (_dot(latb, wkb_ref[...]) + _dot(kpeb, esel_ref[...])).astype(BF16)
    vt_ref[...] = _dot_nt(wvb_ref[...], latb).astype(BF16)


def _proj_sample_kernel(x_ref, tab_ref, gmix_ref, w1_ref, wba_ref, wcq_ref, wckv_ref, wkpe_ref,
                        wkpesw_ref, qg_ref, kvg_ref, wqn_ref, wqpe_ref, wqpesw_ref, wabs_ref,
                        qkv_ref, z_ref, ba_ref, lat_ref, kpe_ref, qlat_ref, qpe_ref):
    pe_w = MLA_HEADS * MLA_D_ROPE
    qcos = tab_ref[:, 0:pe_w]
    qsin = tab_ref[:, pe_w:2 * pe_w]
    kcos = tab_ref[:, 2 * pe_w:2 * pe_w + LANES]
    ksin = tab_ref[:, 2 * pe_w + LANES:2 * pe_w + 2 * LANES]
    cqn, _, _ = _proj_common(x_ref, gmix_ref, w1_ref, wba_ref, wcq_ref, wckv_ref, wkpe_ref,
                             wkpesw_ref, qg_ref, kvg_ref, kcos, ksin,
                             qkv_ref, z_ref, ba_ref, lat_ref, kpe_ref)
    qn = _dot(cqn, wqn_ref[...]).astype(BF16)
    qlat_ref[...] = (_dot(qn, wabs_ref[...]) * MLA_SCALE).astype(BF16)
    qpe_ref[...] = (_dot(cqn, wqpe_ref[...]) * qcos + _dot(cqn, wqpesw_ref[...]) * qsin).astype(BF16)


def _proj_call(kernel_fn, x2d, tab, consts, out_widths, out_dtypes, tm, name, transposed_outs=()):
    n_rows, d_model = x2d.shape
    assert n_rows % tm == 0
    row_spec = lambda w: pl.BlockSpec((tm, w), lambda i: (i, 0))
    col_spec = lambda w: pl.BlockSpec((w, tm), lambda i: (0, i))
    is_t = [k in transposed_outs for k in range(len(out_widths))]
    assert tab.shape[0] % tm == 0 and n_rows % tab.shape[0] == 0
    tab_blocks = tab.shape[0] // tm
    tab_spec = pl.BlockSpec((tm, tab.shape[1]), lambda i: (i % tab_blocks, 0))
    in_specs = [row_spec(d_model), tab_spec] + [_const_spec(c.shape) for c in consts]
    return pl.pallas_call(
        kernel_fn,
        grid=(n_rows // tm,),
        in_specs=in_specs,
        out_specs=[col_spec(w) if t else row_spec(w) for w, t in zip(out_widths, is_t)],
        out_shape=[jax.ShapeDtypeStruct((w, n_rows) if t else (n_rows, w), dt)
                   for w, dt, t in zip(out_widths, out_dtypes, is_t)],
        compiler_params=pltpu.CompilerParams(dimension_semantics=("arbitrary",),
                                             vmem_limit_bytes=VMEM_LIMIT),
        name=name,
    )(x2d, tab, *consts)


def _block_diag2(x, half):
    lane = lax.broadcasted_iota(jnp.int32, x.shape, 1)
    zero = jnp.zeros_like(x)
    return jnp.concatenate([jnp.where(lane < half, x, zero), jnp.where(lane < half, zero, x)], axis=0)


def _gdn_intra_prep(bb, cinit_bb, qkv_ref, prev_ref, ba_ref, cinit_ref, cw_ref, alog_ref, dtb_ref,
                    qk_mask, kg_ref, qg_ref, kdt_ref, vb_ref, egl_ref, xbuf, *, chunk, valid_len, n_chunks):
    c = pl.program_id(1)
    C = chunk
    H = GDN_HEADS
    tail0 = SUBLANES - (CONV_W - 1)

    xbuf[bb, 0:SUBLANES, :] = jnp.where(c == 0, cinit_ref[cinit_bb], prev_ref[bb])
    xbuf[bb, SUBLANES:SUBLANES + C, :] = qkv_ref[bb]
    cw = cw_ref[...]
    y = xbuf[bb, tail0:tail0 + C, :] * cw[0:1, :]
    for j in range(1, CONV_W):
        y = y + xbuf[bb, tail0 + j:tail0 + j + C, :] * cw[j:j + 1, :]
    y = y * _sigmoid(y)

    ba = ba_ref[bb]
    beta_all = _sigmoid(ba)
    sp_in = ba + dtb_ref[...]
    softplus = jnp.maximum(sp_in, 0.0) + jnp.log1p(jnp.exp(-jnp.abs(sp_in)))
    g_all = -jnp.exp(alog_ref[...]) * softplus
    if valid_len < n_chunks * C:
        row = c * C + lax.broadcasted_iota(jnp.int32, (C, LANES), 0)
        beta_all = jnp.where(row < valid_len, beta_all, 0.0)
        g_all = jnp.where(row < valid_len, g_all, 0.0)

    ri = lax.broadcasted_iota(jnp.int32, (C, C), 0)
    ci = lax.broadcasted_iota(jnp.int32, (C, C), 1)
    lower = ri >= ci
    gc_all = lax.dot_general(lower.astype(F32), g_all, (((1,), (0,)), ((), ())),
                             precision=lax.Precision.HIGHEST, preferred_element_type=F32)
    sel = (lax.broadcasted_iota(jnp.int32, (SUBLANES, LANES), 1)
           == lax.broadcasted_iota(jnp.int32, (SUBLANES, LANES), 0) + GDN_HEADS).astype(F32)
    gc_t = lax.dot_general(sel, gc_all, (((1,), (1,)), ((), ())),
                           precision=lax.Precision.HIGHEST, preferred_element_type=F32)

    pairs = range(H // 2)
    lane = lax.broadcasted_iota(jnp.int32, (C, LANES), 1)
    lo = lane < GDN_DK
    lo_c, lower_p = qk_mask
    ii_r = lax.broadcasted_iota(jnp.int32, (GDN_DK, LANES), 0)
    ii_c = lax.broadcasted_iota(jnp.int32, (GDN_DK, LANES), 1)
    eye2 = ((ii_c == ii_r) | (ii_c == ii_r + GDN_DK)).astype(BF16)

    def col(x, idx):
        return jnp.sum(jnp.where(lane == idx, x, 0.0), axis=-1, keepdims=True)

    def half_sums(x):
        return jnp.where(lo, jnp.sum(jnp.where(lo, x, 0.0), axis=-1, keepdims=True),
                         jnp.sum(jnp.where(lo, 0.0, x), axis=-1, keepdims=True))

    kn_b, kb_b, qn_b, decay = [], [], [], []
    for j in pairs:
        ps = slice(j * LANES, (j + 1) * LANES)
        qp = y[:, j * LANES:(j + 1) * LANES]
        kp = y[:, GDN_QK_W + j * LANES:GDN_QK_W + (j + 1) * LANES]
        vp = y[:, 2 * GDN_QK_W + j * LANES:2 * GDN_QK_W + (j + 1) * LANES]
        qn = qp * lax.rsqrt(half_sums(qp * qp) + EPS) * (GDN_DK ** -0.5)
        kn = kp * lax.rsqrt(half_sums(kp * kp) + EPS)
        beta = jnp.where(lo, col(beta_all, 2 * j), col(beta_all, 2 * j + 1))
        gc0, gc1 = col(gc_all, H + 2 * j), col(gc_all, H + 2 * j + 1)
        gcol = jnp.where(lo, gc0, gc1)
        grow = jnp.concatenate([gc_t[2 * j:2 * j + 1, :], gc_t[2 * j + 1:2 * j + 2, :]], axis=1)
        diff = jnp.where(lo_c, gc0, gc1) - grow
        decay.append(jnp.where(lower_p, jnp.exp(jnp.where(lower_p, diff, 0.0)), 0.0))
        egc = jnp.exp(gcol)
        gl = gcol[C - 1:C, :]
        egl_ref[bb, 0, :, ps] = jnp.exp(gl)
        kb = kn * beta
        vb_ref[bb, 0, :, ps] = vp * beta
        kg_ref[bb, 0, :, ps] = (kb * egc).astype(BF16)
        qg_ref[bb, 0, :, ps] = (qn * egc).astype(BF16)
        k_dec = (kn * jnp.exp(gl - gcol)).astype(BF16)
        kdt_ref[bb, 0, j] = _dot_nt(eye2, _block_diag2(k_dec, GDN_DK)).astype(BF16)
        kn_b.append(_block_diag2(kn.astype(BF16), GDN_DK))
        kb_b.append(kb.astype(BF16))
        qn_b.append(qn.astype(BF16))
    return kn_b, kb_b, qn_b, decay


def _gdn_intra_kernel(qkv_ref, prev_ref, ba_ref, cinit_ref, cw_ref, alog_ref, dtb_ref,
                      t_ref, qk_ref, kg_ref, qg_ref, kdt_ref, vb_ref, egl_ref, xbuf,
                      *, chunk, valid_len, n_chunks, nb, shared_cinit):
    C = chunk
    rp = lax.broadcasted_iota(jnp.int32, (C, 2 * C), 0)
    cp = lax.broadcasted_iota(jnp.int32, (C, 2 * C), 1)
    lo_c = cp < C
    cmod = jnp.where(lo_c, cp, cp - C)
    lower_p, strict_p = rp >= cmod, rp > cmod
    eye_p = (rp == cmod).astype(F32)
    kn_b, kb_b, qn_b, decay = [], [], [], []
    for bb in range(nb):
        parts = _gdn_intra_prep(bb, 0 if shared_cinit else bb, qkv_ref, prev_ref, ba_ref, cinit_ref, cw_ref,
                                alog_ref, dtb_ref, (lo_c, lower_p), kg_ref, qg_ref, kdt_ref, vb_ref, egl_ref, xbuf,
                                chunk=chunk, valid_len=valid_len, n_chunks=n_chunks)
        for dst, src in zip((kn_b, kb_b, qn_b, decay), parts):
            dst.extend(src)
    n_pairs = GDN_HEADS // 2
    probs = range(nb * n_pairs)
    kk = [_dot_nt(kb_b[i], kn_b[i]) for i in probs]
    qk = [_dot_nt(qn_b[i], kn_b[i]) for i in probs]
    for i in probs:
        qk_ref[i // n_pairs, 0, i % n_pairs] = jnp.where(lower_p, qk[i] * decay[i], 0.0).astype(BF16)
    p = [jnp.where(strict_p, -(kk[i] * decay[i]), 0.0) for i in probs]
    t = [eye_p + p[i] for i in probs]
    n_iter = max(1, int(math.ceil(math.log2(C))))
    for _ in range(1, n_iter):
        pb = [p[i].astype(BF16) for i in probs]
        p = [_dot(pb[i], _block_diag2(pb[i], C)) for i in probs]
        t = [t[i] + _dot(t[i].astype(BF16), _block_diag2(p[i].astype(BF16), C)) for i in probs]
    for i in probs:
        t_ref[i // n_pairs, 0, i % n_pairs] = t[i].astype(BF16)


def _gdn_scan_kernel(t_ref, qk_ref, kg_ref, qg_ref, kdt_ref, vb_ref, egl_ref, z_ref, sinit_ref, ng_ref,
                     o_ref, sfin_ref, s_scr, *, chunk, n_chunks, nb, shared_sinit):
    c = pl.program_id(1)
    C = chunk
    n_pairs = GDN_HEADS // 2
    probs = [(bb, j) for bb in range(nb) for j in range(n_pairs)]
    ps = [slice(j * LANES, (j + 1) * LANES) for j in range(n_pairs)]
    lo = lax.broadcasted_iota(jnp.int32, (C, LANES), 1) < GDN_DV

    @pl.when(c == 0)
    def _():
        for bb in range(nb):
            s_scr[bb] = sinit_ref[0 if shared_sinit else bb]

    s = [s_scr[bb, j] for bb, j in probs]
    sbd = [_block_diag2(x.astype(BF16), GDN_DV) for x in s]
    ks = [_dot(kg_ref[bb, 0, :, ps[j]], sbd[i]) for i, (bb, j) in enumerate(probs)]
    qs = [_dot(qg_ref[bb, 0, :, ps[j]], sbd[i]) for i, (bb, j) in enumerate(probs)]
    r = [(vb_ref[bb, 0, :, ps[j]] - ks[i]).astype(BF16) for i, (bb, j) in enumerate(probs)]
    v_new = [_dot(t_ref[bb, 0, j], _block_diag2(r[i], GDN_DV)).astype(BF16) for i, (bb, j) in enumerate(probs)]
    vbd = [_block_diag2(x, GDN_DV) for x in v_new]
    o = [qs[i] + _dot(qk_ref[bb, 0, j], vbd[i]) for i, (bb, j) in enumerate(probs)]
    for i, (bb, j) in enumerate(probs):
        s_scr[bb, j] = s[i] * egl_ref[bb, 0, :, ps[j]] + _dot(kdt_ref[bb, 0, j], vbd[i])
    ng = ng_ref[...]
    for i, (bb, j) in enumerate(probs):
        o2 = o[i] * o[i]
        ms = jnp.where(lo, jnp.sum(jnp.where(lo, o2, 0.0), axis=-1, keepdims=True),
                       jnp.sum(jnp.where(lo, 0.0, o2), axis=-1, keepdims=True)) * (1.0 / GDN_DV)
        zp = z_ref[bb, :, ps[j]]
        o_ref[bb, :, ps[j]] = (o[i] * lax.rsqrt(ms + EPS) * ng * (zp * _sigmoid(zp))).astype(o_ref.dtype)

    @pl.when(c == n_chunks - 1)
    def _():
        sfin_ref[...] = s_scr[...]


def _gdn_call(qkv, z, ba, conv_init, s_init, cw, alog, dtb, ng, chunk, valid_len, name, nb_intra=1, nb_scan=1):
    B, T, _ = qkv.shape
    assert T % chunk == 0 and chunk % SUBLANES == 0 and GDN_DK == GDN_DV and 2 * GDN_DV == LANES
    n_chunks = T // chunk
    H, C = GDN_HEADS, chunk
    HP = H // 2
    shared_c = conv_init.shape[0] == 1
    shared_s = s_init.shape[0] == 1
    to_pairs = lambda s: s.reshape(-1, HP, 2, GDN_DK, GDN_DV).transpose(0, 1, 3, 2, 4).reshape(-1, HP, GDN_DK, LANES)
    from_pairs = lambda s: s.reshape(-1, HP, GDN_DK, 2, GDN_DV).transpose(0, 1, 3, 2, 4).reshape(-1, H, GDN_DK, GDN_DV)
    s_init = to_pairs(s_init)
    ng = jnp.tile(ng, (1, 2))
    params = pltpu.CompilerParams(dimension_semantics=("arbitrary", "arbitrary"), vmem_limit_bytes=VMEM_LIMIT)
    inter_dims = [(HP, C, 2 * C), (HP, C, 2 * C), (C, GDN_QK_W), (C, GDN_QK_W), (HP, GDN_DK, 2 * C),
                  (C, GDN_V_W), (1, GDN_V_W)]
    inter_dtypes = [BF16, BF16, BF16, BF16, BF16, F32, F32]

    def specs(nb):
        tok = lambda w: pl.BlockSpec((nb, chunk, w), lambda b, c: (b, c, 0))
        per_chunk = lambda *dims: pl.BlockSpec((nb, 1) + dims, lambda b, c: (b, c) + (0,) * len(dims))
        per_seq = lambda shared, *dims: pl.BlockSpec(
            ((1 if shared else nb),) + dims, lambda b, c: ((0 if shared else b),) + (0,) * len(dims))
        return tok, per_chunk, per_seq

    nb = nb_intra
    assert B % nb == 0
    tok, per_chunk, per_seq = specs(nb)
    prev_rows = pl.BlockSpec((nb, SUBLANES, CONV_CH),
                             lambda b, c: (b, jnp.maximum(c * (C // SUBLANES) - 1, 0), 0))
    inter = pl.pallas_call(
        functools.partial(_gdn_intra_kernel, chunk=chunk, valid_len=valid_len, n_chunks=n_chunks,
                          nb=nb, shared_cinit=shared_c),
        grid=(B // nb, n_chunks),
        in_specs=[
            tok(CONV_CH), prev_rows, tok(LANES), per_seq(shared_c, SUBLANES, CONV_CH),
            _const_spec(cw.shape), _const_spec(alog.shape), _const_spec(dtb.shape),
        ],
        out_specs=[per_chunk(*d) for d in inter_dims],
        out_shape=[jax.ShapeDtypeStruct((B, n_chunks) + d, dt) for d, dt in zip(inter_dims, inter_dtypes)],
        scratch_shapes=[pltpu.VMEM((nb, SUBLANES + chunk, CONV_CH), F32)],
        compiler_params=params,
        name=name + "_intra",
    )(qkv, qkv, ba, conv_init, cw, alog, dtb)

    nb = nb_scan
    assert B % nb == 0
    tok, per_chunk, per_seq = specs(nb)
    o, s_fin = pl.pallas_call(
        functools.partial(_gdn_scan_kernel, chunk=chunk, n_chunks=n_chunks, nb=nb, shared_sinit=shared_s),
        grid=(B // nb, n_chunks),
        in_specs=[per_chunk(*d) for d in inter_dims] + [
            tok(GDN_V_W), per_seq(shared_s, HP, GDN_DK, LANES), _const_spec(ng.shape),
        ],
        out_specs=[tok(GDN_V_W), per_seq(False, HP, GDN_DK, LANES)],
        out_shape=[
            jax.ShapeDtypeStruct((B, T, GDN_V_W), BF16),
            jax.ShapeDtypeStruct((B, HP, GDN_DK, LANES), F32),
        ],
        scratch_shapes=[pltpu.VMEM((nb, HP, GDN_DK, LANES), F32)],
        compiler_params=params,
        name=name + "_scan",
    )(*inter, z, s_init, ng)
    return o, from_pairs(s_fin)


def _attn_kernel(q_ref, k_ref, vt_ref, km_ref, vmt_ref, o_ref, m_scr, l_scr, acc_scr, *, tq, n_meta, hps):
    qi = pl.program_id(2)
    neg = -jnp.inf
    key_m = lax.broadcasted_iota(jnp.int32, (km_ref.shape[0], tq), 0)
    key_d = lax.broadcasted_iota(jnp.int32, (tq, tq), 0)
    qry_d = lax.broadcasted_iota(jnp.int32, (tq, tq), 1)
    heads = range(hps)
    ls = [slice(h * HEAD_PAD, (h + 1) * HEAD_PAD) for h in heads]

    s = [jnp.where(key_m < n_meta, _dot_nt(km_ref[:, ls[h]], q_ref[0, :, ls[h]]), neg) for h in heads]
    m = [jnp.max(s[h], axis=0, keepdims=True) for h in heads]
    p = [jnp.exp(s[h] - m[h]) for h in heads]
    for h in heads:
        m_scr[h] = m[h]
        l_scr[h] = jnp.sum(p[h], axis=0, keepdims=True)
        acc_scr[h] = _dot(vmt_ref[ls[h], :], p[h].astype(BF16))

    def block(start, diagonal):
        s = [_dot_nt(k_ref[0, pl.ds(start, tq), ls[h]], q_ref[0, :, ls[h]]) for h in heads]
        if diagonal:
            s = [jnp.where(key_d <= qry_d, s[h], neg) for h in heads]
        m_old = [m_scr[h] for h in heads]
        m_new = [jnp.maximum(m_old[h], jnp.max(s[h], axis=0, keepdims=True)) for h in heads]
        alpha = [jnp.exp(m_old[h] - m_new[h]) for h in heads]
        p = [jnp.exp(s[h] - m_new[h]) for h in heads]
        pv = [_dot(vt_ref[ls[h], pl.ds(start, tq)], p[h].astype(BF16)) for h in heads]
        for h in heads:
            l_scr[h] = alpha[h] * l_scr[h] + jnp.sum(p[h], axis=0, keepdims=True)
            acc_scr[h] = alpha[h] * acc_scr[h] + pv[h]
            m_scr[h] = m_new[h]

    def body(kb, carry):
        block(pl.multiple_of(kb * tq, tq), False)
        return carry

    lax.fori_loop(0, qi, body, 0)
    block(pl.multiple_of(qi * tq, tq), True)
    for j in range(hps // 2):
        pair = acc_scr[2 * j] / l_scr[2 * j] + acc_scr[2 * j + 1] / l_scr[2 * j + 1]
        o_ref[0, :, j * LANES:(j + 1) * LANES] = pair.T.astype(o_ref.dtype)


def _attn_call(qcat, kcat, vt, k_meta, vt_meta, tq, hps):
    B, T, _ = qcat.shape
    assert MLA_HEADS % hps == 0 and hps % 2 == 0
    n_groups = MLA_HEADS // hps
    gw = hps * HEAD_PAD
    ow = (hps // 2) * LANES
    kern = functools.partial(_attn_kernel, tq=tq, n_meta=N_META, hps=hps)
    return pl.pallas_call(
        kern,
        grid=(B, n_groups, T // tq),
        in_specs=[
            pl.BlockSpec((1, tq, gw), lambda b, g, qi: (b, qi, g)),
            pl.BlockSpec((1, T, gw), lambda b, g, qi: (b, 0, g)),
            pl.BlockSpec((gw, T), lambda b, g, qi: (g, b)),
            pl.BlockSpec((k_meta.shape[0], gw), lambda b, g, qi: (0, g)),
            pl.BlockSpec((gw, vt_meta.shape[1]), lambda b, g, qi: (g, 0)),
        ],
        out_specs=pl.BlockSpec((1, tq, ow), lambda b, g, qi: (b, qi, g)),
        out_shape=jax.ShapeDtypeStruct((B, T, n_groups * ow), BF16),
        scratch_shapes=[pltpu.VMEM((hps, 1, tq), F32), pltpu.VMEM((hps, 1, tq), F32),
                        pltpu.VMEM((hps, HEAD_PAD, tq), F32)],
        compiler_params=pltpu.CompilerParams(dimension_semantics=("arbitrary", "arbitrary", "arbitrary"),
                                             vmem_limit_bytes=VMEM_LIMIT),
        name="attn",
    )(qcat, kcat, vt, k_meta, vt_meta)


def _dattn_kernel(pt_ref, qlat_ref, qpe_ref, latn_ref, kpen_ref, lat_hbm, kpe_hbm, o_ref,
                  lat_buf, kpe_buf, sem_lat, sem_kpe, m_scr, l_scr, acc_scr, *, group, n_groups, t_new, page):
    b = pl.program_id(0)
    n_seq = pl.num_programs(0)
    neg = -jnp.inf

    def page_copies(seq, g, k, slot):
        pid = pt_ref[seq, g * group + k]
        rows = pl.ds(pl.multiple_of(k * page, page), page)
        return (pltpu.make_async_copy(lat_hbm.at[0, pid], lat_buf.at[slot, rows, :], sem_lat.at[slot]),
                pltpu.make_async_copy(kpe_hbm.at[0, pid], kpe_buf.at[slot, :, rows], sem_kpe.at[slot]))

    def start_group(seq, g, slot):
        def body(k, carry):
            for cp in page_copies(seq, g, k, slot):
                cp.start()
            return carry
        lax.fori_loop(0, group, body, 0)

    def wait_group(seq, g, slot):
        def body(k, carry):
            for cp in page_copies(seq, g, k, slot):
                cp.wait()
            return carry
        lax.fori_loop(0, group, body, 0)

    @pl.when(b == 0)
    def _():
        start_group(0, 0, 0)

    m_scr[...] = jnp.full(m_scr.shape, neg, F32)
    l_scr[...] = jnp.zeros(l_scr.shape, F32)
    acc_scr[...] = jnp.zeros(acc_scr.shape, F32)

    ql = qlat_ref[0]
    qp = qpe_ref[0]

    def update(s, vals):
        m = m_scr[...]
        m_new = jnp.maximum(m, jnp.max(s, axis=-1, keepdims=True))
        alpha = jnp.exp(m - m_new)
        p = jnp.exp(s - m_new)
        l_scr[...] = alpha * l_scr[...] + jnp.sum(p, axis=-1, keepdims=True)
        acc_scr[...] = alpha * acc_scr[...] + _dot(p.astype(BF16), vals)
        m_scr[...] = m_new

    for g in range(n_groups):
        slot = g % 2
        if g + 1 < n_groups:
            start_group(b, g + 1, 1 - slot)
        else:
            @pl.when(b + 1 < n_seq)
            def _():
                start_group(b + 1, 0, 1 - slot)
        wait_group(b, g, slot)
        cb = lat_buf[slot].astype(BF16)
        kbt = kpe_buf[slot].astype(BF16)
        update(_dot_nt(ql, cb) + _dot(qp, kbt), cb)

    pad = 2 * SUBLANES - latn_ref.shape[1]
    cn = jnp.concatenate([latn_ref[0], jnp.zeros((pad, latn_ref.shape[2]), F32)], axis=0).astype(BF16)
    kn = jnp.concatenate([kpen_ref[0], jnp.zeros((pad, kpen_ref.shape[2]), F32)], axis=0).astype(BF16)
    s = _dot_nt(ql, cn) + _dot_nt(qp, kn)
    tok = lax.shift_right_logical(lax.broadcasted_iota(jnp.int32, s.shape, 0), int(math.log2(MLA_HEADS)))
    key = lax.broadcasted_iota(jnp.int32, s.shape, 1)
    s = jnp.where((key <= tok) & (key < t_new), s, neg)
    update(s, cn)
    o_ref[0] = (acc_scr[...] / l_scr[...]).astype(o_ref.dtype)


def _dattn_call(page_table, qlat, qpe, lat_new, kpe_new, cache_latent, cache_krope_t, t_new, group):
    Bs, R, kv_lora = qlat.shape
    n_pages = page_table.shape[1]
    assert n_pages % (2 * group) == 0, "the two buffer slots alternate, so a sequence needs an even group count"
    n_groups = n_pages // group
    page = cache_latent.shape[2]
    kern = functools.partial(_dattn_kernel, group=group, n_groups=n_groups, t_new=t_new, page=page)
    per_b = lambda shape: pl.BlockSpec((1,) + shape, lambda b, pt: (b, 0, 0))
    grid_spec = pltpu.PrefetchScalarGridSpec(
        num_scalar_prefetch=1,
        grid=(Bs,),
        in_specs=[per_b((R, kv_lora)), per_b((R, MLA_D_ROPE)),
                  per_b(lat_new.shape[1:]), per_b(kpe_new.shape[1:]),
                  pl.BlockSpec(memory_space=pl.ANY), pl.BlockSpec(memory_space=pl.ANY)],
        out_specs=per_b((R, kv_lora)),
        scratch_shapes=[pltpu.VMEM((2, group * page, kv_lora), F32),
                        pltpu.VMEM((2, MLA_D_ROPE, group * page), F32),
                        pltpu.SemaphoreType.DMA((2,)), pltpu.SemaphoreType.DMA((2,)),
                        pltpu.VMEM((R, 1), F32), pltpu.VMEM((R, 1), F32), pltpu.VMEM((R, kv_lora), F32)],
    )
    return pl.pallas_call(
        kern,
        grid_spec=grid_spec,
        out_shape=jax.ShapeDtypeStruct((Bs, R, kv_lora), BF16),
        compiler_params=pltpu.CompilerParams(dimension_semantics=("arbitrary",),
                                             vmem_limit_bytes=VMEM_LIMIT),
        name="dattn",
    )(page_table, qlat, qpe, lat_new, kpe_new, cache_latent, cache_krope_t)


def _post_kernel(*refs, absorbed_values):
    if absorbed_values:
        h_ref, og_ref, om_ref, wvbd_ref, wo1_ref, wo2_ref, g2_ref, wup_ref, wdn_ref, gf_ref, y_ref = refs
        om = _dot(om_ref[...], wvbd_ref[...]).astype(BF16)
    else:
        h_ref, og_ref, om_ref, wo1_ref, wo2_ref, g2_ref, wup_ref, wdn_ref, gf_ref, y_ref = refs
        om = om_ref[...]
    h2 = h_ref[...] + _dot(og_ref[...], wo1_ref[...]) + _dot(om, wo2_ref[...])
    hn = _rms(h2, g2_ref[...]).astype(BF16)
    u = jnp.maximum(_dot(hn, wup_ref[...]), 0.0)
    h3 = h2 + _dot((u * u).astype(BF16), wdn_ref[...])
    y_ref[...] = _rms(h3, gf_ref[...])


def _post_call(h2d, og, om, consts, tm, absorbed_values, name):
    n_rows, d_model = h2d.shape
    assert n_rows % tm == 0
    row_spec = lambda w: pl.BlockSpec((tm, w), lambda i: (i, 0))
    kern = functools.partial(_post_kernel, absorbed_values=absorbed_values)
    return pl.pallas_call(
        kern,
        grid=(n_rows // tm,),
        in_specs=[row_spec(d_model), row_spec(og.shape[1]), row_spec(om.shape[1])]
                 + [_const_spec(c.shape) for c in consts],
        out_specs=row_spec(d_model),
        out_shape=jax.ShapeDtypeStruct((n_rows, d_model), F32),
        compiler_params=pltpu.CompilerParams(dimension_semantics=("arbitrary",),
                                             vmem_limit_bytes=VMEM_LIMIT),
        name=name,
    )(h2d, og, om, *consts)


def _rope_parts(pos):
    half = MLA_D_ROPE // 2
    inv = ROPE_THETA ** (-jnp.arange(half, dtype=F32) / half)
    ang = pos.astype(F32)[:, None] * inv[None, :]
    cos = jnp.concatenate([jnp.cos(ang), jnp.cos(ang)], -1)
    sin_signed = jnp.concatenate([-jnp.sin(ang), jnp.sin(ang)], -1)
    return cos, sin_signed


def _pad_lanes(x, width):
    return jnp.pad(x, ((0, 0), (0, width - x.shape[1])))


def _key_tables(cos, sin_signed):
    return _pad_lanes(cos, LANES), _pad_lanes(sin_signed, LANES)


def _prompt_tables(pos):
    cos, sin_signed = _rope_parts(pos)
    n = pos.shape[0]
    qcos = jnp.concatenate([jnp.full((n, MLA_D_NOPE), MLA_SCALE, F32), cos * MLA_SCALE], -1)
    qsin = jnp.concatenate([jnp.zeros((n, MLA_D_NOPE), F32), sin_signed * MLA_SCALE], -1)
    kcos, ksin = _key_tables(cos, sin_signed)
    return jnp.concatenate([_pad_lanes(qcos, LANES), _pad_lanes(qsin, LANES), kcos, ksin], -1)


def _sample_tables(pos):
    cos, sin_signed = _rope_parts(pos)
    kcos, ksin = _key_tables(cos, sin_signed)
    return jnp.concatenate([jnp.tile(cos * MLA_SCALE, (1, MLA_HEADS)),
                            jnp.tile(sin_signed * MLA_SCALE, (1, MLA_HEADS)), kcos, ksin], -1)


def _swap_halves(w):
    half = MLA_D_ROPE // 2
    return jnp.concatenate([w[..., half:], w[..., :half]], -1)


def kernel(x_prompt, x_sample, cache_latent, cache_krope, state_conv, state_ssm, page_table,
           meta_tokens, norm_mix_g, w_in, conv_w, a_log, dt_bias, gdn_norm_g, q_norm_g, w_q_b,
           kv_norm_g, w_kv_b, w_out, norm_mlp_g, w_up, w_down, final_norm_g):
    assert w_in.shape[0] == 1, "single-layer problem"
    B, T, D = x_prompt.shape
    Bs, Ts, _ = x_sample.shape
    assert CONV_W - 1 <= Ts <= SAMPLE_T_PAD
    past_len = page_table.shape[1] * cache_latent.shape[2]
    q_lora = q_norm_g.shape[1]
    kv_lora = kv_norm_g.shape[1]
    H = MLA_HEADS

    wi = w_in[0]
    c0 = CONV_CH + GDN_V_W
    w1 = wi[:, :c0].astype(BF16)
    wba = _pad_lanes(wi[:, c0:c0 + 2 * GDN_HEADS], LANES).astype(BF16)
    c1 = c0 + 2 * GDN_HEADS
    wcq = wi[:, c1:c1 + q_lora].astype(BF16)
    wckv = wi[:, c1 + q_lora:c1 + q_lora + kv_lora].astype(BF16)
    wkpe_raw = wi[:, c1 + q_lora + kv_lora:]
    wkpe = _pad_lanes(wkpe_raw, LANES).astype(BF16)
    wkpesw = _pad_lanes(_swap_halves(wkpe_raw), LANES).astype(BF16)
    gmix = norm_mix_g[0][None, :]
    qg = q_norm_g[0][None, :]
    kvg = kv_norm_g[0][None, :]
    wq = w_q_b[0]
    wq_nope, wq_pe = wq[..., :MLA_D_NOPE], wq[..., MLA_D_NOPE:]
    wq_pesw = _swap_halves(wq_pe)
    zq = lambda n: jnp.zeros((q_lora, H, n), F32)
    wqcat = jnp.concatenate([wq_nope, wq_pe, zq(HEAD_PAD - MLA_D_NOPE - MLA_D_ROPE)], -1)
    wqsw = jnp.concatenate([zq(MLA_D_NOPE), wq_pesw, zq(HEAD_PAD - MLA_D_NOPE - MLA_D_ROPE)], -1)
    wqcat = wqcat.reshape(q_lora, H * HEAD_PAD).astype(BF16)
    wqsw = wqsw.reshape(q_lora, H * HEAD_PAD).astype(BF16)
    wkv = w_kv_b[0]
    wkb, wvb = wkv[..., :MLA_D_NOPE], wkv[..., MLA_D_NOPE:]
    zk = lambda n: jnp.zeros((kv_lora, H, n), F32)
    wkb_sp = jnp.concatenate([wkb, zk(HEAD_PAD - MLA_D_NOPE)], -1).reshape(kv_lora, H * HEAD_PAD).astype(BF16)
    odd = (jnp.arange(H) % 2 == 1)[None, :, None]
    wvb_sp = jnp.where(odd, jnp.concatenate([zk(HEAD_PAD - MLA_D_V), wvb], -1),
                       jnp.concatenate([wvb, zk(HEAD_PAD - MLA_D_V)], -1))
    wvb_sp_t = wvb_sp.reshape(kv_lora, H * HEAD_PAD).T.astype(BF16)
    jj =jnp.arange(LANES)[:, None]
    ll = jnp.arange(H * HEAD_PAD)[None, :]
    esel = ((ll % HEAD_PAD == jj + MLA_D_NOPE) & (jj < MLA_D_ROPE)).astype(BF16)
    eye_h = jnp.eye(H, dtype=bool)
    wabs = jnp.where(eye_h[:, None, :, None], wkb.transpose(1, 2, 0)[:, :, None, :], 0.0)
    wabs = wabs.reshape(H * MLA_D_NOPE, H * kv_lora).astype(BF16)
    wvbd = jnp.where(eye_h[:, None, :, None], wvb.transpose(1, 0, 2)[:, :, None, :], 0.0)
    wvbd = wvbd.reshape(H * kv_lora, H * MLA_D_V).astype(BF16)
    wo1 = w_out[0][:GDN_V_W].astype(BF16)
    wo2 = w_out[0][GDN_V_W:].astype(BF16)
    g2 = norm_mlp_g[0][None, :]
    wup = w_up[0].astype(BF16)
    wdn = w_down[0].astype(BF16)
    gf = final_norm_g[None, :]
    cw = jnp.pad(conv_w[0], ((0, SUBLANES - CONV_W), (0, 0)))
    lane_row = lambda v: jnp.pad(v[None, :], ((0, 0), (GDN_HEADS, LANES - 2 * GDN_HEADS)))
    alog = lane_row(a_log[0])
    dtb = lane_row(dt_bias[0])
    ng = gdn_norm_g[0][None, :]

    common = (gmix, w1, wba, wcq, wckv, wkpe, wkpesw, qg, kvg)
    prompt_consts = common + (wqcat, wqsw, wkb_sp, wvb_sp_t, esel)
    sample_consts = common + (wq_nope.reshape(q_lora, -1).astype(BF16), wq_pe.reshape(q_lora, -1).astype(BF16),
                              wq_pesw.reshape(q_lora, -1).astype(BF16), wabs)
    pw = [CONV_CH, GDN_V_W, LANES, kv_lora, MLA_D_ROPE, H * HEAD_PAD, H * HEAD_PAD, H * HEAD_PAD]
    pd = [F32, F32, F32, F32, F32, BF16, BF16, BF16]

    tab_m = _prompt_tables(jnp.arange(N_META))
    qkv_m, z_m, ba_m, lat_m, kpe_m, _, kcat_m, vt_m = _proj_call(
        _proj_prompt_kernel, meta_tokens, tab_m, prompt_consts, pw, pd, N_META, "proj_meta", transposed_outs=(7,))
    zero_conv = jnp.zeros((1, SUBLANES, CONV_CH), F32)
    zero_s = jnp.zeros((1, GDN_HEADS, GDN_DK, GDN_DV), F32)
    _, s1 = _gdn_call(qkv_m[None], z_m[None], ba_m[None], zero_conv, zero_s, cw, alog, dtb, ng,
                      N_META, N_META, "gdn_meta")
    conv1 = jnp.pad(qkv_m[N_META - (CONV_W - 1):], ((SUBLANES - (CONV_W - 1), 0), (0, 0)))[None]
    k_meta = jnp.pad(kcat_m, ((0, LANES - N_META), (0, 0)))
    vt_meta = jnp.pad(vt_m, ((0, 0), (0, LANES - N_META)))

    tab_p = _prompt_tables(N_META + jnp.arange(T))
    xp2d = x_prompt.reshape(B * T, D)
    qkv_p, z_p, ba_p, lat_p, kpe_p, qcat_p, kcat_p, vt_p = _proj_call(
        _proj_prompt_kernel, xp2d, tab_p, prompt_consts, pw, pd, 256, "proj_prompt", transposed_outs=(7,))
    r3 = lambda a: a.reshape(B, T, a.shape[-1])
    qkv_p3 = r3(qkv_p)
    og_p, s2 = _gdn_call(qkv_p3, r3(z_p), r3(ba_p), conv1, s1, cw, alog, dtb, ng, GDN_CHUNK, T, "gdn_prompt",
                         nb_intra=4, nb_scan=8)
    om_p = _attn_call(r3(qcat_p), r3(kcat_p), vt_p, k_meta, vt_meta, 256, 8)
    post_consts = (wo1, wo2, g2, wup, wdn, gf)
    y_p = _post_call(xp2d, og_p.reshape(B * T, -1), om_p.reshape(B * T, -1), post_consts, 512, False, "post_prompt")

    tp = SAMPLE_T_PAD
    xs = jnp.pad(x_sample, ((0, 0), (0, tp - Ts), (0, 0))).reshape(Bs * tp, D)
    tab_s = jnp.tile(_sample_tables(past_len + jnp.arange(tp)), (Bs, 1))
    sw = [CONV_CH, GDN_V_W, LANES, kv_lora, MLA_D_ROPE, H * kv_lora, H * MLA_D_ROPE]
    sd = [F32, F32, F32, F32, F32, BF16, BF16]
    qkv_s, z_s, ba_s, lat_s, kpe_s, qlat_s, qpe_s = _proj_call(
        _proj_sample_kernel, xs, tab_s, sample_consts, sw, sd, Bs * tp, "proj_sample")
    s3 = lambda a: a.reshape(Bs, tp, a.shape[-1])
    qkv_s3 = s3(qkv_s)
    conv_in_s = jnp.pad(state_conv[0], ((0, 0), (SUBLANES - (CONV_W - 1), 0), (0, 0)))
    og_s, s_new = _gdn_call(qkv_s3, s3(z_s), s3(ba_s), conv_in_s, state_ssm[0], cw, alog, dtb, ng, tp, Ts, "gdn_sample",
                            nb_intra=4, nb_scan=4)
    out_lat = _dattn_call(page_table, qlat_s.reshape(Bs, tp * H, kv_lora), qpe_s.reshape(Bs, tp * H, MLA_D_ROPE),
                          s3(lat_s), s3(kpe_s), cache_latent, jnp.swapaxes(cache_krope, 2, 3), Ts, 32)
    sample_post_consts = (wvbd,) + post_consts
    y_s = _post_call(xs, og_s.reshape(Bs * tp, -1), out_lat.reshape(Bs * tp, H * kv_lora),
                     sample_post_consts, Bs * tp, True, "post_sample")

    bc = lambda a: jnp.broadcast_to(a[None], (B,) + a.shape)
    lat_po = jnp.concatenate([bc(lat_m), r3(lat_p)], axis=1)[None]
    kpe_po = jnp.concatenate([bc(kpe_m), r3(kpe_p)], axis=1)[None]
    conv_po = qkv_p3[:, T - (CONV_W - 1):][None]
    conv_so = qkv_s3[:, Ts - (CONV_W - 1):Ts][None]
    return (y_p.reshape(B, T, D), y_s.reshape(Bs, tp, D)[:, :Ts],
            lat_po, kpe_po, conv_po, s2[None],
            s3(lat_s)[:, :Ts][None], s3(kpe_s)[:, :Ts][None], conv_so, s_new[None])
```

```python
import functools
import math

import jax
import jax.numpy as jnp
from jax import lax
from jax.experimental import pallas as pl
from jax.experimental.pallas import tpu as pltpu

F32 = jnp.float32
BF16 = jnp.bfloat16

N_META = 16
EPS = 1e-6
GDN_HEADS = 8
GDN_DK = 64
GDN_DV = 64
CONV_W = 4
GDN_CHUNK = 64
MLA_HEADS = 8
MLA_D_NOPE = 64
MLA_D_ROPE = 32
MLA_D_V = 64
ROPE_THETA = 10000.0
MLA_SCALE = (MLA_D_NOPE + MLA_D_ROPE) ** -0.5
GDN_QK_W = GDN_HEADS * GDN_DK
GDN_V_W = GDN_HEADS * GDN_DV
CONV_CH = 2 * GDN_QK_W + GDN_V_W

LANES = 128
SUBLANES = 8
HEAD_PAD = 128
SAMPLE_T_PAD = 8
VMEM_LIMIT = 56 * 1024 * 1024


def _dot(a, b):
    return jnp.dot(a, b, preferred_element_type=F32)


def _dot_nt(a, b):
    return lax.dot_general(a, b, (((1,), (1,)), ((), ())), preferred_element_type=F32)


def _rms(x, g):
    return x * lax.rsqrt(jnp.mean(x * x, axis=-1, keepdims=True) + EPS) * g


def _sigmoid(x):
    return 1.0 / (1.0 + jnp.exp(-x))


def _const_spec(shape):
    nd = len(shape)
    return pl.BlockSpec(shape, lambda *_: (0,) * nd)


def _proj_common(x_ref, gmix_ref, w1_ref, wba_ref, wcq_ref, wckv_ref, wkpe_ref, wkpesw_ref,
                 qg_ref, kvg_ref, kcos, ksin, qkv_ref, z_ref, ba_ref, lat_ref, kpe_ref):
    hn = _rms(x_ref[...], gmix_ref[...]).astype(BF16)
    qkvz = _dot(hn, w1_ref[...])
    qkv_ref[...] = qkvz[:, :CONV_CH]
    z_ref[...] = qkvz[:, CONV_CH:]
    ba_ref[...] = _dot(hn, wba_ref[...])
    cq = _dot(hn, wcq_ref[...])
    ckv = _dot(hn, wckv_ref[...])
    kpe = _dot(hn, wkpe_ref[...]) * kcos + _dot(hn, wkpesw_ref[...]) * ksin
    kpe_ref[...] = kpe[:, :MLA_D_ROPE]
    lat = _rms(ckv, kvg_ref[...])
    lat_ref[...] = lat
    cqn = _rms(cq, qg_ref[...]).astype(BF16)
    return cqn, lat.astype(BF16), kpe.astype(BF16)


def _proj_prompt_kernel(x_ref, tab_ref, gmix_ref, w1_ref, wba_ref, wcq_ref, wckv_ref, wkpe_ref,
                        wkpesw_ref, qg_ref, kvg_ref, wq_ref, wqsw_ref, wkb_ref, wvb_ref, esel_ref, vones_ref,
                        qkv_ref, z_ref, ba_ref, lat_ref, kpe_ref, qcat_ref, kcat_ref, vt_ref):
    qcos = tab_ref[:, 0 * LANES:1 * LANES]
    qsin = tab_ref[:, 1 * LANES:2 * LANES]
    kcos = tab_ref[:, 2 * LANES:3 * LANES]
    ksin = tab_ref[:, 3 * LANES:4 * LANES]
    cqn, latb, kpeb = _proj_common(x_ref, gmix_ref, w1_ref, wba_ref, wcq_ref, wckv_ref, wkpe_ref,
                                   wkpesw_ref, qg_ref, kvg_ref, kcos, ksin,
                                   qkv_ref, z_ref, ba_ref, lat_ref, kpe_ref)
    qa = _dot(cqn, wq_ref[...])
    qb = _dot(cqn, wqsw_ref[...])
    for h in range(MLA_HEADS):
        sl = slice(h * HEAD_PAD, (h + 1) * HEAD_PAD)
        qcat_ref[:, sl] = (qa[:, sl] * qcos + qb[:, sl] * qsin).astype(BF16)
    kcat_ref[...] = (_dot(latb, wkb_ref[...]) + _dot(kpeb, esel_ref[...])).astype(BF16)
    vt_ref[...] = (_dot_nt(wvb_ref[...], latb) + vones_ref[...]).astype(BF16)


def _proj_sample_kernel(x_ref, tab_ref, gmix_ref, w1_ref, wba_ref, wcq_ref, wckv_ref, wkpe_ref,
                        wkpesw_ref, qg_ref, kvg_ref, wqn_ref, wqpe_ref, wqpesw_ref, wabs_ref,
                        qkv_ref, z_ref, ba_ref, lat_ref, kpe_ref, qlat_ref, qpe_ref):
    pe_w = MLA_HEADS * MLA_D_ROPE
    qcos = tab_ref[:, 0:pe_w]
    qsin = tab_ref[:, pe_w:2 * pe_w]
    kcos = tab_ref[:, 2 * pe_w:2 * pe_w + LANES]
    ksin = tab_ref[:, 2 * pe_w + LANES:2 * pe_w + 2 * LANES]
    cqn, _, _ = _proj_common(x_ref, gmix_ref, w1_ref, wba_ref, wcq_ref, wckv_ref, wkpe_ref,
                             wkpesw_ref, qg_ref, kvg_ref, kcos, ksin,
                             qkv_ref, z_ref, ba_ref, lat_ref, kpe_ref)
    qn = _dot(cqn, wqn_ref[...]).astype(BF16)
    qlat_ref[...] = (_dot(qn, wabs_ref[...]) * MLA_SCALE).astype(BF16)
    qpe_ref[...] = (_dot(cqn, wqpe_ref[...]) * qcos + _dot(cqn, wqpesw_ref[...]) * qsin).astype(BF16)


def _proj_call(kernel_fn, x2d, tab, consts, out_widths, out_dtypes, tm, name, transposed_outs=()):
    n_rows, d_model = x2d.shape
    assert n_rows % tm == 0
    row_spec = lambda w: pl.BlockSpec((tm, w), lambda i: (i, 0))
    col_spec = lambda w: pl.BlockSpec((w, tm), lambda i: (0, i))
    is_t = [k in transposed_outs for k in range(len(out_widths))]
    assert tab.shape[0] % tm == 0 and n_rows % tab.shape[0] == 0
    tab_blocks = tab.shape[0] // tm
    tab_spec = pl.BlockSpec((tm, tab.shape[1]), lambda i: (i % tab_blocks, 0))
    in_specs = [row_spec(d_model), tab_spec] + [_const_spec(c.shape) for c in consts]
    return pl.pallas_call(
        kernel_fn,
        grid=(n_rows // tm,),
        in_specs=in_specs,
        out_specs=[col_spec(w) if t else row_spec(w) for w, t in zip(out_widths, is_t)],
        out_shape=[jax.ShapeDtypeStruct((w, n_rows) if t else (n_rows, w), dt)
                   for w, dt, t in zip(out_widths, out_dtypes, is_t)],
        compiler_params=pltpu.CompilerParams(dimension_semantics=("arbitrary",),
                                             vmem_limit_bytes=VMEM_LIMIT),
        name=name,
    )(x2d, tab, *consts)


def _block_diag2(x, half):
    lane = lax.broadcasted_iota(jnp.int32, x.shape, 1)
    zero = jnp.zeros_like(x)
    return jnp.concatenate([jnp.where(lane < half, x, zero), jnp.where(lane < half, zero, x)], axis=0)


def _gdn_intra_prep(bb, cinit_bb, qkv_ref, prev_ref, ba_ref, cinit_ref, cw_ref, alog_ref, dtb_ref,
                    qk_mask, kg_ref, qg_ref, kdt_ref, vb_ref, egl_ref, xbuf, *, chunk, valid_len, n_chunks):
    c = pl.program_id(1)
    C = chunk
    H = GDN_HEADS
    tail0 = SUBLANES - (CONV_W - 1)

    xbuf[bb, 0:SUBLANES, :] = jnp.where(c == 0, cinit_ref[cinit_bb], prev_ref[bb])
    xbuf[bb, SUBLANES:SUBLANES + C, :] = qkv_ref[bb]
    cw = cw_ref[...]
    y = xbuf[bb, tail0:tail0 + C, :] * cw[0:1, :]
    for j in range(1, CONV_W):
        y = y + xbuf[bb, tail0 + j:tail0 + j + C, :] * cw[j:j + 1, :]
    y = y * _sigmoid(y)

    ba = ba_ref[bb]
    beta_all = _sigmoid(ba)
    sp_in = ba + dtb_ref[...]
    softplus = jnp.maximum(sp_in, 0.0) + jnp.log1p(jnp.exp(-jnp.abs(sp_in)))
    g_all = -jnp.exp(alog_ref[...]) * softplus
    if valid_len < n_chunks * C:
        row = c * C + lax.broadcasted_iota(jnp.int32, (C, LANES), 0)
        beta_all = jnp.where(row < valid_len, beta_all, 0.0)
        g_all = jnp.where(row < valid_len, g_all, 0.0)

    ri = lax.broadcasted_iota(jnp.int32, (C, C), 0)
    ci = lax.broadcasted_iota(jnp.int32, (C, C), 1)
    lower = ri >= ci
    gc_all = lax.dot_general(lower.astype(F32), g_all, (((1,), (0,)), ((), ())),
                             precision=lax.Precision.HIGHEST, preferred_element_type=F32)
    sel = (lax.broadcasted_iota(jnp.int32, (SUBLANES, LANES), 1)
           == lax.broadcasted_iota(jnp.int32, (SUBLANES, LANES), 0) + GDN_HEADS).astype(F32)
    gc_t = lax.dot_general(sel, gc_all, (((1,), (1,)), ((), ())),
                           precision=lax.Precision.HIGHEST, preferred_element_type=F32)

    pairs = range(H // 2)
    lane = lax.broadcasted_iota(jnp.int32, (C, LANES), 1)
    lo = lane < GDN_DK
    lo_c, lower_p = qk_mask
    ii_r = lax.broadcasted_iota(jnp.int32, (GDN_DK, LANES), 0)
    ii_c = lax.broadcasted_iota(jnp.int32, (GDN_DK, LANES), 1)
    eye2 = ((ii_c == ii_r) | (ii_c == ii_r + GDN_DK)).astype(BF16)

    def col(x, idx):
        return jnp.sum(jnp.where(lane == idx, x, 0.0), axis=-1, keepdims=True)

    def half_sums(x):
        return jnp.where(lo, jnp.sum(jnp.where(lo, x, 0.0), axis=-1, keepdims=True),
                         jnp.sum(jnp.where(lo, 0.0, x), axis=-1, keepdims=True))

    kn_b, kb_b, qn_b, decay = [], [], [], []
    for j in pairs:
        ps = slice(j * LANES, (j + 1) * LANES)
        qp = y[:, j * LANES:(j + 1) * LANES]
        kp = y[:, GDN_QK_W + j * LANES:GDN_QK_W + (j + 1) * LANES]
        vp = y[:, 2 * GDN_QK_W + j * LANES:2 * GDN_QK_W + (j + 1) * LANES]
        qn = qp * lax.rsqrt(half_sums(qp * qp) + EPS) * (GDN_DK ** -0.5)
        kn = kp * lax.rsqrt(half_sums(kp * kp) + EPS)
        beta = jnp.where(lo, col(beta_all, 2 * j), col(beta_all, 2 * j + 1))
        gc0, gc1 = col(gc_all, H + 2 * j), col(gc_all, H + 2 * j + 1)
        gcol = jnp.where(lo, gc0, gc1)
        grow = jnp.concatenate([gc_t[2 * j:2 * j + 1, :], gc_t[2 * j + 1:2 * j + 2, :]], axis=1)
        diff = jnp.where(lo_c, gc0, gc1) - grow
        decay.append(jnp.where(lower_p, jnp.exp(jnp.where(lower_p, diff, 0.0)), 0.0))
        egc = jnp.exp(gcol)
        gl = gcol[C - 1:C, :]
        egl_ref[bb, 0, :, ps] = jnp.exp(gl)
        kb = kn * beta
        vb_ref[bb, 0, :, ps] = vp * beta
        kg_ref[bb, 0, :, ps] = (kb * egc).astype(BF16)
        qg_ref[bb, 0, :, ps] = (qn * egc).astype(BF16)
        k_dec = (kn * jnp.exp(gl - gcol)).astype(BF16)
        kdt_ref[bb, 0, j] = _dot_nt(eye2, _block_diag2(k_dec, GDN_DK)).astype(BF16)
        kn_b.append(_block_diag2(kn.astype(BF16), GDN_DK))
        kb_b.append(kb.astype(BF16))
        qn_b.append(qn.astype(BF16))
    return kn_b, kb_b, qn_b, decay


def _gdn_intra_kernel(qkv_ref, prev_ref, ba_ref, cinit_ref, cw_ref, alog_ref, dtb_ref,
                      t_ref, qk_ref, kg_ref, qg_ref, kdt_ref, vb_ref, egl_ref, xbuf,
                      *, chunk, valid_len, n_chunks, nb, shared_cinit):
    C = chunk
    rp = lax.broadcasted_iota(jnp.int32, (C, 2 * C), 0)
    cp = lax.broadcasted_iota(jnp.int32, (C, 2 * C), 1)
    lo_c = cp < C
    cmod = jnp.where(lo_c, cp, cp - C)
    lower_p, strict_p = rp >= cmod, rp > cmod
    eye_p = (rp == cmod).astype(F32)
    kn_b, kb_b, qn_b, decay = [], [], [], []
    for bb in range(nb):
        parts = _gdn_intra_prep(bb, 0 if shared_cinit else bb, qkv_ref, prev_ref, ba_ref, cinit_ref, cw_ref,
                                alog_ref, dtb_ref, (lo_c, lower_p), kg_ref, qg_ref, kdt_ref, vb_ref, egl_ref, xbuf,
                                chunk=chunk, valid_len=valid_len, n_chunks=n_chunks)
        for dst, src in zip((kn_b, kb_b, qn_b, decay), parts):
            dst.extend(src)
    n_pairs = GDN_HEADS // 2
    probs = range(nb * n_pairs)
    kk = [_dot_nt(kb_b[i], kn_b[i]) for i in probs]
    qk = [_dot_nt(qn_b[i], kn_b[i]) for i in probs]
    for i in probs:
        qk_ref[i // n_pairs, 0, i % n_pairs] = jnp.where(lower_p, qk[i] * decay[i], 0.0).astype(BF16)
    p = [jnp.where(strict_p, -(kk[i] * decay[i]), 0.0) for i in probs]
    t = [eye_p + p[i] for i in probs]
    n_iter = max(1, int(math.ceil(math.log2(C))))
    for _ in range(1, n_iter):
        pb = [p[i].astype(BF16) for i in probs]
        p = [_dot(pb[i], _block_diag2(pb[i], C)) for i in probs]
        t = [t[i] + _dot(t[i].astype(BF16), _block_diag2(p[i].astype(BF16), C)) for i in probs]
    for i in probs:
        t_ref[i // n_pairs, 0, i % n_pairs] = t[i].astype(BF16)


def _gdn_scan_kernel(t_ref, qk_ref, kg_ref, qg_ref, kdt_ref, vb_ref, egl_ref, z_ref, sinit_ref, ng_ref,
                     o_ref, sfin_ref, s_scr, *, chunk, n_chunks, nb, shared_sinit):
    c = pl.program_id(1)
    C = chunk
    n_pairs = GDN_HEADS // 2
    probs = [(bb, j) for bb in range(nb) for j in range(n_pairs)]
    ps = [slice(j * LANES, (j + 1) * LANES) for j in range(n_pairs)]
    lo = lax.broadcasted_iota(jnp.int32, (C, LANES), 1) < GDN_DV

    @pl.when(c == 0)
    def _():
        for bb, j in probs:
            si = 0 if shared_sinit else bb
            s_scr[bb, j] = jnp.concatenate([sinit_ref[si, 2 * j], sinit_ref[si, 2 * j + 1]], axis=1)

    s = [s_scr[bb, j] for bb, j in probs]
    sbd = [_block_diag2(x.astype(BF16), GDN_DV) for x in s]
    ks = [_dot(kg_ref[bb, 0, :, ps[j]], sbd[i]) for i, (bb, j) in enumerate(probs)]
    qs = [_dot(qg_ref[bb, 0, :, ps[j]], sbd[i]) for i, (bb, j) in enumerate(probs)]
    r = [(vb_ref[bb, 0, :, ps[j]] - ks[i]).astype(BF16) for i, (bb, j) in enumerate(probs)]
    v_new = [_dot(t_ref[bb, 0, j], _block_diag2(r[i], GDN_DV)).astype(BF16) for i, (bb, j) in enumerate(probs)]
    vbd = [_block_diag2(x, GDN_DV) for x in v_new]
    o = [qs[i] + _dot(qk_ref[bb, 0, j], vbd[i]) for i, (bb, j) in enumerate(probs)]
    for i, (bb, j) in enumerate(probs):
        s_scr[bb, j] = s[i] * egl_ref[bb, 0, :, ps[j]] + _dot(kdt_ref[bb, 0, j], vbd[i])
    ng = ng_ref[...]
    for i, (bb, j) in enumerate(probs):
        o2 = o[i] * o[i]
        ms = jnp.where(lo, jnp.sum(jnp.where(lo, o2, 0.0), axis=-1, keepdims=True),
                       jnp.sum(jnp.where(lo, 0.0, o2), axis=-1, keepdims=True)) * (1.0 / GDN_DV)
        zp = z_ref[bb, :, ps[j]]
        o_ref[bb, :, ps[j]] = (o[i] * lax.rsqrt(ms + EPS) * ng * (zp * _sigmoid(zp))).astype(o_ref.dtype)

    @pl.when(c == n_chunks - 1)
    def _():
        for bb, j in probs:
            s_pair = s_scr[bb, j]
            sfin_ref[bb, 2 * j] = s_pair[:, :GDN_DV]
            sfin_ref[bb, 2 * j + 1] = s_pair[:, GDN_DV:]


def _gdn_call(qkv, z, ba, conv_init, s_init, cw, alog, dtb, ng, chunk, valid_len, name, nb_intra=1, nb_scan=1):
    B, T, _ = qkv.shape
    assert T % chunk == 0 and chunk % SUBLANES == 0 and GDN_DK == GDN_DV and 2 * GDN_DV == LANES
    n_chunks = T // chunk
    H, C = GDN_HEADS, chunk
    HP = H // 2
    shared_c = conv_init.shape[0] == 1
    shared_s = s_init.shape[0] == 1
    ng = jnp.tile(ng, (1, 2))
    params = pltpu.CompilerParams(dimension_semantics=("arbitrary", "arbitrary"), vmem_limit_bytes=VMEM_LIMIT)
    inter_dims = [(HP, C, 2 * C), (HP, C, 2 * C), (C, GDN_QK_W), (C, GDN_QK_W), (HP, GDN_DK, 2 * C),
                  (C, GDN_V_W), (1, GDN_V_W)]
    inter_dtypes = [BF16, BF16, BF16, BF16, BF16, F32, F32]

    def specs(nb):
        tok = lambda w: pl.BlockSpec((nb, chunk, w), lambda b, c: (b, c, 0))
        per_chunk = lambda *dims: pl.BlockSpec((nb, 1) + dims, lambda b, c: (b, c) + (0,) * len(dims))
        per_seq = lambda shared, *dims: pl.BlockSpec(
            ((1 if shared else nb),) + dims, lambda b, c: ((0 if shared else b),) + (0,) * len(dims))
        return tok, per_chunk, per_seq

    nb = nb_intra
    assert B % nb == 0
    tok, per_chunk, per_seq = specs(nb)
    prev_rows = pl.BlockSpec((nb, SUBLANES, CONV_CH),
                             lambda b, c: (b, jnp.maximum(c * (C // SUBLANES) - 1, 0), 0))
    inter = pl.pallas_call(
        functools.partial(_gdn_intra_kernel, chunk=chunk, valid_len=valid_len, n_chunks=n_chunks,
                          nb=nb, shared_cinit=shared_c),
        grid=(B // nb, n_chunks),
        in_specs=[
            tok(CONV_CH), prev_rows, tok(LANES), per_seq(shared_c, SUBLANES, CONV_CH),
            _const_spec(cw.shape), _const_spec(alog.shape), _const_spec(dtb.shape),
        ],
        out_specs=[per_chunk(*d) for d in inter_dims],
        out_shape=[jax.ShapeDtypeStruct((B, n_chunks) + d, dt) for d, dt in zip(inter_dims, inter_dtypes)],
        scratch_shapes=[pltpu.VMEM((nb, SUBLANES + chunk, CONV_CH), F32)],
        compiler_params=params,
        name=name + "_intra",
    )(qkv, qkv, ba, conv_init, cw, alog, dtb)

    nb = nb_scan
    assert B % nb == 0
    tok, per_chunk, per_seq = specs(nb)
    o, s_fin = pl.pallas_call(
        functools.partial(_gdn_scan_kernel, chunk=chunk, n_chunks=n_chunks, nb=nb, shared_sinit=shared_s),
        grid=(B // nb, n_chunks),
        in_specs=[per_chunk(*d) for d in inter_dims] + [
            tok(GDN_V_W), per_seq(shared_s, H, GDN_DK, GDN_DV), _const_spec(ng.shape),
        ],
        out_specs=[tok(GDN_V_W), per_seq(False, H, GDN_DK, GDN_DV)],
        out_shape=[
            jax.ShapeDtypeStruct((B, T, GDN_V_W), BF16),
            jax.ShapeDtypeStruct((B, H, GDN_DK, GDN_DV), F32),
        ],
        scratch_shapes=[pltpu.VMEM((nb, HP, GDN_DK, LANES), F32)],
        compiler_params=params,
        name=name + "_scan",
    )(*inter, z, s_init, ng)
    return o, s_fin


def _denominator_row(h):
    return MLA_D_V * (1 - h % 2)


def _attn_kernel(q_ref, k_ref, vt_ref, km_ref, vmt_ref, o_ref, m_scr, acc_scr, *, tq, n_meta, hps):
    qi = pl.program_id(2)
    neg = -jnp.inf
    key_m = lax.broadcasted_iota(jnp.int32, (km_ref.shape[0], tq), 0)
    key_d = lax.broadcasted_iota(jnp.int32, (tq, tq), 0)
    qry_d = lax.broadcasted_iota(jnp.int32, (tq, tq), 1)
    heads = range(hps)
    ls = [slice(h * HEAD_PAD, (h + 1) * HEAD_PAD) for h in heads]

    s = [jnp.where(key_m < n_meta, _dot_nt(km_ref[:, ls[h]], q_ref[0, :, ls[h]]), neg) for h in heads]
    m = [jnp.max(s[h], axis=0, keepdims=True) for h in heads]
    p = [jnp.exp2(s[h] - m[h]) for h in heads]
    for h in heads:
        m_scr[h] = m[h]
        acc_scr[h] = _dot(vmt_ref[ls[h], :], p[h].astype(BF16))

    def block(start, diagonal):
        s = [_dot_nt(k_ref[0, pl.ds(start, tq), ls[h]], q_ref[0, :, ls[h]]) for h in heads]
        if diagonal:
            s = [jnp.where(key_d <= qry_d, s[h], neg) for h in heads]
        m_old = [m_scr[h] for h in heads]
        m_new = [jnp.maximum(m_old[h], jnp.max(s[h], axis=0, keepdims=True)) for h in heads]
        alpha = [jnp.exp2(m_old[h] - m_new[h]) for h in heads]
        p = [jnp.exp2(s[h] - m_new[h]) for h in heads]
        pv = [_dot(vt_ref[ls[h], pl.ds(start, tq)], p[h].astype(BF16)) for h in heads]
        for h in heads:
            acc_scr[h] = alpha[h] * acc_scr[h] + pv[h]
            m_scr[h] = m_new[h]

    def body(kb, carry):
        block(pl.multiple_of(kb * tq, tq), False)
        return carry

    lax.fori_loop(0, qi, body, 0)
    block(pl.multiple_of(qi * tq, tq), True)
    slot_row = lax.broadcasted_iota(jnp.int32, (HEAD_PAD, tq), 0)

    def normalised(h):
        acc = acc_scr[h]
        r = _denominator_row(h)
        return jnp.where(slot_row == r, 0.0, acc) / acc[r:r + 1, :]

    for j in range(hps // 2):
        pair = normalised(2 * j) + normalised(2 * j + 1)
        o_ref[0, :, j * LANES:(j + 1) * LANES] = pair.T.astype(o_ref.dtype)


def _attn_call(qcat, kcat, vt, k_meta, vt_meta, tq, hps):
    B, T, _ = qcat.shape
    assert MLA_HEADS % hps == 0 and hps % 2 == 0
    n_groups = MLA_HEADS // hps
    gw = hps * HEAD_PAD
    ow = (hps // 2) * LANES
    kern = functools.partial(_attn_kernel, tq=tq, n_meta=N_META, hps=hps)
    return pl.pallas_call(
        kern,
        grid=(B, n_groups, T // tq),
        in_specs=[
            pl.BlockSpec((1, tq, gw), lambda b, g, qi: (b, qi, g)),
            pl.BlockSpec((1, T, gw), lambda b, g, qi: (b, 0, g)),
            pl.BlockSpec((gw, T), lambda b, g, qi: (g, b)),
            pl.BlockSpec((k_meta.shape[0], gw), lambda b, g, qi: (0, g)),
            pl.BlockSpec((gw, vt_meta.shape[1]), lambda b, g, qi: (g, 0)),
        ],
        out_specs=pl.BlockSpec((1, tq, ow), lambda b, g, qi: (b, qi, g)),
        out_shape=jax.ShapeDtypeStruct((B, T, n_groups * ow), BF16),
        scratch_shapes=[pltpu.VMEM((hps, 1, tq), F32), pltpu.VMEM((hps, HEAD_PAD, tq), F32)],
        compiler_params=pltpu.CompilerParams(dimension_semantics=("arbitrary", "arbitrary", "arbitrary"),
                                             vmem_limit_bytes=VMEM_LIMIT),
        name="attn",
    )(qcat, kcat, vt, k_meta, vt_meta)


def _dattn_kernel(pt_ref, qlat_ref, qpe_ref, latn_ref, kpen_ref, lat_hbm, kpe_hbm, o_ref,
                  lat_buf, kpe_buf, sem_lat, sem_kpe, m_scr, l_scr, acc_scr, *, group, n_groups, t_new, page):
    b = pl.program_id(0)
    n_seq = pl.num_programs(0)
    neg = -jnp.inf

    def page_copies(seq, g, k, slot):
        pid = pt_ref[seq, g * group + k]
        rows = pl.ds(pl.multiple_of(k * page, page), page)
        return (pltpu.make_async_copy(lat_hbm.at[0, pid], lat_buf.at[slot, rows, :], sem_lat.at[slot]),
                pltpu.make_async_copy(kpe_hbm.at[0, pid], kpe_buf.at[slot, :, rows], sem_kpe.at[slot]))

    def start_group(seq, g, slot):
        def body(k, carry):
            for cp in page_copies(seq, g, k, slot):
                cp.start()
            return carry
        lax.fori_loop(0, group, body, 0)

    def wait_group(seq, g, slot):
        def body(k, carry):
            for cp in page_copies(seq, g, k, slot):
                cp.wait()
            return carry
        lax.fori_loop(0, group, body, 0)

    @pl.when(b == 0)
    def _():
        start_group(0, 0, 0)

    m_scr[...] = jnp.full(m_scr.shape, neg, F32)
    l_scr[...] = jnp.zeros(l_scr.shape, F32)
    acc_scr[...] = jnp.zeros(acc_scr.shape, F32)

    ql = qlat_ref[0]
    qp = qpe_ref[0]

    def update(s, vals):
        m = m_scr[...]
        m_new = jnp.maximum(m, jnp.max(s, axis=-1, keepdims=True))
        alpha = jnp.exp(m - m_new)
        p = jnp.exp(s - m_new)
        l_scr[...] = alpha * l_scr[...] + jnp.sum(p, axis=-1, keepdims=True)
        acc_scr[...] = alpha * acc_scr[...] + _dot(p.astype(BF16), vals)
        m_scr[...] = m_new

    for g in range(n_groups):
        slot = g % 2
        if g + 1 < n_groups:
            start_group(b, g + 1, 1 - slot)
        else:
            @pl.when(b + 1 < n_seq)
            def _():
                start_group(b + 1, 0, 1 - slot)
        wait_group(b, g, slot)
        cb = lat_buf[slot].astype(BF16)
        kbt = kpe_buf[slot].astype(BF16)
        update(_dot_nt(ql, cb) + _dot(qp, kbt), cb)

    pad = 2 * SUBLANES - latn_ref.shape[1]
    cn = jnp.concatenate([latn_ref[0], jnp.zeros((pad, latn_ref.shape[2]), F32)], axis=0).astype(BF16)
    kn = jnp.concatenate([kpen_ref[0], jnp.zeros((pad, kpen_ref.shape[2]), F32)], axis=0).astype(BF16)
    s = _dot_nt(ql, cn) + _dot_nt(qp, kn)
    tok = lax.shift_right_logical(lax.broadcasted_iota(jnp.int32, s.shape, 0), int(math.log2(MLA_HEADS)))
    key = lax.broadcasted_iota(jnp.int32, s.shape, 1)
    s = jnp.where((key <= tok) & (key < t_new), s, neg)
    update(s, cn)
    o_ref[0] = (acc_scr[...] / l_scr[...]).astype(o_ref.dtype)


def _dattn_call(page_table, qlat, qpe, lat_new, kpe_new, cache_latent, cache_krope_t, t_new, group):
    Bs, R, kv_lora = qlat.shape
    n_pages = page_table.shape[1]
    assert n_pages % (2 * group) == 0, "the two buffer slots alternate, so a sequence needs an even group count"
    n_groups = n_pages // group
    page = cache_latent.shape[2]
    kern = functools.partial(_dattn_kernel, group=group, n_groups=n_groups, t_new=t_new, page=page)
    per_b = lambda shape: pl.BlockSpec((1,) + shape, lambda b, pt: (b, 0, 0))
    grid_spec = pltpu.PrefetchScalarGridSpec(
        num_scalar_prefetch=1,
        grid=(Bs,),
        in_specs=[per_b((R, kv_lora)), per_b((R, MLA_D_ROPE)),
                  per_b(lat_new.shape[1:]), per_b(kpe_new.shape[1:]),
                  pl.BlockSpec(memory_space=pl.ANY), pl.BlockSpec(memory_space=pl.ANY)],
        out_specs=per_b((R, kv_lora)),
        scratch_shapes=[pltpu.VMEM((2, group * page, kv_lora), F32),
                        pltpu.VMEM((2, MLA_D_ROPE, group * page), F32),
                        pltpu.SemaphoreType.DMA((2,)), pltpu.SemaphoreType.DMA((2,)),
                        pltpu.VMEM((R, 1), F32), pltpu.VMEM((R, 1), F32), pltpu.VMEM((R, kv_lora), F32)],
    )
    return pl.pallas_call(
        kern,
        grid_spec=grid_spec,
        out_shape=jax.ShapeDtypeStruct((Bs, R, kv_lora), BF16),
        compiler_params=pltpu.CompilerParams(dimension_semantics=("arbitrary",),
                                             vmem_limit_bytes=VMEM_LIMIT),
        name="dattn",
    )(page_table, qlat, qpe, lat_new, kpe_new, cache_latent, cache_krope_t)


def _post_kernel(*refs, absorbed_values):
    if absorbed_values:
        h_ref, og_ref, om_ref, wvbd_ref, wo1_ref, wo2_ref, g2_ref, wup_ref, wdn_ref, gf_ref, y_ref = refs
        om = _dot(om_ref[...], wvbd_ref[...]).astype(BF16)
    else:
        h_ref, og_ref, om_ref, wo1_ref, wo2_ref, g2_ref, wup_ref, wdn_ref, gf_ref, y_ref = refs
        om = om_ref[...]
    h2 = h_ref[...] + _dot(og_ref[...], wo1_ref[...]) + _dot(om, wo2_ref[...])
    hn = _rms(h2, g2_ref[...]).astype(BF16)
    u = jnp.maximum(_dot(hn, wup_ref[...]), 0.0)
    h3 = h2 + _dot((u * u).astype(BF16), wdn_ref[...])
    y_ref[...] = _rms(h3, gf_ref[...])


def _post_call(h2d, og, om, consts, tm, absorbed_values, name):
    n_rows, d_model = h2d.shape
    assert n_rows % tm == 0
    row_spec = lambda w: pl.BlockSpec((tm, w), lambda i: (i, 0))
    kern = functools.partial(_post_kernel, absorbed_values=absorbed_values)
    return pl.pallas_call(
        kern,
        grid=(n_rows // tm,),
        in_specs=[row_spec(d_model), row_spec(og.shape[1]), row_spec(om.shape[1])]
                 + [_const_spec(c.shape) for c in consts],
        out_specs=row_spec(d_model),
        out_shape=jax.ShapeDtypeStruct((n_rows, d_model), F32),
        compiler_params=pltpu.CompilerParams(dimension_semantics=("arbitrary",),
                                             vmem_limit_bytes=VMEM_LIMIT),
        name=name,
    )(h2d, og, om, *consts)


def _rope_parts(pos):
    half = MLA_D_ROPE // 2
    inv = ROPE_THETA ** (-jnp.arange(half, dtype=F32) / half)
    ang = pos.astype(F32)[:, None] * inv[None, :]
    cos = jnp.concatenate([jnp.cos(ang), jnp.cos(ang)], -1)
    sin_signed = jnp.concatenate([-jnp.sin(ang), jnp.sin(ang)], -1)
    return cos, sin_signed


def _pad_lanes(x, width):
    return jnp.pad(x, ((0, 0), (0, width - x.shape[1])))


def _key_tables(cos, sin_signed):
    return _pad_lanes(cos, LANES), _pad_lanes(sin_signed, LANES)


def _prompt_tables(pos):
    cos, sin_signed = _rope_parts(pos)
    n = pos.shape[0]
    scale = MLA_SCALE * math.log2(math.e)
    qcos = jnp.concatenate([jnp.full((n, MLA_D_NOPE), scale, F32), cos * scale], -1)
    qsin = jnp.concatenate([jnp.zeros((n, MLA_D_NOPE), F32), sin_signed * scale], -1)
    kcos, ksin = _key_tables(cos, sin_signed)
    return jnp.concatenate([_pad_lanes(qcos, LANES), _pad_lanes(qsin, LANES), kcos, ksin], -1)


def _sample_tables(pos):
    cos, sin_signed = _rope_parts(pos)
    kcos, ksin = _key_tables(cos, sin_signed)
    return jnp.concatenate([jnp.tile(cos * MLA_SCALE, (1, MLA_HEADS)),
                            jnp.tile(sin_signed * MLA_SCALE, (1, MLA_HEADS)), kcos, ksin], -1)


def _swap_halves(w):
    half = MLA_D_ROPE // 2
    return jnp.concatenate([w[..., half:], w[..., :half]], -1)


def kernel(x_prompt, x_sample, cache_latent, cache_krope, state_conv, state_ssm, page_table,
           meta_tokens, norm_mix_g, w_in, conv_w, a_log, dt_bias, gdn_norm_g, q_norm_g, w_q_b,
           kv_norm_g, w_kv_b, w_out, norm_mlp_g, w_up, w_down, final_norm_g):
    assert w_in.shape[0] == 1, "single-layer problem"
    B, T, D = x_prompt.shape
    Bs, Ts, _ = x_sample.shape
    assert CONV_W - 1 <= Ts <= SAMPLE_T_PAD
    past_len = page_table.shape[1] * cache_latent.shape[2]
    q_lora = q_norm_g.shape[1]
    kv_lora = kv_norm_g.shape[1]
    H = MLA_HEADS

    wi = w_in[0]
    c0 = CONV_CH + GDN_V_W
    w1 = wi[:, :c0].astype(BF16)
    wba = _pad_lanes(wi[:, c0:c0 + 2 * GDN_HEADS], LANES).astype(BF16)
    c1 = c0 + 2 * GDN_HEADS
    wcq = wi[:, c1:c1 + q_lora].astype(BF16)
    wckv = wi[:, c1 + q_lora:c1 + q_lora + kv_lora].astype(BF16)
    wkpe_raw = wi[:, c1 + q_lora + kv_lora:]
    wkpe = _pad_lanes(wkpe_raw, LANES).astype(BF16)
    wkpesw = _pad_lanes(_swap_halves(wkpe_raw), LANES).astype(BF16)
    gmix = norm_mix_g[0][None, :]
    qg = q_norm_g[0][None, :]
    kvg = kv_norm_g[0][None, :]
    wq = w_q_b[0]
    wq_nope, wq_pe = wq[..., :MLA_D_NOPE], wq[..., MLA_D_NOPE:]
    wq_pesw = _swap_halves(wq_pe)
    zq = lambda n: jnp.zeros((q_lora, H, n), F32)
    wqcat = jnp.concatenate([wq_nope, wq_pe, zq(HEAD_PAD - MLA_D_NOPE - MLA_D_ROPE)], -1)
    wqsw = jnp.concatenate([zq(MLA_D_NOPE), wq_pesw, zq(HEAD_PAD - MLA_D_NOPE - MLA_D_ROPE)], -1)
    wqcat = wqcat.reshape(q_lora, H * HEAD_PAD).astype(BF16)
    wqsw = wqsw.reshape(q_lora, H * HEAD_PAD).astype(BF16)
    wkv = w_kv_b[0]
    wkb, wvb = wkv[..., :MLA_D_NOPE], wkv[..., MLA_D_NOPE:]
    zk = lambda n: jnp.zeros((kv_lora, H, n), F32)
    wkb_sp = jnp.concatenate([wkb, zk(HEAD_PAD - MLA_D_NOPE)], -1).reshape(kv_lora, H * HEAD_PAD).astype(BF16)
    odd = (jnp.arange(H) % 2 == 1)[None, :, None]
    wvb_sp = jnp.where(odd, jnp.concatenate([zk(HEAD_PAD - MLA_D_V), wvb], -1),
                       jnp.concatenate([wvb, zk(HEAD_PAD - MLA_D_V)], -1))
    wvb_sp_t = wvb_sp.reshape(kv_lora, H * HEAD_PAD).T.astype(BF16)
    jj =jnp.arange(LANES)[:, None]
    ll = jnp.arange(H * HEAD_PAD)[None, :]
    esel = ((ll % HEAD_PAD == jj + MLA_D_NOPE) & (jj < MLA_D_ROPE)).astype(BF16)
    eye_h = jnp.eye(H, dtype=bool)
    wabs = jnp.where(eye_h[:, None, :, None], wkb.transpose(1, 2, 0)[:, :, None, :], 0.0)
    wabs = wabs.reshape(H * MLA_D_NOPE, H * kv_lora).astype(BF16)
    wvbd = jnp.where(eye_h[:, None, :, None], wvb.transpose(1, 0, 2)[:, :, None, :], 0.0)
    wvbd = wvbd.reshape(H * kv_lora, H * MLA_D_V).astype(BF16)
    wo1 = w_out[0][:GDN_V_W].astype(BF16)
    wo2 = w_out[0][GDN_V_W:].astype(BF16)
    g2 = norm_mlp_g[0][None, :]
    wup = w_up[0].astype(BF16)
    wdn = w_down[0].astype(BF16)
    gf = final_norm_g[None, :]
    cw = jnp.pad(conv_w[0], ((0, SUBLANES - CONV_W), (0, 0)))
    lane_row = lambda v: jnp.pad(v[None, :], ((0, 0), (GDN_HEADS, LANES - 2 * GDN_HEADS)))
    alog = lane_row(a_log[0])
    dtb = lane_row(dt_bias[0])
    ng = gdn_norm_g[0][None, :]

    common = (gmix, w1, wba, wcq, wckv, wkpe, wkpesw, qg, kvg)
    slot_row = jnp.arange(H * HEAD_PAD) % HEAD_PAD
    vones = (slot_row == _denominator_row(jnp.arange(H * HEAD_PAD) // HEAD_PAD)).astype(F32)[:, None]
    prompt_consts = common + (wqcat, wqsw, wkb_sp, wvb_sp_t, esel, vones)
    sample_consts = common + (wq_nope.reshape(q_lora, -1).astype(BF16), wq_pe.reshape(q_lora, -1).astype(BF16),
                              wq_pesw.reshape(q_lora, -1).astype(BF16), wabs)
    pw = [CONV_CH, GDN_V_W, LANES, kv_lora, MLA_D_ROPE, H * HEAD_PAD, H * HEAD_PAD, H * HEAD_PAD]
    pd = [F32, F32, F32, F32, F32, BF16, BF16, BF16]

    tab_m = _prompt_tables(jnp.arange(N_META))
    qkv_m, z_m, ba_m, lat_m, kpe_m, _, kcat_m, vt_m = _proj_call(
        _proj_prompt_kernel, meta_tokens, tab_m, prompt_consts, pw, pd, N_META, "proj_meta", transposed_outs=(7,))
    zero_conv = jnp.zeros((1, SUBLANES, CONV_CH), F32)
    zero_s = jnp.zeros((1, GDN_HEADS, GDN_DK, GDN_DV), F32)
    _, s1 = _gdn_call(qkv_m[None], z_m[None], ba_m[None], zero_conv, zero_s, cw, alog, dtb, ng,
                      N_META, N_META, "gdn_meta")
    conv1 = jnp.pad(qkv_m[N_META - (CONV_W - 1):], ((SUBLANES - (CONV_W - 1), 0), (0, 0)))[None]
    k_meta = jnp.pad(kcat_m, ((0, LANES - N_META), (0, 0)))
    vt_meta = jnp.pad(vt_m, ((0, 0), (0, LANES - N_META)))

    tab_p = _prompt_tables(N_META + jnp.arange(T))
    xp2d = x_prompt.reshape(B * T, D)
    qkv_p, z_p, ba_p, lat_p, kpe_p, qcat_p, kcat_p, vt_p = _proj_call(
        _proj_prompt_kernel, xp2d, tab_p, prompt_consts, pw, pd, 256, "proj_prompt", transposed_outs=(7,))
    r3 = lambda a: a.reshape(B, T, a.shape[-1])
    qkv_p3 = r3(qkv_p)
    og_p, s2 = _gdn_call(qkv_p3, r3(z_p), r3(ba_p), conv1, s1, cw, alog, dtb, ng, GDN_CHUNK, T, "gdn_prompt",
                         nb_intra=4, nb_scan=8)
    om_p = _attn_call(r3(qcat_p), r3(kcat_p), vt_p, k_meta, vt_meta, 256, 8)
    post_consts = (wo1, wo2, g2, wup, wdn, gf)
    y_p = _post_call(xp2d, og_p.reshape(B * T, -1), om_p.reshape(B * T, -1), post_consts, 512, False, "post_prompt")

    tp = SAMPLE_T_PAD
    xs = jnp.pad(x_sample, ((0, 0), (0, tp - Ts), (0, 0))).reshape(Bs * tp, D)
    tab_s = jnp.tile(_sample_tables(past_len + jnp.arange(tp)), (Bs, 1))
    sw = [CONV_CH, GDN_V_W, LANES, kv_lora, MLA_D_ROPE, H * kv_lora, H * MLA_D_ROPE]
    sd = [F32, F32, F32, F32, F32, BF16, BF16]
    qkv_s, z_s, ba_s, lat_s, kpe_s, qlat_s, qpe_s = _proj_call(
        _proj_sample_kernel, xs, tab_s, sample_consts, sw, sd, Bs * tp, "proj_sample")
    s3 = lambda a: a.reshape(Bs, tp, a.shape[-1])
    qkv_s3 = s3(qkv_s)
    conv_in_s = jnp.pad(state_conv[0], ((0, 0), (SUBLANES - (CONV_W - 1), 0), (0, 0)))
    og_s, s_new = _gdn_call(qkv_s3, s3(z_s), s3(ba_s), conv_in_s, state_ssm[0], cw, alog, dtb, ng, tp, Ts, "gdn_sample",
                            nb_intra=4, nb_scan=4)
    out_lat = _dattn_call(page_table, qlat_s.reshape(Bs, tp * H, kv_lora), qpe_s.reshape(Bs, tp * H, MLA_D_ROPE),
                          s3(lat_s), s3(kpe_s), cache_latent, jnp.swapaxes(cache_krope, 2, 3), Ts, 32)
    sample_post_consts = (wvbd,) + post_consts
    y_s = _post_call(xs, og_s.reshape(Bs * tp, -1), out_lat.reshape(Bs * tp, H * kv_lora),
                     sample_post_consts, Bs * tp, True, "post_sample")

    bc = lambda a: jnp.broadcast_to(a[None], (B,) + a.shape)
    lat_po = jnp.concatenate([bc(lat_m), r3(lat_p)], axis=1)[None]
    kpe_po = jnp.concatenate([bc(kpe_m), r3(kpe_p)], axis=1)[None]
    conv_po = qkv_p3[:, T - (CONV_W - 1):][None]
    conv_so = qkv_s3[:, Ts - (CONV_W - 1):Ts][None]
    return (y_p.reshape(B, T, D), y_s.reshape(Bs, tp, D)[:, :Ts],
            lat_po, kpe_po, conv_po, s2[None],
            s3(lat_s)[:, :Ts][None], s3(kpe_s)[:, :Ts][None], conv_so, s_new[None])
```

```python
import functools
import math

import jax
import jax.numpy as jnp
from jax import lax
from jax.experimental import pallas as pl
from jax.experimental.pallas import tpu as pltpu

F32 = jnp.float32
BF16 = jnp.bfloat16

N_META = 16
EPS = 1e-6
GDN_HEADS = 8
GDN_DK = 64
GDN_DV = 64
CONV_W = 4
GDN_CHUNK = 64
MLA_HEADS = 8
MLA_D_NOPE = 64
MLA_D_ROPE = 32
MLA_D_V = 64
ROPE_THETA = 10000.0
MLA_SCALE = (MLA_D_NOPE + MLA_D_ROPE) ** -0.5
Q_SCALE_LOG2 = MLA_SCALE * math.log2(math.e)
GDN_QK_W = GDN_HEADS * GDN_DK
GDN_V_W = GDN_HEADS * GDN_DV
CONV_CH = 2 * GDN_QK_W + GDN_V_W

LANES = 128
SUBLANES = 8
HEAD_PAD = 128
SAMPLE_T_PAD = 8
VMEM_LIMIT = 56 * 1024 * 1024


def _dot(a, b):
    return jnp.dot(a, b, preferred_element_type=F32)


def _dot_nt(a, b):
    return lax.dot_general(a, b, (((1,), (1,)), ((), ())), preferred_element_type=F32)


def _rms(x, g):
    return x * lax.rsqrt(jnp.mean(x * x, axis=-1, keepdims=True) + EPS) * g


def _sigmoid(x):
    return 1.0 / (1.0 + jnp.exp(-x))


def _const_spec(shape):
    nd = len(shape)
    return pl.BlockSpec(shape, lambda *_: (0,) * nd)


def _proj_common(x_ref, gmix_ref, w1_ref, wba_ref, wcq_ref, wckv_ref, wkpe_ref, wkpesw_ref,
                 qg_ref, kvg_ref, kcos, ksin, qkv_ref, z_ref, ba_ref, lat_ref, kpe_ref):
    hn = _rms(x_ref[...], gmix_ref[...]).astype(BF16)
    qkvz = _dot(hn, w1_ref[...])
    qkv_ref[...] = qkvz[:, :CONV_CH]
    z_ref[...] = qkvz[:, CONV_CH:]
    ba_ref[...] = _dot(hn, wba_ref[...])
    cq = _dot(hn, wcq_ref[...])
    ckv = _dot(hn, wckv_ref[...])
    kpe = _dot(hn, wkpe_ref[...]) * kcos + _dot(hn, wkpesw_ref[...]) * ksin
    kpe_ref[...] = kpe[:, :MLA_D_ROPE]
    lat = _rms(ckv, kvg_ref[...])
    lat_ref[...] = lat
    cqn = _rms(cq, qg_ref[...]).astype(BF16)
    return cqn, lat.astype(BF16), kpe.astype(BF16)


def _proj_prompt_kernel(x_ref, tab_ref, gmix_ref, w1_ref, wba_ref, wcq_ref, wckv_ref, wkpe_ref,
                        wkpesw_ref, qg_ref, kvg_ref, wq_ref, wqsw_ref, wkb_ref, wvb_ref, esel_ref, vones_ref,
                        qkv_ref, z_ref, ba_ref, lat_ref, kpe_ref, qcat_ref, kcat_ref, vt_ref):
    qcos = tab_ref[:, 0 * LANES:1 * LANES]
    qsin = tab_ref[:, 1 * LANES:2 * LANES]
    kcos = tab_ref[:, 2 * LANES:3 * LANES]
    ksin = tab_ref[:, 3 * LANES:4 * LANES]
    cqn, latb, kpeb = _proj_common(x_ref, gmix_ref, w1_ref, wba_ref, wcq_ref, wckv_ref, wkpe_ref,
                                   wkpesw_ref, qg_ref, kvg_ref, kcos, ksin,
                                   qkv_ref, z_ref, ba_ref, lat_ref, kpe_ref)
    qa = _dot(cqn, wq_ref[...])
    qb = _dot(cqn, wqsw_ref[...])
    for h in range(MLA_HEADS):
        sl = slice(h * HEAD_PAD, (h + 1) * HEAD_PAD)
        qcat_ref[:, sl] = (qa[:, sl] * qcos + qb[:, sl] * qsin).astype(BF16)
    kcat_ref[...] = (_dot(latb, wkb_ref[...]) + _dot(kpeb, esel_ref[...])).astype(BF16)
    vt_ref[...] = (_dot_nt(wvb_ref[...], latb) + vones_ref[...]).astype(BF16)


def _proj_sample_kernel(x_ref, tab_ref, gmix_ref, w1_ref, wba_ref, wcq_ref, wckv_ref, wkpe_ref,
                        wkpesw_ref, qg_ref, kvg_ref, wqn_ref, wqpe_ref, wqpesw_ref, wabs_ref,
                        qkv_ref, z_ref, ba_ref, lat_ref, kpe_ref, qlat_ref, qpe_ref):
    pe_w = MLA_HEADS * MLA_D_ROPE
    qcos = tab_ref[:, 0:pe_w]
    qsin = tab_ref[:, pe_w:2 * pe_w]
    kcos = tab_ref[:, 2 * pe_w:2 * pe_w + LANES]
    ksin = tab_ref[:, 2 * pe_w + LANES:2 * pe_w + 2 * LANES]
    cqn, _, _ = _proj_common(x_ref, gmix_ref, w1_ref, wba_ref, wcq_ref, wckv_ref, wkpe_ref,
                             wkpesw_ref, qg_ref, kvg_ref, kcos, ksin,
                             qkv_ref, z_ref, ba_ref, lat_ref, kpe_ref)
    qn = _dot(cqn, wqn_ref[...]).astype(BF16)
    qlat_ref[...] = (_dot(qn, wabs_ref[...]) * Q_SCALE_LOG2).astype(BF16)
    qpe_ref[...] = (_dot(cqn, wqpe_ref[...]) * qcos + _dot(cqn, wqpesw_ref[...]) * qsin).astype(BF16)


def _proj_call(kernel_fn, x2d, tab, consts, out_widths, out_dtypes, tm, name, transposed_outs=()):
    n_rows, d_model = x2d.shape
    assert n_rows % tm == 0
    row_spec = lambda w: pl.BlockSpec((tm, w), lambda i: (i, 0))
    col_spec = lambda w: pl.BlockSpec((w, tm), lambda i: (0, i))
    is_t = [k in transposed_outs for k in range(len(out_widths))]
    assert tab.shape[0] % tm == 0 and n_rows % tab.shape[0] == 0
    tab_blocks = tab.shape[0] // tm
    tab_spec = pl.BlockSpec((tm, tab.shape[1]), lambda i: (i % tab_blocks, 0))
    in_specs = [row_spec(d_model), tab_spec] + [_const_spec(c.shape) for c in consts]
    return pl.pallas_call(
        kernel_fn,
        grid=(n_rows // tm,),
        in_specs=in_specs,
        out_specs=[col_spec(w) if t else row_spec(w) for w, t in zip(out_widths, is_t)],
        out_shape=[jax.ShapeDtypeStruct((w, n_rows) if t else (n_rows, w), dt)
                   for w, dt, t in zip(out_widths, out_dtypes, is_t)],
        compiler_params=pltpu.CompilerParams(dimension_semantics=("arbitrary",),
                                             vmem_limit_bytes=VMEM_LIMIT),
        name=name,
    )(x2d, tab, *consts)


def _block_diag2(x, half):
    lane = lax.broadcasted_iota(jnp.int32, x.shape, 1)
    zero = jnp.zeros_like(x)
    return jnp.concatenate([jnp.where(lane < half, x, zero), jnp.where(lane < half, zero, x)], axis=0)


def _gdn_intra_prep(bb, cinit_bb, qkv_ref, prev_ref, ba_ref, cinit_ref, cw_ref, alog_ref, dtb_ref,
                    qk_mask, kg_ref, qg_ref, kdt_ref, vb_ref, egl_ref, xbuf, *, chunk, valid_len, n_chunks):
    c = pl.program_id(1)
    C = chunk
    H = GDN_HEADS
    tail0 = SUBLANES - (CONV_W - 1)

    xbuf[bb, 0:SUBLANES, :] = jnp.where(c == 0, cinit_ref[cinit_bb], prev_ref[bb])
    xbuf[bb, SUBLANES:SUBLANES + C, :] = qkv_ref[bb]
    cw = cw_ref[...]
    y = xbuf[bb, tail0:tail0 + C, :] * cw[0:1, :]
    for j in range(1, CONV_W):
        y = y + xbuf[bb, tail0 + j:tail0 + j + C, :] * cw[j:j + 1, :]
    y = y * _sigmoid(y)

    ba = ba_ref[bb]
    beta_all = _sigmoid(ba)
    sp_in = ba + dtb_ref[...]
    softplus = jnp.maximum(sp_in, 0.0) + jnp.log1p(jnp.exp(-jnp.abs(sp_in)))
    g_all = -jnp.exp(alog_ref[...]) * softplus
    if valid_len < n_chunks * C:
        row = c * C + lax.broadcasted_iota(jnp.int32, (C, LANES), 0)
        beta_all = jnp.where(row < valid_len, beta_all, 0.0)
        g_all = jnp.where(row < valid_len, g_all, 0.0)

    ri = lax.broadcasted_iota(jnp.int32, (C, C), 0)
    ci = lax.broadcasted_iota(jnp.int32, (C, C), 1)
    lower = ri >= ci
    gc_all = lax.dot_general(lower.astype(F32), g_all, (((1,), (0,)), ((), ())),
                             precision=lax.Precision.HIGHEST, preferred_element_type=F32)
    sel = (lax.broadcasted_iota(jnp.int32, (SUBLANES, LANES), 1)
           == lax.broadcasted_iota(jnp.int32, (SUBLANES, LANES), 0) + GDN_HEADS).astype(F32)
    gc_t = lax.dot_general(sel, gc_all, (((1,), (1,)), ((), ())),
                           precision=lax.Precision.HIGHEST, preferred_element_type=F32)

    pairs = range(H // 2)
    lane = lax.broadcasted_iota(jnp.int32, (C, LANES), 1)
    lo = lane < GDN_DK
    lo_c, lower_p = qk_mask
    ii_r = lax.broadcasted_iota(jnp.int32, (GDN_DK, LANES), 0)
    ii_c = lax.broadcasted_iota(jnp.int32, (GDN_DK, LANES), 1)
    eye2 = ((ii_c == ii_r) | (ii_c == ii_r + GDN_DK)).astype(BF16)

    def col(x, idx):
        return jnp.sum(jnp.where(lane == idx, x, 0.0), axis=-1, keepdims=True)

    def half_sums(x):
        return jnp.where(lo, jnp.sum(jnp.where(lo, x, 0.0), axis=-1, keepdims=True),
                         jnp.sum(jnp.where(lo, 0.0, x), axis=-1, keepdims=True))

    kn_b, kb_b, qn_b, decay = [], [], [], []
    for j in pairs:
        ps = slice(j * LANES, (j + 1) * LANES)
        qp = y[:, j * LANES:(j + 1) * LANES]
        kp = y[:, GDN_QK_W + j * LANES:GDN_QK_W + (j + 1) * LANES]
        vp = y[:, 2 * GDN_QK_W + j * LANES:2 * GDN_QK_W + (j + 1) * LANES]
        qn = qp * lax.rsqrt(half_sums(qp * qp) + EPS) * (GDN_DK ** -0.5)
        kn = kp * lax.rsqrt(half_sums(kp * kp) + EPS)
        beta = jnp.where(lo, col(beta_all, 2 * j), col(beta_all, 2 * j + 1))
        gc0, gc1 = col(gc_all, H + 2 * j), col(gc_all, H + 2 * j + 1)
        gcol = jnp.where(lo, gc0, gc1)
        grow = jnp.concatenate([gc_t[2 * j:2 * j + 1, :], gc_t[2 * j + 1:2 * j + 2, :]], axis=1)
        diff = jnp.where(lo_c, gc0, gc1) - grow
        decay.append(jnp.where(lower_p, jnp.exp(jnp.where(lower_p, diff, 0.0)), 0.0))
        egc = jnp.exp(gcol)
        gl = gcol[C - 1:C, :]
        egl_ref[bb, 0, :, ps] = jnp.exp(gl)
        kb = kn * beta
        vb_ref[bb, 0, :, ps] = vp * beta
        kg_ref[bb, 0, :, ps] = (kb * egc).astype(BF16)
        qg_ref[bb, 0, :, ps] = (qn * egc).astype(BF16)
        k_dec = (kn * jnp.exp(gl - gcol)).astype(BF16)
        kdt_ref[bb, 0, j] = _dot_nt(eye2, _block_diag2(k_dec, GDN_DK)).astype(BF16)
        kn_b.append(_block_diag2(kn.astype(BF16), GDN_DK))
        kb_b.append(kb.astype(BF16))
        qn_b.append(qn.astype(BF16))
    return kn_b, kb_b, qn_b, decay


def _gdn_intra_kernel(qkv_ref, prev_ref, ba_ref, cinit_ref, cw_ref, alog_ref, dtb_ref,
                      t_ref, qk_ref, kg_ref, qg_ref, kdt_ref, vb_ref, egl_ref, xbuf,
                      *, chunk, valid_len, n_chunks, nb, shared_cinit):
    C = chunk
    rp = lax.broadcasted_iota(jnp.int32, (C, 2 * C), 0)
    cp = lax.broadcasted_iota(jnp.int32, (C, 2 * C), 1)
    lo_c = cp < C
    cmod = jnp.where(lo_c, cp, cp - C)
    lower_p, strict_p = rp >= cmod, rp > cmod
    eye_p = (rp == cmod).astype(F32)
    kn_b, kb_b, qn_b, decay = [], [], [], []
    for bb in range(nb):
        parts = _gdn_intra_prep(bb, 0 if shared_cinit else bb, qkv_ref, prev_ref, ba_ref, cinit_ref, cw_ref,
                                alog_ref, dtb_ref, (lo_c, lower_p), kg_ref, qg_ref, kdt_ref, vb_ref, egl_ref, xbuf,
                                chunk=chunk, valid_len=valid_len, n_chunks=n_chunks)
        for dst, src in zip((kn_b, kb_b, qn_b, decay), parts):
            dst.extend(src)
    n_pairs = GDN_HEADS // 2
    probs = range(nb * n_pairs)
    kk = [_dot_nt(kb_b[i], kn_b[i]) for i in probs]
    qk = [_dot_nt(qn_b[i], kn_b[i]) for i in probs]
    for i in probs:
        qk_ref[i // n_pairs, 0, i % n_pairs] = jnp.where(lower_p, qk[i] * decay[i], 0.0).astype(BF16)
    p = [jnp.where(strict_p, -(kk[i] * decay[i]), 0.0) for i in probs]
    t = [eye_p + p[i] for i in probs]
    n_iter = max(1, int(math.ceil(math.log2(C))))
    for _ in range(1, n_iter):
        pb = [p[i].astype(BF16) for i in probs]
        p = [_dot(pb[i], _block_diag2(pb[i], C)) for i in probs]
        t = [t[i] + _dot(t[i].astype(BF16), _block_diag2(p[i].astype(BF16), C)) for i in probs]
    for i in probs:
        t_ref[i // n_pairs, 0, i % n_pairs] = t[i].astype(BF16)


def _gdn_scan_kernel(t_ref, qk_ref, kg_ref, qg_ref, kdt_ref, vb_ref, egl_ref, z_ref, sinit_ref, ng_ref,
                     o_ref, sfin_ref, s_scr, *, chunk, n_chunks, nb, shared_sinit):
    c = pl.program_id(1)
    C = chunk
    n_pairs = GDN_HEADS // 2
    probs = [(bb, j) for bb in range(nb) for j in range(n_pairs)]
    ps = [slice(j * LANES, (j + 1) * LANES) for j in range(n_pairs)]
    lo = lax.broadcasted_iota(jnp.int32, (C, LANES), 1) < GDN_DV

    @pl.when(c == 0)
    def _():
        for bb, j in probs:
            si = 0 if shared_sinit else bb
            s_scr[bb, j] = jnp.concatenate([sinit_ref[si, 2 * j], sinit_ref[si, 2 * j + 1]], axis=1)

    s = [s_scr[bb, j] for bb, j in probs]
    sbd = [_block_diag2(x.astype(BF16), GDN_DV) for x in s]
    ks = [_dot(kg_ref[bb, 0, :, ps[j]], sbd[i]) for i, (bb, j) in enumerate(probs)]
    qs = [_dot(qg_ref[bb, 0, :, ps[j]], sbd[i]) for i, (bb, j) in enumerate(probs)]
    r = [(vb_ref[bb, 0, :, ps[j]] - ks[i]).astype(BF16) for i, (bb, j) in enumerate(probs)]
    v_new = [_dot(t_ref[bb, 0, j], _block_diag2(r[i], GDN_DV)).astype(BF16) for i, (bb, j) in enumerate(probs)]
    vbd = [_block_diag2(x, GDN_DV) for x in v_new]
    o = [qs[i] + _dot(qk_ref[bb, 0, j], vbd[i]) for i, (bb, j) in enumerate(probs)]
    for i, (bb, j) in enumerate(probs):
        s_scr[bb, j] = s[i] * egl_ref[bb, 0, :, ps[j]] + _dot(kdt_ref[bb, 0, j], vbd[i])
    ng = ng_ref[...]
    for i, (bb, j) in enumerate(probs):
        o2 = o[i] * o[i]
        ms = jnp.where(lo, jnp.sum(jnp.where(lo, o2, 0.0), axis=-1, keepdims=True),
                       jnp.sum(jnp.where(lo, 0.0, o2), axis=-1, keepdims=True)) * (1.0 / GDN_DV)
        zp = z_ref[bb, :, ps[j]]
        o_ref[bb, :, ps[j]] = (o[i] * lax.rsqrt(ms + EPS) * ng * (zp * _sigmoid(zp))).astype(o_ref.dtype)

    @pl.when(c == n_chunks - 1)
    def _():
        for bb, j in probs:
            s_pair = s_scr[bb, j]
            sfin_ref[bb, 2 * j] = s_pair[:, :GDN_DV]
            sfin_ref[bb, 2 * j + 1] = s_pair[:, GDN_DV:]


def _gdn_call(qkv, z, ba, conv_init, s_init, cw, alog, dtb, ng, chunk, valid_len, name, nb_intra=1, nb_scan=1):
    B, T, _ = qkv.shape
    assert T % chunk == 0 and chunk % SUBLANES == 0 and GDN_DK == GDN_DV and 2 * GDN_DV == LANES
    n_chunks = T // chunk
    H, C = GDN_HEADS, chunk
    HP = H // 2
    shared_c = conv_init.shape[0] == 1
    shared_s = s_init.shape[0] == 1
    ng = jnp.tile(ng, (1, 2))
    params = pltpu.CompilerParams(dimension_semantics=("arbitrary", "arbitrary"), vmem_limit_bytes=VMEM_LIMIT)
    inter_dims = [(HP, C, 2 * C), (HP, C, 2 * C), (C, GDN_QK_W), (C, GDN_QK_W), (HP, GDN_DK, 2 * C),
                  (C, GDN_V_W), (1, GDN_V_W)]
    inter_dtypes = [BF16, BF16, BF16, BF16, BF16, F32, F32]

    def specs(nb):
        tok = lambda w: pl.BlockSpec((nb, chunk, w), lambda b, c: (b, c, 0))
        per_chunk = lambda *dims: pl.BlockSpec((nb, 1) + dims, lambda b, c: (b, c) + (0,) * len(dims))
        per_seq = lambda shared, *dims: pl.BlockSpec(
            ((1 if shared else nb),) + dims, lambda b, c: ((0 if shared else b),) + (0,) * len(dims))
        return tok, per_chunk, per_seq

    nb = nb_intra
    assert B % nb == 0
    tok, per_chunk, per_seq = specs(nb)
    prev_rows = pl.BlockSpec((nb, SUBLANES, CONV_CH),
                             lambda b, c: (b, jnp.maximum(c * (C // SUBLANES) - 1, 0), 0))
    inter = pl.pallas_call(
        functools.partial(_gdn_intra_kernel, chunk=chunk, valid_len=valid_len, n_chunks=n_chunks,
                          nb=nb, shared_cinit=shared_c),
        grid=(B // nb, n_chunks),
        in_specs=[
            tok(CONV_CH), prev_rows, tok(LANES), per_seq(shared_c, SUBLANES, CONV_CH),
            _const_spec(cw.shape), _const_spec(alog.shape), _const_spec(dtb.shape),
        ],
        out_specs=[per_chunk(*d) for d in inter_dims],
        out_shape=[jax.ShapeDtypeStruct((B, n_chunks) + d, dt) for d, dt in zip(inter_dims, inter_dtypes)],
        scratch_shapes=[pltpu.VMEM((nb, SUBLANES + chunk, CONV_CH), F32)],
        compiler_params=params,
        name=name + "_intra",
    )(qkv, qkv, ba, conv_init, cw, alog, dtb)

    nb = nb_scan
    assert B % nb == 0
    tok, per_chunk, per_seq = specs(nb)
    o, s_fin = pl.pallas_call(
        functools.partial(_gdn_scan_kernel, chunk=chunk, n_chunks=n_chunks, nb=nb, shared_sinit=shared_s),
        grid=(B // nb, n_chunks),
        in_specs=[per_chunk(*d) for d in inter_dims] + [
            tok(GDN_V_W), per_seq(shared_s, H, GDN_DK, GDN_DV), _const_spec(ng.shape),
        ],
        out_specs=[tok(GDN_V_W), per_seq(False, H, GDN_DK, GDN_DV)],
        out_shape=[
            jax.ShapeDtypeStruct((B, T, GDN_V_W), BF16),
            jax.ShapeDtypeStruct((B, H, GDN_DK, GDN_DV), F32),
        ],
        scratch_shapes=[pltpu.VMEM((nb, HP, GDN_DK, LANES), F32)],
        compiler_params=params,
        name=name + "_scan",
    )(*inter, z, s_init, ng)
    return o, s_fin


def _denominator_row(h):
    return MLA_D_V * (1 - h % 2)


def _attn_kernel(q_ref, k_ref, vt_ref, km_ref, vmt_ref, o_ref, m_scr, acc_scr, *, tq, n_meta, hps):
    qi = pl.program_id(2)
    neg = -jnp.inf
    key_m = lax.broadcasted_iota(jnp.int32, (km_ref.shape[0], tq), 0)
    key_d = lax.broadcasted_iota(jnp.int32, (tq, tq), 0)
    qry_d = lax.broadcasted_iota(jnp.int32, (tq, tq), 1)
    heads = range(hps)
    ls = [slice(h * HEAD_PAD, (h + 1) * HEAD_PAD) for h in heads]

    s = [jnp.where(key_m < n_meta, _dot_nt(km_ref[:, ls[h]], q_ref[0, :, ls[h]]), neg) for h in heads]
    m = [jnp.max(s[h], axis=0, keepdims=True) for h in heads]
    p = [jnp.exp2(s[h] - m[h]) for h in heads]
    for h in heads:
        m_scr[h] = m[h]
        acc_scr[h] = _dot(vmt_ref[ls[h], :], p[h].astype(BF16))

    def block(start, diagonal):
        s = [_dot_nt(k_ref[0, pl.ds(start, tq), ls[h]], q_ref[0, :, ls[h]]) for h in heads]
        if diagonal:
            s = [jnp.where(key_d <= qry_d, s[h], neg) for h in heads]
        m_old = [m_scr[h] for h in heads]
        m_new = [jnp.maximum(m_old[h], jnp.max(s[h], axis=0, keepdims=True)) for h in heads]
        alpha = [jnp.exp2(m_old[h] - m_new[h]) for h in heads]
        p = [jnp.exp2(s[h] - m_new[h]) for h in heads]
        pv = [_dot(vt_ref[ls[h], pl.ds(start, tq)], p[h].astype(BF16)) for h in heads]
        for h in heads:
            acc_scr[h] = alpha[h] * acc_scr[h] + pv[h]
            m_scr[h] = m_new[h]

    def body(kb, carry):
        block(pl.multiple_of(kb * tq, tq), False)
        return carry

    lax.fori_loop(0, qi, body, 0)
    block(pl.multiple_of(qi * tq, tq), True)
    slot_row = lax.broadcasted_iota(jnp.int32, (HEAD_PAD, tq), 0)

    def normalised(h):
        acc = acc_scr[h]
        r = _denominator_row(h)
        return jnp.where(slot_row == r, 0.0, acc) / acc[r:r + 1, :]

    for j in range(hps // 2):
        pair = normalised(2 * j) + normalised(2 * j + 1)
        o_ref[0, :, j * LANES:(j + 1) * LANES] = pair.T.astype(o_ref.dtype)


def _attn_call(qcat, kcat, vt, k_meta, vt_meta, tq, hps):
    B, T, _ = qcat.shape
    assert MLA_HEADS % hps == 0 and hps % 2 == 0
    n_groups = MLA_HEADS // hps
    gw = hps * HEAD_PAD
    ow = (hps // 2) * LANES
    kern = functools.partial(_attn_kernel, tq=tq, n_meta=N_META, hps=hps)
    return pl.pallas_call(
        kern,
        grid=(B, n_groups, T // tq),
        in_specs=[
            pl.BlockSpec((1, tq, gw), lambda b, g, qi: (b, qi, g)),
            pl.BlockSpec((1, T, gw), lambda b, g, qi: (b, 0, g)),
            pl.BlockSpec((gw, T), lambda b, g, qi: (g, b)),
            pl.BlockSpec((k_meta.shape[0], gw), lambda b, g, qi: (0, g)),
            pl.BlockSpec((gw, vt_meta.shape[1]), lambda b, g, qi: (g, 0)),
        ],
        out_specs=pl.BlockSpec((1, tq, ow), lambda b, g, qi: (b, qi, g)),
        out_shape=jax.ShapeDtypeStruct((B, T, n_groups * ow), BF16),
        scratch_shapes=[pltpu.VMEM((hps, 1, tq), F32), pltpu.VMEM((hps, HEAD_PAD, tq), F32)],
        compiler_params=pltpu.CompilerParams(dimension_semantics=("arbitrary", "arbitrary", "arbitrary"),
                                             vmem_limit_bytes=VMEM_LIMIT),
        name="attn",
    )(qcat, kcat, vt, k_meta, vt_meta)


def _dattn_kernel(pt_ref, qlat_ref, qpe_ref, latn_ref, kpen_ref, lat_hbm, kpe_hbm, o_ref,
                  lat_buf, kpe_buf, sem_lat, sem_kpe, m_scr, l_scr, acc_scr, *, group, n_groups, t_new, page):
    b = pl.program_id(0)
    n_seq = pl.num_programs(0)
    neg = -jnp.inf

    def page_copies(seq, g, k, slot):
        pid = pt_ref[seq, g * group + k]
        rows = pl.ds(pl.multiple_of(k * page, page), page)
        return (pltpu.make_async_copy(lat_hbm.at[0, pid], lat_buf.at[slot, rows, :], sem_lat.at[slot]),
                pltpu.make_async_copy(kpe_hbm.at[0, pid], kpe_buf.at[slot, k], sem_kpe.at[slot]))

    def start_group(seq, g, slot):
        def body(k2, carry):
            for prio in range(2):
                for cp in page_copies(seq, g, 2 * k2 + prio, slot):
                    cp.start(priority=prio)
            return carry
        lax.fori_loop(0, group // 2, body, 0)

    def wait_group(seq, g, slot):
        def body(k, carry):
            for cp in page_copies(seq, g, k, slot):
                cp.wait()
            return carry
        lax.fori_loop(0, group, body, 0)

    @pl.when(b == 0)
    def _():
        start_group(0, 0, 0)

    m_scr[...] = jnp.full(m_scr.shape, neg, F32)
    l_scr[...] = jnp.zeros(l_scr.shape, F32)
    acc_scr[...] = jnp.zeros(acc_scr.shape, F32)

    ql = qlat_ref[0]
    qp = qpe_ref[0]

    def update(s, vals):
        m = m_scr[...]
        m_new = jnp.maximum(m, jnp.max(s, axis=-1, keepdims=True))
        alpha = jnp.exp2(m - m_new)
        p = jnp.exp2(s - m_new)
        l_scr[...] = alpha * l_scr[...] + jnp.sum(p, axis=-1, keepdims=True)
        acc_scr[...] = alpha * acc_scr[...] + _dot(p.astype(BF16), vals)
        m_scr[...] = m_new

    for g in range(n_groups):
        slot = g % 2
        if g + 1 < n_groups:
            start_group(b, g + 1, 1 - slot)
        else:
            @pl.when(b + 1 < n_seq)
            def _():
                start_group(b + 1, 0, 1 - slot)
        wait_group(b, g, slot)
        cb = lat_buf[slot].astype(BF16)
        kbt = jnp.concatenate([kpe_buf[slot, k] for k in range(group)], axis=1).astype(BF16)
        update(_dot_nt(ql, cb) + _dot(qp, kbt), cb)

    pad = 2 * SUBLANES - latn_ref.shape[1]
    cn = jnp.concatenate([latn_ref[0], jnp.zeros((pad, latn_ref.shape[2]), F32)], axis=0).astype(BF16)
    kn = jnp.concatenate([kpen_ref[0], jnp.zeros((pad, kpen_ref.shape[2]), F32)], axis=0).astype(BF16)
    s = _dot_nt(ql, cn) + _dot_nt(qp, kn)
    tok = lax.shift_right_logical(lax.broadcasted_iota(jnp.int32, s.shape, 0), int(math.log2(MLA_HEADS)))
    key = lax.broadcasted_iota(jnp.int32, s.shape, 1)
    s = jnp.where((key <= tok) & (key < t_new), s, neg)
    update(s, cn)
    o_ref[0] = (acc_scr[...] / l_scr[...]).astype(o_ref.dtype)


def _dattn_call(page_table, qlat, qpe, lat_new, kpe_new, cache_latent, cache_krope_t, t_new, group):
    Bs, R, kv_lora = qlat.shape
    n_pages = page_table.shape[1]
    assert n_pages % (2 * group) == 0, "the two buffer slots alternate, so a sequence needs an even group count"
    n_groups = n_pages // group
    page = cache_latent.shape[2]
    kern = functools.partial(_dattn_kernel, group=group, n_groups=n_groups, t_new=t_new, page=page)
    per_b = lambda shape: pl.BlockSpec((1,) + shape, lambda b, pt: (b, 0, 0))
    grid_spec = pltpu.PrefetchScalarGridSpec(
        num_scalar_prefetch=1,
        grid=(Bs,),
        in_specs=[per_b((R, kv_lora)), per_b((R, MLA_D_ROPE)),
                  per_b(lat_new.shape[1:]), per_b(kpe_new.shape[1:]),
                  pl.BlockSpec(memory_space=pl.ANY), pl.BlockSpec(memory_space=pl.ANY)],
        out_specs=per_b((R, kv_lora)),
        scratch_shapes=[pltpu.VMEM((2, group * page, kv_lora), F32),
                        pltpu.VMEM((2, group, MLA_D_ROPE, page), F32),
                        pltpu.SemaphoreType.DMA((2,)), pltpu.SemaphoreType.DMA((2,)),
                        pltpu.VMEM((R, 1), F32), pltpu.VMEM((R, 1), F32), pltpu.VMEM((R, kv_lora), F32)],
    )
    return pl.pallas_call(
        kern,
        grid_spec=grid_spec,
        out_shape=jax.ShapeDtypeStruct((Bs, R, kv_lora), BF16),
        compiler_params=pltpu.CompilerParams(dimension_semantics=("arbitrary",),
                                             vmem_limit_bytes=VMEM_LIMIT),
        name="dattn",
    )(page_table, qlat, qpe, lat_new, kpe_new, cache_latent, cache_krope_t)


def _post_kernel(*refs, absorbed_values):
    if absorbed_values:
        h_ref, og_ref, om_ref, wvbd_ref, wo1_ref, wo2_ref, g2_ref, wup_ref, wdn_ref, gf_ref, y_ref = refs
        om = _dot(om_ref[...], wvbd_ref[...]).astype(BF16)
    else:
        h_ref, og_ref, om_ref, wo1_ref, wo2_ref, g2_ref, wup_ref, wdn_ref, gf_ref, y_ref = refs
        om = om_ref[...]
    h2 = h_ref[...] + _dot(og_ref[...], wo1_ref[...]) + _dot(om, wo2_ref[...])
    hn = _rms(h2, g2_ref[...]).astype(BF16)
    u = jnp.maximum(_dot(hn, wup_ref[...]), 0.0)
    h3 = h2 + _dot((u * u).astype(BF16), wdn_ref[...])
    y_ref[...] = _rms(h3, gf_ref[...])


def _post_call(h2d, og, om, consts, tm, absorbed_values, name):
    n_rows, d_model = h2d.shape
    assert n_rows % tm == 0
    row_spec = lambda w: pl.BlockSpec((tm, w), lambda i: (i, 0))
    kern = functools.partial(_post_kernel, absorbed_values=absorbed_values)
    return pl.pallas_call(
        kern,
        grid=(n_rows // tm,),
        in_specs=[row_spec(d_model), row_spec(og.shape[1]), row_spec(om.shape[1])]
                 + [_const_spec(c.shape) for c in consts],
        out_specs=row_spec(d_model),
        out_shape=jax.ShapeDtypeStruct((n_rows, d_model), F32),
        compiler_params=pltpu.CompilerParams(dimension_semantics=("arbitrary",),
                                             vmem_limit_bytes=VMEM_LIMIT),
        name=name,
    )(h2d, og, om, *consts)


def _rope_parts(pos):
    half = MLA_D_ROPE // 2
    inv = ROPE_THETA ** (-jnp.arange(half, dtype=F32) / half)
    ang = pos.astype(F32)[:, None] * inv[None, :]
    cos = jnp.concatenate([jnp.cos(ang), jnp.cos(ang)], -1)
    sin_signed = jnp.concatenate([-jnp.sin(ang), jnp.sin(ang)], -1)
    return cos, sin_signed


def _pad_lanes(x, width):
    return jnp.pad(x, ((0, 0), (0, width - x.shape[1])))


def _key_tables(cos, sin_signed):
    return _pad_lanes(cos, LANES), _pad_lanes(sin_signed, LANES)


def _prompt_tables(pos):
    cos, sin_signed = _rope_parts(pos)
    n = pos.shape[0]
    qcos = jnp.concatenate([jnp.full((n, MLA_D_NOPE), Q_SCALE_LOG2, F32), cos * Q_SCALE_LOG2], -1)
    qsin = jnp.concatenate([jnp.zeros((n, MLA_D_NOPE), F32), sin_signed * Q_SCALE_LOG2], -1)
    kcos, ksin = _key_tables(cos, sin_signed)
    return jnp.concatenate([_pad_lanes(qcos, LANES), _pad_lanes(qsin, LANES), kcos, ksin], -1)


def _sample_tables(pos):
    cos, sin_signed = _rope_parts(pos)
    kcos, ksin = _key_tables(cos, sin_signed)
    return jnp.concatenate([jnp.tile(cos * Q_SCALE_LOG2, (1, MLA_HEADS)),
                            jnp.tile(sin_signed * Q_SCALE_LOG2, (1, MLA_HEADS)), kcos, ksin], -1)


def _swap_halves(w):
    half = MLA_D_ROPE // 2
    return jnp.concatenate([w[..., half:], w[..., :half]], -1)


def kernel(x_prompt, x_sample, cache_latent, cache_krope, state_conv, state_ssm, page_table,
           meta_tokens, norm_mix_g, w_in, conv_w, a_log, dt_bias, gdn_norm_g, q_norm_g, w_q_b,
           kv_norm_g, w_kv_b, w_out, norm_mlp_g, w_up, w_down, final_norm_g):
    assert w_in.shape[0] == 1, "single-layer problem"
    B, T, D = x_prompt.shape
    Bs, Ts, _ = x_sample.shape
    assert CONV_W - 1 <= Ts <= SAMPLE_T_PAD
    past_len = page_table.shape[1] * cache_latent.shape[2]
    q_lora = q_norm_g.shape[1]
    kv_lora = kv_norm_g.shape[1]
    H = MLA_HEADS

    wi = w_in[0]
    c0 = CONV_CH + GDN_V_W
    w1 = wi[:, :c0].astype(BF16)
    wba = _pad_lanes(wi[:, c0:c0 + 2 * GDN_HEADS], LANES).astype(BF16)
    c1 = c0 + 2 * GDN_HEADS
    wcq = wi[:, c1:c1 + q_lora].astype(BF16)
    wckv = wi[:, c1 + q_lora:c1 + q_lora + kv_lora].astype(BF16)
    wkpe_raw = wi[:, c1 + q_lora + kv_lora:]
    wkpe = _pad_lanes(wkpe_raw, LANES).astype(BF16)
    wkpesw = _pad_lanes(_swap_halves(wkpe_raw), LANES).astype(BF16)
    gmix = norm_mix_g[0][None, :]
    qg = q_norm_g[0][None, :]
    kvg = kv_norm_g[0][None, :]
    wq = w_q_b[0]
    wq_nope, wq_pe = wq[..., :MLA_D_NOPE], wq[..., MLA_D_NOPE:]
    wq_pesw = _swap_halves(wq_pe)
    zq = lambda n: jnp.zeros((q_lora, H, n), F32)
    wqcat = jnp.concatenate([wq_nope, wq_pe, zq(HEAD_PAD - MLA_D_NOPE - MLA_D_ROPE)], -1)
    wqsw = jnp.concatenate([zq(MLA_D_NOPE), wq_pesw, zq(HEAD_PAD - MLA_D_NOPE - MLA_D_ROPE)], -1)
    wqcat = wqcat.reshape(q_lora, H * HEAD_PAD).astype(BF16)
    wqsw = wqsw.reshape(q_lora, H * HEAD_PAD).astype(BF16)
    wkv = w_kv_b[0]
    wkb, wvb = wkv[..., :MLA_D_NOPE], wkv[..., MLA_D_NOPE:]
    zk = lambda n: jnp.zeros((kv_lora, H, n), F32)
    wkb_sp = jnp.concatenate([wkb, zk(HEAD_PAD - MLA_D_NOPE)], -1).reshape(kv_lora, H * HEAD_PAD).astype(BF16)
    odd = (jnp.arange(H) % 2 == 1)[None, :, None]
    wvb_sp = jnp.where(odd, jnp.concatenate([zk(HEAD_PAD - MLA_D_V), wvb], -1),
                       jnp.concatenate([wvb, zk(HEAD_PAD - MLA_D_V)], -1))
    wvb_sp_t = wvb_sp.reshape(kv_lora, H * HEAD_PAD).T.astype(BF16)
    jj =jnp.arange(LANES)[:, None]
    ll = jnp.arange(H * HEAD_PAD)[None, :]
    esel = ((ll % HEAD_PAD == jj + MLA_D_NOPE) & (jj < MLA_D_ROPE)).astype(BF16)
    eye_h = jnp.eye(H, dtype=bool)
    wabs = jnp.where(eye_h[:, None, :, None], wkb.transpose(1, 2, 0)[:, :, None, :], 0.0)
    wabs = wabs.reshape(H * MLA_D_NOPE, H * kv_lora).astype(BF16)
    wvbd = jnp.where(eye_h[:, None, :, None], wvb.transpose(1, 0, 2)[:, :, None, :], 0.0)
    wvbd = wvbd.reshape(H * kv_lora, H * MLA_D_V).astype(BF16)
    wo1 = w_out[0][:GDN_V_W].astype(BF16)
    wo2 = w_out[0][GDN_V_W:].astype(BF16)
    g2 = norm_mlp_g[0][None, :]
    wup = w_up[0].astype(BF16)
    wdn = w_down[0].astype(BF16)
    gf = final_norm_g[None, :]
    cw = jnp.pad(conv_w[0], ((0, SUBLANES - CONV_W), (0, 0)))
    lane_row = lambda v: jnp.pad(v[None, :], ((0, 0), (GDN_HEADS, LANES - 2 * GDN_HEADS)))
    alog = lane_row(a_log[0])
    dtb = lane_row(dt_bias[0])
    ng = gdn_norm_g[0][None, :]

    common = (gmix, w1, wba, wcq, wckv, wkpe, wkpesw, qg, kvg)
    slot_row = jnp.arange(H * HEAD_PAD) % HEAD_PAD
    vones = (slot_row == _denominator_row(jnp.arange(H * HEAD_PAD) // HEAD_PAD)).astype(F32)[:, None]
    prompt_consts = common + (wqcat, wqsw, wkb_sp, wvb_sp_t, esel, vones)
    sample_consts = common + (wq_nope.reshape(q_lora, -1).astype(BF16), wq_pe.reshape(q_lora, -1).astype(BF16),
                              wq_pesw.reshape(q_lora, -1).astype(BF16), wabs)
    pw = [CONV_CH, GDN_V_W, LANES, kv_lora, MLA_D_ROPE, H * HEAD_PAD, H * HEAD_PAD, H * HEAD_PAD]
    pd = [F32, F32, F32, F32, F32, BF16, BF16, BF16]

    tab_m = _prompt_tables(jnp.arange(N_META))
    qkv_m, z_m, ba_m, lat_m, kpe_m, _, kcat_m, vt_m = _proj_call(
        _proj_prompt_kernel, meta_tokens, tab_m, prompt_consts, pw, pd, N_META, "proj_meta", transposed_outs=(7,))
    zero_conv = jnp.zeros((1, SUBLANES, CONV_CH), F32)
    zero_s = jnp.zeros((1, GDN_HEADS, GDN_DK, GDN_DV), F32)
    _, s1 = _gdn_call(qkv_m[None], z_m[None], ba_m[None], zero_conv, zero_s, cw, alog, dtb, ng,
                      N_META, N_META, "gdn_meta")
    conv1 = jnp.pad(qkv_m[N_META - (CONV_W - 1):], ((SUBLANES - (CONV_W - 1), 0), (0, 0)))[None]
    k_meta = jnp.pad(kcat_m, ((0, LANES - N_META), (0, 0)))
    vt_meta = jnp.pad(vt_m, ((0, 0), (0, LANES - N_META)))

    tab_p = _prompt_tables(N_META + jnp.arange(T))
    xp2d = x_prompt.reshape(B * T, D)
    qkv_p, z_p, ba_p, lat_p, kpe_p, qcat_p, kcat_p, vt_p = _proj_call(
        _proj_prompt_kernel, xp2d, tab_p, prompt_consts, pw, pd, 256, "proj_prompt", transposed_outs=(7,))
    r3 = lambda a: a.reshape(B, T, a.shape[-1])
    qkv_p3 = r3(qkv_p)
    og_p, s2 = _gdn_call(qkv_p3, r3(z_p), r3(ba_p), conv1, s1, cw, alog, dtb, ng, GDN_CHUNK, T, "gdn_prompt",
                         nb_intra=4, nb_scan=8)
    om_p = _attn_call(r3(qcat_p), r3(kcat_p), vt_p, k_meta, vt_meta, 256, 8)
    post_consts = (wo1, wo2, g2, wup, wdn, gf)
    y_p = _post_call(xp2d, og_p.reshape(B * T, -1), om_p.reshape(B * T, -1), post_consts, 512, False, "post_prompt")

    tp = SAMPLE_T_PAD
    xs = jnp.pad(x_sample, ((0, 0), (0, tp - Ts), (0, 0))).reshape(Bs * tp, D)
    tab_s = jnp.tile(_sample_tables(past_len + jnp.arange(tp)), (Bs, 1))
    sw = [CONV_CH, GDN_V_W, LANES, kv_lora, MLA_D_ROPE, H * kv_lora, H * MLA_D_ROPE]
    sd = [F32, F32, F32, F32, F32, BF16, BF16]
    qkv_s, z_s, ba_s, lat_s, kpe_s, qlat_s, qpe_s = _proj_call(
        _proj_sample_kernel, xs, tab_s, sample_consts, sw, sd, Bs * tp, "proj_sample")
    s3 = lambda a: a.reshape(Bs, tp, a.shape[-1])
    qkv_s3 = s3(qkv_s)
    conv_in_s = jnp.pad(state_conv[0], ((0, 0), (SUBLANES - (CONV_W - 1), 0), (0, 0)))
    og_s, s_new = _gdn_call(qkv_s3, s3(z_s), s3(ba_s), conv_in_s, state_ssm[0], cw, alog, dtb, ng, tp, Ts, "gdn_sample",
                            nb_intra=4, nb_scan=4)
    out_lat = _dattn_call(page_table, qlat_s.reshape(Bs, tp * H, kv_lora), qpe_s.reshape(Bs, tp * H, MLA_D_ROPE),
                          s3(lat_s), s3(kpe_s), cache_latent, jnp.swapaxes(cache_krope, 2, 3), Ts, 32)
    sample_post_consts = (wvbd,) + post_consts
    y_s = _post_call(xs, og_s.reshape(Bs * tp, -1), out_lat.reshape(Bs * tp, H * kv_lora),
                     sample_post_consts, Bs * tp, True, "post_sample")

    bc = lambda a: jnp.broadcast_to(a[None], (B,) + a.shape)
    lat_po = jnp.concatenate([bc(lat_m), r3(lat_p)], axis=1)[None]
    kpe_po = jnp.concatenate([bc(kpe_m), r3(kpe_p)], axis=1)[None]
    conv_po = qkv_p3[:, T - (CONV_W - 1):][None]
    conv_so = qkv_s3[:, Ts - (CONV_W - 1):Ts][None]
    return (y_p.reshape(B, T, D), y_s.reshape(Bs, tp, D)[:, :Ts],
            lat_po, kpe_po, conv_po, s2[None],
            s3(lat_s)[:, :Ts][None], s3(kpe_s)[:, :Ts][None], conv_so, s_new[None])
```

```python
import functools
import math

import jax
import jax.numpy as jnp
from jax import lax
from jax.experimental import pallas as pl
from jax.experimental.pallas import tpu as pltpu

F32 = jnp.float32
BF16 = jnp.bfloat16

N_META = 16
EPS = 1e-6
GDN_HEADS = 8
GDN_DK = 64
GDN_DV = 64
CONV_W = 4
GDN_CHUNK = 64
MLA_HEADS = 8
MLA_D_NOPE = 64
MLA_D_ROPE = 32
MLA_D_V = 64
ROPE_THETA = 10000.0
MLA_SCALE = (MLA_D_NOPE + MLA_D_ROPE) ** -0.5
Q_SCALE_LOG2 = MLA_SCALE * math.log2(math.e)
GDN_QK_W = GDN_HEADS * GDN_DK
GDN_V_W = GDN_HEADS * GDN_DV
CONV_CH = 2 * GDN_QK_W + GDN_V_W

LANES = 128
SUBLANES = 8
HEAD_PAD = 128
SAMPLE_T_PAD = 8
VMEM_LIMIT = 56 * 1024 * 1024


def _dot(a, b):
    return jnp.dot(a, b, preferred_element_type=F32)


def _dot_nt(a, b):
    return lax.dot_general(a, b, (((1,), (1,)), ((), ())), preferred_element_type=F32)


def _rms(x, g):
    return x * lax.rsqrt(jnp.mean(x * x, axis=-1, keepdims=True) + EPS) * g


def _sigmoid(x):
    return 1.0 / (1.0 + jnp.exp(-x))


def _const_spec(shape):
    nd = len(shape)
    return pl.BlockSpec(shape, lambda *_: (0,) * nd)


def _proj_common(x_ref, gmix_ref, w1_ref, wba_ref, wcq_ref, wckv_ref, wkpe_ref, wkpesw_ref,
                 qg_ref, kvg_ref, kcos, ksin, qkv_ref, z_ref, ba_ref, lat_ref, kpe_ref):
    hn = _rms(x_ref[...], gmix_ref[...]).astype(BF16)
    qkvz = _dot(hn, w1_ref[...])
    qkv_ref[...] = qkvz[:, :CONV_CH]
    z_ref[...] = qkvz[:, CONV_CH:]
    ba_ref[...] = _dot(hn, wba_ref[...])
    cq = _dot(hn, wcq_ref[...])
    ckv = _dot(hn, wckv_ref[...])
    kpe = _dot(hn, wkpe_ref[...]) * kcos + _dot(hn, wkpesw_ref[...]) * ksin
    kpe_ref[...] = kpe[:, :MLA_D_ROPE]
    lat = _rms(ckv, kvg_ref[...])
    lat_ref[...] = lat
    cqn = _rms(cq, qg_ref[...]).astype(BF16)
    return cqn, lat.astype(BF16), kpe.astype(BF16)


def _proj_prompt_kernel(x_ref, tab_ref, gmix_ref, w1_ref, wba_ref, wcq_ref, wckv_ref, wkpe_ref,
                        wkpesw_ref, qg_ref, kvg_ref, wq_ref, wqsw_ref, wkb_ref, wvb_ref, esel_ref, vones_ref,
                        qkv_ref, z_ref, ba_ref, lat_ref, kpe_ref, qcat_ref, kcat_ref, vt_ref):
    qcos = tab_ref[:, 0 * LANES:1 * LANES]
    qsin = tab_ref[:, 1 * LANES:2 * LANES]
    kcos = tab_ref[:, 2 * LANES:3 * LANES]
    ksin = tab_ref[:, 3 * LANES:4 * LANES]
    cqn, latb, kpeb = _proj_common(x_ref, gmix_ref, w1_ref, wba_ref, wcq_ref, wckv_ref, wkpe_ref,
                                   wkpesw_ref, qg_ref, kvg_ref, kcos, ksin,
                                   qkv_ref, z_ref, ba_ref, lat_ref, kpe_ref)
    qa = _dot(cqn, wq_ref[...])
    qb = _dot(cqn, wqsw_ref[...])
    for h in range(MLA_HEADS):
        sl = slice(h * HEAD_PAD, (h + 1) * HEAD_PAD)
        qcat_ref[:, sl] = (qa[:, sl] * qcos + qb[:, sl] * qsin).astype(BF16)
    kcat_ref[...] = (_dot(latb, wkb_ref[...]) + _dot(kpeb, esel_ref[...])).astype(BF16)
    vt_ref[...] = (_dot_nt(wvb_ref[...], latb) + vones_ref[...]).astype(BF16)


def _proj_sample_kernel(x_ref, tab_ref, gmix_ref, w1_ref, wba_ref, wcq_ref, wckv_ref, wkpe_ref,
                        wkpesw_ref, qg_ref, kvg_ref, wqn_ref, wqpe_ref, wqpesw_ref, wabs_ref,
                        qkv_ref, z_ref, ba_ref, lat_ref, kpe_ref, qlat_ref, qpe_ref):
    pe_w = MLA_HEADS * MLA_D_ROPE
    qcos = tab_ref[:, 0:pe_w]
    qsin = tab_ref[:, pe_w:2 * pe_w]
    kcos = tab_ref[:, 2 * pe_w:2 * pe_w + LANES]
    ksin = tab_ref[:, 2 * pe_w + LANES:2 * pe_w + 2 * LANES]
    cqn, _, _ = _proj_common(x_ref, gmix_ref, w1_ref, wba_ref, wcq_ref, wckv_ref, wkpe_ref,
                             wkpesw_ref, qg_ref, kvg_ref, kcos, ksin,
                             qkv_ref, z_ref, ba_ref, lat_ref, kpe_ref)
    qn = _dot(cqn, wqn_ref[...]).astype(BF16)
    qlat_ref[...] = (_dot(qn, wabs_ref[...]) * Q_SCALE_LOG2).astype(BF16)
    qpe_ref[...] = (_dot(cqn, wqpe_ref[...]) * qcos + _dot(cqn, wqpesw_ref[...]) * qsin).astype(BF16)


def _proj_call(kernel_fn, x2d, tab, consts, out_widths, out_dtypes, tm, name, transposed_outs=()):
    n_rows, d_model = x2d.shape
    assert n_rows % tm == 0
    row_spec = lambda w: pl.BlockSpec((tm, w), lambda i: (i, 0))
    col_spec = lambda w: pl.BlockSpec((w, tm), lambda i: (0, i))
    is_t = [k in transposed_outs for k in range(len(out_widths))]
    assert tab.shape[0] % tm == 0 and n_rows % tab.shape[0] == 0
    tab_blocks = tab.shape[0] // tm
    tab_spec = pl.BlockSpec((tm, tab.shape[1]), lambda i: (i % tab_blocks, 0))
    in_specs = [row_spec(d_model), tab_spec] + [_const_spec(c.shape) for c in consts]
    return pl.pallas_call(
        kernel_fn,
        grid=(n_rows // tm,),
        in_specs=in_specs,
        out_specs=[col_spec(w) if t else row_spec(w) for w, t in zip(out_widths, is_t)],
        out_shape=[jax.ShapeDtypeStruct((w, n_rows) if t else (n_rows, w), dt)
                   for w, dt, t in zip(out_widths, out_dtypes, is_t)],
        compiler_params=pltpu.CompilerParams(dimension_semantics=("arbitrary",),
                                             vmem_limit_bytes=VMEM_LIMIT),
        name=name,
    )(x2d, tab, *consts)


def _block_diag2(x, half):
    lane = lax.broadcasted_iota(jnp.int32, x.shape, 1)
    zero = jnp.zeros_like(x)
    return jnp.concatenate([jnp.where(lane < half, x, zero), jnp.where(lane < half, zero, x)], axis=0)


def _gdn_intra_prep(bb, cinit_bb, qkv_ref, prev_ref, ba_ref, cinit_ref, cw_ref, alog_ref, dtb_ref,
                    qk_mask, kg_ref, qg_ref, kdt_ref, vb_ref, egl_ref, xbuf, *, chunk, valid_len, n_chunks):
    c = pl.program_id(1)
    C = chunk
    H = GDN_HEADS
    tail0 = SUBLANES - (CONV_W - 1)

    xbuf[bb, 0:SUBLANES, :] = jnp.where(c == 0, cinit_ref[cinit_bb], prev_ref[bb])
    xbuf[bb, SUBLANES:SUBLANES + C, :] = qkv_ref[bb]
    cw = cw_ref[...]
    y = xbuf[bb, tail0:tail0 + C, :] * cw[0:1, :]
    for j in range(1, CONV_W):
        y = y + xbuf[bb, tail0 + j:tail0 + j + C, :] * cw[j:j + 1, :]
    y = y * _sigmoid(y)

    ba = ba_ref[bb]
    beta_all = _sigmoid(ba)
    sp_in = ba + dtb_ref[...]
    softplus = jnp.maximum(sp_in, 0.0) + jnp.log1p(jnp.exp(-jnp.abs(sp_in)))
    g_all = -jnp.exp(alog_ref[...]) * softplus
    if valid_len < n_chunks * C:
        row = c * C + lax.broadcasted_iota(jnp.int32, (C, LANES), 0)
        beta_all = jnp.where(row < valid_len, beta_all, 0.0)
        g_all = jnp.where(row < valid_len, g_all, 0.0)

    ri = lax.broadcasted_iota(jnp.int32, (C, C), 0)
    ci = lax.broadcasted_iota(jnp.int32, (C, C), 1)
    lower = ri >= ci
    gc_all = lax.dot_general(lower.astype(F32), g_all, (((1,), (0,)), ((), ())),
                             precision=lax.Precision.HIGHEST, preferred_element_type=F32)
    sel = (lax.broadcasted_iota(jnp.int32, (SUBLANES, LANES), 1)
           == lax.broadcasted_iota(jnp.int32, (SUBLANES, LANES), 0) + GDN_HEADS).astype(F32)
    gc_t = lax.dot_general(sel, gc_all, (((1,), (1,)), ((), ())),
                           precision=lax.Precision.HIGHEST, preferred_element_type=F32)

    pairs = range(H // 2)
    lane = lax.broadcasted_iota(jnp.int32, (C, LANES), 1)
    lo = lane < GDN_DK
    lo_c, lower_p = qk_mask
    ii_r = lax.broadcasted_iota(jnp.int32, (GDN_DK, LANES), 0)
    ii_c = lax.broadcasted_iota(jnp.int32, (GDN_DK, LANES), 1)
    eye2 = ((ii_c == ii_r) | (ii_c == ii_r + GDN_DK)).astype(BF16)

    def col(x, idx):
        return jnp.sum(jnp.where(lane == idx, x, 0.0), axis=-1, keepdims=True)

    def half_sums(x):
        return jnp.where(lo, jnp.sum(jnp.where(lo, x, 0.0), axis=-1, keepdims=True),
                         jnp.sum(jnp.where(lo, 0.0, x), axis=-1, keepdims=True))

    kn_b, kb_b, qn_b, decay = [], [], [], []
    for j in pairs:
        ps = slice(j * LANES, (j + 1) * LANES)
        qp = y[:, j * LANES:(j + 1) * LANES]
        kp = y[:, GDN_QK_W + j * LANES:GDN_QK_W + (j + 1) * LANES]
        vp = y[:, 2 * GDN_QK_W + j * LANES:2 * GDN_QK_W + (j + 1) * LANES]
        qn = qp * lax.rsqrt(half_sums(qp * qp) + EPS) * (GDN_DK ** -0.5)
        kn = kp * lax.rsqrt(half_sums(kp * kp) + EPS)
        beta = jnp.where(lo, col(beta_all, 2 * j), col(beta_all, 2 * j + 1))
        gc0, gc1 = col(gc_all, H + 2 * j), col(gc_all, H + 2 * j + 1)
        gcol = jnp.where(lo, gc0, gc1)
        grow = jnp.concatenate([gc_t[2 * j:2 * j + 1, :], gc_t[2 * j + 1:2 * j + 2, :]], axis=1)
        diff = jnp.where(lo_c, gc0, gc1) - grow
        decay.append(jnp.where(lower_p, jnp.exp(jnp.where(lower_p, diff, 0.0)), 0.0))
        egc = jnp.exp(gcol)
        gl = gcol[C - 1:C, :]
        egl_ref[bb, 0, :, ps] = jnp.exp(gl)
        kb = kn * beta
        vb_ref[bb, 0, :, ps] = vp * beta
        kg_ref[bb, 0, :, ps] = (kb * egc).astype(BF16)
        qg_ref[bb, 0, :, ps] = (qn * egc).astype(BF16)
        k_dec = (kn * jnp.exp(gl - gcol)).astype(BF16)
        kdt_ref[bb, 0, j] = _dot_nt(eye2, _block_diag2(k_dec, GDN_DK)).astype(BF16)
        kn_b.append(_block_diag2(kn.astype(BF16), GDN_DK))
        kb_b.append(kb.astype(BF16))
        qn_b.append(qn.astype(BF16))
    return kn_b, kb_b, qn_b, decay


def _gdn_intra_kernel(qkv_ref, prev_ref, ba_ref, cinit_ref, cw_ref, alog_ref, dtb_ref,
                      t_ref, qk_ref, kg_ref, qg_ref, kdt_ref, vb_ref, egl_ref, xbuf,
                      *, chunk, valid_len, n_chunks, nb, shared_cinit):
    C = chunk
    rp = lax.broadcasted_iota(jnp.int32, (C, 2 * C), 0)
    cp = lax.broadcasted_iota(jnp.int32, (C, 2 * C), 1)
    lo_c = cp < C
    cmod = jnp.where(lo_c, cp, cp - C)
    lower_p, strict_p = rp >= cmod, rp > cmod
    eye_p = (rp == cmod).astype(F32)
    kn_b, kb_b, qn_b, decay = [], [], [], []
    for bb in range(nb):
        parts = _gdn_intra_prep(bb, 0 if shared_cinit else bb, qkv_ref, prev_ref, ba_ref, cinit_ref, cw_ref,
                                alog_ref, dtb_ref, (lo_c, lower_p), kg_ref, qg_ref, kdt_ref, vb_ref, egl_ref, xbuf,
                                chunk=chunk, valid_len=valid_len, n_chunks=n_chunks)
        for dst, src in zip((kn_b, kb_b, qn_b, decay), parts):
            dst.extend(src)
    n_pairs = GDN_HEADS // 2
    probs = range(nb * n_pairs)
    kk = [_dot_nt(kb_b[i], kn_b[i]) for i in probs]
    qk = [_dot_nt(qn_b[i], kn_b[i]) for i in probs]
    for i in probs:
        qk_ref[i // n_pairs, 0, i % n_pairs] = jnp.where(lower_p, qk[i] * decay[i], 0.0).astype(BF16)
    p = [jnp.where(strict_p, -(kk[i] * decay[i]), 0.0) for i in probs]
    t = [eye_p + p[i] for i in probs]
    n_iter = max(1, int(math.ceil(math.log2(C))))
    for _ in range(1, n_iter):
        pb = [p[i].astype(BF16) for i in probs]
        p = [_dot(pb[i], _block_diag2(pb[i], C)) for i in probs]
        t = [t[i] + _dot(t[i].astype(BF16), _block_diag2(p[i].astype(BF16), C)) for i in probs]
    for i in probs:
        t_ref[i // n_pairs, 0, i % n_pairs] = t[i].astype(BF16)


def _gdn_scan_kernel(t_ref, qk_ref, kg_ref, qg_ref, kdt_ref, vb_ref, egl_ref, z_ref, sinit_ref, ng_ref,
                     o_ref, sfin_ref, s_scr, *, chunk, n_chunks, nb, shared_sinit):
    c = pl.program_id(1)
    C = chunk
    n_pairs = GDN_HEADS // 2
    probs = [(bb, j) for bb in range(nb) for j in range(n_pairs)]
    ps = [slice(j * LANES, (j + 1) * LANES) for j in range(n_pairs)]
    lo = lax.broadcasted_iota(jnp.int32, (C, LANES), 1) < GDN_DV

    @pl.when(c == 0)
    def _():
        for bb, j in probs:
            si = 0 if shared_sinit else bb
            s_scr[bb, j] = jnp.concatenate([sinit_ref[si, 2 * j], sinit_ref[si, 2 * j + 1]], axis=1)

    s = [s_scr[bb, j] for bb, j in probs]
    sbd = [_block_diag2(x.astype(BF16), GDN_DV) for x in s]
    ks = [_dot(kg_ref[bb, 0, :, ps[j]], sbd[i]) for i, (bb, j) in enumerate(probs)]
    qs = [_dot(qg_ref[bb, 0, :, ps[j]], sbd[i]) for i, (bb, j) in enumerate(probs)]
    r = [(vb_ref[bb, 0, :, ps[j]] - ks[i]).astype(BF16) for i, (bb, j) in enumerate(probs)]
    v_new = [_dot(t_ref[bb, 0, j], _block_diag2(r[i], GDN_DV)).astype(BF16) for i, (bb, j) in enumerate(probs)]
    vbd = [_block_diag2(x, GDN_DV) for x in v_new]
    o = [qs[i] + _dot(qk_ref[bb, 0, j], vbd[i]) for i, (bb, j) in enumerate(probs)]
    for i, (bb, j) in enumerate(probs):
        s_scr[bb, j] = s[i] * egl_ref[bb, 0, :, ps[j]] + _dot(kdt_ref[bb, 0, j], vbd[i])
    ng = ng_ref[...]
    for i, (bb, j) in enumerate(probs):
        o2 = o[i] * o[i]
        ms = jnp.where(lo, jnp.sum(jnp.where(lo, o2, 0.0), axis=-1, keepdims=True),
                       jnp.sum(jnp.where(lo, 0.0, o2), axis=-1, keepdims=True)) * (1.0 / GDN_DV)
        zp = z_ref[bb, :, ps[j]]
        o_ref[bb, :, ps[j]] = (o[i] * lax.rsqrt(ms + EPS) * ng * (zp * _sigmoid(zp))).astype(o_ref.dtype)

    @pl.when(c == n_chunks - 1)
    def _():
        for bb, j in probs:
            s_pair = s_scr[bb, j]
            sfin_ref[bb, 2 * j] = s_pair[:, :GDN_DV]
            sfin_ref[bb, 2 * j + 1] = s_pair[:, GDN_DV:]


def _gdn_call(qkv, z, ba, conv_init, s_init, cw, alog, dtb, ng, chunk, valid_len, name, nb_intra=1, nb_scan=1):
    B, T, _ = qkv.shape
    assert T % chunk == 0 and chunk % SUBLANES == 0 and GDN_DK == GDN_DV and 2 * GDN_DV == LANES
    n_chunks = T // chunk
    H, C = GDN_HEADS, chunk
    HP = H // 2
    shared_c = conv_init.shape[0] == 1
    shared_s = s_init.shape[0] == 1
    ng = jnp.tile(ng, (1, 2))
    params = pltpu.CompilerParams(dimension_semantics=("arbitrary", "arbitrary"), vmem_limit_bytes=VMEM_LIMIT)
    inter_dims = [(HP, C, 2 * C), (HP, C, 2 * C), (C, GDN_QK_W), (C, GDN_QK_W), (HP, GDN_DK, 2 * C),
                  (C, GDN_V_W), (1, GDN_V_W)]
    inter_dtypes = [BF16, BF16, BF16, BF16, BF16, F32, F32]

    def specs(nb):
        tok = lambda w: pl.BlockSpec((nb, chunk, w), lambda b, c: (b, c, 0))
        per_chunk = lambda *dims: pl.BlockSpec((nb, 1) + dims, lambda b, c: (b, c) + (0,) * len(dims))
        per_seq = lambda shared, *dims: pl.BlockSpec(
            ((1 if shared else nb),) + dims, lambda b, c: ((0 if shared else b),) + (0,) * len(dims))
        return tok, per_chunk, per_seq

    nb = nb_intra
    assert B % nb == 0
    tok, per_chunk, per_seq = specs(nb)
    prev_rows = pl.BlockSpec((nb, SUBLANES, CONV_CH),
                             lambda b, c: (b, jnp.maximum(c * (C // SUBLANES) - 1, 0), 0))
    inter = pl.pallas_call(
        functools.partial(_gdn_intra_kernel, chunk=chunk, valid_len=valid_len, n_chunks=n_chunks,
                          nb=nb, shared_cinit=shared_c),
        grid=(B // nb, n_chunks),
        in_specs=[
            tok(CONV_CH), prev_rows, tok(LANES), per_seq(shared_c, SUBLANES, CONV_CH),
            _const_spec(cw.shape), _const_spec(alog.shape), _const_spec(dtb.shape),
        ],
        out_specs=[per_chunk(*d) for d in inter_dims],
        out_shape=[jax.ShapeDtypeStruct((B, n_chunks) + d, dt) for d, dt in zip(inter_dims, inter_dtypes)],
        scratch_shapes=[pltpu.VMEM((nb, SUBLANES + chunk, CONV_CH), F32)],
        compiler_params=params,
        name=name + "_intra",
    )(qkv, qkv, ba, conv_init, cw, alog, dtb)

    nb = nb_scan
    assert B % nb == 0
    tok, per_chunk, per_seq = specs(nb)
    o, s_fin = pl.pallas_call(
        functools.partial(_gdn_scan_kernel, chunk=chunk, n_chunks=n_chunks, nb=nb, shared_sinit=shared_s),
        grid=(B // nb, n_chunks),
        in_specs=[per_chunk(*d) for d in inter_dims] + [
            tok(GDN_V_W), per_seq(shared_s, H, GDN_DK, GDN_DV), _const_spec(ng.shape),
        ],
        out_specs=[tok(GDN_V_W), per_seq(False, H, GDN_DK, GDN_DV)],
        out_shape=[
            jax.ShapeDtypeStruct((B, T, GDN_V_W), BF16),
            jax.ShapeDtypeStruct((B, H, GDN_DK, GDN_DV), F32),
        ],
        scratch_shapes=[pltpu.VMEM((nb, HP, GDN_DK, LANES), F32)],
        compiler_params=params,
        name=name + "_scan",
    )(*inter, z, s_init, ng)
    return o, s_fin


def _denominator_row(h):
    return MLA_D_V * (1 - h % 2)


def _attn_kernel(q_ref, k_ref, vt_ref, km_ref, vmt_ref, o_ref, m_scr, acc_scr, *, tq, n_meta, hps):
    qi = pl.program_id(2)
    neg = -jnp.inf
    key_m = lax.broadcasted_iota(jnp.int32, (km_ref.shape[0], tq), 0)
    key_d = lax.broadcasted_iota(jnp.int32, (tq, tq), 0)
    qry_d = lax.broadcasted_iota(jnp.int32, (tq, tq), 1)
    heads = range(hps)
    ls = [slice(h * HEAD_PAD, (h + 1) * HEAD_PAD) for h in heads]

    s = [jnp.where(key_m < n_meta, _dot_nt(km_ref[:, ls[h]], q_ref[0, :, ls[h]]), neg) for h in heads]
    m = [jnp.max(s[h], axis=0, keepdims=True) for h in heads]
    p = [jnp.exp2(s[h] - m[h]) for h in heads]
    for h in heads:
        m_scr[h] = m[h]
        acc_scr[h] = _dot(vmt_ref[ls[h], :], p[h].astype(BF16))

    def block(start, diagonal):
        s = [_dot_nt(k_ref[0, pl.ds(start, tq), ls[h]], q_ref[0, :, ls[h]]) for h in heads]
        if diagonal:
            s = [jnp.where(key_d <= qry_d, s[h], neg) for h in heads]
        m_old = [m_scr[h] for h in heads]
        m_new = [jnp.maximum(m_old[h], jnp.max(s[h], axis=0, keepdims=True)) for h in heads]
        alpha = [jnp.exp2(m_old[h] - m_new[h]) for h in heads]
        p = [jnp.exp2(s[h] - m_new[h]) for h in heads]
        pv = [_dot(vt_ref[ls[h], pl.ds(start, tq)], p[h].astype(BF16)) for h in heads]
        for h in heads:
            acc_scr[h] = alpha[h] * acc_scr[h] + pv[h]
            m_scr[h] = m_new[h]

    def body(kb, carry):
        block(pl.multiple_of(kb * tq, tq), False)
        return carry

    lax.fori_loop(0, qi, body, 0)
    block(pl.multiple_of(qi * tq, tq), True)
    slot_row = lax.broadcasted_iota(jnp.int32, (HEAD_PAD, tq), 0)

    def normalised(h):
        acc = acc_scr[h]
        r = _denominator_row(h)
        return jnp.where(slot_row == r, 0.0, acc) / acc[r:r + 1, :]

    for j in range(hps // 2):
        pair = normalised(2 * j) + normalised(2 * j + 1)
        o_ref[0, :, j * LANES:(j + 1) * LANES] = pair.T.astype(o_ref.dtype)


def _attn_call(qcat, kcat, vt, k_meta, vt_meta, tq, hps):
    B, T, _ = qcat.shape
    assert MLA_HEADS % hps == 0 and hps % 2 == 0
    n_groups = MLA_HEADS // hps
    gw = hps * HEAD_PAD
    ow = (hps // 2) * LANES
    kern = functools.partial(_attn_kernel, tq=tq, n_meta=N_META, hps=hps)
    return pl.pallas_call(
        kern,
        grid=(B, n_groups, T // tq),
        in_specs=[
            pl.BlockSpec((1, tq, gw), lambda b, g, qi: (b, qi, g)),
            pl.BlockSpec((1, T, gw), lambda b, g, qi: (b, 0, g)),
            pl.BlockSpec((gw, T), lambda b, g, qi: (g, b)),
            pl.BlockSpec((k_meta.shape[0], gw), lambda b, g, qi: (0, g)),
            pl.BlockSpec((gw, vt_meta.shape[1]), lambda b, g, qi: (g, 0)),
        ],
        out_specs=pl.BlockSpec((1, tq, ow), lambda b, g, qi: (b, qi, g)),
        out_shape=jax.ShapeDtypeStruct((B, T, n_groups * ow), BF16),
        scratch_shapes=[pltpu.VMEM((hps, 1, tq), F32), pltpu.VMEM((hps, HEAD_PAD, tq), F32)],
        compiler_params=pltpu.CompilerParams(dimension_semantics=("arbitrary", "arbitrary", "arbitrary"),
                                             vmem_limit_bytes=VMEM_LIMIT),
        name="attn",
    )(qcat, kcat, vt, k_meta, vt_meta)


def _dattn_kernel(pt_ref, qlat_ref, qpe_ref, latn_ref, kpen_ref, lat_hbm, kpe_hbm, o_ref,
                  lat_buf, kpe_buf, sem_lat, sem_kpe, m_scr, l_scr, acc_scr,
                  *, group, n_groups, n_sub, t_new, page):
    b = pl.program_id(0)
    n_seq = pl.num_programs(0)
    neg = -jnp.inf

    def page_copies(seq, g, k, slot):
        pid = pt_ref[seq, g * group + k]
        rows = pl.ds(pl.multiple_of(k * page, page), page)
        return (pltpu.make_async_copy(lat_hbm.at[0, pid], lat_buf.at[slot, rows, :], sem_lat.at[slot]),
                pltpu.make_async_copy(kpe_hbm.at[0, pid], kpe_buf.at[slot, k], sem_kpe.at[slot]))

    def start_group(seq, g, slot):
        def body(k2, carry):
            for prio in range(2):
                for cp in page_copies(seq, g, 2 * k2 + prio, slot):
                    cp.start(priority=prio)
            return carry
        lax.fori_loop(0, group // 2, body, 0)

    def wait_group(seq, g, slot):
        def body(k, carry):
            for cp in page_copies(seq, g, k, slot):
                cp.wait()
            return carry
        lax.fori_loop(0, group, body, 0)

    @pl.when(b == 0)
    def _():
        start_group(0, 0, 0)

    m_scr[...] = jnp.full(m_scr.shape, neg, F32)
    l_scr[...] = jnp.zeros(l_scr.shape, F32)
    acc_scr[...] = jnp.zeros(acc_scr.shape, F32)

    ql = qlat_ref[0]
    qp = qpe_ref[0]

    def update(scores, vals):
        ms = [jnp.max(s, axis=-1, keepdims=True) for s in scores]
        ps = [jnp.exp2(s - m) for s, m in zip(scores, ms)]
        ls = [jnp.sum(p, axis=-1, keepdims=True) for p in ps]
        pvs = [_dot(p.astype(BF16), v) for p, v in zip(ps, vals)]
        m_old = m_scr[...]
        m_new = m_old
        for m in ms:
            m_new = jnp.maximum(m_new, m)
        alpha = jnp.exp2(m_old - m_new)
        l_new = alpha * l_scr[...]
        acc_new = alpha * acc_scr[...]
        for m, l, pv in zip(ms, ls, pvs):
            w = jnp.exp2(m - m_new)
            l_new = l_new + w * l
            acc_new = acc_new + w * pv
        l_scr[...] = l_new
        acc_scr[...] = acc_new
        m_scr[...] = m_new

    sub_pages = group // n_sub
    sub_keys = sub_pages * page
    for g in range(n_groups):
        slot = g % 2
        if g + 1 < n_groups:
            start_group(b, g + 1, 1 - slot)
        else:
            @pl.when(b + 1 < n_seq)
            def _():
                start_group(b + 1, 0, 1 - slot)
        wait_group(b, g, slot)
        cbs = [lat_buf[slot, i * sub_keys:(i + 1) * sub_keys, :].astype(BF16) for i in range(n_sub)]
        kbts = [jnp.concatenate([kpe_buf[slot, k] for k in range(i * sub_pages, (i + 1) * sub_pages)],
                                axis=1).astype(BF16) for i in range(n_sub)]
        update([_dot_nt(ql, cb) + _dot(qp, kbt) for cb, kbt in zip(cbs, kbts)], cbs)

    pad = 2 * SUBLANES - latn_ref.shape[1]
    cn = jnp.concatenate([latn_ref[0], jnp.zeros((pad, latn_ref.shape[2]), F32)], axis=0).astype(BF16)
    kn = jnp.concatenate([kpen_ref[0], jnp.zeros((pad, kpen_ref.shape[2]), F32)], axis=0).astype(BF16)
    s = _dot_nt(ql, cn) + _dot_nt(qp, kn)
    tok = lax.shift_right_logical(lax.broadcasted_iota(jnp.int32, s.shape, 0), int(math.log2(MLA_HEADS)))
    key = lax.broadcasted_iota(jnp.int32, s.shape, 1)
    s = jnp.where((key <= tok) & (key < t_new), s, neg)
    update([s], [cn])
    o_ref[0] = (acc_scr[...] / l_scr[...]).astype(o_ref.dtype)


def _dattn_call(page_table, qlat, qpe, lat_new, kpe_new, cache_latent, cache_krope_t, t_new, group, n_sub):
    assert group % n_sub == 0 and group % 2 == 0
    Bs, R, kv_lora = qlat.shape
    n_pages = page_table.shape[1]
    assert n_pages % (2 * group) == 0, "the two buffer slots alternate, so a sequence needs an even group count"
    n_groups = n_pages // group
    page = cache_latent.shape[2]
    kern = functools.partial(_dattn_kernel, group=group, n_groups=n_groups, n_sub=n_sub, t_new=t_new, page=page)
    per_b = lambda shape: pl.BlockSpec((1,) + shape, lambda b, pt: (b, 0, 0))
    grid_spec = pltpu.PrefetchScalarGridSpec(
        num_scalar_prefetch=1,
        grid=(Bs,),
        in_specs=[per_b((R, kv_lora)), per_b((R, MLA_D_ROPE)),
                  per_b(lat_new.shape[1:]), per_b(kpe_new.shape[1:]),
                  pl.BlockSpec(memory_space=pl.ANY), pl.BlockSpec(memory_space=pl.ANY)],
        out_specs=per_b((R, kv_lora)),
        scratch_shapes=[pltpu.VMEM((2, group * page, kv_lora), F32),
                        pltpu.VMEM((2, group, MLA_D_ROPE, page), F32),
                        pltpu.SemaphoreType.DMA((2,)), pltpu.SemaphoreType.DMA((2,)),
                        pltpu.VMEM((R, 1), F32), pltpu.VMEM((R, 1), F32), pltpu.VMEM((R, kv_lora), F32)],
    )
    return pl.pallas_call(
        kern,
        grid_spec=grid_spec,
        out_shape=jax.ShapeDtypeStruct((Bs, R, kv_lora), BF16),
        compiler_params=pltpu.CompilerParams(dimension_semantics=("arbitrary",),
                                             vmem_limit_bytes=VMEM_LIMIT),
        name="dattn",
    )(page_table, qlat, qpe, lat_new, kpe_new, cache_latent, cache_krope_t)


def _post_kernel(*refs, absorbed_values):
    if absorbed_values:
        h_ref, og_ref, om_ref, wvbd_ref, wo1_ref, wo2_ref, g2_ref, wup_ref, wdn_ref, gf_ref, y_ref = refs
        om = _dot(om_ref[...], wvbd_ref[...]).astype(BF16)
    else:
        h_ref, og_ref, om_ref, wo1_ref, wo2_ref, g2_ref, wup_ref, wdn_ref, gf_ref, y_ref = refs
        om = om_ref[...]
    h2 = h_ref[...] + _dot(og_ref[...], wo1_ref[...]) + _dot(om, wo2_ref[...])
    hn = _rms(h2, g2_ref[...]).astype(BF16)
    u = jnp.maximum(_dot(hn, wup_ref[...]), 0.0)
    h3 = h2 + _dot((u * u).astype(BF16), wdn_ref[...])
    y_ref[...] = _rms(h3, gf_ref[...])


def _post_call(h2d, og, om, consts, tm, absorbed_values, name):
    n_rows, d_model = h2d.shape
    assert n_rows % tm == 0
    row_spec = lambda w: pl.BlockSpec((tm, w), lambda i: (i, 0))
    kern = functools.partial(_post_kernel, absorbed_values=absorbed_values)
    return pl.pallas_call(
        kern,
        grid=(n_rows // tm,),
        in_specs=[row_spec(d_model), row_spec(og.shape[1]), row_spec(om.shape[1])]
                 + [_const_spec(c.shape) for c in consts],
        out_specs=row_spec(d_model),
        out_shape=jax.ShapeDtypeStruct((n_rows, d_model), F32),
        compiler_params=pltpu.CompilerParams(dimension_semantics=("arbitrary",),
                                             vmem_limit_bytes=VMEM_LIMIT),
        name=name,
    )(h2d, og, om, *consts)


def _rope_parts(pos):
    half = MLA_D_ROPE // 2
    inv = ROPE_THETA ** (-jnp.arange(half, dtype=F32) / half)
    ang = pos.astype(F32)[:, None] * inv[None, :]
    cos = jnp.concatenate([jnp.cos(ang), jnp.cos(ang)], -1)
    sin_signed = jnp.concatenate([-jnp.sin(ang), jnp.sin(ang)], -1)
    return cos, sin_signed


def _pad_lanes(x, width):
    return jnp.pad(x, ((0, 0), (0, width - x.shape[1])))


def _key_tables(cos, sin_signed):
    return _pad_lanes(cos, LANES), _pad_lanes(sin_signed, LANES)


def _prompt_tables(pos):
    cos, sin_signed = _rope_parts(pos)
    n = pos.shape[0]
    qcos = jnp.concatenate([jnp.full((n, MLA_D_NOPE), Q_SCALE_LOG2, F32), cos * Q_SCALE_LOG2], -1)
    qsin = jnp.concatenate([jnp.zeros((n, MLA_D_NOPE), F32), sin_signed * Q_SCALE_LOG2], -1)
    kcos, ksin = _key_tables(cos, sin_signed)
    return jnp.concatenate([_pad_lanes(qcos, LANES), _pad_lanes(qsin, LANES), kcos, ksin], -1)


def _sample_tables(pos):
    cos, sin_signed = _rope_parts(pos)
    kcos, ksin = _key_tables(cos, sin_signed)
    return jnp.concatenate([jnp.tile(cos * Q_SCALE_LOG2, (1, MLA_HEADS)),
                            jnp.tile(sin_signed * Q_SCALE_LOG2, (1, MLA_HEADS)), kcos, ksin], -1)


def _swap_halves(w):
    half = MLA_D_ROPE // 2
    return jnp.concatenate([w[..., half:], w[..., :half]], -1)


def kernel(x_prompt, x_sample, cache_latent, cache_krope, state_conv, state_ssm, page_table,
           meta_tokens, norm_mix_g, w_in, conv_w, a_log, dt_bias, gdn_norm_g, q_norm_g, w_q_b,
           kv_norm_g, w_kv_b, w_out, norm_mlp_g, w_up, w_down, final_norm_g):
    assert w_in.shape[0] == 1, "single-layer problem"
    B, T, D = x_prompt.shape
    Bs, Ts, _ = x_sample.shape
    assert CONV_W - 1 <= Ts <= SAMPLE_T_PAD
    past_len = page_table.shape[1] * cache_latent.shape[2]
    q_lora = q_norm_g.shape[1]
    kv_lora = kv_norm_g.shape[1]
    H = MLA_HEADS

    wi = w_in[0]
    c0 = CONV_CH + GDN_V_W
    w1 = wi[:, :c0].astype(BF16)
    wba = _pad_lanes(wi[:, c0:c0 + 2 * GDN_HEADS], LANES).astype(BF16)
    c1 = c0 + 2 * GDN_HEADS
    wcq = wi[:, c1:c1 + q_lora].astype(BF16)
    wckv = wi[:, c1 + q_lora:c1 + q_lora + kv_lora].astype(BF16)
    wkpe_raw = wi[:, c1 + q_lora + kv_lora:]
    wkpe = _pad_lanes(wkpe_raw, LANES).astype(BF16)
    wkpesw = _pad_lanes(_swap_halves(wkpe_raw), LANES).astype(BF16)
    gmix = norm_mix_g[0][None, :]
    qg = q_norm_g[0][None, :]
    kvg = kv_norm_g[0][None, :]
    wq = w_q_b[0]
    wq_nope, wq_pe = wq[..., :MLA_D_NOPE], wq[..., MLA_D_NOPE:]
    wq_pesw = _swap_halves(wq_pe)
    zq = lambda n: jnp.zeros((q_lora, H, n), F32)
    wqcat = jnp.concatenate([wq_nope, wq_pe, zq(HEAD_PAD - MLA_D_NOPE - MLA_D_ROPE)], -1)
    wqsw = jnp.concatenate([zq(MLA_D_NOPE), wq_pesw, zq(HEAD_PAD - MLA_D_NOPE - MLA_D_ROPE)], -1)
    wqcat = wqcat.reshape(q_lora, H * HEAD_PAD).astype(BF16)
    wqsw = wqsw.reshape(q_lora, H * HEAD_PAD).astype(BF16)
    wkv = w_kv_b[0]
    wkb, wvb = wkv[..., :MLA_D_NOPE], wkv[..., MLA_D_NOPE:]
    zk = lambda n: jnp.zeros((kv_lora, H, n), F32)
    wkb_sp = jnp.concatenate([wkb, zk(HEAD_PAD - MLA_D_NOPE)], -1).reshape(kv_lora, H * HEAD_PAD).astype(BF16)
    odd = (jnp.arange(H) % 2 == 1)[None, :, None]
    wvb_sp = jnp.where(odd, jnp.concatenate([zk(HEAD_PAD - MLA_D_V), wvb], -1),
                       jnp.concatenate([wvb, zk(HEAD_PAD - MLA_D_V)], -1))
    wvb_sp_t = wvb_sp.reshape(kv_lora, H * HEAD_PAD).T.astype(BF16)
    jj =jnp.arange(LANES)[:, None]
    ll = jnp.arange(H * HEAD_PAD)[None, :]
    esel = ((ll % HEAD_PAD == jj + MLA_D_NOPE) & (jj < MLA_D_ROPE)).astype(BF16)
    eye_h = jnp.eye(H, dtype=bool)
    wabs = jnp.where(eye_h[:, None, :, None], wkb.transpose(1, 2, 0)[:, :, None, :], 0.0)
    wabs = wabs.reshape(H * MLA_D_NOPE, H * kv_lora).astype(BF16)
    wvbd = jnp.where(eye_h[:, None, :, None], wvb.transpose(1, 0, 2)[:, :, None, :], 0.0)
    wvbd = wvbd.reshape(H * kv_lora, H * MLA_D_V).astype(BF16)
    wo1 = w_out[0][:GDN_V_W].astype(BF16)
    wo2 = w_out[0][GDN_V_W:].astype(BF16)
    g2 = norm_mlp_g[0][None, :]
    wup = w_up[0].astype(BF16)
    wdn = w_down[0].astype(BF16)
    gf = final_norm_g[None, :]
    cw = jnp.pad(conv_w[0], ((0, SUBLANES - CONV_W), (0, 0)))
    lane_row = lambda v: jnp.pad(v[None, :], ((0, 0), (GDN_HEADS, LANES - 2 * GDN_HEADS)))
    alog = lane_row(a_log[0])
    dtb = lane_row(dt_bias[0])
    ng = gdn_norm_g[0][None, :]

    common = (gmix, w1, wba, wcq, wckv, wkpe, wkpesw, qg, kvg)
    slot_row = jnp.arange(H * HEAD_PAD) % HEAD_PAD
    vones = (slot_row == _denominator_row(jnp.arange(H * HEAD_PAD) // HEAD_PAD)).astype(F32)[:, None]
    prompt_consts = common + (wqcat, wqsw, wkb_sp, wvb_sp_t, esel, vones)
    sample_consts = common + (wq_nope.reshape(q_lora, -1).astype(BF16), wq_pe.reshape(q_lora, -1).astype(BF16),
                              wq_pesw.reshape(q_lora, -1).astype(BF16), wabs)
    pw = [CONV_CH, GDN_V_W, LANES, kv_lora, MLA_D_ROPE, H * HEAD_PAD, H * HEAD_PAD, H * HEAD_PAD]
    pd = [F32, F32, F32, F32, F32, BF16, BF16, BF16]

    tab_m = _prompt_tables(jnp.arange(N_META))
    qkv_m, z_m, ba_m, lat_m, kpe_m, _, kcat_m, vt_m = _proj_call(
        _proj_prompt_kernel, meta_tokens, tab_m, prompt_consts, pw, pd, N_META, "proj_meta", transposed_outs=(7,))
    zero_conv = jnp.zeros((1, SUBLANES, CONV_CH), F32)
    zero_s = jnp.zeros((1, GDN_HEADS, GDN_DK, GDN_DV), F32)
    _, s1 = _gdn_call(qkv_m[None], z_m[None], ba_m[None], zero_conv, zero_s, cw, alog, dtb, ng,
                      N_META, N_META, "gdn_meta")
    conv1 = jnp.pad(qkv_m[N_META - (CONV_W - 1):], ((SUBLANES - (CONV_W - 1), 0), (0, 0)))[None]
    k_meta = jnp.pad(kcat_m, ((0, LANES - N_META), (0, 0)))
    vt_meta = jnp.pad(vt_m, ((0, 0), (0, LANES - N_META)))

    tab_p = _prompt_tables(N_META + jnp.arange(T))
    xp2d = x_prompt.reshape(B * T, D)
    qkv_p, z_p, ba_p, lat_p, kpe_p, qcat_p, kcat_p, vt_p = _proj_call(
        _proj_prompt_kernel, xp2d, tab_p, prompt_consts, pw, pd, 256, "proj_prompt", transposed_outs=(7,))
    r3 = lambda a: a.reshape(B, T, a.shape[-1])
    qkv_p3 = r3(qkv_p)
    og_p, s2 = _gdn_call(qkv_p3, r3(z_p), r3(ba_p), conv1, s1, cw, alog, dtb, ng, GDN_CHUNK, T, "gdn_prompt",
                         nb_intra=4, nb_scan=8)
    om_p = _attn_call(r3(qcat_p), r3(kcat_p), vt_p, k_meta, vt_meta, 256, 8)
    post_consts = (wo1, wo2, g2, wup, wdn, gf)
    y_p = _post_call(xp2d, og_p.reshape(B * T, -1), om_p.reshape(B * T, -1), post_consts, 512, False, "post_prompt")

    tp = SAMPLE_T_PAD
    xs = jnp.pad(x_sample, ((0, 0), (0, tp - Ts), (0, 0))).reshape(Bs * tp, D)
    tab_s = jnp.tile(_sample_tables(past_len + jnp.arange(tp)), (Bs, 1))
    sw = [CONV_CH, GDN_V_W, LANES, kv_lora, MLA_D_ROPE, H * kv_lora, H * MLA_D_ROPE]
    sd = [F32, F32, F32, F32, F32, BF16, BF16]
    qkv_s, z_s, ba_s, lat_s, kpe_s, qlat_s, qpe_s = _proj_call(
        _proj_sample_kernel, xs, tab_s, sample_consts, sw, sd, Bs * tp, "proj_sample")
    s3 = lambda a: a.reshape(Bs, tp, a.shape[-1])
    qkv_s3 = s3(qkv_s)
    conv_in_s = jnp.pad(state_conv[0], ((0, 0), (SUBLANES - (CONV_W - 1), 0), (0, 0)))
    og_s, s_new = _gdn_call(qkv_s3, s3(z_s), s3(ba_s), conv_in_s, state_ssm[0], cw, alog, dtb, ng, tp, Ts, "gdn_sample",
                            nb_intra=4, nb_scan=4)
    out_lat = _dattn_call(page_table, qlat_s.reshape(Bs, tp * H, kv_lora), qpe_s.reshape(Bs, tp * H, MLA_D_ROPE),
                          s3(lat_s), s3(kpe_s), cache_latent, jnp.swapaxes(cache_krope, 2, 3), Ts, 32, 4)
    sample_post_consts = (wvbd,) + post_consts
    y_s = _post_call(xs, og_s.reshape(Bs * tp, -1), out_lat.reshape(Bs * tp, H * kv_lora),
                     sample_post_consts, Bs * tp, True, "post_sample")

    bc = lambda a: jnp.broadcast_to(a[None], (B,) + a.shape)
    lat_po = jnp.concatenate([bc(lat_m), r3(lat_p)], axis=1)[None]
    kpe_po = jnp.concatenate([bc(kpe_m), r3(kpe_p)], axis=1)[None]
    conv_po = qkv_p3[:, T - (CONV_W - 1):][None]
    conv_so = qkv_s3[:, Ts - (CONV_W - 1):Ts][None]
    return (y_p.reshape(B, T, D), y_s.reshape(Bs, tp, D)[:, :Ts],
            lat_po, kpe_po, conv_po, s2[None],
            s3(lat_s)[:, :Ts][None], s3(kpe_s)[:, :Ts][None], conv_so, s_new[None])
```

```python
import functools
import math

import jax
import jax.numpy as jnp
from jax import lax
from jax.experimental import pallas as pl
from jax.experimental.pallas import tpu as pltpu

F32 = jnp.float32
BF16 = jnp.bfloat16

N_META = 16
EPS = 1e-6
GDN_HEADS = 8
GDN_DK = 64
GDN_DV = 64
CONV_W = 4
GDN_CHUNK = 64
MLA_HEADS = 8
MLA_D_NOPE = 64
MLA_D_ROPE = 32
MLA_D_V = 64
ROPE_THETA = 10000.0
MLA_SCALE = (MLA_D_NOPE + MLA_D_ROPE) ** -0.5
Q_SCALE_LOG2 = MLA_SCALE * math.log2(math.e)
GDN_QK_W = GDN_HEADS * GDN_DK
GDN_V_W = GDN_HEADS * GDN_DV
CONV_CH = 2 * GDN_QK_W + GDN_V_W

LANES = 128
SUBLANES = 8
HEAD_PAD = 128
SAMPLE_T_PAD = 8
KPE_LANE0 = 32
VMEM_LIMIT = 56 * 1024 * 1024


def _dot(a, b):
    return jnp.dot(a, b, preferred_element_type=F32)


def _dot_nt(a, b):
    return lax.dot_general(a, b, (((1,), (1,)), ((), ())), preferred_element_type=F32)


def _rms(x, g):
    return x * lax.rsqrt(jnp.mean(x * x, axis=-1, keepdims=True) + EPS) * g


def _sigmoid(x):
    return 1.0 / (1.0 + jnp.exp(-x))


def _const_spec(shape):
    nd = len(shape)
    return pl.BlockSpec(shape, lambda *_: (0,) * nd)


def _proj_common(x_ref, gmix_ref, win_ref, qg_ref, kvg_ref, kcos, ksin, qkv_ref, z_ref, ba_ref, lat_ref, kpe_ref,
                 *, q_lora, kv_lora):
    hn = _rms(x_ref[...], gmix_ref[...]).astype(BF16)
    p = _dot(hn, win_ref[...])
    c0 = CONV_CH + GDN_V_W
    c1 = c0 + q_lora
    c2 = c1 + LANES
    c3 = c2 + kv_lora
    qkv_ref[...] = p[:, :CONV_CH]
    z_ref[...] = p[:, CONV_CH:c0]
    blk_a = p[:, c1:c2]
    ba_ref[...] = blk_a
    kpe = blk_a * kcos + p[:, c3:c3 + LANES] * ksin
    kpe_ref[...] = kpe[:, KPE_LANE0:KPE_LANE0 + MLA_D_ROPE]
    lat = _rms(p[:, c2:c3], kvg_ref[...])
    lat_ref[...] = lat
    cqn = _rms(p[:, c0:c1], qg_ref[...]).astype(BF16)
    return cqn, lat.astype(BF16), kpe


def _proj_prompt_kernel(x_ref, tab_ref, gmix_ref, win_ref, qg_ref, kvg_ref, wq_ref, wkb_ref, wvb_ref, vones_ref,
                        qkv_ref, z_ref, ba_ref, lat_ref, kpe_ref, qcat_ref, kcat_ref, vt_ref, *, q_lora, kv_lora):
    qcos = tab_ref[:, 0 * LANES:1 * LANES]
    qsin_lo = tab_ref[:, 1 * LANES:2 * LANES]
    qsin_hi = tab_ref[:, 2 * LANES:3 * LANES]
    kcos = tab_ref[:, 3 * LANES:4 * LANES]
    ksin = tab_ref[:, 4 * LANES:5 * LANES]
    cqn, latb, kpe = _proj_common(x_ref, gmix_ref, win_ref, qg_ref, kvg_ref, kcos, ksin,
                                  qkv_ref, z_ref, ba_ref, lat_ref, kpe_ref, q_lora=q_lora, kv_lora=kv_lora)
    qa = _dot(cqn, wq_ref[...])
    knope = _dot(latb, wkb_ref[...])
    half = MLA_D_ROPE // 2
    kpe_slot = pltpu.roll(kpe, MLA_D_NOPE - KPE_LANE0, axis=1)
    for h in range(MLA_HEADS):
        sl = slice(h * HEAD_PAD, (h + 1) * HEAD_PAD)
        qs = qa[:, sl]
        q_dn = pltpu.roll(qs, HEAD_PAD - half, axis=1)
        q_up = pltpu.roll(qs, half, axis=1)
        qcat_ref[:, sl] = (qs * qcos + q_dn * qsin_lo + q_up * qsin_hi).astype(BF16)
        kcat_ref[:, sl] = (knope[:, sl] + kpe_slot).astype(BF16)
    vt_ref[...] = (_dot_nt(wvb_ref[...], latb) + vones_ref[...]).astype(BF16)


def _proj_sample_kernel(x_ref, tab_ref, gmix_ref, win_ref, qg_ref, kvg_ref, wqn_ref, wqpe_ref, wqpesw_ref, wabs_ref,
                        qkv_ref, z_ref, ba_ref, lat_ref, kpe_ref, qlat_ref, qpe_ref, *, q_lora, kv_lora):
    pe_w = MLA_HEADS * MLA_D_ROPE
    qcos = tab_ref[:, 0:pe_w]
    qsin = tab_ref[:, pe_w:2 * pe_w]
    kcos = tab_ref[:, 2 * pe_w:2 * pe_w + LANES]
    ksin = tab_ref[:, 2 * pe_w + LANES:2 * pe_w + 2 * LANES]
    cqn, _, _ = _proj_common(x_ref, gmix_ref, win_ref, qg_ref, kvg_ref, kcos, ksin,
                             qkv_ref, z_ref, ba_ref, lat_ref, kpe_ref, q_lora=q_lora, kv_lora=kv_lora)
    qn = _dot(cqn, wqn_ref[...]).astype(BF16)
    qlat_ref[...] = (_dot(qn, wabs_ref[...]) * Q_SCALE_LOG2).astype(BF16)
    qpe_ref[...] = (_dot(cqn, wqpe_ref[...]) * qcos + _dot(cqn, wqpesw_ref[...]) * qsin).astype(BF16)


def _proj_call(kernel_fn, x2d, tab, consts, out_widths, out_dtypes, tm, name, transposed_outs=()):
    n_rows, d_model = x2d.shape
    assert n_rows % tm == 0
    row_spec = lambda w: pl.BlockSpec((tm, w), lambda i: (i, 0))
    col_spec = lambda w: pl.BlockSpec((w, tm), lambda i: (0, i))
    is_t = [k in transposed_outs for k in range(len(out_widths))]
    assert tab.shape[0] % tm == 0 and n_rows % tab.shape[0] == 0
    tab_blocks = tab.shape[0] // tm
    tab_spec = pl.BlockSpec((tm, tab.shape[1]), lambda i: (i % tab_blocks, 0))
    in_specs = [row_spec(d_model), tab_spec] + [_const_spec(c.shape) for c in consts]
    return pl.pallas_call(
        kernel_fn,
        grid=(n_rows // tm,),
        in_specs=in_specs,
        out_specs=[col_spec(w) if t else row_spec(w) for w, t in zip(out_widths, is_t)],
        out_shape=[jax.ShapeDtypeStruct((w, n_rows) if t else (n_rows, w), dt)
                   for w, dt, t in zip(out_widths, out_dtypes, is_t)],
        compiler_params=pltpu.CompilerParams(dimension_semantics=("arbitrary",),
                                             vmem_limit_bytes=VMEM_LIMIT),
        name=name,
    )(x2d, tab, *consts)


def _block_diag2(x, half):
    lane = lax.broadcasted_iota(jnp.int32, x.shape, 1)
    zero = jnp.zeros_like(x)
    return jnp.concatenate([jnp.where(lane < half, x, zero), jnp.where(lane < half, zero, x)], axis=0)


def _gdn_intra_prep(bb, cinit_bb, qkv_ref, prev_ref, ba_ref, cinit_ref, cw_ref, alog_ref, dtb_ref,
                    qk_mask, kg_ref, qg_ref, kdt_ref, vb_ref, egl_ref, xbuf, *, chunk, valid_len, n_chunks):
    c = pl.program_id(1)
    C = chunk
    H = GDN_HEADS
    tail0 = SUBLANES - (CONV_W - 1)

    xbuf[bb, 0:SUBLANES, :] = jnp.where(c == 0, cinit_ref[cinit_bb], prev_ref[bb])
    xbuf[bb, SUBLANES:SUBLANES + C, :] = qkv_ref[bb]
    cw = cw_ref[...]
    y = xbuf[bb, tail0:tail0 + C, :] * cw[0:1, :]
    for j in range(1, CONV_W):
        y = y + xbuf[bb, tail0 + j:tail0 + j + C, :] * cw[j:j + 1, :]
    y = y * _sigmoid(y)

    ba = ba_ref[bb]
    beta_all = _sigmoid(ba)
    sp_in = ba + dtb_ref[...]
    softplus = jnp.maximum(sp_in, 0.0) + jnp.log1p(jnp.exp(-jnp.abs(sp_in)))
    g_all = -jnp.exp(alog_ref[...]) * softplus
    if valid_len < n_chunks * C:
        row = c * C + lax.broadcasted_iota(jnp.int32, (C, LANES), 0)
        beta_all = jnp.where(row < valid_len, beta_all, 0.0)
        g_all = jnp.where(row < valid_len, g_all, 0.0)

    ri = lax.broadcasted_iota(jnp.int32, (C, C), 0)
    ci = lax.broadcasted_iota(jnp.int32, (C, C), 1)
    lower = ri >= ci
    gc_all = lax.dot_general(lower.astype(F32), g_all, (((1,), (0,)), ((), ())),
                             precision=lax.Precision.HIGHEST, preferred_element_type=F32)
    sel = (lax.broadcasted_iota(jnp.int32, (SUBLANES, LANES), 1)
           == lax.broadcasted_iota(jnp.int32, (SUBLANES, LANES), 0) + GDN_HEADS).astype(F32)
    gc_t = lax.dot_general(sel, gc_all, (((1,), (1,)), ((), ())),
                           precision=lax.Precision.HIGHEST, preferred_element_type=F32)

    pairs = range(H // 2)
    lane = lax.broadcasted_iota(jnp.int32, (C, LANES), 1)
    lo = lane < GDN_DK
    lo_c, lower_p = qk_mask
    ii_r = lax.broadcasted_iota(jnp.int32, (GDN_DK, LANES), 0)
    ii_c = lax.broadcasted_iota(jnp.int32, (GDN_DK, LANES), 1)
    eye2 = ((ii_c == ii_r) | (ii_c == ii_r + GDN_DK)).astype(BF16)

    def col(x, idx):
        return jnp.sum(jnp.where(lane == idx, x, 0.0), axis=-1, keepdims=True)

    def half_sums(x):
        return jnp.where(lo, jnp.sum(jnp.where(lo, x, 0.0), axis=-1, keepdims=True),
                         jnp.sum(jnp.where(lo, 0.0, x), axis=-1, keepdims=True))

    kn_b, kb_b, qn_b, decay = [], [], [], []
    for j in pairs:
        ps = slice(j * LANES, (j + 1) * LANES)
        qp = y[:, j * LANES:(j + 1) * LANES]
        kp = y[:, GDN_QK_W + j * LANES:GDN_QK_W + (j + 1) * LANES]
        vp = y[:, 2 * GDN_QK_W + j * LANES:2 * GDN_QK_W + (j + 1) * LANES]
        qn = qp * lax.rsqrt(half_sums(qp * qp) + EPS) * (GDN_DK ** -0.5)
        kn = kp * lax.rsqrt(half_sums(kp * kp) + EPS)
        beta = jnp.where(lo, col(beta_all, 2 * j), col(beta_all, 2 * j + 1))
        gc0, gc1 = col(gc_all, H + 2 * j), col(gc_all, H + 2 * j + 1)
        gcol = jnp.where(lo, gc0, gc1)
        grow = jnp.concatenate([gc_t[2 * j:2 * j + 1, :], gc_t[2 * j + 1:2 * j + 2, :]], axis=1)
        diff = jnp.where(lo_c, gc0, gc1) - grow
        decay.append(jnp.where(lower_p, jnp.exp(jnp.where(lower_p, diff, 0.0)), 0.0))
        egc = jnp.exp(gcol)
        gl = gcol[C - 1:C, :]
        egl_ref[bb, 0, :, ps] = jnp.exp(gl)
        kb = kn * beta
        vb_ref[bb, 0, :, ps] = vp * beta
        kg_ref[bb, 0, :, ps] = (kb * egc).astype(BF16)
        qg_ref[bb, 0, :, ps] = (qn * egc).astype(BF16)
        k_dec = (kn * jnp.exp(gl - gcol)).astype(BF16)
        kdt_ref[bb, 0, j] = _dot_nt(eye2, _block_diag2(k_dec, GDN_DK)).astype(BF16)
        kn_b.append(_block_diag2(kn.astype(BF16), GDN_DK))
        kb_b.append(kb.astype(BF16))
        qn_b.append(qn.astype(BF16))
    return kn_b, kb_b, qn_b, decay


def _gdn_intra_kernel(qkv_ref, prev_ref, ba_ref, cinit_ref, cw_ref, alog_ref, dtb_ref,
                      t_ref, qk_ref, kg_ref, qg_ref, kdt_ref, vb_ref, egl_ref, xbuf,
                      *, chunk, valid_len, n_chunks, nb, shared_cinit):
    C = chunk
    rp = lax.broadcasted_iota(jnp.int32, (C, 2 * C), 0)
    cp = lax.broadcasted_iota(jnp.int32, (C, 2 * C), 1)
    lo_c = cp < C
    cmod = jnp.where(lo_c, cp, cp - C)
    lower_p, strict_p = rp >= cmod, rp > cmod
    eye_p = (rp == cmod).astype(F32)
    kn_b, kb_b, qn_b, decay = [], [], [], []
    for bb in range(nb):
        parts = _gdn_intra_prep(bb, 0 if shared_cinit else bb, qkv_ref, prev_ref, ba_ref, cinit_ref, cw_ref,
                                alog_ref, dtb_ref, (lo_c, lower_p), kg_ref, qg_ref, kdt_ref, vb_ref, egl_ref, xbuf,
                                chunk=chunk, valid_len=valid_len, n_chunks=n_chunks)
        for dst, src in zip((kn_b, kb_b, qn_b, decay), parts):
            dst.extend(src)
    n_pairs = GDN_HEADS // 2
    probs = range(nb * n_pairs)
    kk = [_dot_nt(kb_b[i], kn_b[i]) for i in probs]
    qk = [_dot_nt(qn_b[i], kn_b[i]) for i in probs]
    for i in probs:
        qk_ref[i // n_pairs, 0, i % n_pairs] = jnp.where(lower_p, qk[i] * decay[i], 0.0).astype(BF16)
    p = [jnp.where(strict_p, -(kk[i] * decay[i]), 0.0) for i in probs]
    t = [eye_p + p[i] for i in probs]
    n_iter = max(1, int(math.ceil(math.log2(C))))
    for _ in range(1, n_iter):
        pb = [p[i].astype(BF16) for i in probs]
        p = [_dot(pb[i], _block_diag2(pb[i], C)) for i in probs]
        t = [t[i] + _dot(t[i].astype(BF16), _block_diag2(p[i].astype(BF16), C)) for i in probs]
    for i in probs:
        t_ref[i // n_pairs, 0, i % n_pairs] = t[i].astype(BF16)


def _gdn_scan_kernel(t_ref, qk_ref, kg_ref, qg_ref, kdt_ref, vb_ref, egl_ref, z_ref, sinit_ref, ng_ref,
                     o_ref, sfin_ref, s_scr, *, chunk, n_chunks, nb, shared_sinit):
    c = pl.program_id(1)
    C = chunk
    n_pairs = GDN_HEADS // 2
    probs = [(bb, j) for bb in range(nb) for j in range(n_pairs)]
    ps = [slice(j * LANES, (j + 1) * LANES) for j in range(n_pairs)]
    lo = lax.broadcasted_iota(jnp.int32, (C, LANES), 1) < GDN_DV

    @pl.when(c == 0)
    def _():
        for bb, j in probs:
            si = 0 if shared_sinit else bb
            s_scr[bb, j] = jnp.concatenate([sinit_ref[si, 2 * j], sinit_ref[si, 2 * j + 1]], axis=1)

    s = [s_scr[bb, j] for bb, j in probs]
    sbd = [_block_diag2(x.astype(BF16), GDN_DV) for x in s]
    ks = [_dot(kg_ref[bb, 0, :, ps[j]], sbd[i]) for i, (bb, j) in enumerate(probs)]
    qs = [_dot(qg_ref[bb, 0, :, ps[j]], sbd[i]) for i, (bb, j) in enumerate(probs)]
    r = [(vb_ref[bb, 0, :, ps[j]] - ks[i]).astype(BF16) for i, (bb, j) in enumerate(probs)]
    v_new = [_dot(t_ref[bb, 0, j], _block_diag2(r[i], GDN_DV)).astype(BF16) for i, (bb, j) in enumerate(probs)]
    vbd = [_block_diag2(x, GDN_DV) for x in v_new]
    o = [qs[i] + _dot(qk_ref[bb, 0, j], vbd[i]) for i, (bb, j) in enumerate(probs)]
    for i, (bb, j) in enumerate(probs):
        s_scr[bb, j] = s[i] * egl_ref[bb, 0, :, ps[j]] + _dot(kdt_ref[bb, 0, j], vbd[i])
    ng = ng_ref[...]
    for i, (bb, j) in enumerate(probs):
        o2 = o[i] * o[i]
        ms = jnp.where(lo, jnp.sum(jnp.where(lo, o2, 0.0), axis=-1, keepdims=True),
                       jnp.sum(jnp.where(lo, 0.0, o2), axis=-1, keepdims=True)) * (1.0 / GDN_DV)
        zp = z_ref[bb, :, ps[j]]
        o_ref[bb, :, ps[j]] = (o[i] * lax.rsqrt(ms + EPS) * ng * (zp * _sigmoid(zp))).astype(o_ref.dtype)

    @pl.when(c == n_chunks - 1)
    def _():
        for bb, j in probs:
            s_pair = s_scr[bb, j]
            sfin_ref[bb, 2 * j] = s_pair[:, :GDN_DV]
            sfin_ref[bb, 2 * j + 1] = s_pair[:, GDN_DV:]


def _gdn_call(qkv, z, ba, conv_init, s_init, cw, alog, dtb, ng, chunk, valid_len, name, nb_intra=1, nb_scan=1):
    B, T, _ = qkv.shape
    assert T % chunk == 0 and chunk % SUBLANES == 0 and GDN_DK == GDN_DV and 2 * GDN_DV == LANES
    n_chunks = T // chunk
    H, C = GDN_HEADS, chunk
    HP = H // 2
    shared_c = conv_init.shape[0] == 1
    shared_s = s_init.shape[0] == 1
    ng = jnp.tile(ng, (1, 2))
    params = pltpu.CompilerParams(dimension_semantics=("arbitrary", "arbitrary"), vmem_limit_bytes=VMEM_LIMIT)
    inter_dims = [(HP, C, 2 * C), (HP, C, 2 * C), (C, GDN_QK_W), (C, GDN_QK_W), (HP, GDN_DK, 2 * C),
                  (C, GDN_V_W), (1, GDN_V_W)]
    inter_dtypes = [BF16, BF16, BF16, BF16, BF16, F32, F32]

    def specs(nb):
        tok = lambda w: pl.BlockSpec((nb, chunk, w), lambda b, c: (b, c, 0))
        per_chunk = lambda *dims: pl.BlockSpec((nb, 1) + dims, lambda b, c: (b, c) + (0,) * len(dims))
        per_seq = lambda shared, *dims: pl.BlockSpec(
            ((1 if shared else nb),) + dims, lambda b, c: ((0 if shared else b),) + (0,) * len(dims))
        return tok, per_chunk, per_seq

    nb = nb_intra
    assert B % nb == 0
    tok, per_chunk, per_seq = specs(nb)
    prev_rows = pl.BlockSpec((nb, SUBLANES, CONV_CH),
                             lambda b, c: (b, jnp.maximum(c * (C // SUBLANES) - 1, 0), 0))
    inter = pl.pallas_call(
        functools.partial(_gdn_intra_kernel, chunk=chunk, valid_len=valid_len, n_chunks=n_chunks,
                          nb=nb, shared_cinit=shared_c),
        grid=(B // nb, n_chunks),
        in_specs=[
            tok(CONV_CH), prev_rows, tok(LANES), per_seq(shared_c, SUBLANES, CONV_CH),
            _const_spec(cw.shape), _const_spec(alog.shape), _const_spec(dtb.shape),
        ],
        out_specs=[per_chunk(*d) for d in inter_dims],
        out_shape=[jax.ShapeDtypeStruct((B, n_chunks) + d, dt) for d, dt in zip(inter_dims, inter_dtypes)],
        scratch_shapes=[pltpu.VMEM((nb, SUBLANES + chunk, CONV_CH), F32)],
        compiler_params=params,
        name=name + "_intra",
    )(qkv, qkv, ba, conv_init, cw, alog, dtb)

    nb = nb_scan
    assert B % nb == 0
    tok, per_chunk, per_seq = specs(nb)
    o, s_fin = pl.pallas_call(
        functools.partial(_gdn_scan_kernel, chunk=chunk, n_chunks=n_chunks, nb=nb, shared_sinit=shared_s),
        grid=(B // nb, n_chunks),
        in_specs=[per_chunk(*d) for d in inter_dims] + [
            tok(GDN_V_W), per_seq(shared_s, H, GDN_DK, GDN_DV), _const_spec(ng.shape),
        ],
        out_specs=[tok(GDN_V_W), per_seq(False, H, GDN_DK, GDN_DV)],
        out_shape=[
            jax.ShapeDtypeStruct((B, T, GDN_V_W), BF16),
            jax.ShapeDtypeStruct((B, H, GDN_DK, GDN_DV), F32),
        ],
        scratch_shapes=[pltpu.VMEM((nb, HP, GDN_DK, LANES), F32)],
        compiler_params=params,
        name=name + "_scan",
    )(*inter, z, s_init, ng)
    return o, s_fin


def _denominator_row(h):
    return MLA_D_V * (1 - h % 2)


def _attn_kernel(q_ref, k_ref, vt_ref, km_ref, vmt_ref, o_ref, m_scr, acc_scr, *, tq, n_meta, hps):
    qi = pl.program_id(2)
    neg = -jnp.inf
    key_m = lax.broadcasted_iota(jnp.int32, (km_ref.shape[0], tq), 0)
    key_d = lax.broadcasted_iota(jnp.int32, (tq, tq), 0)
    qry_d = lax.broadcasted_iota(jnp.int32, (tq, tq), 1)
    heads = range(hps)
    ls = [slice(h * HEAD_PAD, (h + 1) * HEAD_PAD) for h in heads]

    s = [jnp.where(key_m < n_meta, _dot_nt(km_ref[:, ls[h]], q_ref[0, :, ls[h]]), neg) for h in heads]
    m = [jnp.max(s[h], axis=0, keepdims=True) for h in heads]
    p = [jnp.exp2(s[h] - m[h]) for h in heads]
    for h in heads:
        m_scr[h] = m[h]
        acc_scr[h] = _dot(vmt_ref[ls[h], :], p[h].astype(BF16))

    def block(start, diagonal):
        s = [_dot_nt(k_ref[0, pl.ds(start, tq), ls[h]], q_ref[0, :, ls[h]]) for h in heads]
        if diagonal:
            s = [jnp.where(key_d <= qry_d, s[h], neg) for h in heads]
        m_old = [m_scr[h] for h in heads]
        m_new = [jnp.maximum(m_old[h], jnp.max(s[h], axis=0, keepdims=True)) for h in heads]
        alpha = [jnp.exp2(m_old[h] - m_new[h]) for h in heads]
        p = [jnp.exp2(s[h] - m_new[h]) for h in heads]
        pv = [_dot(vt_ref[ls[h], pl.ds(start, tq)], p[h].astype(BF16)) for h in heads]
        for h in heads:
            acc_scr[h] = alpha[h] * acc_scr[h] + pv[h]
            m_scr[h] = m_new[h]

    def body(kb, carry):
        block(pl.multiple_of(kb * tq, tq), False)
        return carry

    lax.fori_loop(0, qi, body, 0)
    block(pl.multiple_of(qi * tq, tq), True)
    slot_row = lax.broadcasted_iota(jnp.int32, (HEAD_PAD, tq), 0)

    def normalised(h):
        acc = acc_scr[h]
        r = _denominator_row(h)
        return jnp.where(slot_row == r, 0.0, acc) / acc[r:r + 1, :]

    for j in range(hps // 2):
        pair = normalised(2 * j) + normalised(2 * j + 1)
        o_ref[0, :, j * LANES:(j + 1) * LANES] = pair.T.astype(o_ref.dtype)


def _attn_call(qcat, kcat, vt, k_meta, vt_meta, tq, hps):
    B, T, _ = qcat.shape
    assert MLA_HEADS % hps == 0 and hps % 2 == 0
    n_groups = MLA_HEADS // hps
    gw = hps * HEAD_PAD
    ow = (hps // 2) * LANES
    kern = functools.partial(_attn_kernel, tq=tq, n_meta=N_META, hps=hps)
    return pl.pallas_call(
        kern,
        grid=(B, n_groups, T // tq),
        in_specs=[
            pl.BlockSpec((1, tq, gw), lambda b, g, qi: (b, qi, g)),
            pl.BlockSpec((1, T, gw), lambda b, g, qi: (b, 0, g)),
            pl.BlockSpec((gw, T), lambda b, g, qi: (g, b)),
            pl.BlockSpec((k_meta.shape[0], gw), lambda b, g, qi: (0, g)),
            pl.BlockSpec((gw, vt_meta.shape[1]), lambda b, g, qi: (g, 0)),
        ],
        out_specs=pl.BlockSpec((1, tq, ow), lambda b, g, qi: (b, qi, g)),
        out_shape=jax.ShapeDtypeStruct((B, T, n_groups * ow), BF16),
        scratch_shapes=[pltpu.VMEM((hps, 1, tq), F32), pltpu.VMEM((hps, HEAD_PAD, tq), F32)],
        compiler_params=pltpu.CompilerParams(dimension_semantics=("arbitrary", "arbitrary", "arbitrary"),
                                             vmem_limit_bytes=VMEM_LIMIT),
        name="attn",
    )(qcat, kcat, vt, k_meta, vt_meta)


def _dattn_kernel(pt_ref, qlat_ref, qpe_ref, latn_ref, kpen_ref, lat_hbm, kpe_hbm, o_ref,
                  lat_buf, kpe_buf, sem_lat, sem_kpe, m_scr, l_scr, acc_scr,
                  *, group, n_groups, n_sub, t_new, page):
    b = pl.program_id(0)
    n_seq = pl.num_programs(0)
    neg = -jnp.inf

    def page_copies(seq, g, k, slot):
        pid = pt_ref[seq, g * group + k]
        rows = pl.ds(pl.multiple_of(k * page, page), page)
        return (pltpu.make_async_copy(lat_hbm.at[0, pid], lat_buf.at[slot, rows, :], sem_lat.at[slot]),
                pltpu.make_async_copy(kpe_hbm.at[0, pid], kpe_buf.at[slot, k], sem_kpe.at[slot]))

    def start_group(seq, g, slot):
        def body(k2, carry):
            for prio in range(2):
                for cp in page_copies(seq, g, 2 * k2 + prio, slot):
                    cp.start(priority=prio)
            return carry
        lax.fori_loop(0, group // 2, body, 0)

    def wait_group(seq, g, slot):
        def body(k, carry):
            for cp in page_copies(seq, g, k, slot):
                cp.wait()
            return carry
        lax.fori_loop(0, group, body, 0)

    @pl.when(b == 0)
    def _():
        start_group(0, 0, 0)

    m_scr[...] = jnp.full(m_scr.shape, neg, F32)
    l_scr[...] = jnp.zeros(l_scr.shape, F32)
    acc_scr[...] = jnp.zeros(acc_scr.shape, F32)

    ql = qlat_ref[0]
    qp = qpe_ref[0]

    def update(scores, vals):
        ms = [jnp.max(s, axis=-1, keepdims=True) for s in scores]
        ps = [jnp.exp2(s - m) for s, m in zip(scores, ms)]
        ls = [jnp.sum(p, axis=-1, keepdims=True) for p in ps]
        pvs = [_dot(p.astype(BF16), v) for p, v in zip(ps, vals)]
        m_old = m_scr[...]
        m_new = m_old
        for m in ms:
            m_new = jnp.maximum(m_new, m)
        alpha = jnp.exp2(m_old - m_new)
        l_new = alpha * l_scr[...]
        acc_new = alpha * acc_scr[...]
        for m, l, pv in zip(ms, ls, pvs):
            w = jnp.exp2(m - m_new)
            l_new = l_new + w * l
            acc_new = acc_new + w * pv
        l_scr[...] = l_new
        acc_scr[...] = acc_new
        m_scr[...] = m_new

    sub_pages = group // n_sub
    sub_keys = sub_pages * page
    for g in range(n_groups):
        slot = g % 2
        if g + 1 < n_groups:
            start_group(b, g + 1, 1 - slot)
        else:
            @pl.when(b + 1 < n_seq)
            def _():
                start_group(b + 1, 0, 1 - slot)
        wait_group(b, g, slot)
        cbs = [lat_buf[slot, i * sub_keys:(i + 1) * sub_keys, :].astype(BF16) for i in range(n_sub)]
        kbts = [jnp.concatenate([kpe_buf[slot, k] for k in range(i * sub_pages, (i + 1) * sub_pages)],
                                axis=1).astype(BF16) for i in range(n_sub)]
        update([_dot_nt(ql, cb) + _dot(qp, kbt) for cb, kbt in zip(cbs, kbts)], cbs)

    pad = 2 * SUBLANES - latn_ref.shape[1]
    cn = jnp.concatenate([latn_ref[0], jnp.zeros((pad, latn_ref.shape[2]), F32)], axis=0).astype(BF16)
    kn = jnp.concatenate([kpen_ref[0], jnp.zeros((pad, kpen_ref.shape[2]), F32)], axis=0).astype(BF16)
    s = _dot_nt(ql, cn) + _dot_nt(qp, kn)
    tok = lax.shift_right_logical(lax.broadcasted_iota(jnp.int32, s.shape, 0), int(math.log2(MLA_HEADS)))
    key = lax.broadcasted_iota(jnp.int32, s.shape, 1)
    s = jnp.where((key <= tok) & (key < t_new), s, neg)
    update([s], [cn])
    o_ref[0] = (acc_scr[...] / l_scr[...]).astype(o_ref.dtype)


def _dattn_call(page_table, qlat, qpe, lat_new, kpe_new, cache_latent, cache_krope_t, t_new, group, n_sub):
    assert group % n_sub == 0 and group % 2 == 0
    Bs, R, kv_lora = qlat.shape
    n_pages = page_table.shape[1]
    assert n_pages % (2 * group) == 0, "the two buffer slots alternate, so a sequence needs an even group count"
    n_groups = n_pages // group
    page = cache_latent.shape[2]
    kern = functools.partial(_dattn_kernel, group=group, n_groups=n_groups, n_sub=n_sub, t_new=t_new, page=page)
    per_b = lambda shape: pl.BlockSpec((1,) + shape, lambda b, pt: (b, 0, 0))
    grid_spec = pltpu.PrefetchScalarGridSpec(
        num_scalar_prefetch=1,
        grid=(Bs,),
        in_specs=[per_b((R, kv_lora)), per_b((R, MLA_D_ROPE)),
                  per_b(lat_new.shape[1:]), per_b(kpe_new.shape[1:]),
                  pl.BlockSpec(memory_space=pl.ANY), pl.BlockSpec(memory_space=pl.ANY)],
        out_specs=per_b((R, kv_lora)),
        scratch_shapes=[pltpu.VMEM((2, group * page, kv_lora), F32),
                        pltpu.VMEM((2, group, MLA_D_ROPE, page), F32),
                        pltpu.SemaphoreType.DMA((2,)), pltpu.SemaphoreType.DMA((2,)),
                        pltpu.VMEM((R, 1), F32), pltpu.VMEM((R, 1), F32), pltpu.VMEM((R, kv_lora), F32)],
    )
    return pl.pallas_call(
        kern,
        grid_spec=grid_spec,
        out_shape=jax.ShapeDtypeStruct((Bs, R, kv_lora), BF16),
        compiler_params=pltpu.CompilerParams(dimension_semantics=("arbitrary",),
                                             vmem_limit_bytes=VMEM_LIMIT),
        name="dattn",
    )(page_table, qlat, qpe, lat_new, kpe_new, cache_latent, cache_krope_t)


def _post_kernel(*refs, absorbed_values):
    if absorbed_values:
        h_ref, og_ref, om_ref, wvbd_ref, wo1_ref, wo2_ref, g2_ref, wup_ref, wdn_ref, gf_ref, y_ref = refs
        om = _dot(om_ref[...], wvbd_ref[...]).astype(BF16)
    else:
        h_ref, og_ref, om_ref, wo1_ref, wo2_ref, g2_ref, wup_ref, wdn_ref, gf_ref, y_ref = refs
        om = om_ref[...]
    h2 = h_ref[...] + _dot(og_ref[...], wo1_ref[...]) + _dot(om, wo2_ref[...])
    hn = _rms(h2, g2_ref[...]).astype(BF16)
    u = jnp.maximum(_dot(hn, wup_ref[...]), 0.0)
    h3 = h2 + _dot((u * u).astype(BF16), wdn_ref[...])
    y_ref[...] = _rms(h3, gf_ref[...])


def _post_call(h2d, og, om, consts, tm, absorbed_values, name):
    n_rows, d_model = h2d.shape
    assert n_rows % tm == 0
    row_spec = lambda w: pl.BlockSpec((tm, w), lambda i: (i, 0))
    kern = functools.partial(_post_kernel, absorbed_values=absorbed_values)
    return pl.pallas_call(
        kern,
        grid=(n_rows // tm,),
        in_specs=[row_spec(d_model), row_spec(og.shape[1]), row_spec(om.shape[1])]
                 + [_const_spec(c.shape) for c in consts],
        out_specs=row_spec(d_model),
        out_shape=jax.ShapeDtypeStruct((n_rows, d_model), F32),
        compiler_params=pltpu.CompilerParams(dimension_semantics=("arbitrary",),
                                             vmem_limit_bytes=VMEM_LIMIT),
        name=name,
    )(h2d, og, om, *consts)


def _rope_parts(pos):
    half = MLA_D_ROPE // 2
    inv = ROPE_THETA ** (-jnp.arange(half, dtype=F32) / half)
    ang = pos.astype(F32)[:, None] * inv[None, :]
    cos = jnp.concatenate([jnp.cos(ang), jnp.cos(ang)], -1)
    sin_signed = jnp.concatenate([-jnp.sin(ang), jnp.sin(ang)], -1)
    return cos, sin_signed


def _pad_lanes(x, width):
    return jnp.pad(x, ((0, 0), (0, width - x.shape[1])))


def _at_lanes(x, lane0, width=LANES):
    return jnp.pad(x, ((0, 0), (lane0, width - lane0 - x.shape[1])))


def _key_tables(cos, sin_signed):
    return _at_lanes(cos, KPE_LANE0), _at_lanes(sin_signed, KPE_LANE0)


def _prompt_tables(pos):
    cos, sin_signed = _rope_parts(pos)
    n = pos.shape[0]
    half = MLA_D_ROPE // 2
    qcos = jnp.concatenate([jnp.full((n, MLA_D_NOPE), Q_SCALE_LOG2, F32), cos * Q_SCALE_LOG2], -1)
    qsin_lo = _at_lanes(sin_signed[:, :half] * Q_SCALE_LOG2, MLA_D_NOPE)
    qsin_hi = _at_lanes(sin_signed[:, half:] * Q_SCALE_LOG2, MLA_D_NOPE + half)
    kcos, ksin = _key_tables(cos, sin_signed)
    return jnp.concatenate([_pad_lanes(qcos, LANES), qsin_lo, qsin_hi, kcos, ksin], -1)


def _sample_tables(pos):
    cos, sin_signed = _rope_parts(pos)
    kcos, ksin = _key_tables(cos, sin_signed)
    return jnp.concatenate([jnp.tile(cos * Q_SCALE_LOG2, (1, MLA_HEADS)),
                            jnp.tile(sin_signed * Q_SCALE_LOG2, (1, MLA_HEADS)), kcos, ksin], -1)


def _swap_halves(w):
    half = MLA_D_ROPE // 2
    return jnp.concatenate([w[..., half:], w[..., :half]], -1)


def kernel(x_prompt, x_sample, cache_latent, cache_krope, state_conv, state_ssm, page_table,
           meta_tokens, norm_mix_g, w_in, conv_w, a_log, dt_bias, gdn_norm_g, q_norm_g, w_q_b,
           kv_norm_g, w_kv_b, w_out, norm_mlp_g, w_up, w_down, final_norm_g):
    assert w_in.shape[0] == 1, "single-layer problem"
    B, T, D = x_prompt.shape
    Bs, Ts, _ = x_sample.shape
    assert CONV_W - 1 <= Ts <= SAMPLE_T_PAD
    past_len = page_table.shape[1] * cache_latent.shape[2]
    q_lora = q_norm_g.shape[1]
    kv_lora = kv_norm_g.shape[1]
    H = MLA_HEADS

    wi = w_in[0]
    c0 = CONV_CH + GDN_V_W
    c1 = c0 + 2 * GDN_HEADS
    wkpe_raw = wi[:, c1 + q_lora + kv_lora:]
    blk_a = jnp.concatenate([_pad_lanes(wi[:, c0:c1], KPE_LANE0), _pad_lanes(wkpe_raw, LANES - KPE_LANE0)], 1)
    blk_b = _at_lanes(_swap_halves(wkpe_raw), KPE_LANE0)
    win = jnp.concatenate([wi[:, :c0], wi[:, c1:c1 + q_lora], blk_a,
                           wi[:, c1 + q_lora:c1 + q_lora + kv_lora], blk_b], axis=1).astype(BF16)
    gmix = norm_mix_g[0][None, :]
    qg = q_norm_g[0][None, :]
    kvg = kv_norm_g[0][None, :]
    wq = w_q_b[0]
    wq_nope, wq_pe = wq[..., :MLA_D_NOPE], wq[..., MLA_D_NOPE:]
    wq_pesw = _swap_halves(wq_pe)
    zq = lambda n: jnp.zeros((q_lora, H, n), F32)
    wqcat = jnp.concatenate([wq_nope, wq_pe, zq(HEAD_PAD - MLA_D_NOPE - MLA_D_ROPE)], -1)
    wqcat = wqcat.reshape(q_lora, H * HEAD_PAD).astype(BF16)
    wkv = w_kv_b[0]
    wkb, wvb = wkv[..., :MLA_D_NOPE], wkv[..., MLA_D_NOPE:]
    zk = lambda n: jnp.zeros((kv_lora, H, n), F32)
    wkb_sp = jnp.concatenate([wkb, zk(HEAD_PAD - MLA_D_NOPE)], -1).reshape(kv_lora, H * HEAD_PAD).astype(BF16)
    odd = (jnp.arange(H) % 2 == 1)[None, :, None]
    wvb_sp = jnp.where(odd, jnp.concatenate([zk(HEAD_PAD - MLA_D_V), wvb], -1),
                       jnp.concatenate([wvb, zk(HEAD_PAD - MLA_D_V)], -1))
    wvb_sp_t = wvb_sp.reshape(kv_lora, H * HEAD_PAD).T.astype(BF16)
    eye_h = jnp.eye(H, dtype=bool)
    wabs = jnp.where(eye_h[:, None, :, None], wkb.transpose(1, 2, 0)[:, :, None, :], 0.0)
    wabs = wabs.reshape(H * MLA_D_NOPE, H * kv_lora).astype(BF16)
    wvbd = jnp.where(eye_h[:, None, :, None], wvb.transpose(1, 0, 2)[:, :, None, :], 0.0)
    wvbd = wvbd.reshape(H * kv_lora, H * MLA_D_V).astype(BF16)
    wo1 = w_out[0][:GDN_V_W].astype(BF16)
    wo2 = w_out[0][GDN_V_W:].astype(BF16)
    g2 = norm_mlp_g[0][None, :]
    wup = w_up[0].astype(BF16)
    wdn = w_down[0].astype(BF16)
    gf = final_norm_g[None, :]
    cw = jnp.pad(conv_w[0], ((0, SUBLANES - CONV_W), (0, 0)))
    lane_row = lambda v: jnp.pad(v[None, :], ((0, 0), (GDN_HEADS, LANES - 2 * GDN_HEADS)))
    alog = lane_row(a_log[0])
    dtb = lane_row(dt_bias[0])
    ng = gdn_norm_g[0][None, :]

    common = (gmix, win, qg, kvg)
    slot_row = jnp.arange(H * HEAD_PAD) % HEAD_PAD
    vones = (slot_row == _denominator_row(jnp.arange(H * HEAD_PAD) // HEAD_PAD)).astype(F32)[:, None]
    prompt_consts = common + (wqcat, wkb_sp, wvb_sp_t, vones)
    proj_prompt = functools.partial(_proj_prompt_kernel, q_lora=q_lora, kv_lora=kv_lora)
    proj_sample = functools.partial(_proj_sample_kernel, q_lora=q_lora, kv_lora=kv_lora)
    sample_consts = common + (wq_nope.reshape(q_lora, -1).astype(BF16), wq_pe.reshape(q_lora, -1).astype(BF16),
                              wq_pesw.reshape(q_lora, -1).astype(BF16), wabs)
    pw = [CONV_CH, GDN_V_W, LANES, kv_lora, MLA_D_ROPE, H * HEAD_PAD, H * HEAD_PAD, H * HEAD_PAD]
    pd = [F32, F32, F32, F32, F32, BF16, BF16, BF16]

    tab_m = _prompt_tables(jnp.arange(N_META))
    qkv_m, z_m, ba_m, lat_m, kpe_m, _, kcat_m, vt_m = _proj_call(
        proj_prompt, meta_tokens, tab_m, prompt_consts, pw, pd, N_META, "proj_meta", transposed_outs=(7,))
    zero_conv = jnp.zeros((1, SUBLANES, CONV_CH), F32)
    zero_s = jnp.zeros((1, GDN_HEADS, GDN_DK, GDN_DV), F32)
    _, s1 = _gdn_call(qkv_m[None], z_m[None], ba_m[None], zero_conv, zero_s, cw, alog, dtb, ng,
                      N_META, N_META, "gdn_meta")
    conv1 = jnp.pad(qkv_m[N_META - (CONV_W - 1):], ((SUBLANES - (CONV_W - 1), 0), (0, 0)))[None]
    k_meta = jnp.pad(kcat_m, ((0, LANES - N_META), (0, 0)))
    vt_meta = jnp.pad(vt_m, ((0, 0), (0, LANES - N_META)))

    tab_p = _prompt_tables(N_META + jnp.arange(T))
    xp2d = x_prompt.reshape(B * T, D)
    qkv_p, z_p, ba_p, lat_p, kpe_p, qcat_p, kcat_p, vt_p = _proj_call(
        proj_prompt, xp2d, tab_p, prompt_consts, pw, pd, 512, "proj_prompt", transposed_outs=(7,))
    r3 = lambda a: a.reshape(B, T, a.shape[-1])
    qkv_p3 = r3(qkv_p)
    og_p, s2 = _gdn_call(qkv_p3, r3(z_p), r3(ba_p), conv1, s1, cw, alog, dtb, ng, GDN_CHUNK, T, "gdn_prompt",
                         nb_intra=8, nb_scan=8)
    om_p = _attn_call(r3(qcat_p), r3(kcat_p), vt_p, k_meta, vt_meta, 256, 8)
    post_consts = (wo1, wo2, g2, wup, wdn, gf)
    y_p = _post_call(xp2d, og_p.reshape(B * T, -1), om_p.reshape(B * T, -1), post_consts, 512, False, "post_prompt")

    tp = SAMPLE_T_PAD
    xs = jnp.pad(x_sample, ((0, 0), (0, tp - Ts), (0, 0))).reshape(Bs * tp, D)
    tab_s = jnp.tile(_sample_tables(past_len + jnp.arange(tp)), (Bs, 1))
    sw = [CONV_CH, GDN_V_W, LANES, kv_lora, MLA_D_ROPE, H * kv_lora, H * MLA_D_ROPE]
    sd = [F32, F32, F32, F32, F32, BF16, BF16]
    qkv_s, z_s, ba_s, lat_s, kpe_s, qlat_s, qpe_s = _proj_call(
        proj_sample, xs, tab_s, sample_consts, sw, sd, Bs * tp, "proj_sample")
    s3 = lambda a: a.reshape(Bs, tp, a.shape[-1])
    qkv_s3 = s3(qkv_s)
    conv_in_s = jnp.pad(state_conv[0], ((0, 0), (SUBLANES - (CONV_W - 1), 0), (0, 0)))
    og_s, s_new = _gdn_call(qkv_s3, s3(z_s), s3(ba_s), conv_in_s, state_ssm[0], cw, alog, dtb, ng, tp, Ts, "gdn_sample",
                            nb_intra=4, nb_scan=4)
    out_lat = _dattn_call(page_table, qlat_s.reshape(Bs, tp * H, kv_lora), qpe_s.reshape(Bs, tp * H, MLA_D_ROPE),
                          s3(lat_s), s3(kpe_s), cache_latent, jnp.swapaxes(cache_krope, 2, 3), Ts, 32, 4)
    sample_post_consts = (wvbd,) + post_consts
    y_s = _post_call(xs, og_s.reshape(Bs * tp, -1), out_lat.reshape(Bs * tp, H * kv_lora),
                     sample_post_consts, Bs * tp, True, "post_sample")

    bc = lambda a: jnp.broadcast_to(a[None], (B,) + a.shape)
    lat_po = jnp.concatenate([bc(lat_m), r3(lat_p)], axis=1)[None]
    kpe_po = jnp.concatenate([bc(kpe_m), r3(kpe_p)], axis=1)[None]
    conv_po = qkv_p3[:, T - (CONV_W - 1):][None]
    conv_so = qkv_s3[:, Ts - (CONV_W - 1):Ts][None]
    return (y_p.reshape(B, T, D), y_s.reshape(Bs, tp, D)[:, :Ts],
            lat_po, kpe_po, conv_po, s2[None],
            s3(lat_s)[:, :Ts][None], s3(kpe_s)[:, :Ts][None], conv_so, s_new[None])
```

```python
import functools
import math

import jax
import jax.numpy as jnp
from jax import lax
from jax.experimental import pallas as pl
from jax.experimental.pallas import tpu as pltpu

F32 = jnp.float32
BF16 = jnp.bfloat16

N_META = 16
EPS = 1e-6
GDN_HEADS = 8
GDN_DK = 64
GDN_DV = 64
CONV_W = 4
GDN_CHUNK = 64
MLA_HEADS = 8
MLA_D_NOPE = 64
MLA_D_ROPE = 32
MLA_D_V = 64
ROPE_THETA = 10000.0
MLA_SCALE = (MLA_D_NOPE + MLA_D_ROPE) ** -0.5
Q_SCALE_LOG2 = MLA_SCALE * math.log2(math.e)
GDN_QK_W = GDN_HEADS * GDN_DK
GDN_V_W = GDN_HEADS * GDN_DV
CONV_CH = 2 * GDN_QK_W + GDN_V_W

LANES = 128
SUBLANES = 8
HEAD_PAD = 128
SAMPLE_T_PAD = 8
KPE_LANE0 = 32
VMEM_LIMIT = 56 * 1024 * 1024


def _dot(a, b):
    return jnp.dot(a, b, preferred_element_type=F32)


def _dot_nt(a, b):
    return lax.dot_general(a, b, (((1,), (1,)), ((), ())), preferred_element_type=F32)


def _rms(x, g):
    return x * lax.rsqrt(jnp.mean(x * x, axis=-1, keepdims=True) + EPS) * g


def _sigmoid(x):
    return 1.0 / (1.0 + jnp.exp(-x))


def _const_spec(shape):
    nd = len(shape)
    return pl.BlockSpec(shape, lambda *_: (0,) * nd)


def _proj_common(x_ref, gmix_ref, win_ref, qg_ref, kvg_ref, kcos, ksin, qkv_ref, z_ref, ba_ref, lat_ref, kpe_ref,
                 *, q_lora, kv_lora):
    hn = _rms(x_ref[...], gmix_ref[...]).astype(BF16)
    p = _dot(hn, win_ref[...])
    c0 = CONV_CH + GDN_V_W
    c1 = c0 + q_lora
    c2 = c1 + LANES
    c3 = c2 + kv_lora
    qkv_ref[...] = p[:, :CONV_CH]
    z_ref[...] = p[:, CONV_CH:c0]
    blk_a = p[:, c1:c2]
    ba_ref[...] = blk_a
    kpe = blk_a * kcos + p[:, c3:c3 + LANES] * ksin
    kpe_ref[...] = kpe[:, KPE_LANE0:KPE_LANE0 + MLA_D_ROPE]
    lat = _rms(p[:, c2:c3], kvg_ref[...])
    lat_ref[...] = lat
    cqn = _rms(p[:, c0:c1], qg_ref[...]).astype(BF16)
    return cqn, lat.astype(BF16), kpe


def _proj_prompt_kernel(x_ref, tab_ref, gmix_ref, win_ref, qg_ref, kvg_ref, wq_ref, wkb_ref, wvb_ref, vones_ref,
                        qkv_ref, z_ref, ba_ref, lat_ref, kpe_ref, qcat_ref, kcat_ref, vt_ref, *, q_lora, kv_lora):
    qcos = tab_ref[:, 0 * LANES:1 * LANES]
    qsin_lo = tab_ref[:, 1 * LANES:2 * LANES]
    qsin_hi = tab_ref[:, 2 * LANES:3 * LANES]
    kcos = tab_ref[:, 3 * LANES:4 * LANES]
    ksin = tab_ref[:, 4 * LANES:5 * LANES]
    cqn, latb, kpe = _proj_common(x_ref, gmix_ref, win_ref, qg_ref, kvg_ref, kcos, ksin,
                                  qkv_ref, z_ref, ba_ref, lat_ref, kpe_ref, q_lora=q_lora, kv_lora=kv_lora)
    qa = _dot(cqn, wq_ref[...])
    knope = _dot(latb, wkb_ref[...])
    half = MLA_D_ROPE // 2
    kpe_slot = pltpu.roll(kpe, MLA_D_NOPE - KPE_LANE0, axis=1)
    for h in range(MLA_HEADS):
        sl = slice(h * HEAD_PAD, (h + 1) * HEAD_PAD)
        qs = qa[:, sl]
        q_dn = pltpu.roll(qs, HEAD_PAD - half, axis=1)
        q_up = pltpu.roll(qs, half, axis=1)
        qcat_ref[:, sl] = (qs * qcos + q_dn * qsin_lo + q_up * qsin_hi).astype(BF16)
        kcat_ref[:, sl] = (knope[:, sl] + kpe_slot).astype(BF16)
    vt_ref[...] = (_dot_nt(wvb_ref[...], latb) + vones_ref[...]).astype(BF16)


def _proj_sample_kernel(x_ref, tab_ref, gmix_ref, win_ref, qg_ref, kvg_ref, wqn_ref, wqpe_ref, wqpesw_ref, wabs_ref,
                        qkv_ref, z_ref, ba_ref, lat_ref, kpe_ref, qlat_ref, qpe_ref, *, q_lora, kv_lora):
    pe_w = MLA_HEADS * MLA_D_ROPE
    qcos = tab_ref[:, 0:pe_w]
    qsin = tab_ref[:, pe_w:2 * pe_w]
    kcos = tab_ref[:, 2 * pe_w:2 * pe_w + LANES]
    ksin = tab_ref[:, 2 * pe_w + LANES:2 * pe_w + 2 * LANES]
    cqn, _, _ = _proj_common(x_ref, gmix_ref, win_ref, qg_ref, kvg_ref, kcos, ksin,
                             qkv_ref, z_ref, ba_ref, lat_ref, kpe_ref, q_lora=q_lora, kv_lora=kv_lora)
    qn = _dot(cqn, wqn_ref[...]).astype(BF16)
    qlat_ref[...] = (_dot(qn, wabs_ref[...]) * Q_SCALE_LOG2).astype(BF16)
    qpe_ref[...] = (_dot(cqn, wqpe_ref[...]) * qcos + _dot(cqn, wqpesw_ref[...]) * qsin).astype(BF16)


def _proj_call(kernel_fn, x2d, tab, consts, out_widths, out_dtypes, tm, name, transposed_outs=()):
    n_rows, d_model = x2d.shape
    assert n_rows % tm == 0
    row_spec = lambda w: pl.BlockSpec((tm, w), lambda i: (i, 0))
    col_spec = lambda w: pl.BlockSpec((w, tm), lambda i: (0, i))
    is_t = [k in transposed_outs for k in range(len(out_widths))]
    assert tab.shape[0] % tm == 0 and n_rows % tab.shape[0] == 0
    tab_blocks = tab.shape[0] // tm
    tab_spec = pl.BlockSpec((tm, tab.shape[1]), lambda i: (i % tab_blocks, 0))
    in_specs = [row_spec(d_model), tab_spec] + [_const_spec(c.shape) for c in consts]
    return pl.pallas_call(
        kernel_fn,
        grid=(n_rows // tm,),
        in_specs=in_specs,
        out_specs=[col_spec(w) if t else row_spec(w) for w, t in zip(out_widths, is_t)],
        out_shape=[jax.ShapeDtypeStruct((w, n_rows) if t else (n_rows, w), dt)
                   for w, dt, t in zip(out_widths, out_dtypes, is_t)],
        compiler_params=pltpu.CompilerParams(dimension_semantics=("arbitrary",),
                                             vmem_limit_bytes=VMEM_LIMIT),
        name=name,
    )(x2d, tab, *consts)


def _block_diag2(x, half):
    lane = lax.broadcasted_iota(jnp.int32, x.shape, 1)
    zero = jnp.zeros_like(x)
    return jnp.concatenate([jnp.where(lane < half, x, zero), jnp.where(lane < half, zero, x)], axis=0)


def _gdn_intra_prep(bb, cinit_bb, qkv_ref, prev_ref, ba_ref, cinit_ref, cw_ref, alog_ref, dtb_ref,
                    qk_mask, kg_ref, qg_ref, kdt_ref, vb_ref, egl_ref, xbuf, *, chunk, valid_len, n_chunks):
    c = pl.program_id(1)
    C = chunk
    H = GDN_HEADS
    tail0 = SUBLANES - (CONV_W - 1)

    xbuf[bb, 0:SUBLANES, :] = jnp.where(c == 0, cinit_ref[cinit_bb], prev_ref[bb])
    xbuf[bb, SUBLANES:SUBLANES + C, :] = qkv_ref[bb]
    cw = cw_ref[...]
    y = xbuf[bb, tail0:tail0 + C, :] * cw[0:1, :]
    for j in range(1, CONV_W):
        y = y + xbuf[bb, tail0 + j:tail0 + j + C, :] * cw[j:j + 1, :]
    y = y * _sigmoid(y)

    ba = ba_ref[bb]
    beta_all = _sigmoid(ba)
    sp_in = ba + dtb_ref[...]
    softplus = jnp.maximum(sp_in, 0.0) + jnp.log1p(jnp.exp(-jnp.abs(sp_in)))
    g_all = -jnp.exp(alog_ref[...]) * softplus
    if valid_len < n_chunks * C:
        row = c * C + lax.broadcasted_iota(jnp.int32, (C, LANES), 0)
        beta_all = jnp.where(row < valid_len, beta_all, 0.0)
        g_all = jnp.where(row < valid_len, g_all, 0.0)

    ri = lax.broadcasted_iota(jnp.int32, (C, C), 0)
    ci = lax.broadcasted_iota(jnp.int32, (C, C), 1)
    lower = ri >= ci
    gc_all = lax.dot_general(lower.astype(F32), g_all, (((1,), (0,)), ((), ())),
                             precision=lax.Precision.HIGHEST, preferred_element_type=F32)
    sel = (lax.broadcasted_iota(jnp.int32, (SUBLANES, LANES), 1)
           == lax.broadcasted_iota(jnp.int32, (SUBLANES, LANES), 0) + GDN_HEADS).astype(F32)
    gc_t = lax.dot_general(sel, gc_all, (((1,), (1,)), ((), ())),
                           precision=lax.Precision.HIGHEST, preferred_element_type=F32)

    pairs = range(H // 2)
    lane = lax.broadcasted_iota(jnp.int32, (C, LANES), 1)
    lo = lane < GDN_DK
    lo_c, lower_p = qk_mask
    ii_r = lax.broadcasted_iota(jnp.int32, (GDN_DK, LANES), 0)
    ii_c = lax.broadcasted_iota(jnp.int32, (GDN_DK, LANES), 1)
    eye2 = ((ii_c == ii_r) | (ii_c == ii_r + GDN_DK)).astype(BF16)

    def col(x, idx):
        return jnp.sum(jnp.where(lane == idx, x, 0.0), axis=-1, keepdims=True)

    def half_sums(x):
        return jnp.where(lo, jnp.sum(jnp.where(lo, x, 0.0), axis=-1, keepdims=True),
                         jnp.sum(jnp.where(lo, 0.0, x), axis=-1, keepdims=True))

    kn_b, kb_b, qn_b, decay = [], [], [], []
    for j in pairs:
        ps = slice(j * LANES, (j + 1) * LANES)
        qp = y[:, j * LANES:(j + 1) * LANES]
        kp = y[:, GDN_QK_W + j * LANES:GDN_QK_W + (j + 1) * LANES]
        vp = y[:, 2 * GDN_QK_W + j * LANES:2 * GDN_QK_W + (j + 1) * LANES]
        qn = qp * lax.rsqrt(half_sums(qp * qp) + EPS) * (GDN_DK ** -0.5)
        kn = kp * lax.rsqrt(half_sums(kp * kp) + EPS)
        beta = jnp.where(lo, col(beta_all, 2 * j), col(beta_all, 2 * j + 1))
        gc0, gc1 = col(gc_all, H + 2 * j), col(gc_all, H + 2 * j + 1)
        gcol = jnp.where(lo, gc0, gc1)
        grow = jnp.concatenate([gc_t[2 * j:2 * j + 1, :], gc_t[2 * j + 1:2 * j + 2, :]], axis=1)
        diff = jnp.where(lo_c, gc0, gc1) - grow
        decay.append(jnp.where(lower_p, jnp.exp(jnp.where(lower_p, diff, 0.0)), 0.0))
        egc = jnp.exp(gcol)
        gl = gcol[C - 1:C, :]
        egl_ref[bb, 0, :, ps] = jnp.exp(gl)
        kb = kn * beta
        vb_ref[bb, 0, :, ps] = vp * beta
        kg_ref[bb, 0, :, ps] = (kb * egc).astype(BF16)
        qg_ref[bb, 0, :, ps] = (qn * egc).astype(BF16)
        k_dec = (kn * jnp.exp(gl - gcol)).astype(BF16)
        kdt_ref[bb, 0, j] = _dot_nt(eye2, _block_diag2(k_dec, GDN_DK)).astype(BF16)
        kn_b.append(_block_diag2(kn.astype(BF16), GDN_DK))
        kb_b.append(kb.astype(BF16))
        qn_b.append(qn.astype(BF16))
    return kn_b, kb_b, qn_b, decay


def _gdn_intra_kernel(qkv_ref, prev_ref, ba_ref, cinit_ref, cw_ref, alog_ref, dtb_ref,
                      t_ref, qk_ref, kg_ref, qg_ref, kdt_ref, vb_ref, egl_ref, xbuf,
                      *, chunk, valid_len, n_chunks, nb, shared_cinit):
    C = chunk
    rp = lax.broadcasted_iota(jnp.int32, (C, 2 * C), 0)
    cp = lax.broadcasted_iota(jnp.int32, (C, 2 * C), 1)
    lo_c = cp < C
    cmod = jnp.where(lo_c, cp, cp - C)
    lower_p, strict_p = rp >= cmod, rp > cmod
    eye_p = (rp == cmod).astype(F32)
    kn_b, kb_b, qn_b, decay = [], [], [], []
    for bb in range(nb):
        parts = _gdn_intra_prep(bb, 0 if shared_cinit else bb, qkv_ref, prev_ref, ba_ref, cinit_ref, cw_ref,
                                alog_ref, dtb_ref, (lo_c, lower_p), kg_ref, qg_ref, kdt_ref, vb_ref, egl_ref, xbuf,
                                chunk=chunk, valid_len=valid_len, n_chunks=n_chunks)
        for dst, src in zip((kn_b, kb_b, qn_b, decay), parts):
            dst.extend(src)
    n_pairs = GDN_HEADS // 2
    probs = range(nb * n_pairs)
    kk = [_dot_nt(kb_b[i], kn_b[i]) for i in probs]
    qk = [_dot_nt(qn_b[i], kn_b[i]) for i in probs]
    for i in probs:
        qk_ref[i // n_pairs, 0, i % n_pairs] = jnp.where(lower_p, qk[i] * decay[i], 0.0).astype(BF16)
    p = [jnp.where(strict_p, -(kk[i] * decay[i]), 0.0) for i in probs]
    t = [eye_p + p[i] for i in probs]
    n_iter = max(1, int(math.ceil(math.log2(C))))
    for _ in range(1, n_iter):
        pb = [p[i].astype(BF16) for i in probs]
        p = [_dot(pb[i], _block_diag2(pb[i], C)) for i in probs]
        t = [t[i] + _dot(t[i].astype(BF16), _block_diag2(p[i].astype(BF16), C)) for i in probs]
    for i in probs:
        t_ref[i // n_pairs, 0, i % n_pairs] = t[i].astype(BF16)


def _gdn_scan_kernel(t_ref, qk_ref, kg_ref, qg_ref, kdt_ref, vb_ref, egl_ref, z_ref, sinit_ref, ng_ref,
                     o_ref, sfin_ref, s_scr, *, chunk, n_chunks, nb, shared_sinit):
    c = pl.program_id(1)
    C = chunk
    n_pairs = GDN_HEADS // 2
    probs = [(bb, j) for bb in range(nb) for j in range(n_pairs)]
    ps = [slice(j * LANES, (j + 1) * LANES) for j in range(n_pairs)]
    lo = lax.broadcasted_iota(jnp.int32, (C, LANES), 1) < GDN_DV

    @pl.when(c == 0)
    def _():
        for bb, j in probs:
            si = 0 if shared_sinit else bb
            s_scr[bb, j] = jnp.concatenate([sinit_ref[si, 2 * j], sinit_ref[si, 2 * j + 1]], axis=1)

    s = [s_scr[bb, j] for bb, j in probs]
    sbd = [_block_diag2(x.astype(BF16), GDN_DV) for x in s]
    ks = [_dot(kg_ref[bb, 0, :, ps[j]], sbd[i]) for i, (bb, j) in enumerate(probs)]
    qs = [_dot(qg_ref[bb, 0, :, ps[j]], sbd[i]) for i, (bb, j) in enumerate(probs)]
    r = [(vb_ref[bb, 0, :, ps[j]] - ks[i]).astype(BF16) for i, (bb, j) in enumerate(probs)]
    v_new = [_dot(t_ref[bb, 0, j], _block_diag2(r[i], GDN_DV)).astype(BF16) for i, (bb, j) in enumerate(probs)]
    vbd = [_block_diag2(x, GDN_DV) for x in v_new]
    o = [qs[i] + _dot(qk_ref[bb, 0, j], vbd[i]) for i, (bb, j) in enumerate(probs)]
    for i, (bb, j) in enumerate(probs):
        s_scr[bb, j] = s[i] * egl_ref[bb, 0, :, ps[j]] + _dot(kdt_ref[bb, 0, j], vbd[i])
    ng = ng_ref[...]
    for i, (bb, j) in enumerate(probs):
        o2 = o[i] * o[i]
        ms = jnp.where(lo, jnp.sum(jnp.where(lo, o2, 0.0), axis=-1, keepdims=True),
                       jnp.sum(jnp.where(lo, 0.0, o2), axis=-1, keepdims=True)) * (1.0 / GDN_DV)
        zp = z_ref[bb, :, ps[j]]
        o_ref[bb, :, ps[j]] = (o[i] * lax.rsqrt(ms + EPS) * ng * (zp * _sigmoid(zp))).astype(o_ref.dtype)

    @pl.when(c == n_chunks - 1)
    def _():
        for bb, j in probs:
            s_pair = s_scr[bb, j]
            sfin_ref[bb, 2 * j] = s_pair[:, :GDN_DV]
            sfin_ref[bb, 2 * j + 1] = s_pair[:, GDN_DV:]


def _gdn_call(qkv, z, ba, conv_init, s_init, cw, alog, dtb, ng, chunk, valid_len, name, nb_intra=1, nb_scan=1):
    B, T, _ = qkv.shape
    assert T % chunk == 0 and chunk % SUBLANES == 0 and GDN_DK == GDN_DV and 2 * GDN_DV == LANES
    n_chunks = T // chunk
    H, C = GDN_HEADS, chunk
    HP = H // 2
    shared_c = conv_init.shape[0] == 1
    shared_s = s_init.shape[0] == 1
    ng = jnp.tile(ng, (1, 2))
    params = pltpu.CompilerParams(dimension_semantics=("arbitrary", "arbitrary"), vmem_limit_bytes=VMEM_LIMIT)
    inter_dims = [(HP, C, 2 * C), (HP, C, 2 * C), (C, GDN_QK_W), (C, GDN_QK_W), (HP, GDN_DK, 2 * C),
                  (C, GDN_V_W), (1, GDN_V_W)]
    inter_dtypes = [BF16, BF16, BF16, BF16, BF16, F32, F32]

    def specs(nb):
        tok = lambda w: pl.BlockSpec((nb, chunk, w), lambda b, c: (b, c, 0))
        per_chunk = lambda *dims: pl.BlockSpec((nb, 1) + dims, lambda b, c: (b, c) + (0,) * len(dims))
        per_seq = lambda shared, *dims: pl.BlockSpec(
            ((1 if shared else nb),) + dims, lambda b, c: ((0 if shared else b),) + (0,) * len(dims))
        return tok, per_chunk, per_seq

    nb = nb_intra
    assert B % nb == 0
    tok, per_chunk, per_seq = specs(nb)
    prev_rows = pl.BlockSpec((nb, SUBLANES, CONV_CH),
                             lambda b, c: (b, jnp.maximum(c * (C // SUBLANES) - 1, 0), 0))
    inter = pl.pallas_call(
        functools.partial(_gdn_intra_kernel, chunk=chunk, valid_len=valid_len, n_chunks=n_chunks,
                          nb=nb, shared_cinit=shared_c),
        grid=(B // nb, n_chunks),
        in_specs=[
            tok(CONV_CH), prev_rows, tok(LANES), per_seq(shared_c, SUBLANES, CONV_CH),
            _const_spec(cw.shape), _const_spec(alog.shape), _const_spec(dtb.shape),
        ],
        out_specs=[per_chunk(*d) for d in inter_dims],
        out_shape=[jax.ShapeDtypeStruct((B, n_chunks) + d, dt) for d, dt in zip(inter_dims, inter_dtypes)],
        scratch_shapes=[pltpu.VMEM((nb, SUBLANES + chunk, CONV_CH), F32)],
        compiler_params=params,
        name=name + "_intra",
    )(qkv, qkv, ba, conv_init, cw, alog, dtb)

    nb = nb_scan
    assert B % nb == 0
    tok, per_chunk, per_seq = specs(nb)
    o, s_fin = pl.pallas_call(
        functools.partial(_gdn_scan_kernel, chunk=chunk, n_chunks=n_chunks, nb=nb, shared_sinit=shared_s),
        grid=(B // nb, n_chunks),
        in_specs=[per_chunk(*d) for d in inter_dims] + [
            tok(GDN_V_W), per_seq(shared_s, H, GDN_DK, GDN_DV), _const_spec(ng.shape),
        ],
        out_specs=[tok(GDN_V_W), per_seq(False, H, GDN_DK, GDN_DV)],
        out_shape=[
            jax.ShapeDtypeStruct((B, T, GDN_V_W), BF16),
            jax.ShapeDtypeStruct((B, H, GDN_DK, GDN_DV), F32),
        ],
        scratch_shapes=[pltpu.VMEM((nb, HP, GDN_DK, LANES), F32)],
        compiler_params=params,
        name=name + "_scan",
    )(*inter, z, s_init, ng)
    return o, s_fin


def _denominator_row(h):
    return MLA_D_V * (1 - h % 2)


def _attn_kernel(q_ref, k_ref, vt_ref, km_ref, vmt_ref, o_ref, m_scr, acc_scr, *, tq, n_meta, hps):
    qi = pl.program_id(2)
    neg = -jnp.inf
    key_m = lax.broadcasted_iota(jnp.int32, (km_ref.shape[0], tq), 0)
    key_d = lax.broadcasted_iota(jnp.int32, (tq, tq), 0)
    qry_d = lax.broadcasted_iota(jnp.int32, (tq, tq), 1)
    heads = range(hps)
    ls = [slice(h * HEAD_PAD, (h + 1) * HEAD_PAD) for h in heads]

    s = [jnp.where(key_m < n_meta, _dot_nt(km_ref[:, ls[h]], q_ref[0, :, ls[h]]), neg) for h in heads]
    m = [jnp.max(s[h], axis=0, keepdims=True) for h in heads]
    p = [jnp.exp2(s[h] - m[h]) for h in heads]
    for h in heads:
        m_scr[h] = m[h]
        acc_scr[h] = _dot(vmt_ref[ls[h], :], p[h].astype(BF16))

    def blocks(specs):
        chains = [(start, diag, h) for start, diag in specs for h in heads]
        s = [_dot_nt(k_ref[0, pl.ds(start, tq), ls[h]], q_ref[0, :, ls[h]]) for start, _, h in chains]
        s = [jnp.where(key_d <= qry_d, x, neg) if diag else x for x, (_, diag, _) in zip(s, chains)]
        mb = [jnp.max(x, axis=0, keepdims=True) for x in s]
        p = [jnp.exp2(x - m) for x, m in zip(s, mb)]
        pv = [_dot(vt_ref[ls[h], pl.ds(start, tq)], x.astype(BF16)) for x, (start, _, h) in zip(p, chains)]
        for h in heads:
            mine = [i for i, c in enumerate(chains) if c[2] == h]
            m_old = m_scr[h]
            m_new = m_old
            for i in mine:
                m_new = jnp.maximum(m_new, mb[i])
            acc = jnp.exp2(m_old - m_new) * acc_scr[h]
            for i in mine:
                acc = acc + jnp.exp2(mb[i] - m_new) * pv[i]
            acc_scr[h] = acc
            m_scr[h] = m_new

    def start_of(kb):
        return pl.multiple_of(kb * tq, tq)

    def body(kp, carry):
        blocks([(start_of(2 * kp), False), (start_of(2 * kp + 1), False)])
        return carry

    lax.fori_loop(0, qi // 2, body, 0)

    @pl.when(qi % 2 == 1)
    def _():
        blocks([(start_of(qi - 1), False), (start_of(qi), True)])

    @pl.when(qi % 2 == 0)
    def _():
        blocks([(start_of(qi), True)])
    slot_row = lax.broadcasted_iota(jnp.int32, (HEAD_PAD, tq), 0)

    def normalised(h):
        acc = acc_scr[h]
        r = _denominator_row(h)
        return jnp.where(slot_row == r, 0.0, acc) / acc[r:r + 1, :]

    for j in range(hps // 2):
        pair = normalised(2 * j) + normalised(2 * j + 1)
        o_ref[0, :, j * LANES:(j + 1) * LANES] = pair.T.astype(o_ref.dtype)


def _attn_call(qcat, kcat, vt, k_meta, vt_meta, tq, hps):
    B, T, _ = qcat.shape
    assert MLA_HEADS % hps == 0 and hps % 2 == 0
    n_groups = MLA_HEADS // hps
    gw = hps * HEAD_PAD
    ow = (hps // 2) * LANES
    kern = functools.partial(_attn_kernel, tq=tq, n_meta=N_META, hps=hps)
    return pl.pallas_call(
        kern,
        grid=(B, n_groups, T // tq),
        in_specs=[
            pl.BlockSpec((1, tq, gw), lambda b, g, qi: (b, qi, g)),
            pl.BlockSpec((1, T, gw), lambda b, g, qi: (b, 0, g)),
            pl.BlockSpec((gw, T), lambda b, g, qi: (g, b)),
            pl.BlockSpec((k_meta.shape[0], gw), lambda b, g, qi: (0, g)),
            pl.BlockSpec((gw, vt_meta.shape[1]), lambda b, g, qi: (g, 0)),
        ],
        out_specs=pl.BlockSpec((1, tq, ow), lambda b, g, qi: (b, qi, g)),
        out_shape=jax.ShapeDtypeStruct((B, T, n_groups * ow), BF16),
        scratch_shapes=[pltpu.VMEM((hps, 1, tq), F32), pltpu.VMEM((hps, HEAD_PAD, tq), F32)],
        compiler_params=pltpu.CompilerParams(dimension_semantics=("arbitrary", "arbitrary", "arbitrary"),
                                             vmem_limit_bytes=VMEM_LIMIT),
        name="attn",
    )(qcat, kcat, vt, k_meta, vt_meta)


def _dattn_kernel(pt_ref, qlat_ref, qpe_ref, latn_ref, kpen_ref, lat_hbm, kpe_hbm, o_ref,
                  lat_buf, kpe_buf, sem_lat, sem_kpe, m_scr, l_scr, acc_scr,
                  *, group, n_groups, n_slots, n_sub, t_new, page):
    b = pl.program_id(0)
    n_seq = pl.num_programs(0)
    neg = -jnp.inf

    def page_copies(seq, g, k, slot):
        pid = pt_ref[seq, g * group + k]
        rows = pl.ds(pl.multiple_of(k * page, page), page)
        return (pltpu.make_async_copy(lat_hbm.at[0, pid], lat_buf.at[slot, rows, :], sem_lat.at[slot]),
                pltpu.make_async_copy(kpe_hbm.at[0, pid], kpe_buf.at[slot, k], sem_kpe.at[slot]))

    def start_group(seq, g, slot):
        def body(k2, carry):
            for prio in range(2):
                for cp in page_copies(seq, g, 2 * k2 + prio, slot):
                    cp.start(priority=prio)
            return carry
        lax.fori_loop(0, group // 2, body, 0)

    def wait_group(seq, g, slot):
        def body(k, carry):
            for cp in page_copies(seq, g, k, slot):
                cp.wait()
            return carry
        lax.fori_loop(0, group, body, 0)

    ahead = n_slots - 1

    @pl.when(b == 0)
    def _():
        for g0 in range(ahead):
            start_group(0, g0, g0 % n_slots)

    m_scr[...] = jnp.full(m_scr.shape, neg, F32)
    l_scr[...] = jnp.zeros(l_scr.shape, F32)
    acc_scr[...] = jnp.zeros(acc_scr.shape, F32)

    ql = qlat_ref[0]
    qp = qpe_ref[0]

    def update(scores, vals):
        ms = [jnp.max(s, axis=-1, keepdims=True) for s in scores]
        ps = [jnp.exp2(s - m) for s, m in zip(scores, ms)]
        ls = [jnp.sum(p, axis=-1, keepdims=True) for p in ps]
        pvs = [_dot(p.astype(BF16), v) for p, v in zip(ps, vals)]
        m_old = m_scr[...]
        m_new = m_old
        for m in ms:
            m_new = jnp.maximum(m_new, m)
        alpha = jnp.exp2(m_old - m_new)
        l_new = alpha * l_scr[...]
        acc_new = alpha * acc_scr[...]
        for m, l, pv in zip(ms, ls, pvs):
            w = jnp.exp2(m - m_new)
            l_new = l_new + w * l
            acc_new = acc_new + w * pv
        l_scr[...] = l_new
        acc_scr[...] = acc_new
        m_scr[...] = m_new

    sub_pages = group // n_sub
    sub_keys = sub_pages * page
    for g in range(n_groups):
        slot = g % n_slots
        nxt = g + ahead
        if nxt < n_groups:
            start_group(b, nxt, nxt % n_slots)
        else:
            @pl.when(b + 1 < n_seq)
            def _(nxt=nxt):
                start_group(b + 1, nxt - n_groups, nxt % n_slots)
        wait_group(b, g, slot)
        cbs = [lat_buf[slot, i * sub_keys:(i + 1) * sub_keys, :].astype(BF16) for i in range(n_sub)]
        kbts = [jnp.concatenate([kpe_buf[slot, k] for k in range(i * sub_pages, (i + 1) * sub_pages)],
                                axis=1).astype(BF16) for i in range(n_sub)]
        update([_dot_nt(ql, cb) + _dot(qp, kbt) for cb, kbt in zip(cbs, kbts)], cbs)

    pad = 2 * SUBLANES - latn_ref.shape[1]
    cn = jnp.concatenate([latn_ref[0], jnp.zeros((pad, latn_ref.shape[2]), F32)], axis=0).astype(BF16)
    kn = jnp.concatenate([kpen_ref[0], jnp.zeros((pad, kpen_ref.shape[2]), F32)], axis=0).astype(BF16)
    s = _dot_nt(ql, cn) + _dot_nt(qp, kn)
    tok = lax.shift_right_logical(lax.broadcasted_iota(jnp.int32, s.shape, 0), int(math.log2(MLA_HEADS)))
    key = lax.broadcasted_iota(jnp.int32, s.shape, 1)
    s = jnp.where((key <= tok) & (key < t_new), s, neg)
    update([s], [cn])
    o_ref[0] = (acc_scr[...] / l_scr[...]).astype(o_ref.dtype)


def _dattn_call(page_table, qlat, qpe, lat_new, kpe_new, cache_latent, cache_krope_t, t_new, group, n_slots, n_sub):
    assert group % n_sub == 0 and group % 2 == 0
    Bs, R, kv_lora = qlat.shape
    n_pages = page_table.shape[1]
    assert n_pages % (n_slots * group) == 0, "a group must keep its slot number across sequences"
    n_groups = n_pages // group
    assert 2 <= n_slots <= n_groups
    page = cache_latent.shape[2]
    kern = functools.partial(_dattn_kernel, group=group, n_groups=n_groups, n_slots=n_slots, n_sub=n_sub,
                             t_new=t_new, page=page)
    per_b = lambda shape: pl.BlockSpec((1,) + shape, lambda b, pt: (b, 0, 0))
    grid_spec = pltpu.PrefetchScalarGridSpec(
        num_scalar_prefetch=1,
        grid=(Bs,),
        in_specs=[per_b((R, kv_lora)), per_b((R, MLA_D_ROPE)),
                  per_b(lat_new.shape[1:]), per_b(kpe_new.shape[1:]),
                  pl.BlockSpec(memory_space=pl.ANY), pl.BlockSpec(memory_space=pl.ANY)],
        out_specs=per_b((R, kv_lora)),
        scratch_shapes=[pltpu.VMEM((n_slots, group * page, kv_lora), F32),
                        pltpu.VMEM((n_slots, group, MLA_D_ROPE, page), F32),
                        pltpu.SemaphoreType.DMA((n_slots,)), pltpu.SemaphoreType.DMA((n_slots,)),
                        pltpu.VMEM((R, 1), F32), pltpu.VMEM((R, 1), F32), pltpu.VMEM((R, kv_lora), F32)],
    )
    return pl.pallas_call(
        kern,
        grid_spec=grid_spec,
        out_shape=jax.ShapeDtypeStruct((Bs, R, kv_lora), BF16),
        compiler_params=pltpu.CompilerParams(dimension_semantics=("arbitrary",),
                                             vmem_limit_bytes=VMEM_LIMIT),
        name="dattn",
    )(page_table, qlat, qpe, lat_new, kpe_new, cache_latent, cache_krope_t)


def _post_kernel(*refs, absorbed_values):
    if absorbed_values:
        h_ref, og_ref, om_ref, wvbd_ref, wo1_ref, wo2_ref, g2_ref, wup_ref, wdn_ref, gf_ref, y_ref = refs
        om = _dot(om_ref[...], wvbd_ref[...]).astype(BF16)
    else:
        h_ref, og_ref, om_ref, wo1_ref, wo2_ref, g2_ref, wup_ref, wdn_ref, gf_ref, y_ref = refs
        om = om_ref[...]
    h2 = h_ref[...] + _dot(og_ref[...], wo1_ref[...]) + _dot(om, wo2_ref[...])
    hn = _rms(h2, g2_ref[...]).astype(BF16)
    u = jnp.maximum(_dot(hn, wup_ref[...]), 0.0)
    h3 = h2 + _dot((u * u).astype(BF16), wdn_ref[...])
    y_ref[...] = _rms(h3, gf_ref[...])


def _post_call(h2d, og, om, consts, tm, absorbed_values, name):
    n_rows, d_model = h2d.shape
    assert n_rows % tm == 0
    row_spec = lambda w: pl.BlockSpec((tm, w), lambda i: (i, 0))
    kern = functools.partial(_post_kernel, absorbed_values=absorbed_values)
    return pl.pallas_call(
        kern,
        grid=(n_rows // tm,),
        in_specs=[row_spec(d_model), row_spec(og.shape[1]), row_spec(om.shape[1])]
                 + [_const_spec(c.shape) for c in consts],
        out_specs=row_spec(d_model),
        out_shape=jax.ShapeDtypeStruct((n_rows, d_model), F32),
        compiler_params=pltpu.CompilerParams(dimension_semantics=("arbitrary",),
                                             vmem_limit_bytes=VMEM_LIMIT),
        name=name,
    )(h2d, og, om, *consts)


def _rope_parts(pos):
    half = MLA_D_ROPE // 2
    inv = ROPE_THETA ** (-jnp.arange(half, dtype=F32) / half)
    ang = pos.astype(F32)[:, None] * inv[None, :]
    cos = jnp.concatenate([jnp.cos(ang), jnp.cos(ang)], -1)
    sin_signed = jnp.concatenate([-jnp.sin(ang), jnp.sin(ang)], -1)
    return cos, sin_signed


def _pad_lanes(x, width):
    return jnp.pad(x, ((0, 0), (0, width - x.shape[1])))


def _at_lanes(x, lane0, width=LANES):
    return jnp.pad(x, ((0, 0), (lane0, width - lane0 - x.shape[1])))


def _key_tables(cos, sin_signed):
    return _at_lanes(cos, KPE_LANE0), _at_lanes(sin_signed, KPE_LANE0)


def _prompt_tables(pos):
    cos, sin_signed = _rope_parts(pos)
    n = pos.shape[0]
    half = MLA_D_ROPE // 2
    qcos = jnp.concatenate([jnp.full((n, MLA_D_NOPE), Q_SCALE_LOG2, F32), cos * Q_SCALE_LOG2], -1)
    qsin_lo = _at_lanes(sin_signed[:, :half] * Q_SCALE_LOG2, MLA_D_NOPE)
    qsin_hi = _at_lanes(sin_signed[:, half:] * Q_SCALE_LOG2, MLA_D_NOPE + half)
    kcos, ksin = _key_tables(cos, sin_signed)
    return jnp.concatenate([_pad_lanes(qcos, LANES), qsin_lo, qsin_hi, kcos, ksin], -1)


def _sample_tables(pos):
    cos, sin_signed = _rope_parts(pos)
    kcos, ksin = _key_tables(cos, sin_signed)
    return jnp.concatenate([jnp.tile(cos * Q_SCALE_LOG2, (1, MLA_HEADS)),
                            jnp.tile(sin_signed * Q_SCALE_LOG2, (1, MLA_HEADS)), kcos, ksin], -1)


def _swap_halves(w):
    half = MLA_D_ROPE // 2
    return jnp.concatenate([w[..., half:], w[..., :half]], -1)


def kernel(x_prompt, x_sample, cache_latent, cache_krope, state_conv, state_ssm, page_table,
           meta_tokens, norm_mix_g, w_in, conv_w, a_log, dt_bias, gdn_norm_g, q_norm_g, w_q_b,
           kv_norm_g, w_kv_b, w_out, norm_mlp_g, w_up, w_down, final_norm_g):
    assert w_in.shape[0] == 1, "single-layer problem"
    B, T, D = x_prompt.shape
    Bs, Ts, _ = x_sample.shape
    assert CONV_W - 1 <= Ts <= SAMPLE_T_PAD
    past_len = page_table.shape[1] * cache_latent.shape[2]
    q_lora = q_norm_g.shape[1]
    kv_lora = kv_norm_g.shape[1]
    H = MLA_HEADS

    wi = w_in[0]
    c0 = CONV_CH + GDN_V_W
    c1 = c0 + 2 * GDN_HEADS
    wkpe_raw = wi[:, c1 + q_lora + kv_lora:]
    blk_a = jnp.concatenate([_pad_lanes(wi[:, c0:c1], KPE_LANE0), _pad_lanes(wkpe_raw, LANES - KPE_LANE0)], 1)
    blk_b = _at_lanes(_swap_halves(wkpe_raw), KPE_LANE0)
    win = jnp.concatenate([wi[:, :c0], wi[:, c1:c1 + q_lora], blk_a,
                           wi[:, c1 + q_lora:c1 + q_lora + kv_lora], blk_b], axis=1).astype(BF16)
    gmix = norm_mix_g[0][None, :]
    qg = q_norm_g[0][None, :]
    kvg = kv_norm_g[0][None, :]
    wq = w_q_b[0]
    wq_nope, wq_pe = wq[..., :MLA_D_NOPE], wq[..., MLA_D_NOPE:]
    wq_pesw = _swap_halves(wq_pe)
    zq = lambda n: jnp.zeros((q_lora, H, n), F32)
    wqcat = jnp.concatenate([wq_nope, wq_pe, zq(HEAD_PAD - MLA_D_NOPE - MLA_D_ROPE)], -1)
    wqcat = wqcat.reshape(q_lora, H * HEAD_PAD).astype(BF16)
    wkv = w_kv_b[0]
    wkb, wvb = wkv[..., :MLA_D_NOPE], wkv[..., MLA_D_NOPE:]
    zk = lambda n: jnp.zeros((kv_lora, H, n), F32)
    wkb_sp = jnp.concatenate([wkb, zk(HEAD_PAD - MLA_D_NOPE)], -1).reshape(kv_lora, H * HEAD_PAD).astype(BF16)
    odd = (jnp.arange(H) % 2 == 1)[None, :, None]
    wvb_sp = jnp.where(odd, jnp.concatenate([zk(HEAD_PAD - MLA_D_V), wvb], -1),
                       jnp.concatenate([wvb, zk(HEAD_PAD - MLA_D_V)], -1))
    wvb_sp_t = wvb_sp.reshape(kv_lora, H * HEAD_PAD).T.astype(BF16)
    eye_h = jnp.eye(H, dtype=bool)
    wabs = jnp.where(eye_h[:, None, :, None], wkb.transpose(1, 2, 0)[:, :, None, :], 0.0)
    wabs = wabs.reshape(H * MLA_D_NOPE, H * kv_lora).astype(BF16)
    wvbd = jnp.where(eye_h[:, None, :, None], wvb.transpose(1, 0, 2)[:, :, None, :], 0.0)
    wvbd = wvbd.reshape(H * kv_lora, H * MLA_D_V).astype(BF16)
    wo1 = w_out[0][:GDN_V_W].astype(BF16)
    wo2 = w_out[0][GDN_V_W:].astype(BF16)
    g2 = norm_mlp_g[0][None, :]
    wup = w_up[0].astype(BF16)
    wdn = w_down[0].astype(BF16)
    gf = final_norm_g[None, :]
    cw = jnp.pad(conv_w[0], ((0, SUBLANES - CONV_W), (0, 0)))
    lane_row = lambda v: jnp.pad(v[None, :], ((0, 0), (GDN_HEADS, LANES - 2 * GDN_HEADS)))
    alog = lane_row(a_log[0])
    dtb = lane_row(dt_bias[0])
    ng = gdn_norm_g[0][None, :]

    common = (gmix, win, qg, kvg)
    slot_row = jnp.arange(H * HEAD_PAD) % HEAD_PAD
    vones = (slot_row == _denominator_row(jnp.arange(H * HEAD_PAD) // HEAD_PAD)).astype(F32)[:, None]
    prompt_consts = common + (wqcat, wkb_sp, wvb_sp_t, vones)
    proj_prompt = functools.partial(_proj_prompt_kernel, q_lora=q_lora, kv_lora=kv_lora)
    proj_sample = functools.partial(_proj_sample_kernel, q_lora=q_lora, kv_lora=kv_lora)
    sample_consts = common + (wq_nope.reshape(q_lora, -1).astype(BF16), wq_pe.reshape(q_lora, -1).astype(BF16),
                              wq_pesw.reshape(q_lora, -1).astype(BF16), wabs)
    pw = [CONV_CH, GDN_V_W, LANES, kv_lora, MLA_D_ROPE, H * HEAD_PAD, H * HEAD_PAD, H * HEAD_PAD]
    pd = [F32, F32, F32, F32, F32, BF16, BF16, BF16]

    tab_m = _prompt_tables(jnp.arange(N_META))
    qkv_m, z_m, ba_m, lat_m, kpe_m, _, kcat_m, vt_m = _proj_call(
        proj_prompt, meta_tokens, tab_m, prompt_consts, pw, pd, N_META, "proj_meta", transposed_outs=(7,))
    zero_conv = jnp.zeros((1, SUBLANES, CONV_CH), F32)
    zero_s = jnp.zeros((1, GDN_HEADS, GDN_DK, GDN_DV), F32)
    _, s1 = _gdn_call(qkv_m[None], z_m[None], ba_m[None], zero_conv, zero_s, cw, alog, dtb, ng,
                      N_META, N_META, "gdn_meta")
    conv1 = jnp.pad(qkv_m[N_META - (CONV_W - 1):], ((SUBLANES - (CONV_W - 1), 0), (0, 0)))[None]
    k_meta = jnp.pad(kcat_m, ((0, LANES - N_META), (0, 0)))
    vt_meta = jnp.pad(vt_m, ((0, 0), (0, LANES - N_META)))

    tab_p = _prompt_tables(N_META + jnp.arange(T))
    xp2d = x_prompt.reshape(B * T, D)
    qkv_p, z_p, ba_p, lat_p, kpe_p, qcat_p, kcat_p, vt_p = _proj_call(
        proj_prompt, xp2d, tab_p, prompt_consts, pw, pd, 512, "proj_prompt", transposed_outs=(7,))
    r3 = lambda a: a.reshape(B, T, a.shape[-1])
    qkv_p3 = r3(qkv_p)
    og_p, s2 = _gdn_call(qkv_p3, r3(z_p), r3(ba_p), conv1, s1, cw, alog, dtb, ng, GDN_CHUNK, T, "gdn_prompt",
                         nb_intra=8, nb_scan=8)
    om_p = _attn_call(r3(qcat_p), r3(kcat_p), vt_p, k_meta, vt_meta, 256, 8)
    post_consts = (wo1, wo2, g2, wup, wdn, gf)
    y_p = _post_call(xp2d, og_p.reshape(B * T, -1), om_p.reshape(B * T, -1), post_consts, 512, False, "post_prompt")

    tp = SAMPLE_T_PAD
    xs = jnp.pad(x_sample, ((0, 0), (0, tp - Ts), (0, 0))).reshape(Bs * tp, D)
    tab_s = jnp.tile(_sample_tables(past_len + jnp.arange(tp)), (Bs, 1))
    sw = [CONV_CH, GDN_V_W, LANES, kv_lora, MLA_D_ROPE, H * kv_lora, H * MLA_D_ROPE]
    sd = [F32, F32, F32, F32, F32, BF16, BF16]
    qkv_s, z_s, ba_s, lat_s, kpe_s, qlat_s, qpe_s = _proj_call(
        proj_sample, xs, tab_s, sample_consts, sw, sd, Bs * tp, "proj_sample")
    s3 = lambda a: a.reshape(Bs, tp, a.shape[-1])
    qkv_s3 = s3(qkv_s)
    conv_in_s = jnp.pad(state_conv[0], ((0, 0), (SUBLANES - (CONV_W - 1), 0), (0, 0)))
    og_s, s_new = _gdn_call(qkv_s3, s3(z_s), s3(ba_s), conv_in_s, state_ssm[0], cw, alog, dtb, ng, tp, Ts, "gdn_sample",
                            nb_intra=4, nb_scan=4)
    out_lat = _dattn_call(page_table, qlat_s.reshape(Bs, tp * H, kv_lora), qpe_s.reshape(Bs, tp * H, MLA_D_ROPE),
                          s3(lat_s), s3(kpe_s), cache_latent, jnp.swapaxes(cache_krope, 2, 3), Ts, 32, 4, 4)
    sample_post_consts = (wvbd,) + post_consts
    y_s = _post_call(xs, og_s.reshape(Bs * tp, -1), out_lat.reshape(Bs * tp, H * kv_lora),
                     sample_post_consts, Bs * tp, True, "post_sample")

    bc = lambda a: jnp.broadcast_to(a[None], (B,) + a.shape)
    lat_po = jnp.concatenate([bc(lat_m), r3(lat_p)], axis=1)[None]
    kpe_po = jnp.concatenate([bc(kpe_m), r3(kpe_p)], axis=1)[None]
    conv_po = qkv_p3[:, T - (CONV_W - 1):][None]
    conv_so = qkv_s3[:, Ts - (CONV_W - 1):Ts][None]
    return (y_p.reshape(B, T, D), y_s.reshape(Bs, tp, D)[:, :Ts],
            lat_po, kpe_po, conv_po, s2[None],
            s3(lat_s)[:, :Ts][None], s3(kpe_s)[:, :Ts][None], conv_so, s_new[None])
```

```python
import functools
import math

import jax
import jax.numpy as jnp
from jax import lax
from jax.experimental import pallas as pl
from jax.experimental.pallas import tpu as pltpu

F32 = jnp.float32
BF16 = jnp.bfloat16

N_META = 16
EPS = 1e-6
GDN_HEADS = 8
GDN_DK = 64
GDN_DV = 64
CONV_W = 4
GDN_CHUNK = 64
MLA_HEADS = 8
MLA_D_NOPE = 64
MLA_D_ROPE = 32
MLA_D_V = 64
ROPE_THETA = 10000.0
MLA_SCALE = (MLA_D_NOPE + MLA_D_ROPE) ** -0.5
Q_SCALE_LOG2 = MLA_SCALE * math.log2(math.e)
GDN_QK_W = GDN_HEADS * GDN_DK
GDN_V_W = GDN_HEADS * GDN_DV
CONV_CH = 2 * GDN_QK_W + GDN_V_W

LANES = 128
SUBLANES = 8
HEAD_PAD = 128
SAMPLE_T_PAD = 8
KPE_LANE0 = 32
VMEM_LIMIT = 56 * 1024 * 1024

PROJ_TM = 512
POST_TM = 512
ATTN_TQ = 256
ATTN_HPS = 8
ATTN_KV_UNROLL = 2
GDN_NB_INTRA = 8
GDN_NB_SCAN = 8
GDN_NB_SAMPLE = 4
DATTN_GROUP = 32
DATTN_SLOTS = 4
DATTN_SUB = 4


def _dot(a, b):
    return jnp.dot(a, b, preferred_element_type=F32)


def _dot_nt(a, b):
    return lax.dot_general(a, b, (((1,), (1,)), ((), ())), preferred_element_type=F32)


def _rms(x, g):
    return x * lax.rsqrt(jnp.mean(x * x, axis=-1, keepdims=True) + EPS) * g


def _sigmoid(x):
    return 1.0 / (1.0 + jnp.exp(-x))


def _const_spec(shape):
    nd = len(shape)
    return pl.BlockSpec(shape, lambda *_: (0,) * nd)


def _proj_common(x_ref, gmix_ref, win_ref, qg_ref, kvg_ref, kcos, ksin, qkv_ref, z_ref, ba_ref, lat_ref, kpe_ref,
                 *, q_lora, kv_lora):
    hn = _rms(x_ref[...], gmix_ref[...]).astype(BF16)
    p = _dot(hn, win_ref[...])
    c0 = CONV_CH + GDN_V_W
    c1 = c0 + q_lora
    c2 = c1 + LANES
    c3 = c2 + kv_lora
    qkv_ref[...] = p[:, :CONV_CH]
    z_ref[...] = p[:, CONV_CH:c0]
    blk_a = p[:, c1:c2]
    ba_ref[...] = blk_a
    kpe = blk_a * kcos + p[:, c3:c3 + LANES] * ksin
    kpe_ref[...] = kpe[:, KPE_LANE0:KPE_LANE0 + MLA_D_ROPE]
    lat = _rms(p[:, c2:c3], kvg_ref[...])
    lat_ref[...] = lat
    cqn = _rms(p[:, c0:c1], qg_ref[...]).astype(BF16)
    return cqn, lat.astype(BF16), kpe


def _proj_prompt_kernel(x_ref, tab_ref, gmix_ref, win_ref, qg_ref, kvg_ref, wq_ref, wkb_ref, wvb_ref, vones_ref,
                        qkv_ref, z_ref, ba_ref, lat_ref, kpe_ref, qcat_ref, kcat_ref, vt_ref, *, q_lora, kv_lora):
    qcos = tab_ref[:, 0 * LANES:1 * LANES]
    qsin_lo = tab_ref[:, 1 * LANES:2 * LANES]
    qsin_hi = tab_ref[:, 2 * LANES:3 * LANES]
    kcos = tab_ref[:, 3 * LANES:4 * LANES]
    ksin = tab_ref[:, 4 * LANES:5 * LANES]
    cqn, latb, kpe = _proj_common(x_ref, gmix_ref, win_ref, qg_ref, kvg_ref, kcos, ksin,
                                  qkv_ref, z_ref, ba_ref, lat_ref, kpe_ref, q_lora=q_lora, kv_lora=kv_lora)
    qa = _dot(cqn, wq_ref[...])
    knope = _dot(latb, wkb_ref[...])
    half = MLA_D_ROPE // 2
    kpe_slot = pltpu.roll(kpe, MLA_D_NOPE - KPE_LANE0, axis=1)
    for h in range(MLA_HEADS):
        sl = slice(h * HEAD_PAD, (h + 1) * HEAD_PAD)
        qs = qa[:, sl]
        q_dn = pltpu.roll(qs, HEAD_PAD - half, axis=1)
        q_up = pltpu.roll(qs, half, axis=1)
        qcat_ref[:, sl] = (qs * qcos + q_dn * qsin_lo + q_up * qsin_hi).astype(BF16)
        kcat_ref[:, sl] = (knope[:, sl] + kpe_slot).astype(BF16)
    vt_ref[...] = (_dot_nt(wvb_ref[...], latb) + vones_ref[...]).astype(BF16)


def _proj_sample_kernel(x_ref, tab_ref, gmix_ref, win_ref, qg_ref, kvg_ref, wqn_ref, wqpe_ref, wqpesw_ref, wabs_ref,
                        qkv_ref, z_ref, ba_ref, lat_ref, kpe_ref, qlat_ref, qpe_ref, *, q_lora, kv_lora):
    pe_w = MLA_HEADS * MLA_D_ROPE
    qcos = tab_ref[:, 0:pe_w]
    qsin = tab_ref[:, pe_w:2 * pe_w]
    kcos = tab_ref[:, 2 * pe_w:2 * pe_w + LANES]
    ksin = tab_ref[:, 2 * pe_w + LANES:2 * pe_w + 2 * LANES]
    cqn, _, _ = _proj_common(x_ref, gmix_ref, win_ref, qg_ref, kvg_ref, kcos, ksin,
                             qkv_ref, z_ref, ba_ref, lat_ref, kpe_ref, q_lora=q_lora, kv_lora=kv_lora)
    qn = _dot(cqn, wqn_ref[...]).astype(BF16)
    qlat_ref[...] = (_dot(qn, wabs_ref[...]) * Q_SCALE_LOG2).astype(BF16)
    qpe_ref[...] = (_dot(cqn, wqpe_ref[...]) * qcos + _dot(cqn, wqpesw_ref[...]) * qsin).astype(BF16)


def _proj_call(kernel_fn, x2d, tab, consts, out_widths, out_dtypes, tm, name, transposed_outs=()):
    n_rows, d_model = x2d.shape
    assert n_rows % tm == 0
    row_spec = lambda w: pl.BlockSpec((tm, w), lambda i: (i, 0))
    col_spec = lambda w: pl.BlockSpec((w, tm), lambda i: (0, i))
    is_t = [k in transposed_outs for k in range(len(out_widths))]
    assert tab.shape[0] % tm == 0 and n_rows % tab.shape[0] == 0
    tab_blocks = tab.shape[0] // tm
    tab_spec = pl.BlockSpec((tm, tab.shape[1]), lambda i: (i % tab_blocks, 0))
    in_specs = [row_spec(d_model), tab_spec] + [_const_spec(c.shape) for c in consts]
    return pl.pallas_call(
        kernel_fn,
        grid=(n_rows // tm,),
        in_specs=in_specs,
        out_specs=[col_spec(w) if t else row_spec(w) for w, t in zip(out_widths, is_t)],
        out_shape=[jax.ShapeDtypeStruct((w, n_rows) if t else (n_rows, w), dt)
                   for w, dt, t in zip(out_widths, out_dtypes, is_t)],
        compiler_params=pltpu.CompilerParams(dimension_semantics=("arbitrary",),
                                             vmem_limit_bytes=VMEM_LIMIT),
        name=name,
    )(x2d, tab, *consts)


def _block_diag2(x, half):
    lane = lax.broadcasted_iota(jnp.int32, x.shape, 1)
    zero = jnp.zeros_like(x)
    return jnp.concatenate([jnp.where(lane < half, x, zero), jnp.where(lane < half, zero, x)], axis=0)


def _gdn_intra_prep(bb, cinit_bb, qkv_ref, prev_ref, ba_ref, cinit_ref, cw_ref, alog_ref, dtb_ref,
                    qk_mask, kg_ref, qg_ref, kdt_ref, vb_ref, egl_ref, xbuf, *, chunk, valid_len, n_chunks):
    c = pl.program_id(1)
    C = chunk
    H = GDN_HEADS
    tail0 = SUBLANES - (CONV_W - 1)

    xbuf[bb, 0:SUBLANES, :] = jnp.where(c == 0, cinit_ref[cinit_bb], prev_ref[bb])
    xbuf[bb, SUBLANES:SUBLANES + C, :] = qkv_ref[bb]
    cw = cw_ref[...]
    y = xbuf[bb, tail0:tail0 + C, :] * cw[0:1, :]
    for j in range(1, CONV_W):
        y = y + xbuf[bb, tail0 + j:tail0 + j + C, :] * cw[j:j + 1, :]
    y = y * _sigmoid(y)

    ba = ba_ref[bb]
    beta_all = _sigmoid(ba)
    sp_in = ba + dtb_ref[...]
    softplus = jnp.maximum(sp_in, 0.0) + jnp.log1p(jnp.exp(-jnp.abs(sp_in)))
    g_all = -jnp.exp(alog_ref[...]) * softplus
    if valid_len < n_chunks * C:
        row = c * C + lax.broadcasted_iota(jnp.int32, (C, LANES), 0)
        beta_all = jnp.where(row < valid_len, beta_all, 0.0)
        g_all = jnp.where(row < valid_len, g_all, 0.0)

    ri = lax.broadcasted_iota(jnp.int32, (C, C), 0)
    ci = lax.broadcasted_iota(jnp.int32, (C, C), 1)
    lower = ri >= ci
    gc_all = lax.dot_general(lower.astype(F32), g_all, (((1,), (0,)), ((), ())),
                             precision=lax.Precision.HIGHEST, preferred_element_type=F32)
    sel = (lax.broadcasted_iota(jnp.int32, (SUBLANES, LANES), 1)
           == lax.broadcasted_iota(jnp.int32, (SUBLANES, LANES), 0) + GDN_HEADS).astype(F32)
    gc_t = lax.dot_general(sel, gc_all, (((1,), (1,)), ((), ())),
                           precision=lax.Precision.HIGHEST, preferred_element_type=F32)

    pairs = range(H // 2)
    lane = lax.broadcasted_iota(jnp.int32, (C, LANES), 1)
    lo = lane < GDN_DK
    lo_c, lower_p = qk_mask
    ii_r = lax.broadcasted_iota(jnp.int32, (GDN_DK, LANES), 0)
    ii_c = lax.broadcasted_iota(jnp.int32, (GDN_DK, LANES), 1)
    eye2 = ((ii_c == ii_r) | (ii_c == ii_r + GDN_DK)).astype(BF16)

    def col(x, idx):
        return jnp.sum(jnp.where(lane == idx, x, 0.0), axis=-1, keepdims=True)

    def half_sums(x):
        return jnp.where(lo, jnp.sum(jnp.where(lo, x, 0.0), axis=-1, keepdims=True),
                         jnp.sum(jnp.where(lo, 0.0, x), axis=-1, keepdims=True))

    kn_b, kb_b, qn_b, decay = [], [], [], []
    for j in pairs:
        ps = slice(j * LANES, (j + 1) * LANES)
        qp = y[:, j * LANES:(j + 1) * LANES]
        kp = y[:, GDN_QK_W + j * LANES:GDN_QK_W + (j + 1) * LANES]
        vp = y[:, 2 * GDN_QK_W + j * LANES:2 * GDN_QK_W + (j + 1) * LANES]
        qn = qp * lax.rsqrt(half_sums(qp * qp) + EPS) * (GDN_DK ** -0.5)
        kn = kp * lax.rsqrt(half_sums(kp * kp) + EPS)
        beta = jnp.where(lo, col(beta_all, 2 * j), col(beta_all, 2 * j + 1))
        gc0, gc1 = col(gc_all, H + 2 * j), col(gc_all, H + 2 * j + 1)
        gcol = jnp.where(lo, gc0, gc1)
        grow = jnp.concatenate([gc_t[2 * j:2 * j + 1, :], gc_t[2 * j + 1:2 * j + 2, :]], axis=1)
        diff = jnp.where(lo_c, gc0, gc1) - grow
        decay.append(jnp.where(lower_p, jnp.exp(jnp.where(lower_p, diff, 0.0)), 0.0))
        egc = jnp.exp(gcol)
        gl = gcol[C - 1:C, :]
        egl_ref[bb, 0, :, ps] = jnp.exp(gl)
        kb = kn * beta
        vb_ref[bb, 0, :, ps] = vp * beta
        kg_ref[bb, 0, :, ps] = (kb * egc).astype(BF16)
        qg_ref[bb, 0, :, ps] = (qn * egc).astype(BF16)
        k_dec = (kn * jnp.exp(gl - gcol)).astype(BF16)
        kdt_ref[bb, 0, j] = _dot_nt(eye2, _block_diag2(k_dec, GDN_DK)).astype(BF16)
        kn_b.append(_block_diag2(kn.astype(BF16), GDN_DK))
        kb_b.append(kb.astype(BF16))
        qn_b.append(qn.astype(BF16))
    return kn_b, kb_b, qn_b, decay


def _gdn_intra_kernel(qkv_ref, prev_ref, ba_ref, cinit_ref, cw_ref, alog_ref, dtb_ref,
                      t_ref, qk_ref, kg_ref, qg_ref, kdt_ref, vb_ref, egl_ref, xbuf,
                      *, chunk, valid_len, n_chunks, nb, shared_cinit):
    C = chunk
    rp = lax.broadcasted_iota(jnp.int32, (C, 2 * C), 0)
    cp = lax.broadcasted_iota(jnp.int32, (C, 2 * C), 1)
    lo_c = cp < C
    cmod = jnp.where(lo_c, cp, cp - C)
    lower_p, strict_p = rp >= cmod, rp > cmod
    eye_p = (rp == cmod).astype(F32)
    kn_b, kb_b, qn_b, decay = [], [], [], []
    for bb in range(nb):
        parts = _gdn_intra_prep(bb, 0 if shared_cinit else bb, qkv_ref, prev_ref, ba_ref, cinit_ref, cw_ref,
                                alog_ref, dtb_ref, (lo_c, lower_p), kg_ref, qg_ref, kdt_ref, vb_ref, egl_ref, xbuf,
                                chunk=chunk, valid_len=valid_len, n_chunks=n_chunks)
        for dst, src in zip((kn_b, kb_b, qn_b, decay), parts):
            dst.extend(src)
    n_pairs = GDN_HEADS // 2
    probs = range(nb * n_pairs)
    kk = [_dot_nt(kb_b[i], kn_b[i]) for i in probs]
    qk = [_dot_nt(qn_b[i], kn_b[i]) for i in probs]
    for i in probs:
        qk_ref[i // n_pairs, 0, i % n_pairs] = jnp.where(lower_p, qk[i] * decay[i], 0.0).astype(BF16)
    p = [jnp.where(strict_p, -(kk[i] * decay[i]), 0.0) for i in probs]
    t = [eye_p + p[i] for i in probs]
    n_iter = max(1, int(math.ceil(math.log2(C))))
    for _ in range(1, n_iter):
        pb = [p[i].astype(BF16) for i in probs]
        p = [_dot(pb[i], _block_diag2(pb[i], C)) for i in probs]
        t = [t[i] + _dot(t[i].astype(BF16), _block_diag2(p[i].astype(BF16), C)) for i in probs]
    for i in probs:
        t_ref[i // n_pairs, 0, i % n_pairs] = t[i].astype(BF16)


def _gdn_scan_kernel(t_ref, qk_ref, kg_ref, qg_ref, kdt_ref, vb_ref, egl_ref, z_ref, sinit_ref, ng_ref,
                     o_ref, sfin_ref, s_scr, *, chunk, n_chunks, nb, shared_sinit):
    c = pl.program_id(1)
    C = chunk
    n_pairs = GDN_HEADS // 2
    probs = [(bb, j) for bb in range(nb) for j in range(n_pairs)]
    ps = [slice(j * LANES, (j + 1) * LANES) for j in range(n_pairs)]
    lo = lax.broadcasted_iota(jnp.int32, (C, LANES), 1) < GDN_DV

    @pl.when(c == 0)
    def _():
        for bb, j in probs:
            si = 0 if shared_sinit else bb
            s_scr[bb, j] = jnp.concatenate([sinit_ref[si, 2 * j], sinit_ref[si, 2 * j + 1]], axis=1)

    s = [s_scr[bb, j] for bb, j in probs]
    sbd = [_block_diag2(x.astype(BF16), GDN_DV) for x in s]
    ks = [_dot(kg_ref[bb, 0, :, ps[j]], sbd[i]) for i, (bb, j) in enumerate(probs)]
    qs = [_dot(qg_ref[bb, 0, :, ps[j]], sbd[i]) for i, (bb, j) in enumerate(probs)]
    r = [(vb_ref[bb, 0, :, ps[j]] - ks[i]).astype(BF16) for i, (bb, j) in enumerate(probs)]
    v_new = [_dot(t_ref[bb, 0, j], _block_diag2(r[i], GDN_DV)).astype(BF16) for i, (bb, j) in enumerate(probs)]
    vbd = [_block_diag2(x, GDN_DV) for x in v_new]
    o = [qs[i] + _dot(qk_ref[bb, 0, j], vbd[i]) for i, (bb, j) in enumerate(probs)]
    for i, (bb, j) in enumerate(probs):
        s_scr[bb, j] = s[i] * egl_ref[bb, 0, :, ps[j]] + _dot(kdt_ref[bb, 0, j], vbd[i])
    ng = ng_ref[...]
    for i, (bb, j) in enumerate(probs):
        o2 = o[i] * o[i]
        ms = jnp.where(lo, jnp.sum(jnp.where(lo, o2, 0.0), axis=-1, keepdims=True),
                       jnp.sum(jnp.where(lo, 0.0, o2), axis=-1, keepdims=True)) * (1.0 / GDN_DV)
        zp = z_ref[bb, :, ps[j]]
        o_ref[bb, :, ps[j]] = (o[i] * lax.rsqrt(ms + EPS) * ng * (zp * _sigmoid(zp))).astype(o_ref.dtype)

    @pl.when(c == n_chunks - 1)
    def _():
        for bb, j in probs:
            s_pair = s_scr[bb, j]
            sfin_ref[bb, 2 * j] = s_pair[:, :GDN_DV]
            sfin_ref[bb, 2 * j + 1] = s_pair[:, GDN_DV:]


def _gdn_call(qkv, z, ba, conv_init, s_init, cw, alog, dtb, ng, chunk, valid_len, name, nb_intra=1, nb_scan=1):
    B, T, _ = qkv.shape
    assert T % chunk == 0 and chunk % SUBLANES == 0 and GDN_DK == GDN_DV and 2 * GDN_DV == LANES
    n_chunks = T // chunk
    H, C = GDN_HEADS, chunk
    HP = H // 2
    shared_c = conv_init.shape[0] == 1
    shared_s = s_init.shape[0] == 1
    ng = jnp.tile(ng, (1, 2))
    params = pltpu.CompilerParams(dimension_semantics=("arbitrary", "arbitrary"), vmem_limit_bytes=VMEM_LIMIT)
    inter_dims = [(HP, C, 2 * C), (HP, C, 2 * C), (C, GDN_QK_W), (C, GDN_QK_W), (HP, GDN_DK, 2 * C),
                  (C, GDN_V_W), (1, GDN_V_W)]
    inter_dtypes = [BF16, BF16, BF16, BF16, BF16, F32, F32]

    def specs(nb):
        tok = lambda w: pl.BlockSpec((nb, chunk, w), lambda b, c: (b, c, 0))
        per_chunk = lambda *dims: pl.BlockSpec((nb, 1) + dims, lambda b, c: (b, c) + (0,) * len(dims))
        per_seq = lambda shared, *dims: pl.BlockSpec(
            ((1 if shared else nb),) + dims, lambda b, c: ((0 if shared else b),) + (0,) * len(dims))
        return tok, per_chunk, per_seq

    nb = nb_intra
    assert B % nb == 0
    tok, per_chunk, per_seq = specs(nb)
    prev_rows = pl.BlockSpec((nb, SUBLANES, CONV_CH),
                             lambda b, c: (b, jnp.maximum(c * (C // SUBLANES) - 1, 0), 0))
    inter = pl.pallas_call(
        functools.partial(_gdn_intra_kernel, chunk=chunk, valid_len=valid_len, n_chunks=n_chunks,
                          nb=nb, shared_cinit=shared_c),
        grid=(B // nb, n_chunks),
        in_specs=[
            tok(CONV_CH), prev_rows, tok(LANES), per_seq(shared_c, SUBLANES, CONV_CH),
            _const_spec(cw.shape), _const_spec(alog.shape), _const_spec(dtb.shape),
        ],
        out_specs=[per_chunk(*d) for d in inter_dims],
        out_shape=[jax.ShapeDtypeStruct((B, n_chunks) + d, dt) for d, dt in zip(inter_dims, inter_dtypes)],
        scratch_shapes=[pltpu.VMEM((nb, SUBLANES + chunk, CONV_CH), F32)],
        compiler_params=params,
        name=name + "_intra",
    )(qkv, qkv, ba, conv_init, cw, alog, dtb)

    nb = nb_scan
    assert B % nb == 0
    tok, per_chunk, per_seq = specs(nb)
    o, s_fin = pl.pallas_call(
        functools.partial(_gdn_scan_kernel, chunk=chunk, n_chunks=n_chunks, nb=nb, shared_sinit=shared_s),
        grid=(B // nb, n_chunks),
        in_specs=[per_chunk(*d) for d in inter_dims] + [
            tok(GDN_V_W), per_seq(shared_s, H, GDN_DK, GDN_DV), _const_spec(ng.shape),
        ],
        out_specs=[tok(GDN_V_W), per_seq(False, H, GDN_DK, GDN_DV)],
        out_shape=[
            jax.ShapeDtypeStruct((B, T, GDN_V_W), BF16),
            jax.ShapeDtypeStruct((B, H, GDN_DK, GDN_DV), F32),
        ],
        scratch_shapes=[pltpu.VMEM((nb, HP, GDN_DK, LANES), F32)],
        compiler_params=params,
        name=name + "_scan",
    )(*inter, z, s_init, ng)
    return o, s_fin


def _denominator_row(h):
    return MLA_D_V * (1 - h % 2)


def _attn_kernel(q_ref, k_ref, vt_ref, km_ref, vmt_ref, o_ref, m_scr, acc_scr, *, tq, n_meta, hps, kv_unroll):
    qi = pl.program_id(2)
    neg = -jnp.inf
    key_m = lax.broadcasted_iota(jnp.int32, (km_ref.shape[0], tq), 0)
    key_d = lax.broadcasted_iota(jnp.int32, (tq, tq), 0)
    qry_d = lax.broadcasted_iota(jnp.int32, (tq, tq), 1)
    heads = range(hps)
    ls = [slice(h * HEAD_PAD, (h + 1) * HEAD_PAD) for h in heads]

    s = [jnp.where(key_m < n_meta, _dot_nt(km_ref[:, ls[h]], q_ref[0, :, ls[h]]), neg) for h in heads]
    m = [jnp.max(s[h], axis=0, keepdims=True) for h in heads]
    p = [jnp.exp2(s[h] - m[h]) for h in heads]
    for h in heads:
        m_scr[h] = m[h]
        acc_scr[h] = _dot(vmt_ref[ls[h], :], p[h].astype(BF16))

    def blocks(specs):
        chains = [(start, diag, h) for start, diag in specs for h in heads]
        s = [_dot_nt(k_ref[0, pl.ds(start, tq), ls[h]], q_ref[0, :, ls[h]]) for start, _, h in chains]
        s = [jnp.where(key_d <= qry_d, x, neg) if diag else x for x, (_, diag, _) in zip(s, chains)]
        mb = [jnp.max(x, axis=0, keepdims=True) for x in s]
        p = [jnp.exp2(x - m) for x, m in zip(s, mb)]
        pv = [_dot(vt_ref[ls[h], pl.ds(start, tq)], x.astype(BF16)) for x, (start, _, h) in zip(p, chains)]
        for h in heads:
            mine = [i for i, c in enumerate(chains) if c[2] == h]
            m_old = m_scr[h]
            m_new = m_old
            for i in mine:
                m_new = jnp.maximum(m_new, mb[i])
            acc = jnp.exp2(m_old - m_new) * acc_scr[h]
            for i in mine:
                acc = acc + jnp.exp2(mb[i] - m_new) * pv[i]
            acc_scr[h] = acc
            m_scr[h] = m_new

    def start_of(kb):
        return pl.multiple_of(kb * tq, tq)

    def body(kp, carry):
        blocks([(start_of(kv_unroll * kp + u), False) for u in range(kv_unroll)])
        return carry

    lax.fori_loop(0, qi // kv_unroll, body, 0)
    for rem in range(kv_unroll):
        @pl.when(qi % kv_unroll == rem)
        def _(rem=rem):
            blocks([(start_of(qi - rem + u), False) for u in range(rem)] + [(start_of(qi), True)])
    slot_row = lax.broadcasted_iota(jnp.int32, (HEAD_PAD, tq), 0)

    def normalised(h):
        acc = acc_scr[h]
        r = _denominator_row(h)
        return jnp.where(slot_row == r, 0.0, acc) / acc[r:r + 1, :]

    for j in range(hps // 2):
        pair = normalised(2 * j) + normalised(2 * j + 1)
        o_ref[0, :, j * LANES:(j + 1) * LANES] = pair.T.astype(o_ref.dtype)


def _attn_call(qcat, kcat, vt, k_meta, vt_meta, tq, hps, kv_unroll):
    B, T, _ = qcat.shape
    assert MLA_HEADS % hps == 0 and hps % 2 == 0
    n_groups = MLA_HEADS // hps
    gw = hps * HEAD_PAD
    ow = (hps // 2) * LANES
    kern = functools.partial(_attn_kernel, tq=tq, n_meta=N_META, hps=hps, kv_unroll=kv_unroll)
    return pl.pallas_call(
        kern,
        grid=(B, n_groups, T // tq),
        in_specs=[
            pl.BlockSpec((1, tq, gw), lambda b, g, qi: (b, qi, g)),
            pl.BlockSpec((1, T, gw), lambda b, g, qi: (b, 0, g)),
            pl.BlockSpec((gw, T), lambda b, g, qi: (g, b)),
            pl.BlockSpec((k_meta.shape[0], gw), lambda b, g, qi: (0, g)),
            pl.BlockSpec((gw, vt_meta.shape[1]), lambda b, g, qi: (g, 0)),
        ],
        out_specs=pl.BlockSpec((1, tq, ow), lambda b, g, qi: (b, qi, g)),
        out_shape=jax.ShapeDtypeStruct((B, T, n_groups * ow), BF16),
        scratch_shapes=[pltpu.VMEM((hps, 1, tq), F32), pltpu.VMEM((hps, HEAD_PAD, tq), F32)],
        compiler_params=pltpu.CompilerParams(dimension_semantics=("arbitrary", "arbitrary", "arbitrary"),
                                             vmem_limit_bytes=VMEM_LIMIT),
        name="attn",
    )(qcat, kcat, vt, k_meta, vt_meta)


def _dattn_kernel(pt_ref, qlat_ref, qpe_ref, latn_ref, kpen_ref, lat_hbm, kpe_hbm, o_ref,
                  lat_buf, kpe_buf, sem_lat, sem_kpe, m_scr, l_scr, acc_scr,
                  *, group, n_groups, n_slots, n_sub, t_new, page):
    b = pl.program_id(0)
    n_seq = pl.num_programs(0)
    neg = -jnp.inf

    def page_copies(seq, g, k, slot):
        pid = pt_ref[seq, g * group + k]
        rows = pl.ds(pl.multiple_of(k * page, page), page)
        return (pltpu.make_async_copy(lat_hbm.at[0, pid], lat_buf.at[slot, rows, :], sem_lat.at[slot]),
                pltpu.make_async_copy(kpe_hbm.at[0, pid], kpe_buf.at[slot, k], sem_kpe.at[slot]))

    def start_group(seq, g, slot):
        def body(k2, carry):
            for prio in range(2):
                for cp in page_copies(seq, g, 2 * k2 + prio, slot):
                    cp.start(priority=prio)
            return carry
        lax.fori_loop(0, group // 2, body, 0)

    def wait_group(seq, g, slot):
        def body(k, carry):
            for cp in page_copies(seq, g, k, slot):
                cp.wait()
            return carry
        lax.fori_loop(0, group, body, 0)

    ahead = n_slots - 1

    @pl.when(b == 0)
    def _():
        for g0 in range(ahead):
            start_group(0, g0, g0 % n_slots)

    m_scr[...] = jnp.full(m_scr.shape, neg, F32)
    l_scr[...] = jnp.zeros(l_scr.shape, F32)
    acc_scr[...] = jnp.zeros(acc_scr.shape, F32)

    ql = qlat_ref[0]
    qp = qpe_ref[0]

    def update(scores, vals):
        ms = [jnp.max(s, axis=-1, keepdims=True) for s in scores]
        ps = [jnp.exp2(s - m) for s, m in zip(scores, ms)]
        ls = [jnp.sum(p, axis=-1, keepdims=True) for p in ps]
        pvs = [_dot(p.astype(BF16), v) for p, v in zip(ps, vals)]
        m_old = m_scr[...]
        m_new = m_old
        for m in ms:
            m_new = jnp.maximum(m_new, m)
        alpha = jnp.exp2(m_old - m_new)
        l_new = alpha * l_scr[...]
        acc_new = alpha * acc_scr[...]
        for m, l, pv in zip(ms, ls, pvs):
            w = jnp.exp2(m - m_new)
            l_new = l_new + w * l
            acc_new = acc_new + w * pv
        l_scr[...] = l_new
        acc_scr[...] = acc_new
        m_scr[...] = m_new

    sub_pages = group // n_sub
    sub_keys = sub_pages * page
    for g in range(n_groups):
        slot = g % n_slots
        nxt = g + ahead
        if nxt < n_groups:
            start_group(b, nxt, nxt % n_slots)
        else:
            @pl.when(b + 1 < n_seq)
            def _(nxt=nxt):
                start_group(b + 1, nxt - n_groups, nxt % n_slots)
        wait_group(b, g, slot)
        cbs = [lat_buf[slot, i * sub_keys:(i + 1) * sub_keys, :].astype(BF16) for i in range(n_sub)]
        kbts = [jnp.concatenate([kpe_buf[slot, k] for k in range(i * sub_pages, (i + 1) * sub_pages)],
                                axis=1).astype(BF16) for i in range(n_sub)]
        update([_dot_nt(ql, cb) + _dot(qp, kbt) for cb, kbt in zip(cbs, kbts)], cbs)

    pad = 2 * SUBLANES - latn_ref.shape[1]
    cn = jnp.concatenate([latn_ref[0], jnp.zeros((pad, latn_ref.shape[2]), F32)], axis=0).astype(BF16)
    kn = jnp.concatenate([kpen_ref[0], jnp.zeros((pad, kpen_ref.shape[2]), F32)], axis=0).astype(BF16)
    s = _dot_nt(ql, cn) + _dot_nt(qp, kn)
    tok = lax.shift_right_logical(lax.broadcasted_iota(jnp.int32, s.shape, 0), int(math.log2(MLA_HEADS)))
    key = lax.broadcasted_iota(jnp.int32, s.shape, 1)
    s = jnp.where((key <= tok) & (key < t_new), s, neg)
    update([s], [cn])
    o_ref[0] = (acc_scr[...] / l_scr[...]).astype(o_ref.dtype)


def _dattn_call(page_table, qlat, qpe, lat_new, kpe_new, cache_latent, cache_krope_t, t_new, group, n_slots, n_sub):
    assert group % n_sub == 0 and group % 2 == 0
    Bs, R, kv_lora = qlat.shape
    n_pages = page_table.shape[1]
    assert n_pages % (n_slots * group) == 0, "a group must keep its slot number across sequences"
    n_groups = n_pages // group
    assert 2 <= n_slots <= n_groups
    page = cache_latent.shape[2]
    kern = functools.partial(_dattn_kernel, group=group, n_groups=n_groups, n_slots=n_slots, n_sub=n_sub,
                             t_new=t_new, page=page)
    per_b = lambda shape: pl.BlockSpec((1,) + shape, lambda b, pt: (b, 0, 0))
    grid_spec = pltpu.PrefetchScalarGridSpec(
        num_scalar_prefetch=1,
        grid=(Bs,),
        in_specs=[per_b((R, kv_lora)), per_b((R, MLA_D_ROPE)),
                  per_b(lat_new.shape[1:]), per_b(kpe_new.shape[1:]),
                  pl.BlockSpec(memory_space=pl.ANY), pl.BlockSpec(memory_space=pl.ANY)],
        out_specs=per_b((R, kv_lora)),
        scratch_shapes=[pltpu.VMEM((n_slots, group * page, kv_lora), F32),
                        pltpu.VMEM((n_slots, group, MLA_D_ROPE, page), F32),
                        pltpu.SemaphoreType.DMA((n_slots,)), pltpu.SemaphoreType.DMA((n_slots,)),
                        pltpu.VMEM((R, 1), F32), pltpu.VMEM((R, 1), F32), pltpu.VMEM((R, kv_lora), F32)],
    )
    return pl.pallas_call(
        kern,
        grid_spec=grid_spec,
        out_shape=jax.ShapeDtypeStruct((Bs, R, kv_lora), BF16),
        compiler_params=pltpu.CompilerParams(dimension_semantics=("arbitrary",),
                                             vmem_limit_bytes=VMEM_LIMIT),
        name="dattn",
    )(page_table, qlat, qpe, lat_new, kpe_new, cache_latent, cache_krope_t)


def _post_kernel(*refs, absorbed_values):
    if absorbed_values:
        h_ref, og_ref, om_ref, wvbd_ref, wo1_ref, wo2_ref, g2_ref, wup_ref, wdn_ref, gf_ref, y_ref = refs
        om = _dot(om_ref[...], wvbd_ref[...]).astype(BF16)
    else:
        h_ref, og_ref, om_ref, wo1_ref, wo2_ref, g2_ref, wup_ref, wdn_ref, gf_ref, y_ref = refs
        om = om_ref[...]
    h2 = h_ref[...] + _dot(og_ref[...], wo1_ref[...]) + _dot(om, wo2_ref[...])
    hn = _rms(h2, g2_ref[...]).astype(BF16)
    u = jnp.maximum(_dot(hn, wup_ref[...]), 0.0)
    h3 = h2 + _dot((u * u).astype(BF16), wdn_ref[...])
    y_ref[...] = _rms(h3, gf_ref[...])


def _post_call(h2d, og, om, consts, tm, absorbed_values, name):
    n_rows, d_model = h2d.shape
    assert n_rows % tm == 0
    row_spec = lambda w: pl.BlockSpec((tm, w), lambda i: (i, 0))
    kern = functools.partial(_post_kernel, absorbed_values=absorbed_values)
    return pl.pallas_call(
        kern,
        grid=(n_rows // tm,),
        in_specs=[row_spec(d_model), row_spec(og.shape[1]), row_spec(om.shape[1])]
                 + [_const_spec(c.shape) for c in consts],
        out_specs=row_spec(d_model),
        out_shape=jax.ShapeDtypeStruct((n_rows, d_model), F32),
        compiler_params=pltpu.CompilerParams(dimension_semantics=("arbitrary",),
                                             vmem_limit_bytes=VMEM_LIMIT),
        name=name,
    )(h2d, og, om, *consts)


def _rope_parts(pos):
    half = MLA_D_ROPE // 2
    inv = ROPE_THETA ** (-jnp.arange(half, dtype=F32) / half)
    ang = pos.astype(F32)[:, None] * inv[None, :]
    cos = jnp.concatenate([jnp.cos(ang), jnp.cos(ang)], -1)
    sin_signed = jnp.concatenate([-jnp.sin(ang), jnp.sin(ang)], -1)
    return cos, sin_signed


def _pad_lanes(x, width):
    return jnp.pad(x, ((0, 0), (0, width - x.shape[1])))


def _at_lanes(x, lane0, width=LANES):
    return jnp.pad(x, ((0, 0), (lane0, width - lane0 - x.shape[1])))


def _key_tables(cos, sin_signed):
    return _at_lanes(cos, KPE_LANE0), _at_lanes(sin_signed, KPE_LANE0)


def _prompt_tables(pos):
    cos, sin_signed = _rope_parts(pos)
    n = pos.shape[0]
    half = MLA_D_ROPE // 2
    qcos = jnp.concatenate([jnp.full((n, MLA_D_NOPE), Q_SCALE_LOG2, F32), cos * Q_SCALE_LOG2], -1)
    qsin_lo = _at_lanes(sin_signed[:, :half] * Q_SCALE_LOG2, MLA_D_NOPE)
    qsin_hi = _at_lanes(sin_signed[:, half:] * Q_SCALE_LOG2, MLA_D_NOPE + half)
    kcos, ksin = _key_tables(cos, sin_signed)
    return jnp.concatenate([_pad_lanes(qcos, LANES), qsin_lo, qsin_hi, kcos, ksin], -1)


def _sample_tables(pos):
    cos, sin_signed = _rope_parts(pos)
    kcos, ksin = _key_tables(cos, sin_signed)
    return jnp.concatenate([jnp.tile(cos * Q_SCALE_LOG2, (1, MLA_HEADS)),
                            jnp.tile(sin_signed * Q_SCALE_LOG2, (1, MLA_HEADS)), kcos, ksin], -1)


def _swap_halves(w):
    half = MLA_D_ROPE // 2
    return jnp.concatenate([w[..., half:], w[..., :half]], -1)


def kernel(x_prompt, x_sample, cache_latent, cache_krope, state_conv, state_ssm, page_table,
           meta_tokens, norm_mix_g, w_in, conv_w, a_log, dt_bias, gdn_norm_g, q_norm_g, w_q_b,
           kv_norm_g, w_kv_b, w_out, norm_mlp_g, w_up, w_down, final_norm_g):
    assert w_in.shape[0] == 1, "single-layer problem"
    B, T, D = x_prompt.shape
    Bs, Ts, _ = x_sample.shape
    assert CONV_W - 1 <= Ts <= SAMPLE_T_PAD
    past_len = page_table.shape[1] * cache_latent.shape[2]
    q_lora = q_norm_g.shape[1]
    kv_lora = kv_norm_g.shape[1]
    H = MLA_HEADS

    wi = w_in[0]
    c0 = CONV_CH + GDN_V_W
    c1 = c0 + 2 * GDN_HEADS
    wkpe_raw = wi[:, c1 + q_lora + kv_lora:]
    blk_a = jnp.concatenate([_pad_lanes(wi[:, c0:c1], KPE_LANE0), _pad_lanes(wkpe_raw, LANES - KPE_LANE0)], 1)
    blk_b = _at_lanes(_swap_halves(wkpe_raw), KPE_LANE0)
    win = jnp.concatenate([wi[:, :c0], wi[:, c1:c1 + q_lora], blk_a,
                           wi[:, c1 + q_lora:c1 + q_lora + kv_lora], blk_b], axis=1).astype(BF16)
    gmix = norm_mix_g[0][None, :]
    qg = q_norm_g[0][None, :]
    kvg = kv_norm_g[0][None, :]
    wq = w_q_b[0]
    wq_nope, wq_pe = wq[..., :MLA_D_NOPE], wq[..., MLA_D_NOPE:]
    wq_pesw = _swap_halves(wq_pe)
    zq = lambda n: jnp.zeros((q_lora, H, n), F32)
    wqcat = jnp.concatenate([wq_nope, wq_pe, zq(HEAD_PAD - MLA_D_NOPE - MLA_D_ROPE)], -1)
    wqcat = wqcat.reshape(q_lora, H * HEAD_PAD).astype(BF16)
    wkv = w_kv_b[0]
    wkb, wvb = wkv[..., :MLA_D_NOPE], wkv[..., MLA_D_NOPE:]
    zk = lambda n: jnp.zeros((kv_lora, H, n), F32)
    wkb_sp = jnp.concatenate([wkb, zk(HEAD_PAD - MLA_D_NOPE)], -1).reshape(kv_lora, H * HEAD_PAD).astype(BF16)
    odd = (jnp.arange(H) % 2 == 1)[None, :, None]
    wvb_sp = jnp.where(odd, jnp.concatenate([zk(HEAD_PAD - MLA_D_V), wvb], -1),
                       jnp.concatenate([wvb, zk(HEAD_PAD - MLA_D_V)], -1))
    wvb_sp_t = wvb_sp.reshape(kv_lora, H * HEAD_PAD).T.astype(BF16)
    eye_h = jnp.eye(H, dtype=bool)
    wabs = jnp.where(eye_h[:, None, :, None], wkb.transpose(1, 2, 0)[:, :, None, :], 0.0)
    wabs = wabs.reshape(H * MLA_D_NOPE, H * kv_lora).astype(BF16)
    wvbd = jnp.where(eye_h[:, None, :, None], wvb.transpose(1, 0, 2)[:, :, None, :], 0.0)
    wvbd = wvbd.reshape(H * kv_lora, H * MLA_D_V).astype(BF16)
    wo1 = w_out[0][:GDN_V_W].astype(BF16)
    wo2 = w_out[0][GDN_V_W:].astype(BF16)
    g2 = norm_mlp_g[0][None, :]
    wup = w_up[0].astype(BF16)
    wdn = w_down[0].astype(BF16)
    gf = final_norm_g[None, :]
    cw = jnp.pad(conv_w[0], ((0, SUBLANES - CONV_W), (0, 0)))
    lane_row = lambda v: jnp.pad(v[None, :], ((0, 0), (GDN_HEADS, LANES - 2 * GDN_HEADS)))
    alog = lane_row(a_log[0])
    dtb = lane_row(dt_bias[0])
    ng = gdn_norm_g[0][None, :]

    common = (gmix, win, qg, kvg)
    slot_row = jnp.arange(H * HEAD_PAD) % HEAD_PAD
    vones = (slot_row == _denominator_row(jnp.arange(H * HEAD_PAD) // HEAD_PAD)).astype(F32)[:, None]
    prompt_consts = common + (wqcat, wkb_sp, wvb_sp_t, vones)
    proj_prompt = functools.partial(_proj_prompt_kernel, q_lora=q_lora, kv_lora=kv_lora)
    proj_sample = functools.partial(_proj_sample_kernel, q_lora=q_lora, kv_lora=kv_lora)
    sample_consts = common + (wq_nope.reshape(q_lora, -1).astype(BF16), wq_pe.reshape(q_lora, -1).astype(BF16),
                              wq_pesw.reshape(q_lora, -1).astype(BF16), wabs)
    pw = [CONV_CH, GDN_V_W, LANES, kv_lora, MLA_D_ROPE, H * HEAD_PAD, H * HEAD_PAD, H * HEAD_PAD]
    pd = [F32, F32, F32, F32, F32, BF16, BF16, BF16]

    tab_m = _prompt_tables(jnp.arange(N_META))
    qkv_m, z_m, ba_m, lat_m, kpe_m, _, kcat_m, vt_m = _proj_call(
        proj_prompt, meta_tokens, tab_m, prompt_consts, pw, pd, N_META, "proj_meta", transposed_outs=(7,))
    zero_conv = jnp.zeros((1, SUBLANES, CONV_CH), F32)
    zero_s = jnp.zeros((1, GDN_HEADS, GDN_DK, GDN_DV), F32)
    _, s1 = _gdn_call(qkv_m[None], z_m[None], ba_m[None], zero_conv, zero_s, cw, alog, dtb, ng,
                      N_META, N_META, "gdn_meta")
    conv1 = jnp.pad(qkv_m[N_META - (CONV_W - 1):], ((SUBLANES - (CONV_W - 1), 0), (0, 0)))[None]
    k_meta = jnp.pad(kcat_m, ((0, LANES - N_META), (0, 0)))
    vt_meta = jnp.pad(vt_m, ((0, 0), (0, LANES - N_META)))

    tab_p = _prompt_tables(N_META + jnp.arange(T))
    xp2d = x_prompt.reshape(B * T, D)
    qkv_p, z_p, ba_p, lat_p, kpe_p, qcat_p, kcat_p, vt_p = _proj_call(
        proj_prompt, xp2d, tab_p, prompt_consts, pw, pd, PROJ_TM, "proj_prompt", transposed_outs=(7,))
    r3 = lambda a: a.reshape(B, T, a.shape[-1])
    qkv_p3 = r3(qkv_p)
    og_p, s2 = _gdn_call(qkv_p3, r3(z_p), r3(ba_p), conv1, s1, cw, alog, dtb, ng, GDN_CHUNK, T, "gdn_prompt",
                         nb_intra=GDN_NB_INTRA, nb_scan=GDN_NB_SCAN)
    om_p = _attn_call(r3(qcat_p), r3(kcat_p), vt_p, k_meta, vt_meta, ATTN_TQ, ATTN_HPS, ATTN_KV_UNROLL)
    post_consts = (wo1, wo2, g2, wup, wdn, gf)
    y_p = _post_call(xp2d, og_p.reshape(B * T, -1), om_p.reshape(B * T, -1), post_consts, POST_TM, False, "post_prompt")

    tp = SAMPLE_T_PAD
    xs = jnp.pad(x_sample, ((0, 0), (0, tp - Ts), (0, 0))).reshape(Bs * tp, D)
    tab_s = jnp.tile(_sample_tables(past_len + jnp.arange(tp)), (Bs, 1))
    sw = [CONV_CH, GDN_V_W, LANES, kv_lora, MLA_D_ROPE, H * kv_lora, H * MLA_D_ROPE]
    sd = [F32, F32, F32, F32, F32, BF16, BF16]
    qkv_s, z_s, ba_s, lat_s, kpe_s, qlat_s, qpe_s = _proj_call(
        proj_sample, xs, tab_s, sample_consts, sw, sd, Bs * tp, "proj_sample")
    s3 = lambda a: a.reshape(Bs, tp, a.shape[-1])
    qkv_s3 = s3(qkv_s)
    conv_in_s = jnp.pad(state_conv[0], ((0, 0), (SUBLANES - (CONV_W - 1), 0), (0, 0)))
    og_s, s_new = _gdn_call(qkv_s3, s3(z_s), s3(ba_s), conv_in_s, state_ssm[0], cw, alog, dtb, ng, tp, Ts, "gdn_sample",
                            nb_intra=GDN_NB_SAMPLE, nb_scan=GDN_NB_SAMPLE)
    out_lat = _dattn_call(page_table, qlat_s.reshape(Bs, tp * H, kv_lora), qpe_s.reshape(Bs, tp * H, MLA_D_ROPE),
                          s3(lat_s), s3(kpe_s), cache_latent, jnp.swapaxes(cache_krope, 2, 3), Ts,
                          DATTN_GROUP, DATTN_SLOTS, DATTN_SUB)
    sample_post_consts = (wvbd,) + post_consts
    y_s = _post_call(xs, og_s.reshape(Bs * tp, -1), out_lat.reshape(Bs * tp, H * kv_lora),
                     sample_post_consts, Bs * tp, True, "post_sample")

    bc = lambda a: jnp.broadcast_to(a[None], (B,) + a.shape)
    lat_po = jnp.concatenate([bc(lat_m), r3(lat_p)], axis=1)[None]
    kpe_po = jnp.concatenate([bc(kpe_m), r3(kpe_p)], axis=1)[None]
    conv_po = qkv_p3[:, T - (CONV_W - 1):][None]
    conv_so = qkv_s3[:, Ts - (CONV_W - 1):Ts][None]
    return (y_p.reshape(B, T, D), y_s.reshape(Bs, tp, D)[:, :Ts],
            lat_po, kpe_po, conv_po, s2[None],
            s3(lat_s)[:, :Ts][None], s3(kpe_s)[:, :Ts][None], conv_so, s_new[None])
```

```python
import functools
import math

import jax
import jax.numpy as jnp
from jax import lax
from jax.experimental import pallas as pl
from jax.experimental.pallas import tpu as pltpu

F32 = jnp.float32
BF16 = jnp.bfloat16

N_META = 16
EPS = 1e-6
GDN_HEADS = 8
GDN_DK = 64
GDN_DV = 64
CONV_W = 4
GDN_CHUNK = 64
MLA_HEADS = 8
MLA_D_NOPE = 64
MLA_D_ROPE = 32
MLA_D_V = 64
ROPE_THETA = 10000.0
MLA_SCALE = (MLA_D_NOPE + MLA_D_ROPE) ** -0.5
Q_SCALE_LOG2 = MLA_SCALE * math.log2(math.e)
GDN_QK_W = GDN_HEADS * GDN_DK
GDN_V_W = GDN_HEADS * GDN_DV
CONV_CH = 2 * GDN_QK_W + GDN_V_W

LANES = 128
SUBLANES = 8
HEAD_PAD = 128
SAMPLE_T_PAD = 8
KPE_LANE0 = 32
VMEM_LIMIT = 56 * 1024 * 1024

PROJ_TM = 512
PROJ_SPLIT = 2
POST_TM = 512
ATTN_TQ = 256
ATTN_HPS = 8
ATTN_KV_UNROLL = 2
GDN_NB_INTRA = 8
GDN_NB_SCAN = 8
GDN_NB_SAMPLE = 8
DATTN_GROUP = 32
DATTN_SLOTS = 4
DATTN_SUB = 4


def _dot(a, b):
    return jnp.dot(a, b, preferred_element_type=F32)


def _dot_nt(a, b):
    return lax.dot_general(a, b, (((1,), (1,)), ((), ())), preferred_element_type=F32)


def _rms(x, g):
    return x * lax.rsqrt(jnp.mean(x * x, axis=-1, keepdims=True) + EPS) * g


def _sigmoid(x):
    return 1.0 / (1.0 + jnp.exp(-x))


def _const_spec(shape):
    nd = len(shape)
    return pl.BlockSpec(shape, lambda *_: (0,) * nd)


def _proj_common(x_ref, gmix_ref, win_ref, qg_ref, kvg_ref, kcos, ksin, qkv_ref, z_ref, ba_ref, lat_ref, kpe_ref,
                 *, q_lora, kv_lora):
    hn = _rms(x_ref[...], gmix_ref[...]).astype(BF16)
    p = _dot(hn, win_ref[...])
    c0 = CONV_CH + GDN_V_W
    c1 = c0 + q_lora
    c2 = c1 + LANES
    c3 = c2 + kv_lora
    qkv_ref[...] = p[:, :CONV_CH]
    z_ref[...] = p[:, CONV_CH:c0]
    blk_a = p[:, c1:c2]
    ba_ref[...] = blk_a
    kpe = blk_a * kcos + p[:, c3:c3 + LANES] * ksin
    kpe_ref[...] = kpe[:, KPE_LANE0:KPE_LANE0 + MLA_D_ROPE]
    lat = _rms(p[:, c2:c3], kvg_ref[...])
    lat_ref[...] = lat
    cqn = _rms(p[:, c0:c1], qg_ref[...]).astype(BF16)
    return cqn, lat.astype(BF16), kpe


def _proj_prompt_kernel(x_ref, tab_ref, gmix_ref, win_ref, qg_ref, kvg_ref, wq_ref, wkb_ref, wvb_ref, vones_ref,
                        qkv_ref, z_ref, ba_ref, lat_ref, kpe_ref, qcat_ref, kcat_ref, vt_ref,
                        *, q_lora, kv_lora, n_split=1):
    rows = x_ref.shape[0] // n_split
    for s in range(n_split):
        rs = pl.ds(s * rows, rows)
        _proj_prompt_rows(x_ref.at[rs], tab_ref.at[rs], gmix_ref, win_ref, qg_ref, kvg_ref, wq_ref, wkb_ref,
                          wvb_ref, vones_ref, qkv_ref.at[rs], z_ref.at[rs], ba_ref.at[rs], lat_ref.at[rs],
                          kpe_ref.at[rs], qcat_ref.at[rs], kcat_ref.at[rs], vt_ref.at[:, rs],
                          q_lora=q_lora, kv_lora=kv_lora)


def _proj_prompt_rows(x_ref, tab_ref, gmix_ref, win_ref, qg_ref, kvg_ref, wq_ref, wkb_ref, wvb_ref, vones_ref,
                      qkv_ref, z_ref, ba_ref, lat_ref, kpe_ref, qcat_ref, kcat_ref, vt_ref, *, q_lora, kv_lora):
    qcos = tab_ref[:, 0 * LANES:1 * LANES]
    qsin_lo = tab_ref[:, 1 * LANES:2 * LANES]
    qsin_hi = tab_ref[:, 2 * LANES:3 * LANES]
    kcos = tab_ref[:, 3 * LANES:4 * LANES]
    ksin = tab_ref[:, 4 * LANES:5 * LANES]
    cqn, latb, kpe = _proj_common(x_ref, gmix_ref, win_ref, qg_ref, kvg_ref, kcos, ksin,
                                  qkv_ref, z_ref, ba_ref, lat_ref, kpe_ref, q_lora=q_lora, kv_lora=kv_lora)
    qa = _dot(cqn, wq_ref[...])
    knope = _dot(latb, wkb_ref[...])
    half = MLA_D_ROPE // 2
    kpe_slot = pltpu.roll(kpe, MLA_D_NOPE - KPE_LANE0, axis=1)
    for h in range(MLA_HEADS):
        sl = slice(h * HEAD_PAD, (h + 1) * HEAD_PAD)
        qs = qa[:, sl]
        q_dn = pltpu.roll(qs, HEAD_PAD - half, axis=1)
        q_up = pltpu.roll(qs, half, axis=1)
        qcat_ref[:, sl] = (qs * qcos + q_dn * qsin_lo + q_up * qsin_hi).astype(BF16)
        kcat_ref[:, sl] = (knope[:, sl] + kpe_slot).astype(BF16)
    vt_ref[...] = (_dot_nt(wvb_ref[...], latb) + vones_ref[...]).astype(BF16)


def _proj_sample_kernel(x_ref, tab_ref, gmix_ref, win_ref, qg_ref, kvg_ref, wqn_ref, wqpe_ref, wqpesw_ref, wabs_ref,
                        qkv_ref, z_ref, ba_ref, lat_ref, kpe_ref, qlat_ref, qpe_ref, *, q_lora, kv_lora):
    pe_w = MLA_HEADS * MLA_D_ROPE
    qcos = tab_ref[:, 0:pe_w]
    qsin = tab_ref[:, pe_w:2 * pe_w]
    kcos = tab_ref[:, 2 * pe_w:2 * pe_w + LANES]
    ksin = tab_ref[:, 2 * pe_w + LANES:2 * pe_w + 2 * LANES]
    cqn, _, _ = _proj_common(x_ref, gmix_ref, win_ref, qg_ref, kvg_ref, kcos, ksin,
                             qkv_ref, z_ref, ba_ref, lat_ref, kpe_ref, q_lora=q_lora, kv_lora=kv_lora)
    qn = _dot(cqn, wqn_ref[...]).astype(BF16)
    qlat_ref[...] = (_dot(qn, wabs_ref[...]) * Q_SCALE_LOG2).astype(BF16)
    qpe_ref[...] = (_dot(cqn, wqpe_ref[...]) * qcos + _dot(cqn, wqpesw_ref[...]) * qsin).astype(BF16)


def _proj_call(kernel_fn, x2d, tab, consts, out_widths, out_dtypes, tm, name, transposed_outs=()):
    n_rows, d_model = x2d.shape
    assert n_rows % tm == 0
    row_spec = lambda w: pl.BlockSpec((tm, w), lambda i: (i, 0))
    col_spec = lambda w: pl.BlockSpec((w, tm), lambda i: (0, i))
    is_t = [k in transposed_outs for k in range(len(out_widths))]
    assert tab.shape[0] % tm == 0 and n_rows % tab.shape[0] == 0
    tab_blocks = tab.shape[0] // tm
    tab_spec = pl.BlockSpec((tm, tab.shape[1]), lambda i: (i % tab_blocks, 0))
    in_specs = [row_spec(d_model), tab_spec] + [_const_spec(c.shape) for c in consts]
    return pl.pallas_call(
        kernel_fn,
        grid=(n_rows // tm,),
        in_specs=in_specs,
        out_specs=[col_spec(w) if t else row_spec(w) for w, t in zip(out_widths, is_t)],
        out_shape=[jax.ShapeDtypeStruct((w, n_rows) if t else (n_rows, w), dt)
                   for w, dt, t in zip(out_widths, out_dtypes, is_t)],
        compiler_params=pltpu.CompilerParams(dimension_semantics=("arbitrary",),
                                             vmem_limit_bytes=VMEM_LIMIT),
        name=name,
    )(x2d, tab, *consts)


def _block_diag2(x, half):
    lane = lax.broadcasted_iota(jnp.int32, x.shape, 1)
    zero = jnp.zeros_like(x)
    return jnp.concatenate([jnp.where(lane < half, x, zero), jnp.where(lane < half, zero, x)], axis=0)


def _gdn_intra_prep(bb, cinit_bb, qkv_ref, prev_ref, ba_ref, cinit_ref, cw_ref, alog_ref, dtb_ref,
                    qk_mask, kg_ref, qg_ref, kdt_ref, vb_ref, egl_ref, xbuf, *, chunk, valid_len, n_chunks):
    c = pl.program_id(1)
    C = chunk
    H = GDN_HEADS
    tail0 = SUBLANES - (CONV_W - 1)

    xbuf[bb, 0:SUBLANES, :] = jnp.where(c == 0, cinit_ref[cinit_bb], prev_ref[bb])
    xbuf[bb, SUBLANES:SUBLANES + C, :] = qkv_ref[bb]
    cw = cw_ref[...]
    y = xbuf[bb, tail0:tail0 + C, :] * cw[0:1, :]
    for j in range(1, CONV_W):
        y = y + xbuf[bb, tail0 + j:tail0 + j + C, :] * cw[j:j + 1, :]
    y = y * _sigmoid(y)

    ba = ba_ref[bb]
    beta_all = _sigmoid(ba)
    sp_in = ba + dtb_ref[...]
    softplus = jnp.maximum(sp_in, 0.0) + jnp.log1p(jnp.exp(-jnp.abs(sp_in)))
    g_all = -jnp.exp(alog_ref[...]) * softplus
    if valid_len < n_chunks * C:
        row = c * C + lax.broadcasted_iota(jnp.int32, (C, LANES), 0)
        beta_all = jnp.where(row < valid_len, beta_all, 0.0)
        g_all = jnp.where(row < valid_len, g_all, 0.0)

    ri = lax.broadcasted_iota(jnp.int32, (C, C), 0)
    ci = lax.broadcasted_iota(jnp.int32, (C, C), 1)
    lower = ri >= ci
    gc_all = lax.dot_general(lower.astype(F32), g_all, (((1,), (0,)), ((), ())),
                             precision=lax.Precision.HIGHEST, preferred_element_type=F32)
    sel = (lax.broadcasted_iota(jnp.int32, (SUBLANES, LANES), 1)
           == lax.broadcasted_iota(jnp.int32, (SUBLANES, LANES), 0) + GDN_HEADS).astype(F32)
    gc_t = lax.dot_general(sel, gc_all, (((1,), (1,)), ((), ())),
                           precision=lax.Precision.HIGHEST, preferred_element_type=F32)

    pairs = range(H // 2)
    lane = lax.broadcasted_iota(jnp.int32, (C, LANES), 1)
    lo = lane < GDN_DK
    lo_c, lower_p = qk_mask
    ii_r = lax.broadcasted_iota(jnp.int32, (GDN_DK, LANES), 0)
    ii_c = lax.broadcasted_iota(jnp.int32, (GDN_DK, LANES), 1)
    eye2 = ((ii_c == ii_r) | (ii_c == ii_r + GDN_DK)).astype(BF16)

    def col(x, idx):
        return jnp.sum(jnp.where(lane == idx, x, 0.0), axis=-1, keepdims=True)

    def half_sums(x):
        return jnp.where(lo, jnp.sum(jnp.where(lo, x, 0.0), axis=-1, keepdims=True),
                         jnp.sum(jnp.where(lo, 0.0, x), axis=-1, keepdims=True))

    kn_b, kb_b, qn_b, decay = [], [], [], []
    for j in pairs:
        ps = slice(j * LANES, (j + 1) * LANES)
        qp = y[:, j * LANES:(j + 1) * LANES]
        kp = y[:, GDN_QK_W + j * LANES:GDN_QK_W + (j + 1) * LANES]
        vp = y[:, 2 * GDN_QK_W + j * LANES:2 * GDN_QK_W + (j + 1) * LANES]
        qn = qp * lax.rsqrt(half_sums(qp * qp) + EPS) * (GDN_DK ** -0.5)
        kn = kp * lax.rsqrt(half_sums(kp * kp) + EPS)
        beta = jnp.where(lo, col(beta_all, 2 * j), col(beta_all, 2 * j + 1))
        gc0, gc1 = col(gc_all, H + 2 * j), col(gc_all, H + 2 * j + 1)
        gcol = jnp.where(lo, gc0, gc1)
        grow = jnp.concatenate([gc_t[2 * j:2 * j + 1, :], gc_t[2 * j + 1:2 * j + 2, :]], axis=1)
        diff = jnp.where(lo_c, gc0, gc1) - grow
        decay.append(jnp.where(lower_p, jnp.exp(jnp.where(lower_p, diff, 0.0)), 0.0))
        egc = jnp.exp(gcol)
        gl = gcol[C - 1:C, :]
        egl_ref[bb, 0, :, ps] = jnp.exp(gl)
        kb = kn * beta
        vb_ref[bb, 0, :, ps] = vp * beta
        kg_ref[bb, 0, :, ps] = (kb * egc).astype(BF16)
        qg_ref[bb, 0, :, ps] = (qn * egc).astype(BF16)
        k_dec = (kn * jnp.exp(gl - gcol)).astype(BF16)
        kdt_ref[bb, 0, j] = _dot_nt(eye2, _block_diag2(k_dec, GDN_DK)).astype(BF16)
        kn_b.append(_block_diag2(kn.astype(BF16), GDN_DK))
        kb_b.append(kb.astype(BF16))
        qn_b.append(qn.astype(BF16))
    return kn_b, kb_b, qn_b, decay


def _gdn_intra_kernel(qkv_ref, prev_ref, ba_ref, cinit_ref, cw_ref, alog_ref, dtb_ref,
                      t_ref, qk_ref, kg_ref, qg_ref, kdt_ref, vb_ref, egl_ref, xbuf,
                      *, chunk, valid_len, n_chunks, nb, shared_cinit):
    C = chunk
    rp = lax.broadcasted_iota(jnp.int32, (C, 2 * C), 0)
    cp = lax.broadcasted_iota(jnp.int32, (C, 2 * C), 1)
    lo_c = cp < C
    cmod = jnp.where(lo_c, cp, cp - C)
    lower_p, strict_p = rp >= cmod, rp > cmod
    eye_p = (rp == cmod).astype(F32)
    kn_b, kb_b, qn_b, decay = [], [], [], []
    for bb in range(nb):
        parts = _gdn_intra_prep(bb, 0 if shared_cinit else bb, qkv_ref, prev_ref, ba_ref, cinit_ref, cw_ref,
                                alog_ref, dtb_ref, (lo_c, lower_p), kg_ref, qg_ref, kdt_ref, vb_ref, egl_ref, xbuf,
                                chunk=chunk, valid_len=valid_len, n_chunks=n_chunks)
        for dst, src in zip((kn_b, kb_b, qn_b, decay), parts):
            dst.extend(src)
    n_pairs = GDN_HEADS // 2
    probs = range(nb * n_pairs)
    kk = [_dot_nt(kb_b[i], kn_b[i]) for i in probs]
    qk = [_dot_nt(qn_b[i], kn_b[i]) for i in probs]
    for i in probs:
        qk_ref[i // n_pairs, 0, i % n_pairs] = jnp.where(lower_p, qk[i] * decay[i], 0.0).astype(BF16)
    p = [jnp.where(strict_p, -(kk[i] * decay[i]), 0.0) for i in probs]
    t = [eye_p + p[i] for i in probs]
    n_iter = max(1, int(math.ceil(math.log2(C))))
    pb = [p[i].astype(BF16) for i in probs]
    pbd = [_block_diag2(pb[i], C) for i in probs]
    for _ in range(1, n_iter):
        p = [_dot(pb[i], pbd[i]) for i in probs]
        pb = [p[i].astype(BF16) for i in probs]
        pbd = [_block_diag2(pb[i], C) for i in probs]
        t = [t[i] + _dot(t[i].astype(BF16), pbd[i]) for i in probs]
    for i in probs:
        t_ref[i // n_pairs, 0, i % n_pairs] = t[i].astype(BF16)


def _gdn_scan_kernel(t_ref, qk_ref, kg_ref, qg_ref, kdt_ref, vb_ref, egl_ref, z_ref, sinit_ref, ng_ref,
                     o_ref, sfin_ref, s_scr, *, chunk, n_chunks, nb, shared_sinit):
    c = pl.program_id(1)
    C = chunk
    n_pairs = GDN_HEADS // 2
    probs = [(bb, j) for bb in range(nb) for j in range(n_pairs)]
    ps = [slice(j * LANES, (j + 1) * LANES) for j in range(n_pairs)]
    lo = lax.broadcasted_iota(jnp.int32, (C, LANES), 1) < GDN_DV

    @pl.when(c == 0)
    def _():
        for bb, j in probs:
            si = 0 if shared_sinit else bb
            s_scr[bb, j] = jnp.concatenate([sinit_ref[si, 2 * j], sinit_ref[si, 2 * j + 1]], axis=1)

    s = [s_scr[bb, j] for bb, j in probs]
    sbd = [_block_diag2(x.astype(BF16), GDN_DV) for x in s]
    ks = [_dot(kg_ref[bb, 0, :, ps[j]], sbd[i]) for i, (bb, j) in enumerate(probs)]
    qs = [_dot(qg_ref[bb, 0, :, ps[j]], sbd[i]) for i, (bb, j) in enumerate(probs)]
    r = [(vb_ref[bb, 0, :, ps[j]] - ks[i]).astype(BF16) for i, (bb, j) in enumerate(probs)]
    v_new = [_dot(t_ref[bb, 0, j], _block_diag2(r[i], GDN_DV)).astype(BF16) for i, (bb, j) in enumerate(probs)]
    vbd = [_block_diag2(x, GDN_DV) for x in v_new]
    o = [qs[i] + _dot(qk_ref[bb, 0, j], vbd[i]) for i, (bb, j) in enumerate(probs)]
    for i, (bb, j) in enumerate(probs):
        s_scr[bb, j] = s[i] * egl_ref[bb, 0, :, ps[j]] + _dot(kdt_ref[bb, 0, j], vbd[i])
    ng = ng_ref[...]
    for i, (bb, j) in enumerate(probs):
        o2 = o[i] * o[i]
        ms = jnp.where(lo, jnp.sum(jnp.where(lo, o2, 0.0), axis=-1, keepdims=True),
                       jnp.sum(jnp.where(lo, 0.0, o2), axis=-1, keepdims=True)) * (1.0 / GDN_DV)
        zp = z_ref[bb, :, ps[j]]
        o_ref[bb, :, ps[j]] = (o[i] * lax.rsqrt(ms + EPS) * ng * (zp * _sigmoid(zp))).astype(o_ref.dtype)

    @pl.when(c == n_chunks - 1)
    def _():
        for bb, j in probs:
            s_pair = s_scr[bb, j]
            sfin_ref[bb, 2 * j] = s_pair[:, :GDN_DV]
            sfin_ref[bb, 2 * j + 1] = s_pair[:, GDN_DV:]


def _gdn_call(qkv, z, ba, conv_init, s_init, cw, alog, dtb, ng, chunk, valid_len, name, nb_intra=1, nb_scan=1):
    B, T, _ = qkv.shape
    assert T % chunk == 0 and chunk % SUBLANES == 0 and GDN_DK == GDN_DV and 2 * GDN_DV == LANES
    n_chunks = T // chunk
    H, C = GDN_HEADS, chunk
    HP = H // 2
    shared_c = conv_init.shape[0] == 1
    shared_s = s_init.shape[0] == 1
    ng = jnp.tile(ng, (1, 2))
    params = pltpu.CompilerParams(dimension_semantics=("arbitrary", "arbitrary"), vmem_limit_bytes=VMEM_LIMIT)
    inter_dims = [(HP, C, 2 * C), (HP, C, 2 * C), (C, GDN_QK_W), (C, GDN_QK_W), (HP, GDN_DK, 2 * C),
                  (C, GDN_V_W), (1, GDN_V_W)]
    inter_dtypes = [BF16, BF16, BF16, BF16, BF16, F32, F32]

    def specs(nb):
        tok = lambda w: pl.BlockSpec((nb, chunk, w), lambda b, c: (b, c, 0))
        per_chunk = lambda *dims: pl.BlockSpec((nb, 1) + dims, lambda b, c: (b, c) + (0,) * len(dims))
        per_seq = lambda shared, *dims: pl.BlockSpec(
            ((1 if shared else nb),) + dims, lambda b, c: ((0 if shared else b),) + (0,) * len(dims))
        return tok, per_chunk, per_seq

    nb = nb_intra
    assert B % nb == 0
    tok, per_chunk, per_seq = specs(nb)
    prev_rows = pl.BlockSpec((nb, SUBLANES, CONV_CH),
                             lambda b, c: (b, jnp.maximum(c * (C // SUBLANES) - 1, 0), 0))
    inter = pl.pallas_call(
        functools.partial(_gdn_intra_kernel, chunk=chunk, valid_len=valid_len, n_chunks=n_chunks,
                          nb=nb, shared_cinit=shared_c),
        grid=(B // nb, n_chunks),
        in_specs=[
            tok(CONV_CH), prev_rows, tok(LANES), per_seq(shared_c, SUBLANES, CONV_CH),
            _const_spec(cw.shape), _const_spec(alog.shape), _const_spec(dtb.shape),
        ],
        out_specs=[per_chunk(*d) for d in inter_dims],
        out_shape=[jax.ShapeDtypeStruct((B, n_chunks) + d, dt) for d, dt in zip(inter_dims, inter_dtypes)],
        scratch_shapes=[pltpu.VMEM((nb, SUBLANES + chunk, CONV_CH), F32)],
        compiler_params=params,
        name=name + "_intra",
    )(qkv, qkv, ba, conv_init, cw, alog, dtb)

    nb = nb_scan
    assert B % nb == 0
    tok, per_chunk, per_seq = specs(nb)
    o, s_fin = pl.pallas_call(
        functools.partial(_gdn_scan_kernel, chunk=chunk, n_chunks=n_chunks, nb=nb, shared_sinit=shared_s),
        grid=(B // nb, n_chunks),
        in_specs=[per_chunk(*d) for d in inter_dims] + [
            tok(GDN_V_W), per_seq(shared_s, H, GDN_DK, GDN_DV), _const_spec(ng.shape),
        ],
        out_specs=[tok(GDN_V_W), per_seq(False, H, GDN_DK, GDN_DV)],
        out_shape=[
            jax.ShapeDtypeStruct((B, T, GDN_V_W), BF16),
            jax.ShapeDtypeStruct((B, H, GDN_DK, GDN_DV), F32),
        ],
        scratch_shapes=[pltpu.VMEM((nb, HP, GDN_DK, LANES), F32)],
        compiler_params=params,
        name=name + "_scan",
    )(*inter, z, s_init, ng)
    return o, s_fin


def _denominator_row(h):
    return MLA_D_V * (1 - h % 2)


def _attn_kernel(q_ref, k_ref, vt_ref, km_ref, vmt_ref, o_ref, m_scr, acc_scr, *, tq, n_meta, hps, kv_unroll):
    qi = pl.program_id(2)
    neg = -jnp.inf
    key_m = lax.broadcasted_iota(jnp.int32, (km_ref.shape[0], tq), 0)
    key_d = lax.broadcasted_iota(jnp.int32, (tq, tq), 0)
    qry_d = lax.broadcasted_iota(jnp.int32, (tq, tq), 1)
    heads = range(hps)
    ls = [slice(h * HEAD_PAD, (h + 1) * HEAD_PAD) for h in heads]

    s = [jnp.where(key_m < n_meta, _dot_nt(km_ref[:, ls[h]], q_ref[0, :, ls[h]]), neg) for h in heads]
    m = [jnp.max(s[h], axis=0, keepdims=True) for h in heads]
    p = [jnp.exp2(s[h] - m[h]) for h in heads]
    for h in heads:
        m_scr[h] = m[h]
        acc_scr[h] = _dot(vmt_ref[ls[h], :], p[h].astype(BF16))

    def blocks(specs):
        chains = [(start, diag, h) for start, diag in specs for h in heads]
        s = [_dot_nt(k_ref[0, pl.ds(start, tq), ls[h]], q_ref[0, :, ls[h]]) for start, _, h in chains]
        s = [jnp.where(key_d <= qry_d, x, neg) if diag else x for x, (_, diag, _) in zip(s, chains)]
        mb = [jnp.max(x, axis=0, keepdims=True) for x in s]
        p = [jnp.exp2(x - m) for x, m in zip(s, mb)]
        pv = [_dot(vt_ref[ls[h], pl.ds(start, tq)], x.astype(BF16)) for x, (start, _, h) in zip(p, chains)]
        for h in heads:
            mine = [i for i, c in enumerate(chains) if c[2] == h]
            m_old = m_scr[h]
            m_new = m_old
            for i in mine:
                m_new = jnp.maximum(m_new, mb[i])
            acc = jnp.exp2(m_old - m_new) * acc_scr[h]
            for i in mine:
                acc = acc + jnp.exp2(mb[i] - m_new) * pv[i]
            acc_scr[h] = acc
            m_scr[h] = m_new

    def start_of(kb):
        return pl.multiple_of(kb * tq, tq)

    def body(kp, carry):
        blocks([(start_of(kv_unroll * kp + u), False) for u in range(kv_unroll)])
        return carry

    lax.fori_loop(0, qi // kv_unroll, body, 0)
    for rem in range(kv_unroll):
        @pl.when(qi % kv_unroll == rem)
        def _(rem=rem):
            blocks([(start_of(qi - rem + u), False) for u in range(rem)] + [(start_of(qi), True)])
    slot_row = lax.broadcasted_iota(jnp.int32, (HEAD_PAD, tq), 0)

    def normalised(h):
        acc = acc_scr[h]
        r = _denominator_row(h)
        return jnp.where(slot_row == r, 0.0, acc) / acc[r:r + 1, :]

    for j in range(hps // 2):
        pair = normalised(2 * j) + normalised(2 * j + 1)
        o_ref[0, :, j * LANES:(j + 1) * LANES] = pair.T.astype(o_ref.dtype)


def _attn_call(qcat, kcat, vt, k_meta, vt_meta, tq, hps, kv_unroll):
    B, T, _ = qcat.shape
    assert MLA_HEADS % hps == 0 and hps % 2 == 0
    n_groups = MLA_HEADS // hps
    gw = hps * HEAD_PAD
    ow = (hps // 2) * LANES
    kern = functools.partial(_attn_kernel, tq=tq, n_meta=N_META, hps=hps, kv_unroll=kv_unroll)
    return pl.pallas_call(
        kern,
        grid=(B, n_groups, T // tq),
        in_specs=[
            pl.BlockSpec((1, tq, gw), lambda b, g, qi: (b, qi, g)),
            pl.BlockSpec((1, T, gw), lambda b, g, qi: (b, 0, g)),
            pl.BlockSpec((gw, T), lambda b, g, qi: (g, b)),
            pl.BlockSpec((k_meta.shape[0], gw), lambda b, g, qi: (0, g)),
            pl.BlockSpec((gw, vt_meta.shape[1]), lambda b, g, qi: (g, 0)),
        ],
        out_specs=pl.BlockSpec((1, tq, ow), lambda b, g, qi: (b, qi, g)),
        out_shape=jax.ShapeDtypeStruct((B, T, n_groups * ow), BF16),
        scratch_shapes=[pltpu.VMEM((hps, 1, tq), F32), pltpu.VMEM((hps, HEAD_PAD, tq), F32)],
        compiler_params=pltpu.CompilerParams(dimension_semantics=("arbitrary", "arbitrary", "arbitrary"),
                                             vmem_limit_bytes=VMEM_LIMIT),
        name="attn",
    )(qcat, kcat, vt, k_meta, vt_meta)


def _dattn_kernel(pt_ref, qlat_ref, qpe_ref, latn_ref, kpen_ref, lat_hbm, kpe_hbm, o_ref,
                  lat_buf, kpe_buf, sem_lat, sem_kpe, m_scr, l_scr, acc_scr,
                  *, group, n_groups, n_slots, n_sub, t_new, page):
    b = pl.program_id(0)
    n_seq = pl.num_programs(0)
    neg = -jnp.inf

    def page_copies(seq, g, k, slot):
        pid = pt_ref[seq, g * group + k]
        rows = pl.ds(pl.multiple_of(k * page, page), page)
        return (pltpu.make_async_copy(lat_hbm.at[0, pid], lat_buf.at[slot, rows, :], sem_lat.at[slot]),
                pltpu.make_async_copy(kpe_hbm.at[0, pid], kpe_buf.at[slot, k], sem_kpe.at[slot]))

    def start_group(seq, g, slot):
        def body(k2, carry):
            for prio in range(2):
                for cp in page_copies(seq, g, 2 * k2 + prio, slot):
                    cp.start(priority=prio)
            return carry
        lax.fori_loop(0, group // 2, body, 0)

    def wait_group(seq, g, slot):
        def body(k, carry):
            for cp in page_copies(seq, g, k, slot):
                cp.wait()
            return carry
        lax.fori_loop(0, group, body, 0)

    ahead = n_slots - 1

    @pl.when(b == 0)
    def _():
        for g0 in range(ahead):
            start_group(0, g0, g0 % n_slots)

    m_scr[...] = jnp.full(m_scr.shape, neg, F32)
    l_scr[...] = jnp.zeros(l_scr.shape, F32)
    acc_scr[...] = jnp.zeros(acc_scr.shape, F32)

    ql = qlat_ref[0]
    qp = qpe_ref[0]

    def update(scores, vals):
        ms = [jnp.max(s, axis=-1, keepdims=True) for s in scores]
        ps = [jnp.exp2(s - m) for s, m in zip(scores, ms)]
        ls = [jnp.sum(p, axis=-1, keepdims=True) for p in ps]
        pvs = [_dot(p.astype(BF16), v) for p, v in zip(ps, vals)]
        m_old = m_scr[...]
        m_new = m_old
        for m in ms:
            m_new = jnp.maximum(m_new, m)
        alpha = jnp.exp2(m_old - m_new)
        l_new = alpha * l_scr[...]
        acc_new = alpha * acc_scr[...]
        for m, l, pv in zip(ms, ls, pvs):
            w = jnp.exp2(m - m_new)
            l_new = l_new + w * l
            acc_new = acc_new + w * pv
        l_scr[...] = l_new
        acc_scr[...] = acc_new
        m_scr[...] = m_new

    sub_pages = group // n_sub
    sub_keys = sub_pages * page
    for g in range(n_groups):
        slot = g % n_slots
        nxt = g + ahead
        if nxt < n_groups:
            start_group(b, nxt, nxt % n_slots)
        else:
            @pl.when(b + 1 < n_seq)
            def _(nxt=nxt):
                start_group(b + 1, nxt - n_groups, nxt % n_slots)
        wait_group(b, g, slot)
        cbs = [lat_buf[slot, i * sub_keys:(i + 1) * sub_keys, :].astype(BF16) for i in range(n_sub)]
        kbts = [jnp.concatenate([kpe_buf[slot, k] for k in range(i * sub_pages, (i + 1) * sub_pages)],
                                axis=1).astype(BF16) for i in range(n_sub)]
        update([_dot_nt(ql, cb) + _dot(qp, kbt) for cb, kbt in zip(cbs, kbts)], cbs)

    pad = 2 * SUBLANES - latn_ref.shape[1]
    cn = jnp.concatenate([latn_ref[0], jnp.zeros((pad, latn_ref.shape[2]), F32)], axis=0).astype(BF16)
    kn = jnp.concatenate([kpen_ref[0], jnp.zeros((pad, kpen_ref.shape[2]), F32)], axis=0).astype(BF16)
    s = _dot_nt(ql, cn) + _dot_nt(qp, kn)
    tok = lax.shift_right_logical(lax.broadcasted_iota(jnp.int32, s.shape, 0), int(math.log2(MLA_HEADS)))
    key = lax.broadcasted_iota(jnp.int32, s.shape, 1)
    s = jnp.where((key <= tok) & (key < t_new), s, neg)
    update([s], [cn])
    o_ref[0] = (acc_scr[...] / l_scr[...]).astype(o_ref.dtype)


def _dattn_call(page_table, qlat, qpe, lat_new, kpe_new, cache_latent, cache_krope_t, t_new, group, n_slots, n_sub):
    assert group % n_sub == 0 and group % 2 == 0
    Bs, R, kv_lora = qlat.shape
    n_pages = page_table.shape[1]
    assert n_pages % (n_slots * group) == 0, "a group must keep its slot number across sequences"
    n_groups = n_pages // group
    assert 2 <= n_slots <= n_groups
    page = cache_latent.shape[2]
    kern = functools.partial(_dattn_kernel, group=group, n_groups=n_groups, n_slots=n_slots, n_sub=n_sub,
                             t_new=t_new, page=page)
    per_b = lambda shape: pl.BlockSpec((1,) + shape, lambda b, pt: (b, 0, 0))
    grid_spec = pltpu.PrefetchScalarGridSpec(
        num_scalar_prefetch=1,
        grid=(Bs,),
        in_specs=[per_b((R, kv_lora)), per_b((R, MLA_D_ROPE)),
                  per_b(lat_new.shape[1:]), per_b(kpe_new.shape[1:]),
                  pl.BlockSpec(memory_space=pl.ANY), pl.BlockSpec(memory_space=pl.ANY)],
        out_specs=per_b((R, kv_lora)),
        scratch_shapes=[pltpu.VMEM((n_slots, group * page, kv_lora), F32),
                        pltpu.VMEM((n_slots, group, MLA_D_ROPE, page), F32),
                        pltpu.SemaphoreType.DMA((n_slots,)), pltpu.SemaphoreType.DMA((n_slots,)),
                        pltpu.VMEM((R, 1), F32), pltpu.VMEM((R, 1), F32), pltpu.VMEM((R, kv_lora), F32)],
    )
    return pl.pallas_call(
        kern,
        grid_spec=grid_spec,
        out_shape=jax.ShapeDtypeStruct((Bs, R, kv_lora), BF16),
        compiler_params=pltpu.CompilerParams(dimension_semantics=("arbitrary",),
                                             vmem_limit_bytes=VMEM_LIMIT),
        name="dattn",
    )(page_table, qlat, qpe, lat_new, kpe_new, cache_latent, cache_krope_t)


def _post_kernel(*refs, absorbed_values):
    if absorbed_values:
        h_ref, og_ref, om_ref, wvbd_ref, wo1_ref, wo2_ref, g2_ref, wup_ref, wdn_ref, gf_ref, y_ref = refs
        om = _dot(om_ref[...], wvbd_ref[...]).astype(BF16)
    else:
        h_ref, og_ref, om_ref, wo1_ref, wo2_ref, g2_ref, wup_ref, wdn_ref, gf_ref, y_ref = refs
        om = om_ref[...]
    h2 = h_ref[...] + _dot(og_ref[...], wo1_ref[...]) + _dot(om, wo2_ref[...])
    hn = _rms(h2, g2_ref[...]).astype(BF16)
    u = jnp.maximum(_dot(hn, wup_ref[...]), 0.0)
    h3 = h2 + _dot((u * u).astype(BF16), wdn_ref[...])
    y_ref[...] = _rms(h3, gf_ref[...])


def _post_call(h2d, og, om, consts, tm, absorbed_values, name):
    n_rows, d_model = h2d.shape
    assert n_rows % tm == 0
    row_spec = lambda w: pl.BlockSpec((tm, w), lambda i: (i, 0))
    kern = functools.partial(_post_kernel, absorbed_values=absorbed_values)
    return pl.pallas_call(
        kern,
        grid=(n_rows // tm,),
        in_specs=[row_spec(d_model), row_spec(og.shape[1]), row_spec(om.shape[1])]
                 + [_const_spec(c.shape) for c in consts],
        out_specs=row_spec(d_model),
        out_shape=jax.ShapeDtypeStruct((n_rows, d_model), F32),
        compiler_params=pltpu.CompilerParams(dimension_semantics=("arbitrary",),
                                             vmem_limit_bytes=VMEM_LIMIT),
        name=name,
    )(h2d, og, om, *consts)


def _rope_parts(pos):
    half = MLA_D_ROPE // 2
    inv = ROPE_THETA ** (-jnp.arange(half, dtype=F32) / half)
    ang = pos.astype(F32)[:, None] * inv[None, :]
    cos = jnp.concatenate([jnp.cos(ang), jnp.cos(ang)], -1)
    sin_signed = jnp.concatenate([-jnp.sin(ang), jnp.sin(ang)], -1)
    return cos, sin_signed


def _pad_lanes(x, width):
    return jnp.pad(x, ((0, 0), (0, width - x.shape[1])))


def _at_lanes(x, lane0, width=LANES):
    return jnp.pad(x, ((0, 0), (lane0, width - lane0 - x.shape[1])))


def _key_tables(cos, sin_signed):
    return _at_lanes(cos, KPE_LANE0), _at_lanes(sin_signed, KPE_LANE0)


def _prompt_tables(pos):
    cos, sin_signed = _rope_parts(pos)
    n = pos.shape[0]
    half = MLA_D_ROPE // 2
    qcos = jnp.concatenate([jnp.full((n, MLA_D_NOPE), Q_SCALE_LOG2, F32), cos * Q_SCALE_LOG2], -1)
    qsin_lo = _at_lanes(sin_signed[:, :half] * Q_SCALE_LOG2, MLA_D_NOPE)
    qsin_hi = _at_lanes(sin_signed[:, half:] * Q_SCALE_LOG2, MLA_D_NOPE + half)
    kcos, ksin = _key_tables(cos, sin_signed)
    return jnp.concatenate([_pad_lanes(qcos, LANES), qsin_lo, qsin_hi, kcos, ksin], -1)


def _sample_tables(pos):
    cos, sin_signed = _rope_parts(pos)
    kcos, ksin = _key_tables(cos, sin_signed)
    return jnp.concatenate([jnp.tile(cos * Q_SCALE_LOG2, (1, MLA_HEADS)),
                            jnp.tile(sin_signed * Q_SCALE_LOG2, (1, MLA_HEADS)), kcos, ksin], -1)


def _swap_halves(w):
    half = MLA_D_ROPE // 2
    return jnp.concatenate([w[..., half:], w[..., :half]], -1)


def kernel(x_prompt, x_sample, cache_latent, cache_krope, state_conv, state_ssm, page_table,
           meta_tokens, norm_mix_g, w_in, conv_w, a_log, dt_bias, gdn_norm_g, q_norm_g, w_q_b,
           kv_norm_g, w_kv_b, w_out, norm_mlp_g, w_up, w_down, final_norm_g):
    assert w_in.shape[0] == 1, "single-layer problem"
    B, T, D = x_prompt.shape
    Bs, Ts, _ = x_sample.shape
    assert CONV_W - 1 <= Ts <= SAMPLE_T_PAD
    past_len = page_table.shape[1] * cache_latent.shape[2]
    q_lora = q_norm_g.shape[1]
    kv_lora = kv_norm_g.shape[1]
    H = MLA_HEADS

    wi = w_in[0]
    c0 = CONV_CH + GDN_V_W
    c1 = c0 + 2 * GDN_HEADS
    wkpe_raw = wi[:, c1 + q_lora + kv_lora:]
    blk_a = jnp.concatenate([_pad_lanes(wi[:, c0:c1], KPE_LANE0), _pad_lanes(wkpe_raw, LANES - KPE_LANE0)], 1)
    blk_b = _at_lanes(_swap_halves(wkpe_raw), KPE_LANE0)
    win = jnp.concatenate([wi[:, :c0], wi[:, c1:c1 + q_lora], blk_a,
                           wi[:, c1 + q_lora:c1 + q_lora + kv_lora], blk_b], axis=1).astype(BF16)
    gmix = norm_mix_g[0][None, :]
    qg = q_norm_g[0][None, :]
    kvg = kv_norm_g[0][None, :]
    wq = w_q_b[0]
    wq_nope, wq_pe = wq[..., :MLA_D_NOPE], wq[..., MLA_D_NOPE:]
    wq_pesw = _swap_halves(wq_pe)
    zq = lambda n: jnp.zeros((q_lora, H, n), F32)
    wqcat = jnp.concatenate([wq_nope, wq_pe, zq(HEAD_PAD - MLA_D_NOPE - MLA_D_ROPE)], -1)
    wqcat = wqcat.reshape(q_lora, H * HEAD_PAD).astype(BF16)
    wkv = w_kv_b[0]
    wkb, wvb = wkv[..., :MLA_D_NOPE], wkv[..., MLA_D_NOPE:]
    zk = lambda n: jnp.zeros((kv_lora, H, n), F32)
    wkb_sp = jnp.concatenate([wkb, zk(HEAD_PAD - MLA_D_NOPE)], -1).reshape(kv_lora, H * HEAD_PAD).astype(BF16)
    odd = (jnp.arange(H) % 2 == 1)[None, :, None]
    wvb_sp = jnp.where(odd, jnp.concatenate([zk(HEAD_PAD - MLA_D_V), wvb], -1),
                       jnp.concatenate([wvb, zk(HEAD_PAD - MLA_D_V)], -1))
    wvb_sp_t = wvb_sp.reshape(kv_lora, H * HEAD_PAD).T.astype(BF16)
    eye_h = jnp.eye(H, dtype=bool)
    wabs = jnp.where(eye_h[:, None, :, None], wkb.transpose(1, 2, 0)[:, :, None, :], 0.0)
    wabs = wabs.reshape(H * MLA_D_NOPE, H * kv_lora).astype(BF16)
    wvbd = jnp.where(eye_h[:, None, :, None], wvb.transpose(1, 0, 2)[:, :, None, :], 0.0)
    wvbd = wvbd.reshape(H * kv_lora, H * MLA_D_V).astype(BF16)
    wo1 = w_out[0][:GDN_V_W].astype(BF16)
    wo2 = w_out[0][GDN_V_W:].astype(BF16)
    g2 = norm_mlp_g[0][None, :]
    wup = w_up[0].astype(BF16)
    wdn = w_down[0].astype(BF16)
    gf = final_norm_g[None, :]
    cw = jnp.pad(conv_w[0], ((0, SUBLANES - CONV_W), (0, 0)))
    lane_row = lambda v: jnp.pad(v[None, :], ((0, 0), (GDN_HEADS, LANES - 2 * GDN_HEADS)))
    alog = lane_row(a_log[0])
    dtb = lane_row(dt_bias[0])
    ng = gdn_norm_g[0][None, :]

    common = (gmix, win, qg, kvg)
    slot_row = jnp.arange(H * HEAD_PAD) % HEAD_PAD
    vones = (slot_row == _denominator_row(jnp.arange(H * HEAD_PAD) // HEAD_PAD)).astype(F32)[:, None]
    prompt_consts = common + (wqcat, wkb_sp, wvb_sp_t, vones)
    proj_prompt = functools.partial(_proj_prompt_kernel, q_lora=q_lora, kv_lora=kv_lora)
    proj_sample = functools.partial(_proj_sample_kernel, q_lora=q_lora, kv_lora=kv_lora)
    sample_consts = common + (wq_nope.reshape(q_lora, -1).astype(BF16), wq_pe.reshape(q_lora, -1).astype(BF16),
                              wq_pesw.reshape(q_lora, -1).astype(BF16), wabs)
    pw = [CONV_CH, GDN_V_W, LANES, kv_lora, MLA_D_ROPE, H * HEAD_PAD, H * HEAD_PAD, H * HEAD_PAD]
    pd = [F32, F32, F32, F32, F32, BF16, BF16, BF16]

    tab_m = _prompt_tables(jnp.arange(N_META))
    qkv_m, z_m, ba_m, lat_m, kpe_m, _, kcat_m, vt_m = _proj_call(
        proj_prompt, meta_tokens, tab_m, prompt_consts, pw, pd, N_META, "proj_meta", transposed_outs=(7,))
    zero_conv = jnp.zeros((1, SUBLANES, CONV_CH), F32)
    zero_s = jnp.zeros((1, GDN_HEADS, GDN_DK, GDN_DV), F32)
    _, s1 = _gdn_call(qkv_m[None], z_m[None], ba_m[None], zero_conv, zero_s, cw, alog, dtb, ng,
                      N_META, N_META, "gdn_meta")
    conv1 = jnp.pad(qkv_m[N_META - (CONV_W - 1):], ((SUBLANES - (CONV_W - 1), 0), (0, 0)))[None]
    k_meta = jnp.pad(kcat_m, ((0, LANES - N_META), (0, 0)))
    vt_meta = jnp.pad(vt_m, ((0, 0), (0, LANES - N_META)))

    tab_p = _prompt_tables(N_META + jnp.arange(T))
    xp2d = x_prompt.reshape(B * T, D)
    qkv_p, z_p, ba_p, lat_p, kpe_p, qcat_p, kcat_p, vt_p = _proj_call(
        functools.partial(proj_prompt, n_split=PROJ_SPLIT), xp2d, tab_p, prompt_consts, pw, pd, PROJ_TM,
        "proj_prompt", transposed_outs=(7,))
    r3 = lambda a: a.reshape(B, T, a.shape[-1])
    qkv_p3 = r3(qkv_p)
    og_p, s2 = _gdn_call(qkv_p3, r3(z_p), r3(ba_p), conv1, s1, cw, alog, dtb, ng, GDN_CHUNK, T, "gdn_prompt",
                         nb_intra=GDN_NB_INTRA, nb_scan=GDN_NB_SCAN)
    om_p = _attn_call(r3(qcat_p), r3(kcat_p), vt_p, k_meta, vt_meta, ATTN_TQ, ATTN_HPS, ATTN_KV_UNROLL)
    post_consts = (wo1, wo2, g2, wup, wdn, gf)
    y_p = _post_call(xp2d, og_p.reshape(B * T, -1), om_p.reshape(B * T, -1), post_consts, POST_TM, False, "post_prompt")

    tp = SAMPLE_T_PAD
    xs = jnp.pad(x_sample, ((0, 0), (0, tp - Ts), (0, 0))).reshape(Bs * tp, D)
    tab_s = jnp.tile(_sample_tables(past_len + jnp.arange(tp)), (Bs, 1))
    sw = [CONV_CH, GDN_V_W, LANES, kv_lora, MLA_D_ROPE, H * kv_lora, H * MLA_D_ROPE]
    sd = [F32, F32, F32, F32, F32, BF16, BF16]
    qkv_s, z_s, ba_s, lat_s, kpe_s, qlat_s, qpe_s = _proj_call(
        proj_sample, xs, tab_s, sample_consts, sw, sd, Bs * tp, "proj_sample")
    s3 = lambda a: a.reshape(Bs, tp, a.shape[-1])
    qkv_s3 = s3(qkv_s)
    conv_in_s = jnp.pad(state_conv[0], ((0, 0), (SUBLANES - (CONV_W - 1), 0), (0, 0)))
    og_s, s_new = _gdn_call(qkv_s3, s3(z_s), s3(ba_s), conv_in_s, state_ssm[0], cw, alog, dtb, ng, tp, Ts, "gdn_sample",
                            nb_intra=GDN_NB_SAMPLE, nb_scan=GDN_NB_SAMPLE)
    out_lat = _dattn_call(page_table, qlat_s.reshape(Bs, tp * H, kv_lora), qpe_s.reshape(Bs, tp * H, MLA_D_ROPE),
                          s3(lat_s), s3(kpe_s), cache_latent, jnp.swapaxes(cache_krope, 2, 3), Ts,
                          DATTN_GROUP, DATTN_SLOTS, DATTN_SUB)
    sample_post_consts = (wvbd,) + post_consts
    y_s = _post_call(xs, og_s.reshape(Bs * tp, -1), out_lat.reshape(Bs * tp, H * kv_lora),
                     sample_post_consts, Bs * tp, True, "post_sample")

    bc = lambda a: jnp.broadcast_to(a[None], (B,) + a.shape)
    lat_po = jnp.concatenate([bc(lat_m), r3(lat_p)], axis=1)[None]
    kpe_po = jnp.concatenate([bc(kpe_m), r3(kpe_p)], axis=1)[None]
    conv_po = qkv_p3[:, T - (CONV_W - 1):][None]
    conv_so = qkv_s3[:, Ts - (CONV_W - 1):Ts][None]
    return (y_p.reshape(B, T, D), y_s.reshape(Bs, tp, D)[:, :Ts],
            lat_po, kpe_po, conv_po, s2[None],
            s3(lat_s)[:, :Ts][None], s3(kpe_s)[:, :Ts][None], conv_so, s_new[None])
```

```python
import functools
import math

import jax
import jax.numpy as jnp
from jax import lax
from jax.experimental import pallas as pl
from jax.experimental.pallas import tpu as pltpu

F32 = jnp.float32
BF16 = jnp.bfloat16

N_META = 16
EPS = 1e-6
GDN_HEADS = 8
GDN_DK = 64
GDN_DV = 64
CONV_W = 4
GDN_CHUNK = 64
MLA_HEADS = 8
MLA_D_NOPE = 64
MLA_D_ROPE = 32
MLA_D_V = 64
ROPE_THETA = 10000.0
MLA_SCALE = (MLA_D_NOPE + MLA_D_ROPE) ** -0.5
Q_SCALE_LOG2 = MLA_SCALE * math.log2(math.e)
GDN_QK_W = GDN_HEADS * GDN_DK
GDN_V_W = GDN_HEADS * GDN_DV
CONV_CH = 2 * GDN_QK_W + GDN_V_W

LANES = 128
SUBLANES = 8
HEAD_PAD = 128
SAMPLE_T_PAD = 8
KPE_LANE0 = 32
VMEM_LIMIT = 56 * 1024 * 1024

PROJ_TM = 512
PROJ_SPLIT = 2
POST_TM = 512
ATTN_TQ = 256
ATTN_HPS = 8
ATTN_KV_UNROLL = 2
GDN_NB_INTRA = 8
GDN_NB_SCAN = 8
GDN_NB_SAMPLE = 8
DATTN_GROUP = 32
DATTN_SLOTS = 4
DATTN_SUB = 4


def _dot(a, b):
    return jnp.dot(a, b, preferred_element_type=F32)


def _dot_nt(a, b):
    return lax.dot_general(a, b, (((1,), (1,)), ((), ())), preferred_element_type=F32)


def _rms(x, g):
    return x * lax.rsqrt(jnp.mean(x * x, axis=-1, keepdims=True) + EPS) * g


def _sigmoid(x):
    return 1.0 / (1.0 + jnp.exp(-x))


def _const_spec(shape):
    nd = len(shape)
    return pl.BlockSpec(shape, lambda *_: (0,) * nd)


def _proj_common(x_ref, gmix_ref, win_ref, qg_ref, kvg_ref, kcos, ksin, qkv_ref, z_ref, ba_ref, lat_ref, kpe_ref,
                 *, q_lora, kv_lora):
    hn = _rms(x_ref[...], gmix_ref[...]).astype(BF16)
    p = _dot(hn, win_ref[...])
    c0 = CONV_CH + GDN_V_W
    c1 = c0 + q_lora
    c2 = c1 + LANES
    c3 = c2 + kv_lora
    qkv_ref[...] = p[:, :CONV_CH]
    z_ref[...] = p[:, CONV_CH:c0]
    blk_a = p[:, c1:c2]
    ba_ref[...] = blk_a
    kpe = blk_a * kcos + p[:, c3:c3 + LANES] * ksin
    kpe_ref[...] = kpe[:, KPE_LANE0:KPE_LANE0 + MLA_D_ROPE]
    lat = _rms(p[:, c2:c3], kvg_ref[...])
    lat_ref[...] = lat
    cqn = _rms(p[:, c0:c1], qg_ref[...]).astype(BF16)
    return cqn, lat.astype(BF16), kpe


def _proj_prompt_kernel(x_ref, tab_ref, gmix_ref, win_ref, qg_ref, kvg_ref, wq_ref, wkb_ref, wvb_ref, vones_ref,
                        qkv_ref, z_ref, ba_ref, lat_ref, kpe_ref, qcat_ref, kcat_ref, vt_ref,
                        *, q_lora, kv_lora, n_split=1):
    rows = x_ref.shape[0] // n_split
    for s in range(n_split):
        rs = pl.ds(s * rows, rows)
        _proj_prompt_rows(x_ref.at[rs], tab_ref.at[rs], gmix_ref, win_ref, qg_ref, kvg_ref, wq_ref, wkb_ref,
                          wvb_ref, vones_ref, qkv_ref.at[rs], z_ref.at[rs], ba_ref.at[rs], lat_ref.at[rs],
                          kpe_ref.at[rs], qcat_ref.at[rs], kcat_ref.at[rs], vt_ref.at[:, rs],
                          q_lora=q_lora, kv_lora=kv_lora)


def _proj_prompt_rows(x_ref, tab_ref, gmix_ref, win_ref, qg_ref, kvg_ref, wq_ref, wkb_ref, wvb_ref, vones_ref,
                      qkv_ref, z_ref, ba_ref, lat_ref, kpe_ref, qcat_ref, kcat_ref, vt_ref, *, q_lora, kv_lora):
    qcos = tab_ref[:, 0 * LANES:1 * LANES]
    qsin_lo = tab_ref[:, 1 * LANES:2 * LANES]
    qsin_hi = tab_ref[:, 2 * LANES:3 * LANES]
    kcos = tab_ref[:, 3 * LANES:4 * LANES]
    ksin = tab_ref[:, 4 * LANES:5 * LANES]
    cqn, latb, kpe = _proj_common(x_ref, gmix_ref, win_ref, qg_ref, kvg_ref, kcos, ksin,
                                  qkv_ref, z_ref, ba_ref, lat_ref, kpe_ref, q_lora=q_lora, kv_lora=kv_lora)
    qa = _dot(cqn, wq_ref[...])
    knope = _dot(latb, wkb_ref[...])
    half = MLA_D_ROPE // 2
    kpe_slot = pltpu.roll(kpe, MLA_D_NOPE - KPE_LANE0, axis=1)
    for h in range(MLA_HEADS):
        sl = slice(h * HEAD_PAD, (h + 1) * HEAD_PAD)
        qs = qa[:, sl]
        q_dn = pltpu.roll(qs, HEAD_PAD - half, axis=1)
        q_up = pltpu.roll(qs, half, axis=1)
        qcat_ref[:, sl] = (qs * qcos + q_dn * qsin_lo + q_up * qsin_hi).astype(BF16)
        kcat_ref[:, sl] = (knope[:, sl] + kpe_slot).astype(BF16)
    vt_ref[...] = (_dot_nt(wvb_ref[...], latb) + vones_ref[...]).astype(BF16)


def _proj_sample_kernel(x_ref, tab_ref, gmix_ref, win_ref, qg_ref, kvg_ref, wqn_ref, wqpe_ref, wqpesw_ref, wabs_ref,
                        qkv_ref, z_ref, ba_ref, lat_ref, kpe_ref, qlat_ref, qpe_ref, *, q_lora, kv_lora):
    pe_w = MLA_HEADS * MLA_D_ROPE
    qcos = tab_ref[:, 0:pe_w]
    qsin = tab_ref[:, pe_w:2 * pe_w]
    kcos = tab_ref[:, 2 * pe_w:2 * pe_w + LANES]
    ksin = tab_ref[:, 2 * pe_w + LANES:2 * pe_w + 2 * LANES]
    cqn, _, _ = _proj_common(x_ref, gmix_ref, win_ref, qg_ref, kvg_ref, kcos, ksin,
                             qkv_ref, z_ref, ba_ref, lat_ref, kpe_ref, q_lora=q_lora, kv_lora=kv_lora)
    qn = _dot(cqn, wqn_ref[...]).astype(BF16)
    qlat_ref[...] = (_dot(qn, wabs_ref[...]) * Q_SCALE_LOG2).astype(BF16)
    qpe_ref[...] = (_dot(cqn, wqpe_ref[...]) * qcos + _dot(cqn, wqpesw_ref[...]) * qsin).astype(BF16)


def _proj_call(kernel_fn, x2d, tab, consts, out_widths, out_dtypes, tm, name, transposed_outs=()):
    n_rows, d_model = x2d.shape
    assert n_rows % tm == 0
    row_spec = lambda w: pl.BlockSpec((tm, w), lambda i: (i, 0))
    col_spec = lambda w: pl.BlockSpec((w, tm), lambda i: (0, i))
    is_t = [k in transposed_outs for k in range(len(out_widths))]
    assert tab.shape[0] % tm == 0 and n_rows % tab.shape[0] == 0
    tab_blocks = tab.shape[0] // tm
    tab_spec = pl.BlockSpec((tm, tab.shape[1]), lambda i: (i % tab_blocks, 0))
    in_specs = [row_spec(d_model), tab_spec] + [_const_spec(c.shape) for c in consts]
    return pl.pallas_call(
        kernel_fn,
        grid=(n_rows // tm,),
        in_specs=in_specs,
        out_specs=[col_spec(w) if t else row_spec(w) for w, t in zip(out_widths, is_t)],
        out_shape=[jax.ShapeDtypeStruct((w, n_rows) if t else (n_rows, w), dt)
                   for w, dt, t in zip(out_widths, out_dtypes, is_t)],
        compiler_params=pltpu.CompilerParams(dimension_semantics=("arbitrary",),
                                             vmem_limit_bytes=VMEM_LIMIT),
        name=name,
    )(x2d, tab, *consts)


def _block_diag2(x, half):
    lane = lax.broadcasted_iota(jnp.int32, x.shape, 1)
    zero = jnp.zeros_like(x)
    return jnp.concatenate([jnp.where(lane < half, x, zero), jnp.where(lane < half, zero, x)], axis=0)


def _gdn_intra_prep(c, bb, cinit_bb, qkv_ref, prev_ref, ba_ref, cinit_ref, cw_ref, alog_ref, dtb_ref,
                    qk_mask, kg_ref, qg_ref, kdt_ref, vb_ref, egl_ref, xbuf, *, chunk, valid_len, n_chunks):
    C = chunk
    H = GDN_HEADS
    tail0 = SUBLANES - (CONV_W - 1)

    xbuf[bb, 0:SUBLANES, :] = jnp.where(c == 0, cinit_ref[cinit_bb], prev_ref[bb])
    xbuf[bb, SUBLANES:SUBLANES + C, :] = qkv_ref[bb]
    cw = cw_ref[...]
    y = xbuf[bb, tail0:tail0 + C, :] * cw[0:1, :]
    for j in range(1, CONV_W):
        y = y + xbuf[bb, tail0 + j:tail0 + j + C, :] * cw[j:j + 1, :]
    y = y * _sigmoid(y)

    ba = ba_ref[bb]
    beta_all = _sigmoid(ba)
    sp_in = ba + dtb_ref[...]
    softplus = jnp.maximum(sp_in, 0.0) + jnp.log1p(jnp.exp(-jnp.abs(sp_in)))
    g_all = -jnp.exp(alog_ref[...]) * softplus
    if valid_len < n_chunks * C:
        row = c * C + lax.broadcasted_iota(jnp.int32, (C, LANES), 0)
        beta_all = jnp.where(row < valid_len, beta_all, 0.0)
        g_all = jnp.where(row < valid_len, g_all, 0.0)

    ri = lax.broadcasted_iota(jnp.int32, (C, C), 0)
    ci = lax.broadcasted_iota(jnp.int32, (C, C), 1)
    lower = ri >= ci
    gc_all = lax.dot_general(lower.astype(F32), g_all, (((1,), (0,)), ((), ())),
                             precision=lax.Precision.HIGHEST, preferred_element_type=F32)
    sel = (lax.broadcasted_iota(jnp.int32, (SUBLANES, LANES), 1)
           == lax.broadcasted_iota(jnp.int32, (SUBLANES, LANES), 0) + GDN_HEADS).astype(F32)
    gc_t = lax.dot_general(sel, gc_all, (((1,), (1,)), ((), ())),
                           precision=lax.Precision.HIGHEST, preferred_element_type=F32)

    pairs = range(H // 2)
    lane = lax.broadcasted_iota(jnp.int32, (C, LANES), 1)
    lo = lane < GDN_DK
    lo_c, lower_p = qk_mask
    ii_r = lax.broadcasted_iota(jnp.int32, (GDN_DK, LANES), 0)
    ii_c = lax.broadcasted_iota(jnp.int32, (GDN_DK, LANES), 1)
    eye2 = ((ii_c == ii_r) | (ii_c == ii_r + GDN_DK)).astype(BF16)

    def col(x, idx):
        return jnp.sum(jnp.where(lane == idx, x, 0.0), axis=-1, keepdims=True)

    def half_sums(x):
        return jnp.where(lo, jnp.sum(jnp.where(lo, x, 0.0), axis=-1, keepdims=True),
                         jnp.sum(jnp.where(lo, 0.0, x), axis=-1, keepdims=True))

    kn_b, kb_b, qn_b, decay = [], [], [], []
    for j in pairs:
        ps = slice(j * LANES, (j + 1) * LANES)
        qp = y[:, j * LANES:(j + 1) * LANES]
        kp = y[:, GDN_QK_W + j * LANES:GDN_QK_W + (j + 1) * LANES]
        vp = y[:, 2 * GDN_QK_W + j * LANES:2 * GDN_QK_W + (j + 1) * LANES]
        qn = qp * lax.rsqrt(half_sums(qp * qp) + EPS) * (GDN_DK ** -0.5)
        kn = kp * lax.rsqrt(half_sums(kp * kp) + EPS)
        beta = jnp.where(lo, col(beta_all, 2 * j), col(beta_all, 2 * j + 1))
        gc0, gc1 = col(gc_all, H + 2 * j), col(gc_all, H + 2 * j + 1)
        gcol = jnp.where(lo, gc0, gc1)
        grow = jnp.concatenate([gc_t[2 * j:2 * j + 1, :], gc_t[2 * j + 1:2 * j + 2, :]], axis=1)
        diff = jnp.where(lo_c, gc0, gc1) - grow
        decay.append(jnp.where(lower_p, jnp.exp(jnp.where(lower_p, diff, 0.0)), 0.0))
        egc = jnp.exp(gcol)
        gl = gcol[C - 1:C, :]
        egl_ref[bb, 0, :, ps] = jnp.exp(gl)
        kb = kn * beta
        vb_ref[bb, 0, :, ps] = vp * beta
        kg_ref[bb, 0, :, ps] = (kb * egc).astype(BF16)
        qg_ref[bb, 0, :, ps] = (qn * egc).astype(BF16)
        k_dec = (kn * jnp.exp(gl - gcol)).astype(BF16)
        kdt_ref[bb, 0, j] = _dot_nt(eye2, _block_diag2(k_dec, GDN_DK)).astype(BF16)
        kn_b.append(_block_diag2(kn.astype(BF16), GDN_DK))
        kb_b.append(kb.astype(BF16))
        qn_b.append(qn.astype(BF16))
    return kn_b, kb_b, qn_b, decay


def _gdn_intra_kernel(*refs, **static):
    _gdn_intra_body(pl.program_id(1), *refs, **static)


def _gdn_intra_body(c, qkv_ref, prev_ref, ba_ref, cinit_ref, cw_ref, alog_ref, dtb_ref,
                    t_ref, qk_ref, kg_ref, qg_ref, kdt_ref, vb_ref, egl_ref, xbuf,
                    *, chunk, valid_len, n_chunks, nb, shared_cinit):
    C = chunk
    rp = lax.broadcasted_iota(jnp.int32, (C, 2 * C), 0)
    cp = lax.broadcasted_iota(jnp.int32, (C, 2 * C), 1)
    lo_c = cp < C
    cmod = jnp.where(lo_c, cp, cp - C)
    lower_p, strict_p = rp >= cmod, rp > cmod
    eye_p = (rp == cmod).astype(F32)
    kn_b, kb_b, qn_b, decay = [], [], [], []
    for bb in range(nb):
        parts = _gdn_intra_prep(c, bb, 0 if shared_cinit else bb, qkv_ref, prev_ref, ba_ref, cinit_ref, cw_ref,
                                alog_ref, dtb_ref, (lo_c, lower_p), kg_ref, qg_ref, kdt_ref, vb_ref, egl_ref, xbuf,
                                chunk=chunk, valid_len=valid_len, n_chunks=n_chunks)
        for dst, src in zip((kn_b, kb_b, qn_b, decay), parts):
            dst.extend(src)
    n_pairs = GDN_HEADS // 2
    probs = range(nb * n_pairs)
    kk = [_dot_nt(kb_b[i], kn_b[i]) for i in probs]
    qk = [_dot_nt(qn_b[i], kn_b[i]) for i in probs]
    for i in probs:
        qk_ref[i // n_pairs, 0, i % n_pairs] = jnp.where(lower_p, qk[i] * decay[i], 0.0).astype(BF16)
    p = [jnp.where(strict_p, -(kk[i] * decay[i]), 0.0) for i in probs]
    t = [eye_p + p[i] for i in probs]
    n_iter = max(1, int(math.ceil(math.log2(C))))
    pb = [p[i].astype(BF16) for i in probs]
    pbd = [_block_diag2(pb[i], C) for i in probs]
    for _ in range(1, n_iter):
        p = [_dot(pb[i], pbd[i]) for i in probs]
        pb = [p[i].astype(BF16) for i in probs]
        pbd = [_block_diag2(pb[i], C) for i in probs]
        t = [t[i] + _dot(t[i].astype(BF16), pbd[i]) for i in probs]
    for i in probs:
        t_ref[i // n_pairs, 0, i % n_pairs] = t[i].astype(BF16)


def _gdn_scan_kernel(t_ref, qk_ref, kg_ref, qg_ref, kdt_ref, vb_ref, egl_ref, z_ref, sinit_ref, ng_ref,
                     o_ref, sfin_ref, s_scr, *, chunk, n_chunks, nb, shared_sinit):
    c = pl.program_id(1)
    C = chunk
    n_pairs = GDN_HEADS // 2
    probs = [(bb, j) for bb in range(nb) for j in range(n_pairs)]
    ps = [slice(j * LANES, (j + 1) * LANES) for j in range(n_pairs)]
    lo = lax.broadcasted_iota(jnp.int32, (C, LANES), 1) < GDN_DV

    @pl.when(c == 0)
    def _():
        for bb, j in probs:
            si = 0 if shared_sinit else bb
            s_scr[bb, j] = jnp.concatenate([sinit_ref[si, 2 * j], sinit_ref[si, 2 * j + 1]], axis=1)

    s = [s_scr[bb, j] for bb, j in probs]
    sbd = [_block_diag2(x.astype(BF16), GDN_DV) for x in s]
    ks = [_dot(kg_ref[bb, 0, :, ps[j]], sbd[i]) for i, (bb, j) in enumerate(probs)]
    qs = [_dot(qg_ref[bb, 0, :, ps[j]], sbd[i]) for i, (bb, j) in enumerate(probs)]
    r = [(vb_ref[bb, 0, :, ps[j]] - ks[i]).astype(BF16) for i, (bb, j) in enumerate(probs)]
    v_new = [_dot(t_ref[bb, 0, j], _block_diag2(r[i], GDN_DV)).astype(BF16) for i, (bb, j) in enumerate(probs)]
    vbd = [_block_diag2(x, GDN_DV) for x in v_new]
    o = [qs[i] + _dot(qk_ref[bb, 0, j], vbd[i]) for i, (bb, j) in enumerate(probs)]
    for i, (bb, j) in enumerate(probs):
        s_scr[bb, j] = s[i] * egl_ref[bb, 0, :, ps[j]] + _dot(kdt_ref[bb, 0, j], vbd[i])
    ng = ng_ref[...]
    for i, (bb, j) in enumerate(probs):
        o2 = o[i] * o[i]
        ms = jnp.where(lo, jnp.sum(jnp.where(lo, o2, 0.0), axis=-1, keepdims=True),
                       jnp.sum(jnp.where(lo, 0.0, o2), axis=-1, keepdims=True)) * (1.0 / GDN_DV)
        zp = z_ref[bb, :, ps[j]]
        o_ref[bb, :, ps[j]] = (o[i] * lax.rsqrt(ms + EPS) * ng * (zp * _sigmoid(zp))).astype(o_ref.dtype)

    @pl.when(c == n_chunks - 1)
    def _():
        for bb, j in probs:
            s_pair = s_scr[bb, j]
            sfin_ref[bb, 2 * j] = s_pair[:, :GDN_DV]
            sfin_ref[bb, 2 * j + 1] = s_pair[:, GDN_DV:]


def _gdn_intra_dattn_kernel(pt_ref, *refs, intra_static, dattn_static, n_i_in, n_d_in, n_i_out, n_i_scr):
    i = pl.program_id(0)
    i_in, d_in = refs[:n_i_in], refs[n_i_in:n_i_in + n_d_in]
    o0 = n_i_in + n_d_in
    i_out, d_out = refs[o0:o0 + n_i_out], refs[o0 + n_i_out]
    s0 = o0 + n_i_out + 1
    i_scr, d_scr = refs[s0:s0 + n_i_scr], refs[s0 + n_i_scr:]
    _dattn_body(i, pl.num_programs(0), pt_ref, *d_in, d_out, *d_scr, **dattn_static)
    _gdn_intra_body(i, *i_in, *i_out, *i_scr, **intra_static)


def _gdn_call(qkv, z, ba, conv_init, s_init, cw, alog, dtb, ng, chunk, valid_len, name, nb_intra=1, nb_scan=1,
              dattn=None):
    B, T, _ = qkv.shape
    assert T % chunk == 0 and chunk % SUBLANES == 0 and GDN_DK == GDN_DV and 2 * GDN_DV == LANES
    n_chunks = T // chunk
    H, C = GDN_HEADS, chunk
    HP = H // 2
    shared_c = conv_init.shape[0] == 1
    shared_s = s_init.shape[0] == 1
    ng = jnp.tile(ng, (1, 2))
    params = pltpu.CompilerParams(dimension_semantics=("arbitrary", "arbitrary"), vmem_limit_bytes=VMEM_LIMIT)
    inter_dims = [(HP, C, 2 * C), (HP, C, 2 * C), (C, GDN_QK_W), (C, GDN_QK_W), (HP, GDN_DK, 2 * C),
                  (C, GDN_V_W), (1, GDN_V_W)]
    inter_dtypes = [BF16, BF16, BF16, BF16, BF16, F32, F32]

    def specs(nb, wrap=lambda f: f):
        tok = lambda w: pl.BlockSpec((nb, chunk, w), wrap(lambda b, c: (b, c, 0)))
        per_chunk = lambda *dims: pl.BlockSpec((nb, 1) + dims, wrap(lambda b, c: (b, c) + (0,) * len(dims)))
        per_seq = lambda shared, *dims: pl.BlockSpec(
            ((1 if shared else nb),) + dims, wrap(lambda b, c: ((0 if shared else b),) + (0,) * len(dims)))
        prev_rows = pl.BlockSpec((nb, SUBLANES, CONV_CH),
                                 wrap(lambda b, c: (b, jnp.maximum(c * (C // SUBLANES) - 1, 0), 0)))
        return tok, per_chunk, per_seq, prev_rows

    nb = nb_intra
    assert B % nb == 0
    intra_static = dict(chunk=chunk, valid_len=valid_len, n_chunks=n_chunks, nb=nb, shared_cinit=shared_c)
    intra_operands = (qkv, qkv, ba, conv_init, cw, alog, dtb)
    intra_out_shape = [jax.ShapeDtypeStruct((B, n_chunks) + d, dt) for d, dt in zip(inter_dims, inter_dtypes)]
    intra_scratch = [pltpu.VMEM((nb, SUBLANES + chunk, CONV_CH), F32)]

    def intra_specs(wrap=lambda f: f):
        tok, per_chunk, per_seq, prev_rows = specs(nb, wrap)
        in_specs = [tok(CONV_CH), prev_rows, tok(LANES), per_seq(shared_c, SUBLANES, CONV_CH),
                    _const_spec(cw.shape), _const_spec(alog.shape), _const_spec(dtb.shape)]
        return in_specs, [per_chunk(*d) for d in inter_dims]

    dattn_out = None
    if dattn is not None and B // nb == 1 and dattn["n_seq"] == n_chunks:
        in_specs, out_specs = intra_specs(lambda f: (lambda i, pt: f(0, i)))
        fused = pl.pallas_call(
            functools.partial(_gdn_intra_dattn_kernel, intra_static=intra_static, dattn_static=dattn["static"],
                              n_i_in=len(in_specs), n_d_in=len(dattn["in_specs"]), n_i_out=len(out_specs),
                              n_i_scr=len(intra_scratch)),
            grid_spec=pltpu.PrefetchScalarGridSpec(
                num_scalar_prefetch=1,
                grid=(n_chunks,),
                in_specs=in_specs + dattn["in_specs"],
                out_specs=out_specs + [dattn["out_spec"]],
                scratch_shapes=intra_scratch + dattn["scratch"],
            ),
            out_shape=intra_out_shape + [dattn["out_shape"]],
            compiler_params=pltpu.CompilerParams(dimension_semantics=("arbitrary",), vmem_limit_bytes=VMEM_LIMIT),
            name=name + "_intra_dattn",
        )(dattn["page_table"], *intra_operands, *dattn["operands"])
        inter, dattn_out = fused[:-1], fused[-1]
    else:
        in_specs, out_specs = intra_specs()
        inter = pl.pallas_call(
            functools.partial(_gdn_intra_kernel, **intra_static),
            grid=(B // nb, n_chunks),
            in_specs=in_specs,
            out_specs=out_specs,
            out_shape=intra_out_shape,
            scratch_shapes=intra_scratch,
            compiler_params=params,
            name=name + "_intra",
        )(*intra_operands)
        if dattn is not None:
            dattn_out = _dattn_run(dattn)

    nb = nb_scan
    assert B % nb == 0
    tok, per_chunk, per_seq, _ = specs(nb)
    o, s_fin = pl.pallas_call(
        functools.partial(_gdn_scan_kernel, chunk=chunk, n_chunks=n_chunks, nb=nb, shared_sinit=shared_s),
        grid=(B // nb, n_chunks),
        in_specs=[per_chunk(*d) for d in inter_dims] + [
            tok(GDN_V_W), per_seq(shared_s, H, GDN_DK, GDN_DV), _const_spec(ng.shape),
        ],
        out_specs=[tok(GDN_V_W), per_seq(False, H, GDN_DK, GDN_DV)],
        out_shape=[
            jax.ShapeDtypeStruct((B, T, GDN_V_W), BF16),
            jax.ShapeDtypeStruct((B, H, GDN_DK, GDN_DV), F32),
        ],
        scratch_shapes=[pltpu.VMEM((nb, HP, GDN_DK, LANES), F32)],
        compiler_params=params,
        name=name + "_scan",
    )(*inter, z, s_init, ng)
    return o, s_fin, dattn_out


def _denominator_row(h):
    return MLA_D_V * (1 - h % 2)


def _attn_kernel(q_ref, k_ref, vt_ref, km_ref, vmt_ref, o_ref, m_scr, acc_scr, *, tq, n_meta, hps, kv_unroll):
    qi = pl.program_id(2)
    neg = -jnp.inf
    key_m = lax.broadcasted_iota(jnp.int32, (km_ref.shape[0], tq), 0)
    key_d = lax.broadcasted_iota(jnp.int32, (tq, tq), 0)
    qry_d = lax.broadcasted_iota(jnp.int32, (tq, tq), 1)
    heads = range(hps)
    ls = [slice(h * HEAD_PAD, (h + 1) * HEAD_PAD) for h in heads]

    s = [jnp.where(key_m < n_meta, _dot_nt(km_ref[:, ls[h]], q_ref[0, :, ls[h]]), neg) for h in heads]
    m = [jnp.max(s[h], axis=0, keepdims=True) for h in heads]
    p = [jnp.exp2(s[h] - m[h]) for h in heads]
    for h in heads:
        m_scr[h] = m[h]
        acc_scr[h] = _dot(vmt_ref[ls[h], :], p[h].astype(BF16))

    def blocks(specs):
        chains = [(start, diag, h) for start, diag in specs for h in heads]
        s = [_dot_nt(k_ref[0, pl.ds(start, tq), ls[h]], q_ref[0, :, ls[h]]) for start, _, h in chains]
        s = [jnp.where(key_d <= qry_d, x, neg) if diag else x for x, (_, diag, _) in zip(s, chains)]
        mb = [jnp.max(x, axis=0, keepdims=True) for x in s]
        p = [jnp.exp2(x - m) for x, m in zip(s, mb)]
        pv = [_dot(vt_ref[ls[h], pl.ds(start, tq)], x.astype(BF16)) for x, (start, _, h) in zip(p, chains)]
        for h in heads:
            mine = [i for i, c in enumerate(chains) if c[2] == h]
            m_old = m_scr[h]
            m_new = m_old
            for i in mine:
                m_new = jnp.maximum(m_new, mb[i])
            acc = jnp.exp2(m_old - m_new) * acc_scr[h]
            for i in mine:
                acc = acc + jnp.exp2(mb[i] - m_new) * pv[i]
            acc_scr[h] = acc
            m_scr[h] = m_new

    def start_of(kb):
        return pl.multiple_of(kb * tq, tq)

    def body(kp, carry):
        blocks([(start_of(kv_unroll * kp + u), False) for u in range(kv_unroll)])
        return carry

    lax.fori_loop(0, qi // kv_unroll, body, 0)
    for rem in range(kv_unroll):
        @pl.when(qi % kv_unroll == rem)
        def _(rem=rem):
            blocks([(start_of(qi - rem + u), False) for u in range(rem)] + [(start_of(qi), True)])
    slot_row = lax.broadcasted_iota(jnp.int32, (HEAD_PAD, tq), 0)

    def normalised(h):
        acc = acc_scr[h]
        r = _denominator_row(h)
        return jnp.where(slot_row == r, 0.0, acc) / acc[r:r + 1, :]

    for j in range(hps // 2):
        pair = normalised(2 * j) + normalised(2 * j + 1)
        o_ref[0, :, j * LANES:(j + 1) * LANES] = pair.T.astype(o_ref.dtype)


def _attn_call(qcat, kcat, vt, k_meta, vt_meta, tq, hps, kv_unroll):
    B, T, _ = qcat.shape
    assert MLA_HEADS % hps == 0 and hps % 2 == 0
    n_groups = MLA_HEADS // hps
    gw = hps * HEAD_PAD
    ow = (hps // 2) * LANES
    kern = functools.partial(_attn_kernel, tq=tq, n_meta=N_META, hps=hps, kv_unroll=kv_unroll)
    return pl.pallas_call(
        kern,
        grid=(B, n_groups, T // tq),
        in_specs=[
            pl.BlockSpec((1, tq, gw), lambda b, g, qi: (b, qi, g)),
            pl.BlockSpec((1, T, gw), lambda b, g, qi: (b, 0, g)),
            pl.BlockSpec((gw, T), lambda b, g, qi: (g, b)),
            pl.BlockSpec((k_meta.shape[0], gw), lambda b, g, qi: (0, g)),
            pl.BlockSpec((gw, vt_meta.shape[1]), lambda b, g, qi: (g, 0)),
        ],
        out_specs=pl.BlockSpec((1, tq, ow), lambda b, g, qi: (b, qi, g)),
        out_shape=jax.ShapeDtypeStruct((B, T, n_groups * ow), BF16),
        scratch_shapes=[pltpu.VMEM((hps, 1, tq), F32), pltpu.VMEM((hps, HEAD_PAD, tq), F32)],
        compiler_params=pltpu.CompilerParams(dimension_semantics=("arbitrary", "arbitrary", "arbitrary"),
                                             vmem_limit_bytes=VMEM_LIMIT),
        name="attn",
    )(qcat, kcat, vt, k_meta, vt_meta)


def _dattn_kernel(*refs, **static):
    _dattn_body(pl.program_id(0), pl.num_programs(0), *refs, **static)


def _dattn_body(b, n_seq, pt_ref, qlat_ref, qpe_ref, latn_ref, kpen_ref, lat_hbm, kpe_hbm, o_ref,
                lat_buf, kpe_buf, sem_lat, sem_kpe, m_scr, l_scr, acc_scr,
                *, group, n_groups, n_slots, n_sub, t_new, page):
    neg = -jnp.inf

    def page_copies(seq, g, k, slot):
        pid = pt_ref[seq, g * group + k]
        rows = pl.ds(pl.multiple_of(k * page, page), page)
        return (pltpu.make_async_copy(lat_hbm.at[0, pid], lat_buf.at[slot, rows, :], sem_lat.at[slot]),
                pltpu.make_async_copy(kpe_hbm.at[0, pid], kpe_buf.at[slot, k], sem_kpe.at[slot]))

    def start_group(seq, g, slot):
        def body(k2, carry):
            for prio in range(2):
                for cp in page_copies(seq, g, 2 * k2 + prio, slot):
                    cp.start(priority=prio)
            return carry
        lax.fori_loop(0, group // 2, body, 0)

    def wait_group(seq, g, slot):
        def body(k, carry):
            for cp in page_copies(seq, g, k, slot):
                cp.wait()
            return carry
        lax.fori_loop(0, group, body, 0)

    ahead = n_slots - 1

    @pl.when(b == 0)
    def _():
        for g0 in range(ahead):
            start_group(0, g0, g0 % n_slots)

    m_scr[...] = jnp.full(m_scr.shape, neg, F32)
    l_scr[...] = jnp.zeros(l_scr.shape, F32)
    acc_scr[...] = jnp.zeros(acc_scr.shape, F32)

    ql = qlat_ref[0]
    qp = qpe_ref[0]

    def update(scores, vals):
        ms = [jnp.max(s, axis=-1, keepdims=True) for s in scores]
        ps = [jnp.exp2(s - m) for s, m in zip(scores, ms)]
        ls = [jnp.sum(p, axis=-1, keepdims=True) for p in ps]
        pvs = [_dot(p.astype(BF16), v) for p, v in zip(ps, vals)]
        m_old = m_scr[...]
        m_new = m_old
        for m in ms:
            m_new = jnp.maximum(m_new, m)
        alpha = jnp.exp2(m_old - m_new)
        l_new = alpha * l_scr[...]
        acc_new = alpha * acc_scr[...]
        for m, l, pv in zip(ms, ls, pvs):
            w = jnp.exp2(m - m_new)
            l_new = l_new + w * l
            acc_new = acc_new + w * pv
        l_scr[...] = l_new
        acc_scr[...] = acc_new
        m_scr[...] = m_new

    sub_pages = group // n_sub
    sub_keys = sub_pages * page
    for g in range(n_groups):
        slot = g % n_slots
        nxt = g + ahead
        if nxt < n_groups:
            start_group(b, nxt, nxt % n_slots)
        else:
            @pl.when(b + 1 < n_seq)
            def _(nxt=nxt):
                start_group(b + 1, nxt - n_groups, nxt % n_slots)
        wait_group(b, g, slot)
        cbs = [lat_buf[slot, i * sub_keys:(i + 1) * sub_keys, :].astype(BF16) for i in range(n_sub)]
        kbts = [jnp.concatenate([kpe_buf[slot, k] for k in range(i * sub_pages, (i + 1) * sub_pages)],
                                axis=1).astype(BF16) for i in range(n_sub)]
        update([_dot_nt(ql, cb) + _dot(qp, kbt) for cb, kbt in zip(cbs, kbts)], cbs)

    pad = 2 * SUBLANES - latn_ref.shape[1]
    cn = jnp.concatenate([latn_ref[0], jnp.zeros((pad, latn_ref.shape[2]), F32)], axis=0).astype(BF16)
    kn = jnp.concatenate([kpen_ref[0], jnp.zeros((pad, kpen_ref.shape[2]), F32)], axis=0).astype(BF16)
    s = _dot_nt(ql, cn) + _dot_nt(qp, kn)
    tok = lax.shift_right_logical(lax.broadcasted_iota(jnp.int32, s.shape, 0), int(math.log2(MLA_HEADS)))
    key = lax.broadcasted_iota(jnp.int32, s.shape, 1)
    s = jnp.where((key <= tok) & (key < t_new), s, neg)
    update([s], [cn])
    o_ref[0] = (acc_scr[...] / l_scr[...]).astype(o_ref.dtype)


def _dattn_parts(page_table, qlat, qpe, lat_new, kpe_new, cache_latent, cache_krope_t, t_new, group, n_slots, n_sub):
    assert group % n_sub == 0 and group % 2 == 0
    Bs, R, kv_lora = qlat.shape
    n_pages = page_table.shape[1]
    assert n_pages % (n_slots * group) == 0, "a group must keep its slot number across sequences"
    n_groups = n_pages // group
    assert 2 <= n_slots <= n_groups
    page = cache_latent.shape[2]
    per_b = lambda shape: pl.BlockSpec((1,) + shape, lambda b, pt: (b, 0, 0))
    return dict(
        n_seq=Bs,
        static=dict(group=group, n_groups=n_groups, n_slots=n_slots, n_sub=n_sub, t_new=t_new, page=page),
        page_table=page_table,
        operands=(qlat, qpe, lat_new, kpe_new, cache_latent, cache_krope_t),
        in_specs=[per_b((R, kv_lora)), per_b((R, MLA_D_ROPE)),
                  per_b(lat_new.shape[1:]), per_b(kpe_new.shape[1:]),
                  pl.BlockSpec(memory_space=pl.ANY), pl.BlockSpec(memory_space=pl.ANY)],
        out_spec=per_b((R, kv_lora)),
        out_shape=jax.ShapeDtypeStruct((Bs, R, kv_lora), BF16),
        scratch=[pltpu.VMEM((n_slots, group * page, kv_lora), F32),
                 pltpu.VMEM((n_slots, group, MLA_D_ROPE, page), F32),
                 pltpu.SemaphoreType.DMA((n_slots,)), pltpu.SemaphoreType.DMA((n_slots,)),
                 pltpu.VMEM((R, 1), F32), pltpu.VMEM((R, 1), F32), pltpu.VMEM((R, kv_lora), F32)],
    )


def _dattn_run(parts):
    grid_spec = pltpu.PrefetchScalarGridSpec(
        num_scalar_prefetch=1,
        grid=(parts["n_seq"],),
        in_specs=parts["in_specs"],
        out_specs=parts["out_spec"],
        scratch_shapes=parts["scratch"],
    )
    return pl.pallas_call(
        functools.partial(_dattn_kernel, **parts["static"]),
        grid_spec=grid_spec,
        out_shape=parts["out_shape"],
        compiler_params=pltpu.CompilerParams(dimension_semantics=("arbitrary",),
                                             vmem_limit_bytes=VMEM_LIMIT),
        name="dattn",
    )(parts["page_table"], *parts["operands"])


def _post_kernel(*refs, absorbed_values):
    if absorbed_values:
        h_ref, og_ref, om_ref, wvbd_ref, wo1_ref, wo2_ref, g2_ref, wup_ref, wdn_ref, gf_ref, y_ref = refs
        om = _dot(om_ref[...], wvbd_ref[...]).astype(BF16)
    else:
        h_ref, og_ref, om_ref, wo1_ref, wo2_ref, g2_ref, wup_ref, wdn_ref, gf_ref, y_ref = refs
        om = om_ref[...]
    h2 = h_ref[...] + _dot(og_ref[...], wo1_ref[...]) + _dot(om, wo2_ref[...])
    hn = _rms(h2, g2_ref[...]).astype(BF16)
    u = jnp.maximum(_dot(hn, wup_ref[...]), 0.0)
    h3 = h2 + _dot((u * u).astype(BF16), wdn_ref[...])
    y_ref[...] = _rms(h3, gf_ref[...])


def _post_call(h2d, og, om, consts, tm, absorbed_values, name):
    n_rows, d_model = h2d.shape
    assert n_rows % tm == 0
    row_spec = lambda w: pl.BlockSpec((tm, w), lambda i: (i, 0))
    kern = functools.partial(_post_kernel, absorbed_values=absorbed_values)
    return pl.pallas_call(
        kern,
        grid=(n_rows // tm,),
        in_specs=[row_spec(d_model), row_spec(og.shape[1]), row_spec(om.shape[1])]
                 + [_const_spec(c.shape) for c in consts],
        out_specs=row_spec(d_model),
        out_shape=jax.ShapeDtypeStruct((n_rows, d_model), F32),
        compiler_params=pltpu.CompilerParams(dimension_semantics=("arbitrary",),
                                             vmem_limit_bytes=VMEM_LIMIT),
        name=name,
    )(h2d, og, om, *consts)


def _rope_parts(pos):
    half = MLA_D_ROPE // 2
    inv = ROPE_THETA ** (-jnp.arange(half, dtype=F32) / half)
    ang = pos.astype(F32)[:, None] * inv[None, :]
    cos = jnp.concatenate([jnp.cos(ang), jnp.cos(ang)], -1)
    sin_signed = jnp.concatenate([-jnp.sin(ang), jnp.sin(ang)], -1)
    return cos, sin_signed


def _pad_lanes(x, width):
    return jnp.pad(x, ((0, 0), (0, width - x.shape[1])))


def _at_lanes(x, lane0, width=LANES):
    return jnp.pad(x, ((0, 0), (lane0, width - lane0 - x.shape[1])))


def _key_tables(cos, sin_signed):
    return _at_lanes(cos, KPE_LANE0), _at_lanes(sin_signed, KPE_LANE0)


def _prompt_tables(pos):
    cos, sin_signed = _rope_parts(pos)
    n = pos.shape[0]
    half = MLA_D_ROPE // 2
    qcos = jnp.concatenate([jnp.full((n, MLA_D_NOPE), Q_SCALE_LOG2, F32), cos * Q_SCALE_LOG2], -1)
    qsin_lo = _at_lanes(sin_signed[:, :half] * Q_SCALE_LOG2, MLA_D_NOPE)
    qsin_hi = _at_lanes(sin_signed[:, half:] * Q_SCALE_LOG2, MLA_D_NOPE + half)
    kcos, ksin = _key_tables(cos, sin_signed)
    return jnp.concatenate([_pad_lanes(qcos, LANES), qsin_lo, qsin_hi, kcos, ksin], -1)


def _sample_tables(pos):
    cos, sin_signed = _rope_parts(pos)
    kcos, ksin = _key_tables(cos, sin_signed)
    return jnp.concatenate([jnp.tile(cos * Q_SCALE_LOG2, (1, MLA_HEADS)),
                            jnp.tile(sin_signed * Q_SCALE_LOG2, (1, MLA_HEADS)), kcos, ksin], -1)


def _swap_halves(w):
    half = MLA_D_ROPE // 2
    return jnp.concatenate([w[..., half:], w[..., :half]], -1)


def kernel(x_prompt, x_sample, cache_latent, cache_krope, state_conv, state_ssm, page_table,
           meta_tokens, norm_mix_g, w_in, conv_w, a_log, dt_bias, gdn_norm_g, q_norm_g, w_q_b,
           kv_norm_g, w_kv_b, w_out, norm_mlp_g, w_up, w_down, final_norm_g):
    assert w_in.shape[0] == 1, "single-layer problem"
    B, T, D = x_prompt.shape
    Bs, Ts, _ = x_sample.shape
    assert CONV_W - 1 <= Ts <= SAMPLE_T_PAD
    past_len = page_table.shape[1] * cache_latent.shape[2]
    q_lora = q_norm_g.shape[1]
    kv_lora = kv_norm_g.shape[1]
    H = MLA_HEADS

    wi = w_in[0]
    c0 = CONV_CH + GDN_V_W
    c1 = c0 + 2 * GDN_HEADS
    wkpe_raw = wi[:, c1 + q_lora + kv_lora:]
    blk_a = jnp.concatenate([_pad_lanes(wi[:, c0:c1], KPE_LANE0), _pad_lanes(wkpe_raw, LANES - KPE_LANE0)], 1)
    blk_b = _at_lanes(_swap_halves(wkpe_raw), KPE_LANE0)
    win = jnp.concatenate([wi[:, :c0], wi[:, c1:c1 + q_lora], blk_a,
                           wi[:, c1 + q_lora:c1 + q_lora + kv_lora], blk_b], axis=1).astype(BF16)
    gmix = norm_mix_g[0][None, :]
    qg = q_norm_g[0][None, :]
    kvg = kv_norm_g[0][None, :]
    wq = w_q_b[0]
    wq_nope, wq_pe = wq[..., :MLA_D_NOPE], wq[..., MLA_D_NOPE:]
    wq_pesw = _swap_halves(wq_pe)
    zq = lambda n: jnp.zeros((q_lora, H, n), F32)
    wqcat = jnp.concatenate([wq_nope, wq_pe, zq(HEAD_PAD - MLA_D_NOPE - MLA_D_ROPE)], -1)
    wqcat = wqcat.reshape(q_lora, H * HEAD_PAD).astype(BF16)
    wkv = w_kv_b[0]
    wkb, wvb = wkv[..., :MLA_D_NOPE], wkv[..., MLA_D_NOPE:]
    zk = lambda n: jnp.zeros((kv_lora, H, n), F32)
    wkb_sp = jnp.concatenate([wkb, zk(HEAD_PAD - MLA_D_NOPE)], -1).reshape(kv_lora, H * HEAD_PAD).astype(BF16)
    odd = (jnp.arange(H) % 2 == 1)[None, :, None]
    wvb_sp = jnp.where(odd, jnp.concatenate([zk(HEAD_PAD - MLA_D_V), wvb], -1),
                       jnp.concatenate([wvb, zk(HEAD_PAD - MLA_D_V)], -1))
    wvb_sp_t = wvb_sp.reshape(kv_lora, H * HEAD_PAD).T.astype(BF16)
    eye_h = jnp.eye(H, dtype=bool)
    wabs = jnp.where(eye_h[:, None, :, None], wkb.transpose(1, 2, 0)[:, :, None, :], 0.0)
    wabs = wabs.reshape(H * MLA_D_NOPE, H * kv_lora).astype(BF16)
    wvbd = jnp.where(eye_h[:, None, :, None], wvb.transpose(1, 0, 2)[:, :, None, :], 0.0)
    wvbd = wvbd.reshape(H * kv_lora, H * MLA_D_V).astype(BF16)
    wo1 = w_out[0][:GDN_V_W].astype(BF16)
    wo2 = w_out[0][GDN_V_W:].astype(BF16)
    g2 = norm_mlp_g[0][None, :]
    wup = w_up[0].astype(BF16)
    wdn = w_down[0].astype(BF16)
    gf = final_norm_g[None, :]
    cw = jnp.pad(conv_w[0], ((0, SUBLANES - CONV_W), (0, 0)))
    lane_row = lambda v: jnp.pad(v[None, :], ((0, 0), (GDN_HEADS, LANES - 2 * GDN_HEADS)))
    alog = lane_row(a_log[0])
    dtb = lane_row(dt_bias[0])
    ng = gdn_norm_g[0][None, :]

    common = (gmix, win, qg, kvg)
    slot_row = jnp.arange(H * HEAD_PAD) % HEAD_PAD
    vones = (slot_row == _denominator_row(jnp.arange(H * HEAD_PAD) // HEAD_PAD)).astype(F32)[:, None]
    prompt_consts = common + (wqcat, wkb_sp, wvb_sp_t, vones)
    proj_prompt = functools.partial(_proj_prompt_kernel, q_lora=q_lora, kv_lora=kv_lora)
    proj_sample = functools.partial(_proj_sample_kernel, q_lora=q_lora, kv_lora=kv_lora)
    sample_consts = common + (wq_nope.reshape(q_lora, -1).astype(BF16), wq_pe.reshape(q_lora, -1).astype(BF16),
                              wq_pesw.reshape(q_lora, -1).astype(BF16), wabs)
    pw = [CONV_CH, GDN_V_W, LANES, kv_lora, MLA_D_ROPE, H * HEAD_PAD, H * HEAD_PAD, H * HEAD_PAD]
    pd = [F32, F32, F32, F32, F32, BF16, BF16, BF16]

    tab_m = _prompt_tables(jnp.arange(N_META))
    qkv_m, z_m, ba_m, lat_m, kpe_m, _, kcat_m, vt_m = _proj_call(
        proj_prompt, meta_tokens, tab_m, prompt_consts, pw, pd, N_META, "proj_meta", transposed_outs=(7,))
    zero_conv = jnp.zeros((1, SUBLANES, CONV_CH), F32)
    zero_s = jnp.zeros((1, GDN_HEADS, GDN_DK, GDN_DV), F32)
    _, s1, _ = _gdn_call(qkv_m[None], z_m[None], ba_m[None], zero_conv, zero_s, cw, alog, dtb, ng,
                         N_META, N_META, "gdn_meta")
    conv1 = jnp.pad(qkv_m[N_META - (CONV_W - 1):], ((SUBLANES - (CONV_W - 1), 0), (0, 0)))[None]
    k_meta = jnp.pad(kcat_m, ((0, LANES - N_META), (0, 0)))
    vt_meta = jnp.pad(vt_m, ((0, 0), (0, LANES - N_META)))

    tp = SAMPLE_T_PAD
    xs = jnp.pad(x_sample, ((0, 0), (0, tp - Ts), (0, 0))).reshape(Bs * tp, D)
    tab_s = jnp.tile(_sample_tables(past_len + jnp.arange(tp)), (Bs, 1))
    sw = [CONV_CH, GDN_V_W, LANES, kv_lora, MLA_D_ROPE, H * kv_lora, H * MLA_D_ROPE]
    sd = [F32, F32, F32, F32, F32, BF16, BF16]
    qkv_s, z_s, ba_s, lat_s, kpe_s, qlat_s, qpe_s = _proj_call(
        proj_sample, xs, tab_s, sample_consts, sw, sd, Bs * tp, "proj_sample")
    s3 = lambda a: a.reshape(Bs, tp, a.shape[-1])
    dattn = _dattn_parts(page_table, qlat_s.reshape(Bs, tp * H, kv_lora), qpe_s.reshape(Bs, tp * H, MLA_D_ROPE),
                         s3(lat_s), s3(kpe_s), cache_latent, jnp.swapaxes(cache_krope, 2, 3), Ts,
                         DATTN_GROUP, DATTN_SLOTS, DATTN_SUB)

    tab_p = _prompt_tables(N_META + jnp.arange(T))
    xp2d = x_prompt.reshape(B * T, D)
    qkv_p, z_p, ba_p, lat_p, kpe_p, qcat_p, kcat_p, vt_p = _proj_call(
        functools.partial(proj_prompt, n_split=PROJ_SPLIT), xp2d, tab_p, prompt_consts, pw, pd, PROJ_TM,
        "proj_prompt", transposed_outs=(7,))
    r3 = lambda a: a.reshape(B, T, a.shape[-1])
    qkv_p3 = r3(qkv_p)
    og_p, s2, out_lat = _gdn_call(qkv_p3, r3(z_p), r3(ba_p), conv1, s1, cw, alog, dtb, ng, GDN_CHUNK, T,
                                  "gdn_prompt", nb_intra=GDN_NB_INTRA, nb_scan=GDN_NB_SCAN, dattn=dattn)
    om_p = _attn_call(r3(qcat_p), r3(kcat_p), vt_p, k_meta, vt_meta, ATTN_TQ, ATTN_HPS, ATTN_KV_UNROLL)
    post_consts = (wo1, wo2, g2, wup, wdn, gf)
    y_p = _post_call(xp2d, og_p.reshape(B * T, -1), om_p.reshape(B * T, -1), post_consts, POST_TM, False, "post_prompt")

    qkv_s3 = s3(qkv_s)
    conv_in_s = jnp.pad(state_conv[0], ((0, 0), (SUBLANES - (CONV_W - 1), 0), (0, 0)))
    og_s, s_new, _ = _gdn_call(qkv_s3, s3(z_s), s3(ba_s), conv_in_s, state_ssm[0], cw, alog, dtb, ng, tp, Ts,
                               "gdn_sample", nb_intra=GDN_NB_SAMPLE, nb_scan=GDN_NB_SAMPLE)
    sample_post_consts = (wvbd,) + post_consts
    y_s = _post_call(xs, og_s.reshape(Bs * tp, -1), out_lat.reshape(Bs * tp, H * kv_lora),
                     sample_post_consts, Bs * tp, True, "post_sample")

    bc = lambda a: jnp.broadcast_to(a[None], (B,) + a.shape)
    lat_po = jnp.concatenate([bc(lat_m), r3(lat_p)], axis=1)[None]
    kpe_po = jnp.concatenate([bc(kpe_m), r3(kpe_p)], axis=1)[None]
    conv_po = qkv_p3[:, T - (CONV_W - 1):][None]
    conv_so = qkv_s3[:, Ts - (CONV_W - 1):Ts][None]
    return (y_p.reshape(B, T, D), y_s.reshape(Bs, tp, D)[:, :Ts],
            lat_po, kpe_po, conv_po, s2[None],
            s3(lat_s)[:, :Ts][None], s3(kpe_s)[:, :Ts][None], conv_so, s_new[None])
```

```python
import functools
import math

import jax
import jax.numpy as jnp
from jax import lax
from jax.experimental import pallas as pl
from jax.experimental.pallas import tpu as pltpu

F32 = jnp.float32
BF16 = jnp.bfloat16

N_META = 16
EPS = 1e-6
GDN_HEADS = 8
GDN_DK = 64
GDN_DV = 64
CONV_W = 4
GDN_CHUNK = 64
MLA_HEADS = 8
MLA_D_NOPE = 64
MLA_D_ROPE = 32
MLA_D_V = 64
ROPE_THETA = 10000.0
MLA_SCALE = (MLA_D_NOPE + MLA_D_ROPE) ** -0.5
Q_SCALE_LOG2 = MLA_SCALE * math.log2(math.e)
GDN_QK_W = GDN_HEADS * GDN_DK
GDN_V_W = GDN_HEADS * GDN_DV
CONV_CH = 2 * GDN_QK_W + GDN_V_W

LANES = 128
SUBLANES = 8
HEAD_PAD = 128
SAMPLE_T_PAD = 8
KPE_LANE0 = 32
VMEM_LIMIT = 56 * 1024 * 1024

PROJ_TM = 512
PROJ_SPLIT = 2
POST_TM = 512
ATTN_TQ = 256
ATTN_HPS = 8
ATTN_KV_UNROLL = 2
GDN_NB_INTRA = 8
GDN_NB_SCAN = 8
GDN_NB_SAMPLE = 8
DATTN_GROUP = 32
DATTN_SLOTS = 4
DATTN_SUB = 4


def _dot(a, b):
    return jnp.dot(a, b, preferred_element_type=F32)


def _dot_nt(a, b):
    return lax.dot_general(a, b, (((1,), (1,)), ((), ())), preferred_element_type=F32)


def _rms(x, g):
    return x * lax.rsqrt(jnp.mean(x * x, axis=-1, keepdims=True) + EPS) * g


def _sigmoid(x):
    return 1.0 / (1.0 + jnp.exp(-x))


def _const_spec(shape):
    nd = len(shape)
    return pl.BlockSpec(shape, lambda *_: (0,) * nd)


def _proj_common(x_ref, gmix_ref, win_ref, qg_ref, kvg_ref, kcos, ksin, qkv_ref, z_ref, ba_ref, lat_ref, kpe_ref,
                 *, q_lora, kv_lora):
    hn = _rms(x_ref[...], gmix_ref[...]).astype(BF16)
    p = _dot(hn, win_ref[...])
    c0 = CONV_CH + GDN_V_W
    c1 = c0 + q_lora
    c2 = c1 + LANES
    c3 = c2 + kv_lora
    qkv_ref[...] = p[:, :CONV_CH]
    z_ref[...] = p[:, CONV_CH:c0]
    blk_a = p[:, c1:c2]
    ba_ref[...] = blk_a
    kpe = blk_a * kcos + p[:, c3:c3 + LANES] * ksin
    kpe_ref[...] = kpe[:, KPE_LANE0:KPE_LANE0 + MLA_D_ROPE]
    lat = _rms(p[:, c2:c3], kvg_ref[...])
    lat_ref[...] = lat
    cqn = _rms(p[:, c0:c1], qg_ref[...]).astype(BF16)
    return cqn, lat.astype(BF16), kpe


def _proj_prompt_kernel(x_ref, tab_ref, gmix_ref, win_ref, qg_ref, kvg_ref, wq_ref, wkb_ref, wvb_ref, vones_ref,
                        qkv_ref, z_ref, ba_ref, lat_ref, kpe_ref, qcat_ref, kcat_ref, vt_ref,
                        *, q_lora, kv_lora, n_split=1):
    rows = x_ref.shape[0] // n_split
    for s in range(n_split):
        rs = pl.ds(s * rows, rows)
        _proj_prompt_rows(x_ref.at[rs], tab_ref.at[rs], gmix_ref, win_ref, qg_ref, kvg_ref, wq_ref, wkb_ref,
                          wvb_ref, vones_ref, qkv_ref.at[rs], z_ref.at[rs], ba_ref.at[rs], lat_ref.at[rs],
                          kpe_ref.at[rs], qcat_ref.at[rs], kcat_ref.at[rs], vt_ref.at[:, rs],
                          q_lora=q_lora, kv_lora=kv_lora)


def _proj_prompt_rows(x_ref, tab_ref, gmix_ref, win_ref, qg_ref, kvg_ref, wq_ref, wkb_ref, wvb_ref, vones_ref,
                      qkv_ref, z_ref, ba_ref, lat_ref, kpe_ref, qcat_ref, kcat_ref, vt_ref, *, q_lora, kv_lora):
    qcos = tab_ref[:, 0 * LANES:1 * LANES]
    qsin_lo = tab_ref[:, 1 * LANES:2 * LANES]
    qsin_hi = tab_ref[:, 2 * LANES:3 * LANES]
    kcos = tab_ref[:, 3 * LANES:4 * LANES]
    ksin = tab_ref[:, 4 * LANES:5 * LANES]
    cqn, latb, kpe = _proj_common(x_ref, gmix_ref, win_ref, qg_ref, kvg_ref, kcos, ksin,
                                  qkv_ref, z_ref, ba_ref, lat_ref, kpe_ref, q_lora=q_lora, kv_lora=kv_lora)
    qa = _dot(cqn, wq_ref[...])
    knope = _dot(latb, wkb_ref[...])
    half = MLA_D_ROPE // 2
    kpe_slot = pltpu.roll(kpe, MLA_D_NOPE - KPE_LANE0, axis=1)
    for h in range(MLA_HEADS):
        sl = slice(h * HEAD_PAD, (h + 1) * HEAD_PAD)
        qs = qa[:, sl]
        q_dn = pltpu.roll(qs, HEAD_PAD - half, axis=1)
        q_up = pltpu.roll(qs, half, axis=1)
        qcat_ref[:, sl] = (qs * qcos + q_dn * qsin_lo + q_up * qsin_hi).astype(BF16)
        kcat_ref[:, sl] = (knope[:, sl] + kpe_slot).astype(BF16)
    vt_ref[...] = (_dot_nt(wvb_ref[...], latb) + vones_ref[...]).astype(BF16)


def _proj_sample_kernel(x_ref, tab_ref, gmix_ref, win_ref, qg_ref, kvg_ref, wqn_ref, wqpe_ref, wqpesw_ref, wabs_ref,
                        qkv_ref, z_ref, ba_ref, lat_ref, kpe_ref, qlat_ref, qpe_ref, *, q_lora, kv_lora):
    pe_w = MLA_HEADS * MLA_D_ROPE
    qcos = tab_ref[:, 0:pe_w]
    qsin = tab_ref[:, pe_w:2 * pe_w]
    kcos = tab_ref[:, 2 * pe_w:2 * pe_w + LANES]
    ksin = tab_ref[:, 2 * pe_w + LANES:2 * pe_w + 2 * LANES]
    cqn, _, _ = _proj_common(x_ref, gmix_ref, win_ref, qg_ref, kvg_ref, kcos, ksin,
                             qkv_ref, z_ref, ba_ref, lat_ref, kpe_ref, q_lora=q_lora, kv_lora=kv_lora)
    qn = _dot(cqn, wqn_ref[...]).astype(BF16)
    qlat_ref[...] = (_dot(qn, wabs_ref[...]) * Q_SCALE_LOG2).astype(BF16)
    qpe_ref[...] = (_dot(cqn, wqpe_ref[...]) * qcos + _dot(cqn, wqpesw_ref[...]) * qsin).astype(BF16)


def _proj_call(kernel_fn, x2d, tab, consts, out_widths, out_dtypes, tm, name, transposed_outs=()):
    n_rows, d_model = x2d.shape
    assert n_rows % tm == 0
    row_spec = lambda w: pl.BlockSpec((tm, w), lambda i: (i, 0))
    col_spec = lambda w: pl.BlockSpec((w, tm), lambda i: (0, i))
    is_t = [k in transposed_outs for k in range(len(out_widths))]
    assert tab.shape[0] % tm == 0 and n_rows % tab.shape[0] == 0
    tab_blocks = tab.shape[0] // tm
    tab_spec = pl.BlockSpec((tm, tab.shape[1]), lambda i: (i % tab_blocks, 0))
    in_specs = [row_spec(d_model), tab_spec] + [_const_spec(c.shape) for c in consts]
    return pl.pallas_call(
        kernel_fn,
        grid=(n_rows // tm,),
        in_specs=in_specs,
        out_specs=[col_spec(w) if t else row_spec(w) for w, t in zip(out_widths, is_t)],
        out_shape=[jax.ShapeDtypeStruct((w, n_rows) if t else (n_rows, w), dt)
                   for w, dt, t in zip(out_widths, out_dtypes, is_t)],
        compiler_params=pltpu.CompilerParams(dimension_semantics=("arbitrary",),
                                             vmem_limit_bytes=VMEM_LIMIT),
        name=name,
    )(x2d, tab, *consts)


def _block_diag2(x, half):
    lane = lax.broadcasted_iota(jnp.int32, x.shape, 1)
    zero = jnp.zeros_like(x)
    return jnp.concatenate([jnp.where(lane < half, x, zero), jnp.where(lane < half, zero, x)], axis=0)


def _gdn_intra_prep(c, bb, cinit_bb, qkv_ref, prev_ref, ba_ref, cinit_ref, cw_ref, alog_ref, dtb_ref,
                    qk_mask, kg_ref, qg_ref, kdt_ref, vb_ref, egl_ref, xbuf, *, chunk, valid_len, n_chunks):
    C = chunk
    H = GDN_HEADS
    tail0 = SUBLANES - (CONV_W - 1)

    xbuf[bb, 0:SUBLANES, :] = jnp.where(c == 0, cinit_ref[cinit_bb], prev_ref[bb])
    xbuf[bb, SUBLANES:SUBLANES + C, :] = qkv_ref[bb]
    cw = cw_ref[...]
    y = xbuf[bb, tail0:tail0 + C, :] * cw[0:1, :]
    for j in range(1, CONV_W):
        y = y + xbuf[bb, tail0 + j:tail0 + j + C, :] * cw[j:j + 1, :]
    y = y * _sigmoid(y)

    ba = ba_ref[bb]
    beta_all = _sigmoid(ba)
    sp_in = ba + dtb_ref[...]
    softplus = jnp.maximum(sp_in, 0.0) + jnp.log1p(jnp.exp(-jnp.abs(sp_in)))
    g_all = -jnp.exp(alog_ref[...]) * softplus
    if valid_len < n_chunks * C:
        row = c * C + lax.broadcasted_iota(jnp.int32, (C, LANES), 0)
        beta_all = jnp.where(row < valid_len, beta_all, 0.0)
        g_all = jnp.where(row < valid_len, g_all, 0.0)

    ri = lax.broadcasted_iota(jnp.int32, (C, C), 0)
    ci = lax.broadcasted_iota(jnp.int32, (C, C), 1)
    lower = ri >= ci
    gc_all = lax.dot_general(lower.astype(F32), g_all, (((1,), (0,)), ((), ())),
                             precision=lax.Precision.HIGHEST, preferred_element_type=F32)
    sel = (lax.broadcasted_iota(jnp.int32, (SUBLANES, LANES), 1)
           == lax.broadcasted_iota(jnp.int32, (SUBLANES, LANES), 0) + GDN_HEADS).astype(F32)
    gc_t = lax.dot_general(sel, gc_all, (((1,), (1,)), ((), ())),
                           precision=lax.Precision.HIGHEST, preferred_element_type=F32)

    pairs = range(H // 2)
    lane = lax.broadcasted_iota(jnp.int32, (C, LANES), 1)
    lo = lane < GDN_DK
    lo_c, lower_p = qk_mask
    ii_r = lax.broadcasted_iota(jnp.int32, (GDN_DK, LANES), 0)
    ii_c = lax.broadcasted_iota(jnp.int32, (GDN_DK, LANES), 1)
    eye2 = ((ii_c == ii_r) | (ii_c == ii_r + GDN_DK)).astype(BF16)

    def col(x, idx):
        return jnp.sum(jnp.where(lane == idx, x, 0.0), axis=-1, keepdims=True)

    def half_sums(x):
        return jnp.where(lo, jnp.sum(jnp.where(lo, x, 0.0), axis=-1, keepdims=True),
                         jnp.sum(jnp.where(lo, 0.0, x), axis=-1, keepdims=True))

    kn_b, kb_b, qn_b, decay = [], [], [], []
    for j in pairs:
        ps = slice(j * LANES, (j + 1) * LANES)
        qp = y[:, j * LANES:(j + 1) * LANES]
        kp = y[:, GDN_QK_W + j * LANES:GDN_QK_W + (j + 1) * LANES]
        vp = y[:, 2 * GDN_QK_W + j * LANES:2 * GDN_QK_W + (j + 1) * LANES]
        qn = qp * lax.rsqrt(half_sums(qp * qp) + EPS) * (GDN_DK ** -0.5)
        kn = kp * lax.rsqrt(half_sums(kp * kp) + EPS)
        beta = jnp.where(lo, col(beta_all, 2 * j), col(beta_all, 2 * j + 1))
        gc0, gc1 = col(gc_all, H + 2 * j), col(gc_all, H + 2 * j + 1)
        gcol = jnp.where(lo, gc0, gc1)
        grow = jnp.concatenate([gc_t[2 * j:2 * j + 1, :], gc_t[2 * j + 1:2 * j + 2, :]], axis=1)
        diff = jnp.where(lo_c, gc0, gc1) - grow
        decay.append(jnp.where(lower_p, jnp.exp(jnp.where(lower_p, diff, 0.0)), 0.0))
        egc = jnp.exp(gcol)
        gl = gcol[C - 1:C, :]
        egl_ref[bb, 0, :, ps] = jnp.exp(gl)
        kb = kn * beta
        vb_ref[bb, 0, :, ps] = vp * beta
        kg_ref[bb, 0, :, ps] = (kb * egc).astype(BF16)
        qg_ref[bb, 0, :, ps] = (qn * egc).astype(BF16)
        k_dec = (kn * jnp.exp(gl - gcol)).astype(BF16)
        kdt_ref[bb, 0, j] = _dot_nt(eye2, _block_diag2(k_dec, GDN_DK)).astype(BF16)
        kn_b.append(_block_diag2(kn.astype(BF16), GDN_DK))
        kb_b.append(kb.astype(BF16))
        qn_b.append(qn.astype(BF16))
    return kn_b, kb_b, qn_b, decay


def _gdn_intra_kernel(*refs, **static):
    _gdn_intra_body(pl.program_id(1), *refs, **static)


def _gdn_intra_body(c, qkv_ref, prev_ref, ba_ref, cinit_ref, cw_ref, alog_ref, dtb_ref,
                    t_ref, qk_ref, kg_ref, qg_ref, kdt_ref, vb_ref, egl_ref, xbuf,
                    *, chunk, valid_len, n_chunks, nb, shared_cinit, after_prep=None):
    C = chunk
    rp = lax.broadcasted_iota(jnp.int32, (C, 2 * C), 0)
    cp = lax.broadcasted_iota(jnp.int32, (C, 2 * C), 1)
    lo_c = cp < C
    cmod = jnp.where(lo_c, cp, cp - C)
    lower_p, strict_p = rp >= cmod, rp > cmod
    eye_p = (rp == cmod).astype(F32)
    kn_b, kb_b, qn_b, decay = [], [], [], []
    for bb in range(nb):
        parts = _gdn_intra_prep(c, bb, 0 if shared_cinit else bb, qkv_ref, prev_ref, ba_ref, cinit_ref, cw_ref,
                                alog_ref, dtb_ref, (lo_c, lower_p), kg_ref, qg_ref, kdt_ref, vb_ref, egl_ref, xbuf,
                                chunk=chunk, valid_len=valid_len, n_chunks=n_chunks)
        for dst, src in zip((kn_b, kb_b, qn_b, decay), parts):
            dst.extend(src)
        if after_prep is not None:
            after_prep(bb)
    n_pairs = GDN_HEADS // 2
    probs = range(nb * n_pairs)
    kk = [_dot_nt(kb_b[i], kn_b[i]) for i in probs]
    qk = [_dot_nt(qn_b[i], kn_b[i]) for i in probs]
    for i in probs:
        qk_ref[i // n_pairs, 0, i % n_pairs] = jnp.where(lower_p, qk[i] * decay[i], 0.0).astype(BF16)
    p = [jnp.where(strict_p, -(kk[i] * decay[i]), 0.0) for i in probs]
    t = [eye_p + p[i] for i in probs]
    n_iter = max(1, int(math.ceil(math.log2(C))))
    pb = [p[i].astype(BF16) for i in probs]
    pbd = [_block_diag2(pb[i], C) for i in probs]
    for _ in range(1, n_iter):
        p = [_dot(pb[i], pbd[i]) for i in probs]
        pb = [p[i].astype(BF16) for i in probs]
        pbd = [_block_diag2(pb[i], C) for i in probs]
        t = [t[i] + _dot(t[i].astype(BF16), pbd[i]) for i in probs]
    for i in probs:
        t_ref[i // n_pairs, 0, i % n_pairs] = t[i].astype(BF16)


def _gdn_scan_kernel(t_ref, qk_ref, kg_ref, qg_ref, kdt_ref, vb_ref, egl_ref, z_ref, sinit_ref, ng_ref,
                     o_ref, sfin_ref, s_scr, *, chunk, n_chunks, nb, shared_sinit):
    c = pl.program_id(1)
    C = chunk
    n_pairs = GDN_HEADS // 2
    probs = [(bb, j) for bb in range(nb) for j in range(n_pairs)]
    ps = [slice(j * LANES, (j + 1) * LANES) for j in range(n_pairs)]
    lo = lax.broadcasted_iota(jnp.int32, (C, LANES), 1) < GDN_DV

    @pl.when(c == 0)
    def _():
        for bb, j in probs:
            si = 0 if shared_sinit else bb
            s_scr[bb, j] = jnp.concatenate([sinit_ref[si, 2 * j], sinit_ref[si, 2 * j + 1]], axis=1)

    s = [s_scr[bb, j] for bb, j in probs]
    sbd = [_block_diag2(x.astype(BF16), GDN_DV) for x in s]
    ks = [_dot(kg_ref[bb, 0, :, ps[j]], sbd[i]) for i, (bb, j) in enumerate(probs)]
    qs = [_dot(qg_ref[bb, 0, :, ps[j]], sbd[i]) for i, (bb, j) in enumerate(probs)]
    r = [(vb_ref[bb, 0, :, ps[j]] - ks[i]).astype(BF16) for i, (bb, j) in enumerate(probs)]
    v_new = [_dot(t_ref[bb, 0, j], _block_diag2(r[i], GDN_DV)).astype(BF16) for i, (bb, j) in enumerate(probs)]
    vbd = [_block_diag2(x, GDN_DV) for x in v_new]
    o = [qs[i] + _dot(qk_ref[bb, 0, j], vbd[i]) for i, (bb, j) in enumerate(probs)]
    for i, (bb, j) in enumerate(probs):
        s_scr[bb, j] = s[i] * egl_ref[bb, 0, :, ps[j]] + _dot(kdt_ref[bb, 0, j], vbd[i])
    ng = ng_ref[...]
    for i, (bb, j) in enumerate(probs):
        o2 = o[i] * o[i]
        ms = jnp.where(lo, jnp.sum(jnp.where(lo, o2, 0.0), axis=-1, keepdims=True),
                       jnp.sum(jnp.where(lo, 0.0, o2), axis=-1, keepdims=True)) * (1.0 / GDN_DV)
        zp = z_ref[bb, :, ps[j]]
        o_ref[bb, :, ps[j]] = (o[i] * lax.rsqrt(ms + EPS) * ng * (zp * _sigmoid(zp))).astype(o_ref.dtype)

    @pl.when(c == n_chunks - 1)
    def _():
        for bb, j in probs:
            s_pair = s_scr[bb, j]
            sfin_ref[bb, 2 * j] = s_pair[:, :GDN_DV]
            sfin_ref[bb, 2 * j + 1] = s_pair[:, GDN_DV:]


def _gdn_intra_dattn_kernel(pt_ref, *refs, intra_static, dattn_static, n_i_in, n_d_in, n_i_out, n_i_scr):
    i = pl.program_id(0)
    i_in, d_in = refs[:n_i_in], refs[n_i_in:n_i_in + n_d_in]
    o0 = n_i_in + n_d_in
    i_out, d_out = refs[o0:o0 + n_i_out], refs[o0 + n_i_out]
    s0 = o0 + n_i_out + 1
    i_scr, d_scr = refs[s0:s0 + n_i_scr], refs[s0 + n_i_scr:]
    n_steps = pl.num_programs(0)
    group, n_groups = dattn_static["group"], dattn_static["n_groups"]
    nb = intra_static["nb"]
    per_prep = group * n_groups // nb
    assert group % per_prep == 0 and per_prep % 2 == 0 and n_groups <= dattn_static["n_slots"]
    copies = _dattn_page_copies(pt_ref, d_in[4], d_in[5], d_scr[0], d_scr[1], d_scr[2], d_scr[3],
                                group=group, page=dattn_static["page"])

    def start_pages(seq, first, count):
        g = first // group
        _dattn_start_pages(copies, seq, g, g, first % group, count)

    @pl.when(i == 0)
    def _():
        start_pages(0, 0, group)
        for g in range(1, n_groups):
            start_pages(0, g * group, group)

    _dattn_body(i, n_steps, pt_ref, *d_in, d_out, *d_scr, issue_inline=False, **dattn_static)

    def after_prep(bb):
        @pl.when(i + 1 < n_steps)
        def _():
            start_pages(i + 1, bb * per_prep, per_prep)

    _gdn_intra_body(i, *i_in, *i_out, *i_scr, after_prep=after_prep, **intra_static)


def _gdn_call(qkv, z, ba, conv_init, s_init, cw, alog, dtb, ng, chunk, valid_len, name, nb_intra=1, nb_scan=1,
              dattn=None):
    B, T, _ = qkv.shape
    assert T % chunk == 0 and chunk % SUBLANES == 0 and GDN_DK == GDN_DV and 2 * GDN_DV == LANES
    n_chunks = T // chunk
    H, C = GDN_HEADS, chunk
    HP = H // 2
    shared_c = conv_init.shape[0] == 1
    shared_s = s_init.shape[0] == 1
    ng = jnp.tile(ng, (1, 2))
    params = pltpu.CompilerParams(dimension_semantics=("arbitrary", "arbitrary"), vmem_limit_bytes=VMEM_LIMIT)
    inter_dims = [(HP, C, 2 * C), (HP, C, 2 * C), (C, GDN_QK_W), (C, GDN_QK_W), (HP, GDN_DK, 2 * C),
                  (C, GDN_V_W), (1, GDN_V_W)]
    inter_dtypes = [BF16, BF16, BF16, BF16, BF16, F32, F32]

    def specs(nb, wrap=lambda f: f):
        tok = lambda w: pl.BlockSpec((nb, chunk, w), wrap(lambda b, c: (b, c, 0)))
        per_chunk = lambda *dims: pl.BlockSpec((nb, 1) + dims, wrap(lambda b, c: (b, c) + (0,) * len(dims)))
        per_seq = lambda shared, *dims: pl.BlockSpec(
            ((1 if shared else nb),) + dims, wrap(lambda b, c: ((0 if shared else b),) + (0,) * len(dims)))
        prev_rows = pl.BlockSpec((nb, SUBLANES, CONV_CH),
                                 wrap(lambda b, c: (b, jnp.maximum(c * (C // SUBLANES) - 1, 0), 0)))
        return tok, per_chunk, per_seq, prev_rows

    nb = nb_intra
    assert B % nb == 0
    intra_static = dict(chunk=chunk, valid_len=valid_len, n_chunks=n_chunks, nb=nb, shared_cinit=shared_c)
    intra_operands = (qkv, qkv, ba, conv_init, cw, alog, dtb)
    intra_out_shape = [jax.ShapeDtypeStruct((B, n_chunks) + d, dt) for d, dt in zip(inter_dims, inter_dtypes)]
    intra_scratch = [pltpu.VMEM((nb, SUBLANES + chunk, CONV_CH), F32)]

    def intra_specs(wrap=lambda f: f):
        tok, per_chunk, per_seq, prev_rows = specs(nb, wrap)
        in_specs = [tok(CONV_CH), prev_rows, tok(LANES), per_seq(shared_c, SUBLANES, CONV_CH),
                    _const_spec(cw.shape), _const_spec(alog.shape), _const_spec(dtb.shape)]
        return in_specs, [per_chunk(*d) for d in inter_dims]

    dattn_out = None
    ds = dattn["static"] if dattn is not None else None
    fusable = (dattn is not None and B // nb == 1 and dattn["n_seq"] == n_chunks
               and ds["n_groups"] <= ds["n_slots"] and (ds["group"] * ds["n_groups"]) % nb == 0
               and ds["group"] % (ds["group"] * ds["n_groups"] // nb) == 0
               and (ds["group"] * ds["n_groups"] // nb) % 2 == 0)
    if fusable:
        in_specs, out_specs = intra_specs(lambda f: (lambda i, pt: f(0, i)))
        fused = pl.pallas_call(
            functools.partial(_gdn_intra_dattn_kernel, intra_static=intra_static, dattn_static=dattn["static"],
                              n_i_in=len(in_specs), n_d_in=len(dattn["in_specs"]), n_i_out=len(out_specs),
                              n_i_scr=len(intra_scratch)),
            grid_spec=pltpu.PrefetchScalarGridSpec(
                num_scalar_prefetch=1,
                grid=(n_chunks,),
                in_specs=in_specs + dattn["in_specs"],
                out_specs=out_specs + [dattn["out_spec"]],
                scratch_shapes=intra_scratch + dattn["scratch"],
            ),
            out_shape=intra_out_shape + [dattn["out_shape"]],
            compiler_params=pltpu.CompilerParams(dimension_semantics=("arbitrary",), vmem_limit_bytes=VMEM_LIMIT),
            name=name + "_intra_dattn",
        )(dattn["page_table"], *intra_operands, *dattn["operands"])
        inter, dattn_out = fused[:-1], fused[-1]
    else:
        in_specs, out_specs = intra_specs()
        inter = pl.pallas_call(
            functools.partial(_gdn_intra_kernel, **intra_static),
            grid=(B // nb, n_chunks),
            in_specs=in_specs,
            out_specs=out_specs,
            out_shape=intra_out_shape,
            scratch_shapes=intra_scratch,
            compiler_params=params,
            name=name + "_intra",
        )(*intra_operands)
        if dattn is not None:
            dattn_out = _dattn_run(dattn)

    nb = nb_scan
    assert B % nb == 0
    tok, per_chunk, per_seq, _ = specs(nb)
    o, s_fin = pl.pallas_call(
        functools.partial(_gdn_scan_kernel, chunk=chunk, n_chunks=n_chunks, nb=nb, shared_sinit=shared_s),
        grid=(B // nb, n_chunks),
        in_specs=[per_chunk(*d) for d in inter_dims] + [
            tok(GDN_V_W), per_seq(shared_s, H, GDN_DK, GDN_DV), _const_spec(ng.shape),
        ],
        out_specs=[tok(GDN_V_W), per_seq(False, H, GDN_DK, GDN_DV)],
        out_shape=[
            jax.ShapeDtypeStruct((B, T, GDN_V_W), BF16),
            jax.ShapeDtypeStruct((B, H, GDN_DK, GDN_DV), F32),
        ],
        scratch_shapes=[pltpu.VMEM((nb, HP, GDN_DK, LANES), F32)],
        compiler_params=params,
        name=name + "_scan",
    )(*inter, z, s_init, ng)
    return o, s_fin, dattn_out


def _denominator_row(h):
    return MLA_D_V * (1 - h % 2)


def _attn_kernel(q_ref, k_ref, vt_ref, km_ref, vmt_ref, o_ref, m_scr, acc_scr, *, tq, n_meta, hps, kv_unroll):
    qi = pl.program_id(2)
    neg = -jnp.inf
    key_m = lax.broadcasted_iota(jnp.int32, (km_ref.shape[0], tq), 0)
    key_d = lax.broadcasted_iota(jnp.int32, (tq, tq), 0)
    qry_d = lax.broadcasted_iota(jnp.int32, (tq, tq), 1)
    heads = range(hps)
    ls = [slice(h * HEAD_PAD, (h + 1) * HEAD_PAD) for h in heads]

    s = [jnp.where(key_m < n_meta, _dot_nt(km_ref[:, ls[h]], q_ref[0, :, ls[h]]), neg) for h in heads]
    m = [jnp.max(s[h], axis=0, keepdims=True) for h in heads]
    p = [jnp.exp2(s[h] - m[h]) for h in heads]
    for h in heads:
        m_scr[h] = m[h]
        acc_scr[h] = _dot(vmt_ref[ls[h], :], p[h].astype(BF16))

    def blocks(specs):
        chains = [(start, diag, h) for start, diag in specs for h in heads]
        s = [_dot_nt(k_ref[0, pl.ds(start, tq), ls[h]], q_ref[0, :, ls[h]]) for start, _, h in chains]
        s = [jnp.where(key_d <= qry_d, x, neg) if diag else x for x, (_, diag, _) in zip(s, chains)]
        mb = [jnp.max(x, axis=0, keepdims=True) for x in s]
        p = [jnp.exp2(x - m) for x, m in zip(s, mb)]
        pv = [_dot(vt_ref[ls[h], pl.ds(start, tq)], x.astype(BF16)) for x, (start, _, h) in zip(p, chains)]
        for h in heads:
            mine = [i for i, c in enumerate(chains) if c[2] == h]
            m_old = m_scr[h]
            m_new = m_old
            for i in mine:
                m_new = jnp.maximum(m_new, mb[i])
            acc = jnp.exp2(m_old - m_new) * acc_scr[h]
            for i in mine:
                acc = acc + jnp.exp2(mb[i] - m_new) * pv[i]
            acc_scr[h] = acc
            m_scr[h] = m_new

    def start_of(kb):
        return pl.multiple_of(kb * tq, tq)

    def body(kp, carry):
        blocks([(start_of(kv_unroll * kp + u), False) for u in range(kv_unroll)])
        return carry

    lax.fori_loop(0, qi // kv_unroll, body, 0)
    for rem in range(kv_unroll):
        @pl.when(qi % kv_unroll == rem)
        def _(rem=rem):
            blocks([(start_of(qi - rem + u), False) for u in range(rem)] + [(start_of(qi), True)])
    slot_row = lax.broadcasted_iota(jnp.int32, (HEAD_PAD, tq), 0)

    def normalised(h):
        acc = acc_scr[h]
        r = _denominator_row(h)
        return jnp.where(slot_row == r, 0.0, acc) / acc[r:r + 1, :]

    for j in range(hps // 2):
        pair = normalised(2 * j) + normalised(2 * j + 1)
        o_ref[0, :, j * LANES:(j + 1) * LANES] = pair.T.astype(o_ref.dtype)


def _attn_call(qcat, kcat, vt, k_meta, vt_meta, tq, hps, kv_unroll):
    B, T, _ = qcat.shape
    assert MLA_HEADS % hps == 0 and hps % 2 == 0
    n_groups = MLA_HEADS // hps
    gw = hps * HEAD_PAD
    ow = (hps // 2) * LANES
    kern = functools.partial(_attn_kernel, tq=tq, n_meta=N_META, hps=hps, kv_unroll=kv_unroll)
    return pl.pallas_call(
        kern,
        grid=(B, n_groups, T // tq),
        in_specs=[
            pl.BlockSpec((1, tq, gw), lambda b, g, qi: (b, qi, g)),
            pl.BlockSpec((1, T, gw), lambda b, g, qi: (b, 0, g)),
            pl.BlockSpec((gw, T), lambda b, g, qi: (g, b)),
            pl.BlockSpec((k_meta.shape[0], gw), lambda b, g, qi: (0, g)),
            pl.BlockSpec((gw, vt_meta.shape[1]), lambda b, g, qi: (g, 0)),
        ],
        out_specs=pl.BlockSpec((1, tq, ow), lambda b, g, qi: (b, qi, g)),
        out_shape=jax.ShapeDtypeStruct((B, T, n_groups * ow), BF16),
        scratch_shapes=[pltpu.VMEM((hps, 1, tq), F32), pltpu.VMEM((hps, HEAD_PAD, tq), F32)],
        compiler_params=pltpu.CompilerParams(dimension_semantics=("arbitrary", "arbitrary", "arbitrary"),
                                             vmem_limit_bytes=VMEM_LIMIT),
        name="attn",
    )(qcat, kcat, vt, k_meta, vt_meta)


def _dattn_kernel(*refs, **static):
    _dattn_body(pl.program_id(0), pl.num_programs(0), *refs, **static)


def _dattn_page_copies(pt_ref, lat_hbm, kpe_hbm, lat_buf, kpe_buf, sem_lat, sem_kpe, *, group, page):
    def copies(seq, g, k, slot):
        pid = pt_ref[seq, g * group + k]
        rows = pl.ds(pl.multiple_of(k * page, page), page)
        return (pltpu.make_async_copy(lat_hbm.at[0, pid], lat_buf.at[slot, rows, :], sem_lat.at[slot]),
                pltpu.make_async_copy(kpe_hbm.at[0, pid], kpe_buf.at[slot, k], sem_kpe.at[slot]))
    return copies


def _dattn_start_pages(copies, seq, g, slot, k0, count):
    def body(k2, carry):
        for prio in range(2):
            for cp in copies(seq, g, k0 + 2 * k2 + prio, slot):
                cp.start(priority=prio)
        return carry
    lax.fori_loop(0, count // 2, body, 0)


def _dattn_body(b, n_seq, pt_ref, qlat_ref, qpe_ref, latn_ref, kpen_ref, lat_hbm, kpe_hbm, o_ref,
                lat_buf, kpe_buf, sem_lat, sem_kpe, m_scr, l_scr, acc_scr,
                *, group, n_groups, n_slots, n_sub, t_new, page, issue_inline=True):
    neg = -jnp.inf
    page_copies = _dattn_page_copies(pt_ref, lat_hbm, kpe_hbm, lat_buf, kpe_buf, sem_lat, sem_kpe,
                                     group=group, page=page)

    def start_group(seq, g, slot):
        _dattn_start_pages(page_copies, seq, g, slot, 0, group)

    def wait_group(seq, g, slot):
        def body(k, carry):
            for cp in page_copies(seq, g, k, slot):
                cp.wait()
            return carry
        lax.fori_loop(0, group, body, 0)

    ahead = n_slots - 1
    assert issue_inline or n_groups <= n_slots

    if issue_inline:
        @pl.when(b == 0)
        def _():
            for g0 in range(ahead):
                start_group(0, g0, g0 % n_slots)

    m_scr[...] = jnp.full(m_scr.shape, neg, F32)
    l_scr[...] = jnp.zeros(l_scr.shape, F32)
    acc_scr[...] = jnp.zeros(acc_scr.shape, F32)

    ql = qlat_ref[0]
    qp = qpe_ref[0]

    def update(scores, vals):
        ms = [jnp.max(s, axis=-1, keepdims=True) for s in scores]
        ps = [jnp.exp2(s - m) for s, m in zip(scores, ms)]
        ls = [jnp.sum(p, axis=-1, keepdims=True) for p in ps]
        pvs = [_dot(p.astype(BF16), v) for p, v in zip(ps, vals)]
        m_old = m_scr[...]
        m_new = m_old
        for m in ms:
            m_new = jnp.maximum(m_new, m)
        alpha = jnp.exp2(m_old - m_new)
        l_new = alpha * l_scr[...]
        acc_new = alpha * acc_scr[...]
        for m, l, pv in zip(ms, ls, pvs):
            w = jnp.exp2(m - m_new)
            l_new = l_new + w * l
            acc_new = acc_new + w * pv
        l_scr[...] = l_new
        acc_scr[...] = acc_new
        m_scr[...] = m_new

    sub_pages = group // n_sub
    sub_keys = sub_pages * page
    for g in range(n_groups):
        slot = g % n_slots
        nxt = g + ahead
        if not issue_inline:
            pass
        elif nxt < n_groups:
            start_group(b, nxt, nxt % n_slots)
        else:
            @pl.when(b + 1 < n_seq)
            def _(nxt=nxt):
                start_group(b + 1, nxt - n_groups, nxt % n_slots)
        wait_group(b, g, slot)
        cbs = [lat_buf[slot, i * sub_keys:(i + 1) * sub_keys, :].astype(BF16) for i in range(n_sub)]
        kbts = [jnp.concatenate([kpe_buf[slot, k] for k in range(i * sub_pages, (i + 1) * sub_pages)],
                                axis=1).astype(BF16) for i in range(n_sub)]
        update([_dot_nt(ql, cb) + _dot(qp, kbt) for cb, kbt in zip(cbs, kbts)], cbs)

    pad = 2 * SUBLANES - latn_ref.shape[1]
    cn = jnp.concatenate([latn_ref[0], jnp.zeros((pad, latn_ref.shape[2]), F32)], axis=0).astype(BF16)
    kn = jnp.concatenate([kpen_ref[0], jnp.zeros((pad, kpen_ref.shape[2]), F32)], axis=0).astype(BF16)
    s = _dot_nt(ql, cn) + _dot_nt(qp, kn)
    tok = lax.shift_right_logical(lax.broadcasted_iota(jnp.int32, s.shape, 0), int(math.log2(MLA_HEADS)))
    key = lax.broadcasted_iota(jnp.int32, s.shape, 1)
    s = jnp.where((key <= tok) & (key < t_new), s, neg)
    update([s], [cn])
    o_ref[0] = (acc_scr[...] / l_scr[...]).astype(o_ref.dtype)


def _dattn_parts(page_table, qlat, qpe, lat_new, kpe_new, cache_latent, cache_krope_t, t_new, group, n_slots, n_sub):
    assert group % n_sub == 0 and group % 2 == 0
    Bs, R, kv_lora = qlat.shape
    n_pages = page_table.shape[1]
    assert n_pages % (n_slots * group) == 0, "a group must keep its slot number across sequences"
    n_groups = n_pages // group
    assert 2 <= n_slots <= n_groups
    page = cache_latent.shape[2]
    per_b = lambda shape: pl.BlockSpec((1,) + shape, lambda b, pt: (b, 0, 0))
    return dict(
        n_seq=Bs,
        static=dict(group=group, n_groups=n_groups, n_slots=n_slots, n_sub=n_sub, t_new=t_new, page=page),
        page_table=page_table,
        operands=(qlat, qpe, lat_new, kpe_new, cache_latent, cache_krope_t),
        in_specs=[per_b((R, kv_lora)), per_b((R, MLA_D_ROPE)),
                  per_b(lat_new.shape[1:]), per_b(kpe_new.shape[1:]),
                  pl.BlockSpec(memory_space=pl.ANY), pl.BlockSpec(memory_space=pl.ANY)],
        out_spec=per_b((R, kv_lora)),
        out_shape=jax.ShapeDtypeStruct((Bs, R, kv_lora), BF16),
        scratch=[pltpu.VMEM((n_slots, group * page, kv_lora), F32),
                 pltpu.VMEM((n_slots, group, MLA_D_ROPE, page), F32),
                 pltpu.SemaphoreType.DMA((n_slots,)), pltpu.SemaphoreType.DMA((n_slots,)),
                 pltpu.VMEM((R, 1), F32), pltpu.VMEM((R, 1), F32), pltpu.VMEM((R, kv_lora), F32)],
    )


def _dattn_run(parts):
    grid_spec = pltpu.PrefetchScalarGridSpec(
        num_scalar_prefetch=1,
        grid=(parts["n_seq"],),
        in_specs=parts["in_specs"],
        out_specs=parts["out_spec"],
        scratch_shapes=parts["scratch"],
    )
    return pl.pallas_call(
        functools.partial(_dattn_kernel, **parts["static"]),
        grid_spec=grid_spec,
        out_shape=parts["out_shape"],
        compiler_params=pltpu.CompilerParams(dimension_semantics=("arbitrary",),
                                             vmem_limit_bytes=VMEM_LIMIT),
        name="dattn",
    )(parts["page_table"], *parts["operands"])


def _post_kernel(*refs, absorbed_values):
    if absorbed_values:
        h_ref, og_ref, om_ref, wvbd_ref, wo1_ref, wo2_ref, g2_ref, wup_ref, wdn_ref, gf_ref, y_ref = refs
        om = _dot(om_ref[...], wvbd_ref[...]).astype(BF16)
    else:
        h_ref, og_ref, om_ref, wo1_ref, wo2_ref, g2_ref, wup_ref, wdn_ref, gf_ref, y_ref = refs
        om = om_ref[...]
    h2 = h_ref[...] + _dot(og_ref[...], wo1_ref[...]) + _dot(om, wo2_ref[...])
    hn = _rms(h2, g2_ref[...]).astype(BF16)
    u = jnp.maximum(_dot(hn, wup_ref[...]), 0.0)
    h3 = h2 + _dot((u * u).astype(BF16), wdn_ref[...])
    y_ref[...] = _rms(h3, gf_ref[...])


def _post_call(h2d, og, om, consts, tm, absorbed_values, name):
    n_rows, d_model = h2d.shape
    assert n_rows % tm == 0
    row_spec = lambda w: pl.BlockSpec((tm, w), lambda i: (i, 0))
    kern = functools.partial(_post_kernel, absorbed_values=absorbed_values)
    return pl.pallas_call(
        kern,
        grid=(n_rows // tm,),
        in_specs=[row_spec(d_model), row_spec(og.shape[1]), row_spec(om.shape[1])]
                 + [_const_spec(c.shape) for c in consts],
        out_specs=row_spec(d_model),
        out_shape=jax.ShapeDtypeStruct((n_rows, d_model), F32),
        compiler_params=pltpu.CompilerParams(dimension_semantics=("arbitrary",),
                                             vmem_limit_bytes=VMEM_LIMIT),
        name=name,
    )(h2d, og, om, *consts)


def _rope_parts(pos):
    half = MLA_D_ROPE // 2
    inv = ROPE_THETA ** (-jnp.arange(half, dtype=F32) / half)
    ang = pos.astype(F32)[:, None] * inv[None, :]
    cos = jnp.concatenate([jnp.cos(ang), jnp.cos(ang)], -1)
    sin_signed = jnp.concatenate([-jnp.sin(ang), jnp.sin(ang)], -1)
    return cos, sin_signed


def _pad_lanes(x, width):
    return jnp.pad(x, ((0, 0), (0, width - x.shape[1])))


def _at_lanes(x, lane0, width=LANES):
    return jnp.pad(x, ((0, 0), (lane0, width - lane0 - x.shape[1])))


def _key_tables(cos, sin_signed):
    return _at_lanes(cos, KPE_LANE0), _at_lanes(sin_signed, KPE_LANE0)


def _prompt_tables(pos):
    cos, sin_signed = _rope_parts(pos)
    n = pos.shape[0]
    half = MLA_D_ROPE // 2
    qcos = jnp.concatenate([jnp.full((n, MLA_D_NOPE), Q_SCALE_LOG2, F32), cos * Q_SCALE_LOG2], -1)
    qsin_lo = _at_lanes(sin_signed[:, :half] * Q_SCALE_LOG2, MLA_D_NOPE)
    qsin_hi = _at_lanes(sin_signed[:, half:] * Q_SCALE_LOG2, MLA_D_NOPE + half)
    kcos, ksin = _key_tables(cos, sin_signed)
    return jnp.concatenate([_pad_lanes(qcos, LANES), qsin_lo, qsin_hi, kcos, ksin], -1)


def _sample_tables(pos):
    cos, sin_signed = _rope_parts(pos)
    kcos, ksin = _key_tables(cos, sin_signed)
    return jnp.concatenate([jnp.tile(cos * Q_SCALE_LOG2, (1, MLA_HEADS)),
                            jnp.tile(sin_signed * Q_SCALE_LOG2, (1, MLA_HEADS)), kcos, ksin], -1)


def _swap_halves(w):
    half = MLA_D_ROPE // 2
    return jnp.concatenate([w[..., half:], w[..., :half]], -1)


def kernel(x_prompt, x_sample, cache_latent, cache_krope, state_conv, state_ssm, page_table,
           meta_tokens, norm_mix_g, w_in, conv_w, a_log, dt_bias, gdn_norm_g, q_norm_g, w_q_b,
           kv_norm_g, w_kv_b, w_out, norm_mlp_g, w_up, w_down, final_norm_g):
    assert w_in.shape[0] == 1, "single-layer problem"
    B, T, D = x_prompt.shape
    Bs, Ts, _ = x_sample.shape
    assert CONV_W - 1 <= Ts <= SAMPLE_T_PAD
    past_len = page_table.shape[1] * cache_latent.shape[2]
    q_lora = q_norm_g.shape[1]
    kv_lora = kv_norm_g.shape[1]
    H = MLA_HEADS

    wi = w_in[0]
    c0 = CONV_CH + GDN_V_W
    c1 = c0 + 2 * GDN_HEADS
    wkpe_raw = wi[:, c1 + q_lora + kv_lora:]
    blk_a = jnp.concatenate([_pad_lanes(wi[:, c0:c1], KPE_LANE0), _pad_lanes(wkpe_raw, LANES - KPE_LANE0)], 1)
    blk_b = _at_lanes(_swap_halves(wkpe_raw), KPE_LANE0)
    win = jnp.concatenate([wi[:, :c0], wi[:, c1:c1 + q_lora], blk_a,
                           wi[:, c1 + q_lora:c1 + q_lora + kv_lora], blk_b], axis=1).astype(BF16)
    gmix = norm_mix_g[0][None, :]
    qg = q_norm_g[0][None, :]
    kvg = kv_norm_g[0][None, :]
    wq = w_q_b[0]
    wq_nope, wq_pe = wq[..., :MLA_D_NOPE], wq[..., MLA_D_NOPE:]
    wq_pesw = _swap_halves(wq_pe)
    zq = lambda n: jnp.zeros((q_lora, H, n), F32)
    wqcat = jnp.concatenate([wq_nope, wq_pe, zq(HEAD_PAD - MLA_D_NOPE - MLA_D_ROPE)], -1)
    wqcat = wqcat.reshape(q_lora, H * HEAD_PAD).astype(BF16)
    wkv = w_kv_b[0]
    wkb, wvb = wkv[..., :MLA_D_NOPE], wkv[..., MLA_D_NOPE:]
    zk = lambda n: jnp.zeros((kv_lora, H, n), F32)
    wkb_sp = jnp.concatenate([wkb, zk(HEAD_PAD - MLA_D_NOPE)], -1).reshape(kv_lora, H * HEAD_PAD).astype(BF16)
    odd = (jnp.arange(H) % 2 == 1)[None, :, None]
    wvb_sp = jnp.where(odd, jnp.concatenate([zk(HEAD_PAD - MLA_D_V), wvb], -1),
                       jnp.concatenate([wvb, zk(HEAD_PAD - MLA_D_V)], -1))
    wvb_sp_t = wvb_sp.reshape(kv_lora, H * HEAD_PAD).T.astype(BF16)
    eye_h = jnp.eye(H, dtype=bool)
    wabs = jnp.where(eye_h[:, None, :, None], wkb.transpose(1, 2, 0)[:, :, None, :], 0.0)
    wabs = wabs.reshape(H * MLA_D_NOPE, H * kv_lora).astype(BF16)
    wvbd = jnp.where(eye_h[:, None, :, None], wvb.transpose(1, 0, 2)[:, :, None, :], 0.0)
    wvbd = wvbd.reshape(H * kv_lora, H * MLA_D_V).astype(BF16)
    wo1 = w_out[0][:GDN_V_W].astype(BF16)
    wo2 = w_out[0][GDN_V_W:].astype(BF16)
    g2 = norm_mlp_g[0][None, :]
    wup = w_up[0].astype(BF16)
    wdn = w_down[0].astype(BF16)
    gf = final_norm_g[None, :]
    cw = jnp.pad(conv_w[0], ((0, SUBLANES - CONV_W), (0, 0)))
    lane_row = lambda v: jnp.pad(v[None, :], ((0, 0), (GDN_HEADS, LANES - 2 * GDN_HEADS)))
    alog = lane_row(a_log[0])
    dtb = lane_row(dt_bias[0])
    ng = gdn_norm_g[0][None, :]

    common = (gmix, win, qg, kvg)
    slot_row = jnp.arange(H * HEAD_PAD) % HEAD_PAD
    vones = (slot_row == _denominator_row(jnp.arange(H * HEAD_PAD) // HEAD_PAD)).astype(F32)[:, None]
    prompt_consts = common + (wqcat, wkb_sp, wvb_sp_t, vones)
    proj_prompt = functools.partial(_proj_prompt_kernel, q_lora=q_lora, kv_lora=kv_lora)
    proj_sample = functools.partial(_proj_sample_kernel, q_lora=q_lora, kv_lora=kv_lora)
    sample_consts = common + (wq_nope.reshape(q_lora, -1).astype(BF16), wq_pe.reshape(q_lora, -1).astype(BF16),
                              wq_pesw.reshape(q_lora, -1).astype(BF16), wabs)
    pw = [CONV_CH, GDN_V_W, LANES, kv_lora, MLA_D_ROPE, H * HEAD_PAD, H * HEAD_PAD, H * HEAD_PAD]
    pd = [F32, F32, F32, F32, F32, BF16, BF16, BF16]

    tab_m = _prompt_tables(jnp.arange(N_META))
    qkv_m, z_m, ba_m, lat_m, kpe_m, _, kcat_m, vt_m = _proj_call(
        proj_prompt, meta_tokens, tab_m, prompt_consts, pw, pd, N_META, "proj_meta", transposed_outs=(7,))
    zero_conv = jnp.zeros((1, SUBLANES, CONV_CH), F32)
    zero_s = jnp.zeros((1, GDN_HEADS, GDN_DK, GDN_DV), F32)
    _, s1, _ = _gdn_call(qkv_m[None], z_m[None], ba_m[None], zero_conv, zero_s, cw, alog, dtb, ng,
                         N_META, N_META, "gdn_meta")
    conv1 = jnp.pad(qkv_m[N_META - (CONV_W - 1):], ((SUBLANES - (CONV_W - 1), 0), (0, 0)))[None]
    k_meta = jnp.pad(kcat_m, ((0, LANES - N_META), (0, 0)))
    vt_meta = jnp.pad(vt_m, ((0, 0), (0, LANES - N_META)))

    tp = SAMPLE_T_PAD
    xs = jnp.pad(x_sample, ((0, 0), (0, tp - Ts), (0, 0))).reshape(Bs * tp, D)
    tab_s = jnp.tile(_sample_tables(past_len + jnp.arange(tp)), (Bs, 1))
    sw = [CONV_CH, GDN_V_W, LANES, kv_lora, MLA_D_ROPE, H * kv_lora, H * MLA_D_ROPE]
    sd = [F32, F32, F32, F32, F32, BF16, BF16]
    qkv_s, z_s, ba_s, lat_s, kpe_s, qlat_s, qpe_s = _proj_call(
        proj_sample, xs, tab_s, sample_consts, sw, sd, Bs * tp, "proj_sample")
    s3 = lambda a: a.reshape(Bs, tp, a.shape[-1])
    dattn = _dattn_parts(page_table, qlat_s.reshape(Bs, tp * H, kv_lora), qpe_s.reshape(Bs, tp * H, MLA_D_ROPE),
                         s3(lat_s), s3(kpe_s), cache_latent, jnp.swapaxes(cache_krope, 2, 3), Ts,
                         DATTN_GROUP, DATTN_SLOTS, DATTN_SUB)

    tab_p = _prompt_tables(N_META + jnp.arange(T))
    xp2d = x_prompt.reshape(B * T, D)
    qkv_p, z_p, ba_p, lat_p, kpe_p, qcat_p, kcat_p, vt_p = _proj_call(
        functools.partial(proj_prompt, n_split=PROJ_SPLIT), xp2d, tab_p, prompt_consts, pw, pd, PROJ_TM,
        "proj_prompt", transposed_outs=(7,))
    r3 = lambda a: a.reshape(B, T, a.shape[-1])
    qkv_p3 = r3(qkv_p)
    og_p, s2, out_lat = _gdn_call(qkv_p3, r3(z_p), r3(ba_p), conv1, s1, cw, alog, dtb, ng, GDN_CHUNK, T,
                                  "gdn_prompt", nb_intra=GDN_NB_INTRA, nb_scan=GDN_NB_SCAN, dattn=dattn)
    om_p = _attn_call(r3(qcat_p), r3(kcat_p), vt_p, k_meta, vt_meta, ATTN_TQ, ATTN_HPS, ATTN_KV_UNROLL)
    post_consts = (wo1, wo2, g2, wup, wdn, gf)
    y_p = _post_call(xp2d, og_p.reshape(B * T, -1), om_p.reshape(B * T, -1), post_consts, POST_TM, False, "post_prompt")

    qkv_s3 = s3(qkv_s)
    conv_in_s = jnp.pad(state_conv[0], ((0, 0), (SUBLANES - (CONV_W - 1), 0), (0, 0)))
    og_s, s_new, _ = _gdn_call(qkv_s3, s3(z_s), s3(ba_s), conv_in_s, state_ssm[0], cw, alog, dtb, ng, tp, Ts,
                               "gdn_sample", nb_intra=GDN_NB_SAMPLE, nb_scan=GDN_NB_SAMPLE)
    sample_post_consts = (wvbd,) + post_consts
    y_s = _post_call(xs, og_s.reshape(Bs * tp, -1), out_lat.reshape(Bs * tp, H * kv_lora),
                     sample_post_consts, Bs * tp, True, "post_sample")

    bc = lambda a: jnp.broadcast_to(a[None], (B,) + a.shape)
    lat_po = jnp.concatenate([bc(lat_m), r3(lat_p)], axis=1)[None]
    kpe_po = jnp.concatenate([bc(kpe_m), r3(kpe_p)], axis=1)[None]
    conv_po = qkv_p3[:, T - (CONV_W - 1):][None]
    conv_so = qkv_s3[:, Ts - (CONV_W - 1):Ts][None]
    return (y_p.reshape(B, T, D), y_s.reshape(Bs, tp, D)[:, :Ts],
            lat_po, kpe_po, conv_po, s2[None],
            s3(lat_s)[:, :Ts][None], s3(kpe_s)[:, :Ts][None], conv_so, s_new[None])
```

```python
import functools
import math

import jax
import jax.numpy as jnp
from jax import lax
from jax.experimental import pallas as pl
from jax.experimental.pallas import tpu as pltpu

F32 = jnp.float32
BF16 = jnp.bfloat16

N_META = 16
EPS = 1e-6
GDN_HEADS = 8
GDN_DK = 64
GDN_DV = 64
CONV_W = 4
GDN_CHUNK = 64
MLA_HEADS = 8
MLA_D_NOPE = 64
MLA_D_ROPE = 32
MLA_D_V = 64
ROPE_THETA = 10000.0
MLA_SCALE = (MLA_D_NOPE + MLA_D_ROPE) ** -0.5
Q_SCALE_LOG2 = MLA_SCALE * math.log2(math.e)
GDN_QK_W = GDN_HEADS * GDN_DK
GDN_V_W = GDN_HEADS * GDN_DV
CONV_CH = 2 * GDN_QK_W + GDN_V_W

LANES = 128
SUBLANES = 8
HEAD_PAD = 128
SAMPLE_T_PAD = 8
KPE_LANE0 = 32
VMEM_LIMIT = 56 * 1024 * 1024

PROJ_TM = 512
PROJ_SPLIT = 2
POST_TM = 512
ATTN_TQ = 256
ATTN_HPS = 8
ATTN_KV_UNROLL = 2
GDN_NB_INTRA = 8
GDN_NB_SCAN = 8
GDN_NB_SAMPLE = 8
DATTN_GROUP = 32
DATTN_SLOTS = 4
DATTN_SUB = 4


def _dot(a, b):
    return jnp.dot(a, b, preferred_element_type=F32)


def _dot_nt(a, b):
    return lax.dot_general(a, b, (((1,), (1,)), ((), ())), preferred_element_type=F32)


def _rms(x, g):
    return x * lax.rsqrt(jnp.mean(x * x, axis=-1, keepdims=True) + EPS) * g


def _sigmoid(x):
    return 1.0 / (1.0 + jnp.exp(-x))


def _const_spec(shape):
    nd = len(shape)
    return pl.BlockSpec(shape, lambda *_: (0,) * nd)


def _proj_common(x_ref, gmix_ref, win_ref, qg_ref, kvg_ref, kcos, ksin, qkv_ref, z_ref, ba_ref, lat_ref, kpe_ref,
                 *, q_lora, kv_lora):
    hn = _rms(x_ref[...], gmix_ref[...]).astype(BF16)
    p = _dot(hn, win_ref[...])
    c0 = CONV_CH + GDN_V_W
    c1 = c0 + q_lora
    c2 = c1 + LANES
    c3 = c2 + kv_lora
    qkv_ref[...] = p[:, :CONV_CH]
    z_ref[...] = p[:, CONV_CH:c0]
    blk_a = p[:, c1:c2]
    ba_ref[...] = blk_a
    kpe = blk_a * kcos + p[:, c3:c3 + LANES] * ksin
    kpe_ref[...] = kpe[:, KPE_LANE0:KPE_LANE0 + MLA_D_ROPE]
    lat = _rms(p[:, c2:c3], kvg_ref[...])
    lat_ref[...] = lat
    cqn = _rms(p[:, c0:c1], qg_ref[...]).astype(BF16)
    return cqn, lat.astype(BF16), kpe


def _proj_prompt_kernel(x_ref, tab_ref, gmix_ref, win_ref, qg_ref, kvg_ref, wq_ref, wkb_ref, wvb_ref, vones_ref,
                        qkv_ref, z_ref, ba_ref, lat_ref, kpe_ref, qcat_ref, kcat_ref, vt_ref,
                        *, q_lora, kv_lora, n_split=1):
    rows = x_ref.shape[0] // n_split
    for s in range(n_split):
        rs = pl.ds(s * rows, rows)
        _proj_prompt_rows(x_ref.at[rs], tab_ref.at[rs], gmix_ref, win_ref, qg_ref, kvg_ref, wq_ref, wkb_ref,
                          wvb_ref, vones_ref, qkv_ref.at[rs], z_ref.at[rs], ba_ref.at[rs], lat_ref.at[rs],
                          kpe_ref.at[rs], qcat_ref.at[rs], kcat_ref.at[rs], vt_ref.at[:, rs],
                          q_lora=q_lora, kv_lora=kv_lora)


def _proj_prompt_rows(x_ref, tab_ref, gmix_ref, win_ref, qg_ref, kvg_ref, wq_ref, wkb_ref, wvb_ref, vones_ref,
                      qkv_ref, z_ref, ba_ref, lat_ref, kpe_ref, qcat_ref, kcat_ref, vt_ref, *, q_lora, kv_lora):
    qcos = tab_ref[:, 0 * LANES:1 * LANES]
    qsin_lo = tab_ref[:, 1 * LANES:2 * LANES]
    qsin_hi = tab_ref[:, 2 * LANES:3 * LANES]
    kcos = tab_ref[:, 3 * LANES:4 * LANES]
    ksin = tab_ref[:, 4 * LANES:5 * LANES]
    cqn, latb, kpe = _proj_common(x_ref, gmix_ref, win_ref, qg_ref, kvg_ref, kcos, ksin,
                                  qkv_ref, z_ref, ba_ref, lat_ref, kpe_ref, q_lora=q_lora, kv_lora=kv_lora)
    qa = _dot(cqn, wq_ref[...])
    knope = _dot(latb, wkb_ref[...])
    half = MLA_D_ROPE // 2
    kpe_slot = pltpu.roll(kpe, MLA_D_NOPE - KPE_LANE0, axis=1)
    for h in range(MLA_HEADS):
        sl = slice(h * HEAD_PAD, (h + 1) * HEAD_PAD)
        qs = qa[:, sl]
        q_dn = pltpu.roll(qs, HEAD_PAD - half, axis=1)
        q_up = pltpu.roll(qs, half, axis=1)
        qcat_ref[:, sl] = (qs * qcos + q_dn * qsin_lo + q_up * qsin_hi).astype(BF16)
        kcat_ref[:, sl] = (knope[:, sl] + kpe_slot).astype(BF16)
    vt_ref[...] = (_dot_nt(wvb_ref[...], latb) + vones_ref[...]).astype(BF16)


def _proj_sample_kernel(x_ref, tab_ref, gmix_ref, win_ref, qg_ref, kvg_ref, wqn_ref, wqpe_ref, wqpesw_ref, wabs_ref,
                        qkv_ref, z_ref, ba_ref, lat_ref, kpe_ref, qlat_ref, qpe_ref, *, q_lora, kv_lora):
    pe_w = MLA_HEADS * MLA_D_ROPE
    qcos = tab_ref[:, 0:pe_w]
    qsin = tab_ref[:, pe_w:2 * pe_w]
    kcos = tab_ref[:, 2 * pe_w:2 * pe_w + LANES]
    ksin = tab_ref[:, 2 * pe_w + LANES:2 * pe_w + 2 * LANES]
    cqn, _, _ = _proj_common(x_ref, gmix_ref, win_ref, qg_ref, kvg_ref, kcos, ksin,
                             qkv_ref, z_ref, ba_ref, lat_ref, kpe_ref, q_lora=q_lora, kv_lora=kv_lora)
    qn = _dot(cqn, wqn_ref[...]).astype(BF16)
    qlat_ref[...] = (_dot(qn, wabs_ref[...]) * Q_SCALE_LOG2).astype(BF16)
    qpe_ref[...] = (_dot(cqn, wqpe_ref[...]) * qcos + _dot(cqn, wqpesw_ref[...]) * qsin).astype(BF16)


def _proj_call(kernel_fn, x2d, tab, consts, out_widths, out_dtypes, tm, name, transposed_outs=()):
    n_rows, d_model = x2d.shape
    assert n_rows % tm == 0
    row_spec = lambda w: pl.BlockSpec((tm, w), lambda i: (i, 0))
    col_spec = lambda w: pl.BlockSpec((w, tm), lambda i: (0, i))
    is_t = [k in transposed_outs for k in range(len(out_widths))]
    assert tab.shape[0] % tm == 0 and n_rows % tab.shape[0] == 0
    tab_blocks = tab.shape[0] // tm
    tab_spec = pl.BlockSpec((tm, tab.shape[1]), lambda i: (i % tab_blocks, 0))
    in_specs = [row_spec(d_model), tab_spec] + [_const_spec(c.shape) for c in consts]
    return pl.pallas_call(
        kernel_fn,
        grid=(n_rows // tm,),
        in_specs=in_specs,
        out_specs=[col_spec(w) if t else row_spec(w) for w, t in zip(out_widths, is_t)],
        out_shape=[jax.ShapeDtypeStruct((w, n_rows) if t else (n_rows, w), dt)
                   for w, dt, t in zip(out_widths, out_dtypes, is_t)],
        compiler_params=pltpu.CompilerParams(dimension_semantics=("arbitrary",),
                                             vmem_limit_bytes=VMEM_LIMIT),
        name=name,
    )(x2d, tab, *consts)


def _block_diag2(x, half):
    lane = lax.broadcasted_iota(jnp.int32, x.shape, 1)
    zero = jnp.zeros_like(x)
    return jnp.concatenate([jnp.where(lane < half, x, zero), jnp.where(lane < half, zero, x)], axis=0)


def _gdn_intra_prep(c, bb, cinit_bb, qkv_ref, prev_ref, ba_ref, cinit_ref, cw_ref, alog_ref, dtb_ref,
                    qk_mask, kg_ref, qg_ref, kdt_ref, vb_ref, egl_ref, xbuf, *, chunk, valid_len, n_chunks):
    C = chunk
    H = GDN_HEADS
    tail0 = SUBLANES - (CONV_W - 1)

    xbuf[bb, 0:SUBLANES, :] = jnp.where(c == 0, cinit_ref[cinit_bb], prev_ref[bb])
    xbuf[bb, SUBLANES:SUBLANES + C, :] = qkv_ref[bb]
    cw = cw_ref[...]
    y = xbuf[bb, tail0:tail0 + C, :] * cw[0:1, :]
    for j in range(1, CONV_W):
        y = y + xbuf[bb, tail0 + j:tail0 + j + C, :] * cw[j:j + 1, :]
    y = y * _sigmoid(y)

    ba = ba_ref[bb]
    beta_all = _sigmoid(ba)
    sp_in = ba + dtb_ref[...]
    softplus = jnp.maximum(sp_in, 0.0) + jnp.log1p(jnp.exp(-jnp.abs(sp_in)))
    g_all = -jnp.exp(alog_ref[...]) * softplus
    if valid_len < n_chunks * C:
        row = c * C + lax.broadcasted_iota(jnp.int32, (C, LANES), 0)
        beta_all = jnp.where(row < valid_len, beta_all, 0.0)
        g_all = jnp.where(row < valid_len, g_all, 0.0)

    ri = lax.broadcasted_iota(jnp.int32, (C, C), 0)
    ci = lax.broadcasted_iota(jnp.int32, (C, C), 1)
    lower = ri >= ci
    gc_all = lax.dot_general(lower.astype(F32), g_all, (((1,), (0,)), ((), ())),
                             precision=lax.Precision.HIGHEST, preferred_element_type=F32)
    sel = (lax.broadcasted_iota(jnp.int32, (SUBLANES, LANES), 1)
           == lax.broadcasted_iota(jnp.int32, (SUBLANES, LANES), 0) + GDN_HEADS).astype(F32)
    gc_t = lax.dot_general(sel, gc_all, (((1,), (1,)), ((), ())),
                           precision=lax.Precision.HIGHEST, preferred_element_type=F32)

    pairs = range(H // 2)
    lane = lax.broadcasted_iota(jnp.int32, (C, LANES), 1)
    lo = lane < GDN_DK
    lo_c, lower_p = qk_mask
    ii_r = lax.broadcasted_iota(jnp.int32, (GDN_DK, LANES), 0)
    ii_c = lax.broadcasted_iota(jnp.int32, (GDN_DK, LANES), 1)
    eye2 = ((ii_c == ii_r) | (ii_c == ii_r + GDN_DK)).astype(BF16)

    def col(x, idx):
        return jnp.sum(jnp.where(lane == idx, x, 0.0), axis=-1, keepdims=True)

    def half_sums(x):
        return jnp.where(lo, jnp.sum(jnp.where(lo, x, 0.0), axis=-1, keepdims=True),
                         jnp.sum(jnp.where(lo, 0.0, x), axis=-1, keepdims=True))

    kn_b, kb_b, qn_b, decay = [], [], [], []
    for j in pairs:
        ps = slice(j * LANES, (j + 1) * LANES)
        qp = y[:, j * LANES:(j + 1) * LANES]
        kp = y[:, GDN_QK_W + j * LANES:GDN_QK_W + (j + 1) * LANES]
        vp = y[:, 2 * GDN_QK_W + j * LANES:2 * GDN_QK_W + (j + 1) * LANES]
        qn = qp * lax.rsqrt(half_sums(qp * qp) + EPS) * (GDN_DK ** -0.5)
        kn = kp * lax.rsqrt(half_sums(kp * kp) + EPS)
        beta = jnp.where(lo, col(beta_all, 2 * j), col(beta_all, 2 * j + 1))
        gc0, gc1 = col(gc_all, H + 2 * j), col(gc_all, H + 2 * j + 1)
        gcol = jnp.where(lo, gc0, gc1)
        grow = jnp.concatenate([gc_t[2 * j:2 * j + 1, :], gc_t[2 * j + 1:2 * j + 2, :]], axis=1)
        diff = jnp.where(lo_c, gc0, gc1) - grow
        decay.append(jnp.where(lower_p, jnp.exp(jnp.where(lower_p, diff, 0.0)), 0.0))
        egc = jnp.exp(gcol)
        gl = gcol[C - 1:C, :]
        egl_ref[bb, 0, :, ps] = jnp.exp(gl)
        kb = kn * beta
        vb_ref[bb, 0, :, ps] = vp * beta
        kg_ref[bb, 0, :, ps] = (kb * egc).astype(BF16)
        qg_ref[bb, 0, :, ps] = (qn * egc).astype(BF16)
        k_dec = (kn * jnp.exp(gl - gcol)).astype(BF16)
        kdt_ref[bb, 0, j] = _dot_nt(eye2, _block_diag2(k_dec, GDN_DK)).astype(BF16)
        kn_b.append(_block_diag2(kn.astype(BF16), GDN_DK))
        kb_b.append(kb.astype(BF16))
        qn_b.append(qn.astype(BF16))
    return kn_b, kb_b, qn_b, decay


def _gdn_intra_kernel(*refs, **static):
    _gdn_intra_body(pl.program_id(1), *refs, **static)


def _gdn_intra_body(c, qkv_ref, prev_ref, ba_ref, cinit_ref, cw_ref, alog_ref, dtb_ref,
                    t_ref, qk_ref, kg_ref, qg_ref, kdt_ref, vb_ref, egl_ref, xbuf,
                    *, chunk, valid_len, n_chunks, nb, shared_cinit):
    C = chunk
    rp = lax.broadcasted_iota(jnp.int32, (C, 2 * C), 0)
    cp = lax.broadcasted_iota(jnp.int32, (C, 2 * C), 1)
    lo_c = cp < C
    cmod = jnp.where(lo_c, cp, cp - C)
    lower_p, strict_p = rp >= cmod, rp > cmod
    eye_p = (rp == cmod).astype(F32)
    kn_b, kb_b, qn_b, decay = [], [], [], []
    for bb in range(nb):
        parts = _gdn_intra_prep(c, bb, 0 if shared_cinit else bb, qkv_ref, prev_ref, ba_ref, cinit_ref, cw_ref,
                                alog_ref, dtb_ref, (lo_c, lower_p), kg_ref, qg_ref, kdt_ref, vb_ref, egl_ref, xbuf,
                                chunk=chunk, valid_len=valid_len, n_chunks=n_chunks)
        for dst, src in zip((kn_b, kb_b, qn_b, decay), parts):
            dst.extend(src)
    n_pairs = GDN_HEADS // 2
    probs = range(nb * n_pairs)
    kk = [_dot_nt(kb_b[i], kn_b[i]) for i in probs]
    qk = [_dot_nt(qn_b[i], kn_b[i]) for i in probs]
    for i in probs:
        qk_ref[i // n_pairs, 0, i % n_pairs] = jnp.where(lower_p, qk[i] * decay[i], 0.0).astype(BF16)
    p = [jnp.where(strict_p, -(kk[i] * decay[i]), 0.0) for i in probs]
    t = [eye_p + p[i] for i in probs]
    n_iter = max(1, int(math.ceil(math.log2(C))))
    pb = [p[i].astype(BF16) for i in probs]
    pbd = [_block_diag2(pb[i], C) for i in probs]
    for _ in range(1, n_iter):
        p = [_dot(pb[i], pbd[i]) for i in probs]
        pb = [p[i].astype(BF16) for i in probs]
        pbd = [_block_diag2(pb[i], C) for i in probs]
        t = [t[i] + _dot(t[i].astype(BF16), pbd[i]) for i in probs]
    for i in probs:
        t_ref[i // n_pairs, 0, i % n_pairs] = t[i].astype(BF16)


def _gdn_scan_kernel(*refs, **static):
    _gdn_scan_body(pl.program_id(1), *refs, **static)


def _gdn_scan_body(c, t_ref, qk_ref, kg_ref, qg_ref, kdt_ref, vb_ref, egl_ref, z_ref, sinit_ref, ng_ref,
                   o_ref, sfin_ref, s_scr, *, chunk, n_chunks, nb, shared_sinit):
    C = chunk
    n_pairs = GDN_HEADS // 2
    probs = [(bb, j) for bb in range(nb) for j in range(n_pairs)]
    ps = [slice(j * LANES, (j + 1) * LANES) for j in range(n_pairs)]
    lo = lax.broadcasted_iota(jnp.int32, (C, LANES), 1) < GDN_DV

    @pl.when(c == 0)
    def _():
        for bb, j in probs:
            si = 0 if shared_sinit else bb
            s_scr[bb, j] = jnp.concatenate([sinit_ref[si, 2 * j], sinit_ref[si, 2 * j + 1]], axis=1)

    s = [s_scr[bb, j] for bb, j in probs]
    sbd = [_block_diag2(x.astype(BF16), GDN_DV) for x in s]
    ks = [_dot(kg_ref[bb, 0, :, ps[j]], sbd[i]) for i, (bb, j) in enumerate(probs)]
    qs = [_dot(qg_ref[bb, 0, :, ps[j]], sbd[i]) for i, (bb, j) in enumerate(probs)]
    r = [(vb_ref[bb, 0, :, ps[j]] - ks[i]).astype(BF16) for i, (bb, j) in enumerate(probs)]
    v_new = [_dot(t_ref[bb, 0, j], _block_diag2(r[i], GDN_DV)).astype(BF16) for i, (bb, j) in enumerate(probs)]
    vbd = [_block_diag2(x, GDN_DV) for x in v_new]
    o = [qs[i] + _dot(qk_ref[bb, 0, j], vbd[i]) for i, (bb, j) in enumerate(probs)]
    for i, (bb, j) in enumerate(probs):
        s_scr[bb, j] = s[i] * egl_ref[bb, 0, :, ps[j]] + _dot(kdt_ref[bb, 0, j], vbd[i])
    ng = ng_ref[...]
    for i, (bb, j) in enumerate(probs):
        o2 = o[i] * o[i]
        ms = jnp.where(lo, jnp.sum(jnp.where(lo, o2, 0.0), axis=-1, keepdims=True),
                       jnp.sum(jnp.where(lo, 0.0, o2), axis=-1, keepdims=True)) * (1.0 / GDN_DV)
        zp = z_ref[bb, :, ps[j]]
        o_ref[bb, :, ps[j]] = (o[i] * lax.rsqrt(ms + EPS) * ng * (zp * _sigmoid(zp))).astype(o_ref.dtype)

    @pl.when(c == n_chunks - 1)
    def _():
        for bb, j in probs:
            s_pair = s_scr[bb, j]
            sfin_ref[bb, 2 * j] = s_pair[:, :GDN_DV]
            sfin_ref[bb, 2 * j + 1] = s_pair[:, GDN_DV:]


def _gdn_dattn_kernel(pt_ref, *refs, intra_static, scan_static, dattn_static, n_i_in, n_d_in, n_s_in, n_inter):
    i = pl.program_id(0)
    a = n_i_in + n_d_in
    i_in, d_in, s_in = refs[:n_i_in], refs[n_i_in:a], refs[a:a + n_s_in]
    o_ref, sfin_ref, d_out = refs[a + n_s_in:a + n_s_in + 3]
    s0 = a + n_s_in + 3
    xbuf, inter, s_scr, d_scr = refs[s0], refs[s0 + 1:s0 + 1 + n_inter], refs[s0 + 1 + n_inter], refs[s0 + 2 + n_inter:]
    _dattn_body(i, pl.num_programs(0), pt_ref, *d_in, d_out, *d_scr, **dattn_static)
    _gdn_intra_body(i, *i_in, *inter, xbuf, **intra_static)
    _gdn_scan_body(i, *inter, *s_in, o_ref, sfin_ref, s_scr, **scan_static)


def _gdn_call(qkv, z, ba, conv_init, s_init, cw, alog, dtb, ng, chunk, valid_len, name, nb_intra=1, nb_scan=1,
              dattn=None):
    B, T, _ = qkv.shape
    assert T % chunk == 0 and chunk % SUBLANES == 0 and GDN_DK == GDN_DV and 2 * GDN_DV == LANES
    n_chunks = T // chunk
    H, C = GDN_HEADS, chunk
    HP = H // 2
    shared_c = conv_init.shape[0] == 1
    shared_s = s_init.shape[0] == 1
    ng = jnp.tile(ng, (1, 2))
    params = pltpu.CompilerParams(dimension_semantics=("arbitrary", "arbitrary"), vmem_limit_bytes=VMEM_LIMIT)
    inter_dims = [(HP, C, 2 * C), (HP, C, 2 * C), (C, GDN_QK_W), (C, GDN_QK_W), (HP, GDN_DK, 2 * C),
                  (C, GDN_V_W), (1, GDN_V_W)]
    inter_dtypes = [BF16, BF16, BF16, BF16, BF16, F32, F32]

    def specs(nb, wrap=lambda f: f):
        tok = lambda w: pl.BlockSpec((nb, chunk, w), wrap(lambda b, c: (b, c, 0)))
        per_chunk = lambda *dims: pl.BlockSpec((nb, 1) + dims, wrap(lambda b, c: (b, c) + (0,) * len(dims)))
        per_seq = lambda shared, *dims: pl.BlockSpec(
            ((1 if shared else nb),) + dims, wrap(lambda b, c: ((0 if shared else b),) + (0,) * len(dims)))
        prev_rows = pl.BlockSpec((nb, SUBLANES, CONV_CH),
                                 wrap(lambda b, c: (b, jnp.maximum(c * (C // SUBLANES) - 1, 0), 0)))
        return tok, per_chunk, per_seq, prev_rows

    nb = nb_intra
    assert B % nb == 0
    intra_static = dict(chunk=chunk, valid_len=valid_len, n_chunks=n_chunks, nb=nb, shared_cinit=shared_c)
    intra_operands = (qkv, qkv, ba, conv_init, cw, alog, dtb)
    intra_out_shape = [jax.ShapeDtypeStruct((B, n_chunks) + d, dt) for d, dt in zip(inter_dims, inter_dtypes)]
    intra_scratch = [pltpu.VMEM((nb, SUBLANES + chunk, CONV_CH), F32)]

    def intra_specs(wrap=lambda f: f):
        tok, per_chunk, per_seq, prev_rows = specs(nb, wrap)
        in_specs = [tok(CONV_CH), prev_rows, tok(LANES), per_seq(shared_c, SUBLANES, CONV_CH),
                    _const_spec(cw.shape), _const_spec(alog.shape), _const_spec(dtb.shape)]
        return in_specs, [per_chunk(*d) for d in inter_dims]

    scan_out_shape = [jax.ShapeDtypeStruct((B, T, GDN_V_W), BF16), jax.ShapeDtypeStruct((B, H, GDN_DK, GDN_DV), F32)]
    dattn_out = None
    if dattn is not None and B // nb == 1 and nb_scan == nb and dattn["n_seq"] == n_chunks:
        wrap = lambda f: (lambda i, pt: f(0, i))
        in_specs, _ = intra_specs(wrap)
        tok, _, per_seq, _ = specs(nb, wrap)
        scan_in_specs = [tok(GDN_V_W), per_seq(shared_s, H, GDN_DK, GDN_DV), _const_spec(ng.shape)]
        scan_static = dict(chunk=chunk, n_chunks=n_chunks, nb=nb, shared_sinit=shared_s)
        inter_scratch = [pltpu.VMEM((nb, 1) + d, dt) for d, dt in zip(inter_dims, inter_dtypes)]
        o, s_fin, dattn_out = pl.pallas_call(
            functools.partial(_gdn_dattn_kernel, intra_static=intra_static, scan_static=scan_static,
                              dattn_static=dattn["static"], n_i_in=len(in_specs), n_d_in=len(dattn["in_specs"]),
                              n_s_in=len(scan_in_specs), n_inter=len(inter_scratch)),
            grid_spec=pltpu.PrefetchScalarGridSpec(
                num_scalar_prefetch=1,
                grid=(n_chunks,),
                in_specs=in_specs + dattn["in_specs"] + scan_in_specs,
                out_specs=[tok(GDN_V_W), per_seq(False, H, GDN_DK, GDN_DV), dattn["out_spec"]],
                scratch_shapes=(intra_scratch + inter_scratch + [pltpu.VMEM((nb, HP, GDN_DK, LANES), F32)]
                                + dattn["scratch"]),
            ),
            out_shape=scan_out_shape + [dattn["out_shape"]],
            compiler_params=pltpu.CompilerParams(dimension_semantics=("arbitrary",), vmem_limit_bytes=VMEM_LIMIT),
            name=name + "_dattn",
        )(dattn["page_table"], *intra_operands, *dattn["operands"], z, s_init, ng)
        return o, s_fin, dattn_out
    else:
        in_specs, out_specs = intra_specs()
        inter = pl.pallas_call(
            functools.partial(_gdn_intra_kernel, **intra_static),
            grid=(B // nb, n_chunks),
            in_specs=in_specs,
            out_specs=out_specs,
            out_shape=intra_out_shape,
            scratch_shapes=intra_scratch,
            compiler_params=params,
            name=name + "_intra",
        )(*intra_operands)
        if dattn is not None:
            dattn_out = _dattn_run(dattn)

    nb = nb_scan
    assert B % nb == 0
    tok, per_chunk, per_seq, _ = specs(nb)
    o, s_fin = pl.pallas_call(
        functools.partial(_gdn_scan_kernel, chunk=chunk, n_chunks=n_chunks, nb=nb, shared_sinit=shared_s),
        grid=(B // nb, n_chunks),
        in_specs=[per_chunk(*d) for d in inter_dims] + [
            tok(GDN_V_W), per_seq(shared_s, H, GDN_DK, GDN_DV), _const_spec(ng.shape),
        ],
        out_specs=[tok(GDN_V_W), per_seq(False, H, GDN_DK, GDN_DV)],
        out_shape=scan_out_shape,
        scratch_shapes=[pltpu.VMEM((nb, HP, GDN_DK, LANES), F32)],
        compiler_params=params,
        name=name + "_scan",
    )(*inter, z, s_init, ng)
    return o, s_fin, dattn_out


def _denominator_row(h):
    return MLA_D_V * (1 - h % 2)


def _attn_kernel(q_ref, k_ref, vt_ref, km_ref, vmt_ref, o_ref, m_scr, acc_scr, *, tq, n_meta, hps, kv_unroll):
    qi = pl.program_id(2)
    neg = -jnp.inf
    key_m = lax.broadcasted_iota(jnp.int32, (km_ref.shape[0], tq), 0)
    key_d = lax.broadcasted_iota(jnp.int32, (tq, tq), 0)
    qry_d = lax.broadcasted_iota(jnp.int32, (tq, tq), 1)
    heads = range(hps)
    ls = [slice(h * HEAD_PAD, (h + 1) * HEAD_PAD) for h in heads]

    s = [jnp.where(key_m < n_meta, _dot_nt(km_ref[:, ls[h]], q_ref[0, :, ls[h]]), neg) for h in heads]
    m = [jnp.max(s[h], axis=0, keepdims=True) for h in heads]
    p = [jnp.exp2(s[h] - m[h]) for h in heads]
    for h in heads:
        m_scr[h] = m[h]
        acc_scr[h] = _dot(vmt_ref[ls[h], :], p[h].astype(BF16))

    def blocks(specs):
        chains = [(start, diag, h) for start, diag in specs for h in heads]
        s = [_dot_nt(k_ref[0, pl.ds(start, tq), ls[h]], q_ref[0, :, ls[h]]) for start, _, h in chains]
        s = [jnp.where(key_d <= qry_d, x, neg) if diag else x for x, (_, diag, _) in zip(s, chains)]
        mb = [jnp.max(x, axis=0, keepdims=True) for x in s]
        p = [jnp.exp2(x - m) for x, m in zip(s, mb)]
        pv = [_dot(vt_ref[ls[h], pl.ds(start, tq)], x.astype(BF16)) for x, (start, _, h) in zip(p, chains)]
        for h in heads:
            mine = [i for i, c in enumerate(chains) if c[2] == h]
            m_old = m_scr[h]
            m_new = m_old
            for i in mine:
                m_new = jnp.maximum(m_new, mb[i])
            acc = jnp.exp2(m_old - m_new) * acc_scr[h]
            for i in mine:
                acc = acc + jnp.exp2(mb[i] - m_new) * pv[i]
            acc_scr[h] = acc
            m_scr[h] = m_new

    def start_of(kb):
        return pl.multiple_of(kb * tq, tq)

    def body(kp, carry):
        blocks([(start_of(kv_unroll * kp + u), False) for u in range(kv_unroll)])
        return carry

    lax.fori_loop(0, qi // kv_unroll, body, 0)
    for rem in range(kv_unroll):
        @pl.when(qi % kv_unroll == rem)
        def _(rem=rem):
            blocks([(start_of(qi - rem + u), False) for u in range(rem)] + [(start_of(qi), True)])
    slot_row = lax.broadcasted_iota(jnp.int32, (HEAD_PAD, tq), 0)

    def normalised(h):
        acc = acc_scr[h]
        r = _denominator_row(h)
        return jnp.where(slot_row == r, 0.0, acc) / acc[r:r + 1, :]

    for j in range(hps // 2):
        pair = normalised(2 * j) + normalised(2 * j + 1)
        o_ref[0, :, j * LANES:(j + 1) * LANES] = pair.T.astype(o_ref.dtype)


def _attn_call(qcat, kcat, vt, k_meta, vt_meta, tq, hps, kv_unroll):
    B, T, _ = qcat.shape
    assert MLA_HEADS % hps == 0 and hps % 2 == 0
    n_groups = MLA_HEADS // hps
    gw = hps * HEAD_PAD
    ow = (hps // 2) * LANES
    kern = functools.partial(_attn_kernel, tq=tq, n_meta=N_META, hps=hps, kv_unroll=kv_unroll)
    return pl.pallas_call(
        kern,
        grid=(B, n_groups, T // tq),
        in_specs=[
            pl.BlockSpec((1, tq, gw), lambda b, g, qi: (b, qi, g)),
            pl.BlockSpec((1, T, gw), lambda b, g, qi: (b, 0, g)),
            pl.BlockSpec((gw, T), lambda b, g, qi: (g, b)),
            pl.BlockSpec((k_meta.shape[0], gw), lambda b, g, qi: (0, g)),
            pl.BlockSpec((gw, vt_meta.shape[1]), lambda b, g, qi: (g, 0)),
        ],
        out_specs=pl.BlockSpec((1, tq, ow), lambda b, g, qi: (b, qi, g)),
        out_shape=jax.ShapeDtypeStruct((B, T, n_groups * ow), BF16),
        scratch_shapes=[pltpu.VMEM((hps, 1, tq), F32), pltpu.VMEM((hps, HEAD_PAD, tq), F32)],
        compiler_params=pltpu.CompilerParams(dimension_semantics=("arbitrary", "arbitrary", "arbitrary"),
                                             vmem_limit_bytes=VMEM_LIMIT),
        name="attn",
    )(qcat, kcat, vt, k_meta, vt_meta)


def _dattn_kernel(*refs, **static):
    _dattn_body(pl.program_id(0), pl.num_programs(0), *refs, **static)


def _dattn_body(b, n_seq, pt_ref, qlat_ref, qpe_ref, latn_ref, kpen_ref, lat_hbm, kpe_hbm, o_ref,
                lat_buf, kpe_buf, sem_lat, sem_kpe, m_scr, l_scr, acc_scr,
                *, group, n_groups, n_slots, n_sub, t_new, page):
    neg = -jnp.inf

    def page_copies(seq, g, k, slot):
        pid = pt_ref[seq, g * group + k]
        rows = pl.ds(pl.multiple_of(k * page, page), page)
        return (pltpu.make_async_copy(lat_hbm.at[0, pid], lat_buf.at[slot, rows, :], sem_lat.at[slot]),
                pltpu.make_async_copy(kpe_hbm.at[0, pid], kpe_buf.at[slot, k], sem_kpe.at[slot]))

    def start_group(seq, g, slot):
        def body(k2, carry):
            for prio in range(2):
                for cp in page_copies(seq, g, 2 * k2 + prio, slot):
                    cp.start(priority=prio)
            return carry
        lax.fori_loop(0, group // 2, body, 0)

    def wait_group(seq, g, slot):
        def body(k, carry):
            for cp in page_copies(seq, g, k, slot):
                cp.wait()
            return carry
        lax.fori_loop(0, group, body, 0)

    ahead = n_slots - 1

    @pl.when(b == 0)
    def _():
        for g0 in range(ahead):
            start_group(0, g0, g0 % n_slots)

    m_scr[...] = jnp.full(m_scr.shape, neg, F32)
    l_scr[...] = jnp.zeros(l_scr.shape, F32)
    acc_scr[...] = jnp.zeros(acc_scr.shape, F32)

    ql = qlat_ref[0]
    qp = qpe_ref[0]

    def update(scores, vals):
        ms = [jnp.max(s, axis=-1, keepdims=True) for s in scores]
        ps = [jnp.exp2(s - m) for s, m in zip(scores, ms)]
        ls = [jnp.sum(p, axis=-1, keepdims=True) for p in ps]
        pvs = [_dot(p.astype(BF16), v) for p, v in zip(ps, vals)]
        m_old = m_scr[...]
        m_new = m_old
        for m in ms:
            m_new = jnp.maximum(m_new, m)
        alpha = jnp.exp2(m_old - m_new)
        l_new = alpha * l_scr[...]
        acc_new = alpha * acc_scr[...]
        for m, l, pv in zip(ms, ls, pvs):
            w = jnp.exp2(m - m_new)
            l_new = l_new + w * l
            acc_new = acc_new + w * pv
        l_scr[...] = l_new
        acc_scr[...] = acc_new
        m_scr[...] = m_new

    sub_pages = group // n_sub
    sub_keys = sub_pages * page
    for g in range(n_groups):
        slot = g % n_slots
        nxt = g + ahead
        if nxt < n_groups:
            start_group(b, nxt, nxt % n_slots)
        else:
            @pl.when(b + 1 < n_seq)
            def _(nxt=nxt):
                start_group(b + 1, nxt - n_groups, nxt % n_slots)
        wait_group(b, g, slot)
        cbs = [lat_buf[slot, i * sub_keys:(i + 1) * sub_keys, :].astype(BF16) for i in range(n_sub)]
        kbts = [jnp.concatenate([kpe_buf[slot, k] for k in range(i * sub_pages, (i + 1) * sub_pages)],
                                axis=1).astype(BF16) for i in range(n_sub)]
        update([_dot_nt(ql, cb) + _dot(qp, kbt) for cb, kbt in zip(cbs, kbts)], cbs)

    pad = 2 * SUBLANES - latn_ref.shape[1]
    cn = jnp.concatenate([latn_ref[0], jnp.zeros((pad, latn_ref.shape[2]), F32)], axis=0).astype(BF16)
    kn = jnp.concatenate([kpen_ref[0], jnp.zeros((pad, kpen_ref.shape[2]), F32)], axis=0).astype(BF16)
    s = _dot_nt(ql, cn) + _dot_nt(qp, kn)
    tok = lax.shift_right_logical(lax.broadcasted_iota(jnp.int32, s.shape, 0), int(math.log2(MLA_HEADS)))
    key = lax.broadcasted_iota(jnp.int32, s.shape, 1)
    s = jnp.where((key <= tok) & (key < t_new), s, neg)
    update([s], [cn])
    o_ref[0] = (acc_scr[...] / l_scr[...]).astype(o_ref.dtype)


def _dattn_parts(page_table, qlat, qpe, lat_new, kpe_new, cache_latent, cache_krope_t, t_new, group, n_slots, n_sub):
    assert group % n_sub == 0 and group % 2 == 0
    Bs, R, kv_lora = qlat.shape
    n_pages = page_table.shape[1]
    assert n_pages % (n_slots * group) == 0, "a group must keep its slot number across sequences"
    n_groups = n_pages // group
    assert 2 <= n_slots <= n_groups
    page = cache_latent.shape[2]
    per_b = lambda shape: pl.BlockSpec((1,) + shape, lambda b, pt: (b, 0, 0))
    return dict(
        n_seq=Bs,
        static=dict(group=group, n_groups=n_groups, n_slots=n_slots, n_sub=n_sub, t_new=t_new, page=page),
        page_table=page_table,
        operands=(qlat, qpe, lat_new, kpe_new, cache_latent, cache_krope_t),
        in_specs=[per_b((R, kv_lora)), per_b((R, MLA_D_ROPE)),
                  per_b(lat_new.shape[1:]), per_b(kpe_new.shape[1:]),
                  pl.BlockSpec(memory_space=pl.ANY), pl.BlockSpec(memory_space=pl.ANY)],
        out_spec=per_b((R, kv_lora)),
        out_shape=jax.ShapeDtypeStruct((Bs, R, kv_lora), BF16),
        scratch=[pltpu.VMEM((n_slots, group * page, kv_lora), F32),
                 pltpu.VMEM((n_slots, group, MLA_D_ROPE, page), F32),
                 pltpu.SemaphoreType.DMA((n_slots,)), pltpu.SemaphoreType.DMA((n_slots,)),
                 pltpu.VMEM((R, 1), F32), pltpu.VMEM((R, 1), F32), pltpu.VMEM((R, kv_lora), F32)],
    )


def _dattn_run(parts):
    grid_spec = pltpu.PrefetchScalarGridSpec(
        num_scalar_prefetch=1,
        grid=(parts["n_seq"],),
        in_specs=parts["in_specs"],
        out_specs=parts["out_spec"],
        scratch_shapes=parts["scratch"],
    )
    return pl.pallas_call(
        functools.partial(_dattn_kernel, **parts["static"]),
        grid_spec=grid_spec,
        out_shape=parts["out_shape"],
        compiler_params=pltpu.CompilerParams(dimension_semantics=("arbitrary",),
                                             vmem_limit_bytes=VMEM_LIMIT),
        name="dattn",
    )(parts["page_table"], *parts["operands"])


def _post_kernel(*refs, absorbed_values):
    if absorbed_values:
        h_ref, og_ref, om_ref, wvbd_ref, wo1_ref, wo2_ref, g2_ref, wup_ref, wdn_ref, gf_ref, y_ref = refs
        om = _dot(om_ref[...], wvbd_ref[...]).astype(BF16)
    else:
        h_ref, og_ref, om_ref, wo1_ref, wo2_ref, g2_ref, wup_ref, wdn_ref, gf_ref, y_ref = refs
        om = om_ref[...]
    h2 = h_ref[...] + _dot(og_ref[...], wo1_ref[...]) + _dot(om, wo2_ref[...])
    hn = _rms(h2, g2_ref[...]).astype(BF16)
    u = jnp.maximum(_dot(hn, wup_ref[...]), 0.0)
    h3 = h2 + _dot((u * u).astype(BF16), wdn_ref[...])
    y_ref[...] = _rms(h3, gf_ref[...])


def _post_call(h2d, og, om, consts, tm, absorbed_values, name):
    n_rows, d_model = h2d.shape
    assert n_rows % tm == 0
    row_spec = lambda w: pl.BlockSpec((tm, w), lambda i: (i, 0))
    kern = functools.partial(_post_kernel, absorbed_values=absorbed_values)
    return pl.pallas_call(
        kern,
        grid=(n_rows // tm,),
        in_specs=[row_spec(d_model), row_spec(og.shape[1]), row_spec(om.shape[1])]
                 + [_const_spec(c.shape) for c in consts],
        out_specs=row_spec(d_model),
        out_shape=jax.ShapeDtypeStruct((n_rows, d_model), F32),
        compiler_params=pltpu.CompilerParams(dimension_semantics=("arbitrary",),
                                             vmem_limit_bytes=VMEM_LIMIT),
        name=name,
    )(h2d, og, om, *consts)


def _rope_parts(pos):
    half = MLA_D_ROPE // 2
    inv = ROPE_THETA ** (-jnp.arange(half, dtype=F32) / half)
    ang = pos.astype(F32)[:, None] * inv[None, :]
    cos = jnp.concatenate([jnp.cos(ang), jnp.cos(ang)], -1)
    sin_signed = jnp.concatenate([-jnp.sin(ang), jnp.sin(ang)], -1)
    return cos, sin_signed


def _pad_lanes(x, width):
    return jnp.pad(x, ((0, 0), (0, width - x.shape[1])))


def _at_lanes(x, lane0, width=LANES):
    return jnp.pad(x, ((0, 0), (lane0, width - lane0 - x.shape[1])))


def _key_tables(cos, sin_signed):
    return _at_lanes(cos, KPE_LANE0), _at_lanes(sin_signed, KPE_LANE0)


def _prompt_tables(pos):
    cos, sin_signed = _rope_parts(pos)
    n = pos.shape[0]
    half = MLA_D_ROPE // 2
    qcos = jnp.concatenate([jnp.full((n, MLA_D_NOPE), Q_SCALE_LOG2, F32), cos * Q_SCALE_LOG2], -1)
    qsin_lo = _at_lanes(sin_signed[:, :half] * Q_SCALE_LOG2, MLA_D_NOPE)
    qsin_hi = _at_lanes(sin_signed[:, half:] * Q_SCALE_LOG2, MLA_D_NOPE + half)
    kcos, ksin = _key_tables(cos, sin_signed)
    return jnp.concatenate([_pad_lanes(qcos, LANES), qsin_lo, qsin_hi, kcos, ksin], -1)


def _sample_tables(pos):
    cos, sin_signed = _rope_parts(pos)
    kcos, ksin = _key_tables(cos, sin_signed)
    return jnp.concatenate([jnp.tile(cos * Q_SCALE_LOG2, (1, MLA_HEADS)),
                            jnp.tile(sin_signed * Q_SCALE_LOG2, (1, MLA_HEADS)), kcos, ksin], -1)


def _swap_halves(w):
    half = MLA_D_ROPE // 2
    return jnp.concatenate([w[..., half:], w[..., :half]], -1)


def kernel(x_prompt, x_sample, cache_latent, cache_krope, state_conv, state_ssm, page_table,
           meta_tokens, norm_mix_g, w_in, conv_w, a_log, dt_bias, gdn_norm_g, q_norm_g, w_q_b,
           kv_norm_g, w_kv_b, w_out, norm_mlp_g, w_up, w_down, final_norm_g):
    assert w_in.shape[0] == 1, "single-layer problem"
    B, T, D = x_prompt.shape
    Bs, Ts, _ = x_sample.shape
    assert CONV_W - 1 <= Ts <= SAMPLE_T_PAD
    past_len = page_table.shape[1] * cache_latent.shape[2]
    q_lora = q_norm_g.shape[1]
    kv_lora = kv_norm_g.shape[1]
    H = MLA_HEADS

    wi = w_in[0]
    c0 = CONV_CH + GDN_V_W
    c1 = c0 + 2 * GDN_HEADS
    wkpe_raw = wi[:, c1 + q_lora + kv_lora:]
    blk_a = jnp.concatenate([_pad_lanes(wi[:, c0:c1], KPE_LANE0), _pad_lanes(wkpe_raw, LANES - KPE_LANE0)], 1)
    blk_b = _at_lanes(_swap_halves(wkpe_raw), KPE_LANE0)
    win = jnp.concatenate([wi[:, :c0], wi[:, c1:c1 + q_lora], blk_a,
                           wi[:, c1 + q_lora:c1 + q_lora + kv_lora], blk_b], axis=1).astype(BF16)
    gmix = norm_mix_g[0][None, :]
    qg = q_norm_g[0][None, :]
    kvg = kv_norm_g[0][None, :]
    wq = w_q_b[0]
    wq_nope, wq_pe = wq[..., :MLA_D_NOPE], wq[..., MLA_D_NOPE:]
    wq_pesw = _swap_halves(wq_pe)
    zq = lambda n: jnp.zeros((q_lora, H, n), F32)
    wqcat = jnp.concatenate([wq_nope, wq_pe, zq(HEAD_PAD - MLA_D_NOPE - MLA_D_ROPE)], -1)
    wqcat = wqcat.reshape(q_lora, H * HEAD_PAD).astype(BF16)
    wkv = w_kv_b[0]
    wkb, wvb = wkv[..., :MLA_D_NOPE], wkv[..., MLA_D_NOPE:]
    zk = lambda n: jnp.zeros((kv_lora, H, n), F32)
    wkb_sp = jnp.concatenate([wkb, zk(HEAD_PAD - MLA_D_NOPE)], -1).reshape(kv_lora, H * HEAD_PAD).astype(BF16)
    odd = (jnp.arange(H) % 2 == 1)[None, :, None]
    wvb_sp = jnp.where(odd, jnp.concatenate([zk(HEAD_PAD - MLA_D_V), wvb], -1),
                       jnp.concatenate([wvb, zk(HEAD_PAD - MLA_D_V)], -1))
    wvb_sp_t = wvb_sp.reshape(kv_lora, H * HEAD_PAD).T.astype(BF16)
    eye_h = jnp.eye(H, dtype=bool)
    wabs = jnp.where(eye_h[:, None, :, None], wkb.transpose(1, 2, 0)[:, :, None, :], 0.0)
    wabs = wabs.reshape(H * MLA_D_NOPE, H * kv_lora).astype(BF16)
    wvbd = jnp.where(eye_h[:, None, :, None], wvb.transpose(1, 0, 2)[:, :, None, :], 0.0)
    wvbd = wvbd.reshape(H * kv_lora, H * MLA_D_V).astype(BF16)
    wo1 = w_out[0][:GDN_V_W].astype(BF16)
    wo2 = w_out[0][GDN_V_W:].astype(BF16)
    g2 = norm_mlp_g[0][None, :]
    wup = w_up[0].astype(BF16)
    wdn = w_down[0].astype(BF16)
    gf = final_norm_g[None, :]
    cw = jnp.pad(conv_w[0], ((0, SUBLANES - CONV_W), (0, 0)))
    lane_row = lambda v: jnp.pad(v[None, :], ((0, 0), (GDN_HEADS, LANES - 2 * GDN_HEADS)))
    alog = lane_row(a_log[0])
    dtb = lane_row(dt_bias[0])
    ng = gdn_norm_g[0][None, :]

    common = (gmix, win, qg, kvg)
    slot_row = jnp.arange(H * HEAD_PAD) % HEAD_PAD
    vones = (slot_row == _denominator_row(jnp.arange(H * HEAD_PAD) // HEAD_PAD)).astype(F32)[:, None]
    prompt_consts = common + (wqcat, wkb_sp, wvb_sp_t, vones)
    proj_prompt = functools.partial(_proj_prompt_kernel, q_lora=q_lora, kv_lora=kv_lora)
    proj_sample = functools.partial(_proj_sample_kernel, q_lora=q_lora, kv_lora=kv_lora)
    sample_consts = common + (wq_nope.reshape(q_lora, -1).astype(BF16), wq_pe.reshape(q_lora, -1).astype(BF16),
                              wq_pesw.reshape(q_lora, -1).astype(BF16), wabs)
    pw = [CONV_CH, GDN_V_W, LANES, kv_lora, MLA_D_ROPE, H * HEAD_PAD, H * HEAD_PAD, H * HEAD_PAD]
    pd = [F32, F32, F32, F32, F32, BF16, BF16, BF16]

    tab_m = _prompt_tables(jnp.arange(N_META))
    qkv_m, z_m, ba_m, lat_m, kpe_m, _, kcat_m, vt_m = _proj_call(
        proj_prompt, meta_tokens, tab_m, prompt_consts, pw, pd, N_META, "proj_meta", transposed_outs=(7,))
    zero_conv = jnp.zeros((1, SUBLANES, CONV_CH), F32)
    zero_s = jnp.zeros((1, GDN_HEADS, GDN_DK, GDN_DV), F32)
    _, s1, _ = _gdn_call(qkv_m[None], z_m[None], ba_m[None], zero_conv, zero_s, cw, alog, dtb, ng,
                         N_META, N_META, "gdn_meta")
    conv1 = jnp.pad(qkv_m[N_META - (CONV_W - 1):], ((SUBLANES - (CONV_W - 1), 0), (0, 0)))[None]
    k_meta = jnp.pad(kcat_m, ((0, LANES - N_META), (0, 0)))
    vt_meta = jnp.pad(vt_m, ((0, 0), (0, LANES - N_META)))

    tp = SAMPLE_T_PAD
    xs = jnp.pad(x_sample, ((0, 0), (0, tp - Ts), (0, 0))).reshape(Bs * tp, D)
    tab_s = jnp.tile(_sample_tables(past_len + jnp.arange(tp)), (Bs, 1))
    sw = [CONV_CH, GDN_V_W, LANES, kv_lora, MLA_D_ROPE, H * kv_lora, H * MLA_D_ROPE]
    sd = [F32, F32, F32, F32, F32, BF16, BF16]
    qkv_s, z_s, ba_s, lat_s, kpe_s, qlat_s, qpe_s = _proj_call(
        proj_sample, xs, tab_s, sample_consts, sw, sd, Bs * tp, "proj_sample")
    s3 = lambda a: a.reshape(Bs, tp, a.shape[-1])
    dattn = _dattn_parts(page_table, qlat_s.reshape(Bs, tp * H, kv_lora), qpe_s.reshape(Bs, tp * H, MLA_D_ROPE),
                         s3(lat_s), s3(kpe_s), cache_latent, jnp.swapaxes(cache_krope, 2, 3), Ts,
                         DATTN_GROUP, DATTN_SLOTS, DATTN_SUB)

    tab_p = _prompt_tables(N_META + jnp.arange(T))
    xp2d = x_prompt.reshape(B * T, D)
    qkv_p, z_p, ba_p, lat_p, kpe_p, qcat_p, kcat_p, vt_p = _proj_call(
        functools.partial(proj_prompt, n_split=PROJ_SPLIT), xp2d, tab_p, prompt_consts, pw, pd, PROJ_TM,
        "proj_prompt", transposed_outs=(7,))
    r3 = lambda a: a.reshape(B, T, a.shape[-1])
    qkv_p3 = r3(qkv_p)
    og_p, s2, out_lat = _gdn_call(qkv_p3, r3(z_p), r3(ba_p), conv1, s1, cw, alog, dtb, ng, GDN_CHUNK, T,
                                  "gdn_prompt", nb_intra=GDN_NB_INTRA, nb_scan=GDN_NB_SCAN, dattn=dattn)
    om_p = _attn_call(r3(qcat_p), r3(kcat_p), vt_p, k_meta, vt_meta, ATTN_TQ, ATTN_HPS, ATTN_KV_UNROLL)
    post_consts = (wo1, wo2, g2, wup, wdn, gf)
    y_p = _post_call(xp2d, og_p.reshape(B * T, -1), om_p.reshape(B * T, -1), post_consts, POST_TM, False, "post_prompt")

    qkv_s3 = s3(qkv_s)
    conv_in_s = jnp.pad(state_conv[0], ((0, 0), (SUBLANES - (CONV_W - 1), 0), (0, 0)))
    og_s, s_new, _ = _gdn_call(qkv_s3, s3(z_s), s3(ba_s), conv_in_s, state_ssm[0], cw, alog, dtb, ng, tp, Ts,
                               "gdn_sample", nb_intra=GDN_NB_SAMPLE, nb_scan=GDN_NB_SAMPLE)
    sample_post_consts = (wvbd,) + post_consts
    y_s = _post_call(xs, og_s.reshape(Bs * tp, -1), out_lat.reshape(Bs * tp, H * kv_lora),
                     sample_post_consts, Bs * tp, True, "post_sample")

    bc = lambda a: jnp.broadcast_to(a[None], (B,) + a.shape)
    lat_po = jnp.concatenate([bc(lat_m), r3(lat_p)], axis=1)[None]
    kpe_po = jnp.concatenate([bc(kpe_m), r3(kpe_p)], axis=1)[None]
    conv_po = qkv_p3[:, T - (CONV_W - 1):][None]
    conv_so = qkv_s3[:, Ts - (CONV_W - 1):Ts][None]
    return (y_p.reshape(B, T, D), y_s.reshape(Bs, tp, D)[:, :Ts],
            lat_po, kpe_po, conv_po, s2[None],
            s3(lat_s)[:, :Ts][None], s3(kpe_s)[:, :Ts][None], conv_so, s_new[None])
```

```python
import functools
import math

import jax
import jax.numpy as jnp
from jax import lax
from jax.experimental import pallas as pl
from jax.experimental.pallas import tpu as pltpu

F32 = jnp.float32
BF16 = jnp.bfloat16

N_META = 16
EPS = 1e-6
GDN_HEADS = 8
GDN_DK = 64
GDN_DV = 64
CONV_W = 4
GDN_CHUNK = 64
MLA_HEADS = 8
MLA_D_NOPE = 64
MLA_D_ROPE = 32
MLA_D_V = 64
ROPE_THETA = 10000.0
MLA_SCALE = (MLA_D_NOPE + MLA_D_ROPE) ** -0.5
Q_SCALE_LOG2 = MLA_SCALE * math.log2(math.e)
GDN_QK_W = GDN_HEADS * GDN_DK
GDN_V_W = GDN_HEADS * GDN_DV
CONV_CH = 2 * GDN_QK_W + GDN_V_W

LANES = 128
SUBLANES = 8
HEAD_PAD = 128
SAMPLE_T_PAD = 8
KPE_LANE0 = 32
VMEM_LIMIT = 56 * 1024 * 1024

PROJ_TM = 512
PROJ_SPLIT = 2
POST_TM = 512
ATTN_TQ = 256
ATTN_HPS = 8
ATTN_KV_UNROLL = 2
GDN_NB_INTRA = 8
GDN_NB_SCAN = 8
GDN_NB_SAMPLE = 8
DATTN_GROUP = 32
DATTN_SLOTS = 4
DATTN_SUB = 4


def _dot(a, b):
    return jnp.dot(a, b, preferred_element_type=F32)


def _dot_nt(a, b):
    return lax.dot_general(a, b, (((1,), (1,)), ((), ())), preferred_element_type=F32)


def _rms(x, g):
    return x * lax.rsqrt(jnp.mean(x * x, axis=-1, keepdims=True) + EPS) * g


def _sigmoid(x):
    return 1.0 / (1.0 + jnp.exp(-x))


def _const_spec(shape):
    nd = len(shape)
    return pl.BlockSpec(shape, lambda *_: (0,) * nd)


def _proj_common(x_ref, gmix_ref, win_ref, qg_ref, kvg_ref, kcos, ksin, qkv_ref, z_ref, ba_ref, lat_ref, kpe_ref,
                 *, q_lora, kv_lora):
    hn = _rms(x_ref[...], gmix_ref[...]).astype(BF16)
    p = _dot(hn, win_ref[...])
    c0 = CONV_CH + GDN_V_W
    c1 = c0 + q_lora
    c2 = c1 + LANES
    c3 = c2 + kv_lora
    qkv_ref[...] = p[:, :CONV_CH]
    z_ref[...] = p[:, CONV_CH:c0]
    blk_a = p[:, c1:c2]
    ba_ref[...] = blk_a
    kpe = blk_a * kcos + p[:, c3:c3 + LANES] * ksin
    kpe_ref[...] = kpe[:, KPE_LANE0:KPE_LANE0 + MLA_D_ROPE]
    lat = _rms(p[:, c2:c3], kvg_ref[...])
    lat_ref[...] = lat
    cqn = _rms(p[:, c0:c1], qg_ref[...]).astype(BF16)
    return cqn, lat.astype(BF16), kpe


def _proj_prompt_kernel(x_ref, tab_ref, gmix_ref, win_ref, qg_ref, kvg_ref, wq_ref, wkb_ref, wvb_ref, vones_ref,
                        qkv_ref, z_ref, ba_ref, lat_ref, kpe_ref, qcat_ref, kcat_ref, vt_ref,
                        *, q_lora, kv_lora, n_split=1):
    rows = x_ref.shape[0] // n_split
    for s in range(n_split):
        rs = pl.ds(s * rows, rows)
        _proj_prompt_rows(x_ref.at[rs], tab_ref.at[rs], gmix_ref, win_ref, qg_ref, kvg_ref, wq_ref, wkb_ref,
                          wvb_ref, vones_ref, qkv_ref.at[rs], z_ref.at[rs], ba_ref.at[rs], lat_ref.at[rs],
                          kpe_ref.at[rs], qcat_ref.at[rs], kcat_ref.at[rs], vt_ref.at[:, rs],
                          q_lora=q_lora, kv_lora=kv_lora)


def _proj_prompt_rows(x_ref, tab_ref, gmix_ref, win_ref, qg_ref, kvg_ref, wq_ref, wkb_ref, wvb_ref, vones_ref,
                      qkv_ref, z_ref, ba_ref, lat_ref, kpe_ref, qcat_ref, kcat_ref, vt_ref, *, q_lora, kv_lora):
    qcos = tab_ref[:, 0 * LANES:1 * LANES]
    qsin_lo = tab_ref[:, 1 * LANES:2 * LANES]
    qsin_hi = tab_ref[:, 2 * LANES:3 * LANES]
    kcos = tab_ref[:, 3 * LANES:4 * LANES]
    ksin = tab_ref[:, 4 * LANES:5 * LANES]
    cqn, latb, kpe = _proj_common(x_ref, gmix_ref, win_ref, qg_ref, kvg_ref, kcos, ksin,
                                  qkv_ref, z_ref, ba_ref, lat_ref, kpe_ref, q_lora=q_lora, kv_lora=kv_lora)
    qa = _dot(cqn, wq_ref[...])
    knope = _dot(latb, wkb_ref[...])
    half = MLA_D_ROPE // 2
    kpe_slot = pltpu.roll(kpe, MLA_D_NOPE - KPE_LANE0, axis=1)
    for h in range(MLA_HEADS):
        sl = slice(h * HEAD_PAD, (h + 1) * HEAD_PAD)
        qs = qa[:, sl]
        q_dn = pltpu.roll(qs, HEAD_PAD - half, axis=1)
        q_up = pltpu.roll(qs, half, axis=1)
        qcat_ref[:, sl] = (qs * qcos + q_dn * qsin_lo + q_up * qsin_hi).astype(BF16)
        kcat_ref[:, sl] = (knope[:, sl] + kpe_slot).astype(BF16)
    vt_ref[...] = (_dot_nt(wvb_ref[...], latb) + vones_ref[...]).astype(BF16)


def _proj_sample_kernel(x_ref, tab_ref, gmix_ref, win_ref, qg_ref, kvg_ref, wqn_ref, wqpe_ref, wqpesw_ref, wabs_ref,
                        qkv_ref, z_ref, ba_ref, lat_ref, kpe_ref, qlat_ref, qpe_ref, *, q_lora, kv_lora):
    pe_w = MLA_HEADS * MLA_D_ROPE
    qcos = tab_ref[:, 0:pe_w]
    qsin = tab_ref[:, pe_w:2 * pe_w]
    kcos = tab_ref[:, 2 * pe_w:2 * pe_w + LANES]
    ksin = tab_ref[:, 2 * pe_w + LANES:2 * pe_w + 2 * LANES]
    cqn, _, _ = _proj_common(x_ref, gmix_ref, win_ref, qg_ref, kvg_ref, kcos, ksin,
                             qkv_ref, z_ref, ba_ref, lat_ref, kpe_ref, q_lora=q_lora, kv_lora=kv_lora)
    qn = _dot(cqn, wqn_ref[...]).astype(BF16)
    qlat_ref[...] = (_dot(qn, wabs_ref[...]) * Q_SCALE_LOG2).astype(BF16)
    qpe_ref[...] = (_dot(cqn, wqpe_ref[...]) * qcos + _dot(cqn, wqpesw_ref[...]) * qsin).astype(BF16)


def _proj_call(kernel_fn, x2d, tab, consts, out_widths, out_dtypes, tm, name, transposed_outs=()):
    n_rows, d_model = x2d.shape
    assert n_rows % tm == 0
    row_spec = lambda w: pl.BlockSpec((tm, w), lambda i: (i, 0))
    col_spec = lambda w: pl.BlockSpec((w, tm), lambda i: (0, i))
    is_t = [k in transposed_outs for k in range(len(out_widths))]
    assert tab.shape[0] % tm == 0 and n_rows % tab.shape[0] == 0
    tab_blocks = tab.shape[0] // tm
    tab_spec = pl.BlockSpec((tm, tab.shape[1]), lambda i: (i % tab_blocks, 0))
    in_specs = [row_spec(d_model), tab_spec] + [_const_spec(c.shape) for c in consts]
    return pl.pallas_call(
        kernel_fn,
        grid=(n_rows // tm,),
        in_specs=in_specs,
        out_specs=[col_spec(w) if t else row_spec(w) for w, t in zip(out_widths, is_t)],
        out_shape=[jax.ShapeDtypeStruct((w, n_rows) if t else (n_rows, w), dt)
                   for w, dt, t in zip(out_widths, out_dtypes, is_t)],
        compiler_params=pltpu.CompilerParams(dimension_semantics=("arbitrary",),
                                             vmem_limit_bytes=VMEM_LIMIT),
        name=name,
    )(x2d, tab, *consts)


def _block_diag2(x, half):
    lane = lax.broadcasted_iota(jnp.int32, x.shape, 1)
    zero = jnp.zeros_like(x)
    return jnp.concatenate([jnp.where(lane < half, x, zero), jnp.where(lane < half, zero, x)], axis=0)


def _gdn_intra_prep(c, bb, cinit_bb, qkv_ref, prev_ref, ba_ref, cinit_ref, cw_ref, alog_ref, dtb_ref,
                    qk_mask, kg_ref, qg_ref, kdt_ref, vb_ref, egl_ref, xbuf, *, chunk, valid_len, n_chunks):
    C = chunk
    H = GDN_HEADS
    tail0 = SUBLANES - (CONV_W - 1)

    xbuf[bb, 0:SUBLANES, :] = jnp.where(c == 0, cinit_ref[cinit_bb], prev_ref[bb])
    xbuf[bb, SUBLANES:SUBLANES + C, :] = qkv_ref[bb]
    cw = cw_ref[...]
    y = xbuf[bb, tail0:tail0 + C, :] * cw[0:1, :]
    for j in range(1, CONV_W):
        y = y + xbuf[bb, tail0 + j:tail0 + j + C, :] * cw[j:j + 1, :]
    y = y * _sigmoid(y)

    ba = ba_ref[bb]
    beta_all = _sigmoid(ba)
    sp_in = ba + dtb_ref[...]
    softplus = jnp.maximum(sp_in, 0.0) + jnp.log1p(jnp.exp(-jnp.abs(sp_in)))
    g_all = -jnp.exp(alog_ref[...]) * softplus
    if valid_len < n_chunks * C:
        row = c * C + lax.broadcasted_iota(jnp.int32, (C, LANES), 0)
        beta_all = jnp.where(row < valid_len, beta_all, 0.0)
        g_all = jnp.where(row < valid_len, g_all, 0.0)

    ri = lax.broadcasted_iota(jnp.int32, (C, C), 0)
    ci = lax.broadcasted_iota(jnp.int32, (C, C), 1)
    lower = ri >= ci
    gc_all = lax.dot_general(lower.astype(F32), g_all, (((1,), (0,)), ((), ())),
                             precision=lax.Precision.HIGHEST, preferred_element_type=F32)
    sel = (lax.broadcasted_iota(jnp.int32, (SUBLANES, LANES), 1)
           == lax.broadcasted_iota(jnp.int32, (SUBLANES, LANES), 0) + GDN_HEADS).astype(F32)
    gc_t = lax.dot_general(sel, gc_all, (((1,), (1,)), ((), ())),
                           precision=lax.Precision.HIGHEST, preferred_element_type=F32)

    pairs = range(H // 2)
    lane = lax.broadcasted_iota(jnp.int32, (C, LANES), 1)
    lo = lane < GDN_DK
    lo_c, lower_p = qk_mask
    ii_r = lax.broadcasted_iota(jnp.int32, (GDN_DK, LANES), 0)
    ii_c = lax.broadcasted_iota(jnp.int32, (GDN_DK, LANES), 1)
    eye2 = ((ii_c == ii_r) | (ii_c == ii_r + GDN_DK)).astype(BF16)

    def col(x, idx):
        return jnp.sum(jnp.where(lane == idx, x, 0.0), axis=-1, keepdims=True)

    def half_sums(x):
        return jnp.where(lo, jnp.sum(jnp.where(lo, x, 0.0), axis=-1, keepdims=True),
                         jnp.sum(jnp.where(lo, 0.0, x), axis=-1, keepdims=True))

    kn_b, kb_b, qn_b, decay = [], [], [], []
    for j in pairs:
        ps = slice(j * LANES, (j + 1) * LANES)
        qp = y[:, j * LANES:(j + 1) * LANES]
        kp = y[:, GDN_QK_W + j * LANES:GDN_QK_W + (j + 1) * LANES]
        vp = y[:, 2 * GDN_QK_W + j * LANES:2 * GDN_QK_W + (j + 1) * LANES]
        qn = qp * lax.rsqrt(half_sums(qp * qp) + EPS) * (GDN_DK ** -0.5)
        kn = kp * lax.rsqrt(half_sums(kp * kp) + EPS)
        beta = jnp.where(lo, col(beta_all, 2 * j), col(beta_all, 2 * j + 1))
        gc0, gc1 = col(gc_all, H + 2 * j), col(gc_all, H + 2 * j + 1)
        gcol = jnp.where(lo, gc0, gc1)
        grow = jnp.concatenate([gc_t[2 * j:2 * j + 1, :], gc_t[2 * j + 1:2 * j + 2, :]], axis=1)
        diff = jnp.where(lo_c, gc0, gc1) - grow
        decay.append(jnp.where(lower_p, jnp.exp(jnp.where(lower_p, diff, 0.0)), 0.0))
        egc = jnp.exp(gcol)
        gl = gcol[C - 1:C, :]
        egl_ref[bb, 0, :, ps] = jnp.exp(gl)
        kb = kn * beta
        vb_ref[bb, 0, :, ps] = vp * beta
        kg_ref[bb, 0, :, ps] = (kb * egc).astype(BF16)
        qg_ref[bb, 0, :, ps] = (qn * egc).astype(BF16)
        k_dec = (kn * jnp.exp(gl - gcol)).astype(BF16)
        kdt_ref[bb, 0, j] = _dot_nt(eye2, _block_diag2(k_dec, GDN_DK)).astype(BF16)
        kn_b.append(_block_diag2(kn.astype(BF16), GDN_DK))
        kb_b.append(kb.astype(BF16))
        qn_b.append(qn.astype(BF16))
    return kn_b, kb_b, qn_b, decay


def _gdn_intra_kernel(*refs, **static):
    _gdn_intra_body(pl.program_id(1), *refs, **static)


def _gdn_intra_body(c, qkv_ref, prev_ref, ba_ref, cinit_ref, cw_ref, alog_ref, dtb_ref,
                    t_ref, qk_ref, kg_ref, qg_ref, kdt_ref, vb_ref, egl_ref, xbuf,
                    *, chunk, valid_len, n_chunks, nb, shared_cinit):
    C = chunk
    rp = lax.broadcasted_iota(jnp.int32, (C, 2 * C), 0)
    cp = lax.broadcasted_iota(jnp.int32, (C, 2 * C), 1)
    lo_c = cp < C
    cmod = jnp.where(lo_c, cp, cp - C)
    lower_p, strict_p = rp >= cmod, rp > cmod
    eye_p = (rp == cmod).astype(F32)
    kn_b, kb_b, qn_b, decay = [], [], [], []
    for bb in range(nb):
        parts = _gdn_intra_prep(c, bb, 0 if shared_cinit else bb, qkv_ref, prev_ref, ba_ref, cinit_ref, cw_ref,
                                alog_ref, dtb_ref, (lo_c, lower_p), kg_ref, qg_ref, kdt_ref, vb_ref, egl_ref, xbuf,
                                chunk=chunk, valid_len=valid_len, n_chunks=n_chunks)
        for dst, src in zip((kn_b, kb_b, qn_b, decay), parts):
            dst.extend(src)
    n_pairs = GDN_HEADS // 2
    probs = range(nb * n_pairs)
    kk = [_dot_nt(kb_b[i], kn_b[i]) for i in probs]
    qk = [_dot_nt(qn_b[i], kn_b[i]) for i in probs]
    for i in probs:
        qk_ref[i // n_pairs, 0, i % n_pairs] = jnp.where(lower_p, qk[i] * decay[i], 0.0).astype(BF16)
    p = [jnp.where(strict_p, -(kk[i] * decay[i]), 0.0) for i in probs]
    t = [eye_p + p[i] for i in probs]
    n_iter = max(1, int(math.ceil(math.log2(C))))
    pb = [p[i].astype(BF16) for i in probs]
    pbd = [_block_diag2(pb[i], C) for i in probs]
    for _ in range(1, n_iter):
        p = [_dot(pb[i], pbd[i]) for i in probs]
        pb = [p[i].astype(BF16) for i in probs]
        pbd = [_block_diag2(pb[i], C) for i in probs]
        t = [t[i] + _dot(t[i].astype(BF16), pbd[i]) for i in probs]
    for i in probs:
        t_ref[i // n_pairs, 0, i % n_pairs] = t[i].astype(BF16)


def _gdn_scan_kernel(*refs, **static):
    _gdn_scan_body(pl.program_id(1), *refs, **static)


def _gdn_scan_body(c, t_ref, qk_ref, kg_ref, qg_ref, kdt_ref, vb_ref, egl_ref, z_ref, sinit_ref, ng_ref,
                   o_ref, sfin_ref, s_scr, *, chunk, n_chunks, nb, shared_sinit):
    C = chunk
    n_pairs = GDN_HEADS // 2
    probs = [(bb, j) for bb in range(nb) for j in range(n_pairs)]
    ps = [slice(j * LANES, (j + 1) * LANES) for j in range(n_pairs)]
    lo = lax.broadcasted_iota(jnp.int32, (C, LANES), 1) < GDN_DV

    @pl.when(c == 0)
    def _():
        for bb, j in probs:
            si = 0 if shared_sinit else bb
            s_scr[bb, j] = jnp.concatenate([sinit_ref[si, 2 * j], sinit_ref[si, 2 * j + 1]], axis=1)

    s = [s_scr[bb, j] for bb, j in probs]
    sbd = [_block_diag2(x.astype(BF16), GDN_DV) for x in s]
    ks = [_dot(kg_ref[bb, 0, :, ps[j]], sbd[i]) for i, (bb, j) in enumerate(probs)]
    qs = [_dot(qg_ref[bb, 0, :, ps[j]], sbd[i]) for i, (bb, j) in enumerate(probs)]
    r = [(vb_ref[bb, 0, :, ps[j]] - ks[i]).astype(BF16) for i, (bb, j) in enumerate(probs)]
    v_new = [_dot(t_ref[bb, 0, j], _block_diag2(r[i], GDN_DV)).astype(BF16) for i, (bb, j) in enumerate(probs)]
    vbd = [_block_diag2(x, GDN_DV) for x in v_new]
    o = [qs[i] + _dot(qk_ref[bb, 0, j], vbd[i]) for i, (bb, j) in enumerate(probs)]
    for i, (bb, j) in enumerate(probs):
        s_scr[bb, j] = s[i] * egl_ref[bb, 0, :, ps[j]] + _dot(kdt_ref[bb, 0, j], vbd[i])
    ng = ng_ref[...]
    for i, (bb, j) in enumerate(probs):
        o2 = o[i] * o[i]
        ms = jnp.where(lo, jnp.sum(jnp.where(lo, o2, 0.0), axis=-1, keepdims=True),
                       jnp.sum(jnp.where(lo, 0.0, o2), axis=-1, keepdims=True)) * (1.0 / GDN_DV)
        zp = z_ref[bb, :, ps[j]]
        o_ref[bb, :, ps[j]] = (o[i] * lax.rsqrt(ms + EPS) * ng * (zp * _sigmoid(zp))).astype(o_ref.dtype)

    @pl.when(c == n_chunks - 1)
    def _():
        for bb, j in probs:
            s_pair = s_scr[bb, j]
            sfin_ref[bb, 2 * j] = s_pair[:, :GDN_DV]
            sfin_ref[bb, 2 * j + 1] = s_pair[:, GDN_DV:]


def _gdn_fused_kernel(*refs, intra_static, scan_static, n_i_in, n_s_in, n_inter):
    c = pl.program_id(1)
    a = n_i_in + n_s_in
    i_in, s_in, (o_ref, sfin_ref) = refs[:n_i_in], refs[n_i_in:a], refs[a:a + 2]
    xbuf, inter, s_scr = refs[a + 2], refs[a + 3:a + 3 + n_inter], refs[a + 3 + n_inter]
    _gdn_intra_body(c, *i_in, *inter, xbuf, **intra_static)
    _gdn_scan_body(c, *inter, *s_in, o_ref, sfin_ref, s_scr, **scan_static)


def _gdn_dattn_kernel(pt_ref, *refs, intra_static, scan_static, dattn_static, n_i_in, n_d_in, n_s_in, n_inter):
    i = pl.program_id(0)
    a = n_i_in + n_d_in
    i_in, d_in, s_in = refs[:n_i_in], refs[n_i_in:a], refs[a:a + n_s_in]
    o_ref, sfin_ref, d_out = refs[a + n_s_in:a + n_s_in + 3]
    s0 = a + n_s_in + 3
    xbuf, inter, s_scr, d_scr = refs[s0], refs[s0 + 1:s0 + 1 + n_inter], refs[s0 + 1 + n_inter], refs[s0 + 2 + n_inter:]
    _dattn_body(i, pl.num_programs(0), pt_ref, *d_in, d_out, *d_scr, **dattn_static)
    _gdn_intra_body(i, *i_in, *inter, xbuf, **intra_static)
    _gdn_scan_body(i, *inter, *s_in, o_ref, sfin_ref, s_scr, **scan_static)


def _gdn_call(qkv, z, ba, conv_init, s_init, cw, alog, dtb, ng, chunk, valid_len, name, nb_intra=1, nb_scan=1,
              dattn=None):
    B, T, _ = qkv.shape
    assert T % chunk == 0 and chunk % SUBLANES == 0 and GDN_DK == GDN_DV and 2 * GDN_DV == LANES
    n_chunks = T // chunk
    H, C = GDN_HEADS, chunk
    HP = H // 2
    shared_c = conv_init.shape[0] == 1
    shared_s = s_init.shape[0] == 1
    ng = jnp.tile(ng, (1, 2))
    params = pltpu.CompilerParams(dimension_semantics=("arbitrary", "arbitrary"), vmem_limit_bytes=VMEM_LIMIT)
    inter_dims = [(HP, C, 2 * C), (HP, C, 2 * C), (C, GDN_QK_W), (C, GDN_QK_W), (HP, GDN_DK, 2 * C),
                  (C, GDN_V_W), (1, GDN_V_W)]
    inter_dtypes = [BF16, BF16, BF16, BF16, BF16, F32, F32]

    def specs(nb, wrap=lambda f: f):
        tok = lambda w: pl.BlockSpec((nb, chunk, w), wrap(lambda b, c: (b, c, 0)))
        per_chunk = lambda *dims: pl.BlockSpec((nb, 1) + dims, wrap(lambda b, c: (b, c) + (0,) * len(dims)))
        per_seq = lambda shared, *dims: pl.BlockSpec(
            ((1 if shared else nb),) + dims, wrap(lambda b, c: ((0 if shared else b),) + (0,) * len(dims)))
        prev_rows = pl.BlockSpec((nb, SUBLANES, CONV_CH),
                                 wrap(lambda b, c: (b, jnp.maximum(c * (C // SUBLANES) - 1, 0), 0)))
        return tok, per_chunk, per_seq, prev_rows

    nb = nb_intra
    assert B % nb == 0
    intra_static = dict(chunk=chunk, valid_len=valid_len, n_chunks=n_chunks, nb=nb, shared_cinit=shared_c)
    intra_operands = (qkv, qkv, ba, conv_init, cw, alog, dtb)
    intra_out_shape = [jax.ShapeDtypeStruct((B, n_chunks) + d, dt) for d, dt in zip(inter_dims, inter_dtypes)]
    intra_scratch = [pltpu.VMEM((nb, SUBLANES + chunk, CONV_CH), F32)]

    def intra_specs(wrap=lambda f: f):
        tok, per_chunk, per_seq, prev_rows = specs(nb, wrap)
        in_specs = [tok(CONV_CH), prev_rows, tok(LANES), per_seq(shared_c, SUBLANES, CONV_CH),
                    _const_spec(cw.shape), _const_spec(alog.shape), _const_spec(dtb.shape)]
        return in_specs, [per_chunk(*d) for d in inter_dims]

    scan_out_shape = [jax.ShapeDtypeStruct((B, T, GDN_V_W), BF16), jax.ShapeDtypeStruct((B, H, GDN_DK, GDN_DV), F32)]
    dattn_out = None
    scan_static = dict(chunk=chunk, n_chunks=n_chunks, nb=nb_scan, shared_sinit=shared_s)
    inter_scratch = [pltpu.VMEM((nb, 1) + d, dt) for d, dt in zip(inter_dims, inter_dtypes)]
    state_scratch = pltpu.VMEM((nb_scan, HP, GDN_DK, LANES), F32)

    def scan_specs(wrap=lambda f: f):
        tok, _, per_seq, _ = specs(nb_scan, wrap)
        return ([tok(GDN_V_W), per_seq(shared_s, H, GDN_DK, GDN_DV), _const_spec(ng.shape)],
                [tok(GDN_V_W), per_seq(False, H, GDN_DK, GDN_DV)])

    if nb_scan == nb and not (dattn is not None and B // nb == 1 and dattn["n_seq"] == n_chunks):
        in_specs, _ = intra_specs()
        scan_in_specs, scan_out_specs = scan_specs()
        o, s_fin = pl.pallas_call(
            functools.partial(_gdn_fused_kernel, intra_static=intra_static, scan_static=scan_static,
                              n_i_in=len(in_specs), n_s_in=len(scan_in_specs), n_inter=len(inter_scratch)),
            grid=(B // nb, n_chunks),
            in_specs=in_specs + scan_in_specs,
            out_specs=scan_out_specs,
            out_shape=scan_out_shape,
            scratch_shapes=intra_scratch + inter_scratch + [state_scratch],
            compiler_params=params,
            name=name,
        )(*intra_operands, z, s_init, ng)
        return o, s_fin, (_dattn_run(dattn) if dattn is not None else None)
    if nb_scan == nb:
        wrap = lambda f: (lambda i, pt: f(0, i))
        in_specs, _ = intra_specs(wrap)
        scan_in_specs, scan_out_specs = scan_specs(wrap)
        o, s_fin, dattn_out = pl.pallas_call(
            functools.partial(_gdn_dattn_kernel, intra_static=intra_static, scan_static=scan_static,
                              dattn_static=dattn["static"], n_i_in=len(in_specs), n_d_in=len(dattn["in_specs"]),
                              n_s_in=len(scan_in_specs), n_inter=len(inter_scratch)),
            grid_spec=pltpu.PrefetchScalarGridSpec(
                num_scalar_prefetch=1,
                grid=(n_chunks,),
                in_specs=in_specs + dattn["in_specs"] + scan_in_specs,
                out_specs=scan_out_specs + [dattn["out_spec"]],
                scratch_shapes=intra_scratch + inter_scratch + [state_scratch] + dattn["scratch"],
            ),
            out_shape=scan_out_shape + [dattn["out_shape"]],
            compiler_params=pltpu.CompilerParams(dimension_semantics=("arbitrary",), vmem_limit_bytes=VMEM_LIMIT),
            name=name + "_dattn",
        )(dattn["page_table"], *intra_operands, *dattn["operands"], z, s_init, ng)
        return o, s_fin, dattn_out
    else:
        in_specs, out_specs = intra_specs()
        inter = pl.pallas_call(
            functools.partial(_gdn_intra_kernel, **intra_static),
            grid=(B // nb, n_chunks),
            in_specs=in_specs,
            out_specs=out_specs,
            out_shape=intra_out_shape,
            scratch_shapes=intra_scratch,
            compiler_params=params,
            name=name + "_intra",
        )(*intra_operands)
        if dattn is not None:
            dattn_out = _dattn_run(dattn)

    assert B % nb_scan == 0
    _, per_chunk, _, _ = specs(nb_scan)
    scan_in_specs, scan_out_specs = scan_specs()
    o, s_fin = pl.pallas_call(
        functools.partial(_gdn_scan_kernel, **scan_static),
        grid=(B // nb_scan, n_chunks),
        in_specs=[per_chunk(*d) for d in inter_dims] + scan_in_specs,
        out_specs=scan_out_specs,
        out_shape=scan_out_shape,
        scratch_shapes=[state_scratch],
        compiler_params=params,
        name=name + "_scan",
    )(*inter, z, s_init, ng)
    return o, s_fin, dattn_out


def _denominator_row(h):
    return MLA_D_V * (1 - h % 2)


def _attn_kernel(q_ref, k_ref, vt_ref, km_ref, vmt_ref, o_ref, m_scr, acc_scr, *, tq, n_meta, hps, kv_unroll):
    qi = pl.program_id(2)
    neg = -jnp.inf
    key_m = lax.broadcasted_iota(jnp.int32, (km_ref.shape[0], tq), 0)
    key_d = lax.broadcasted_iota(jnp.int32, (tq, tq), 0)
    qry_d = lax.broadcasted_iota(jnp.int32, (tq, tq), 1)
    heads = range(hps)
    ls = [slice(h * HEAD_PAD, (h + 1) * HEAD_PAD) for h in heads]

    s = [jnp.where(key_m < n_meta, _dot_nt(km_ref[:, ls[h]], q_ref[0, :, ls[h]]), neg) for h in heads]
    m = [jnp.max(s[h], axis=0, keepdims=True) for h in heads]
    p = [jnp.exp2(s[h] - m[h]) for h in heads]
    for h in heads:
        m_scr[h] = m[h]
        acc_scr[h] = _dot(vmt_ref[ls[h], :], p[h].astype(BF16))

    def blocks(specs):
        chains = [(start, diag, h) for start, diag in specs for h in heads]
        s = [_dot_nt(k_ref[0, pl.ds(start, tq), ls[h]], q_ref[0, :, ls[h]]) for start, _, h in chains]
        s = [jnp.where(key_d <= qry_d, x, neg) if diag else x for x, (_, diag, _) in zip(s, chains)]
        mb = [jnp.max(x, axis=0, keepdims=True) for x in s]
        p = [jnp.exp2(x - m) for x, m in zip(s, mb)]
        pv = [_dot(vt_ref[ls[h], pl.ds(start, tq)], x.astype(BF16)) for x, (start, _, h) in zip(p, chains)]
        for h in heads:
            mine = [i for i, c in enumerate(chains) if c[2] == h]
            m_old = m_scr[h]
            m_new = m_old
            for i in mine:
                m_new = jnp.maximum(m_new, mb[i])
            acc = jnp.exp2(m_old - m_new) * acc_scr[h]
            for i in mine:
                acc = acc + jnp.exp2(mb[i] - m_new) * pv[i]
            acc_scr[h] = acc
            m_scr[h] = m_new

    def start_of(kb):
        return pl.multiple_of(kb * tq, tq)

    def body(kp, carry):
        blocks([(start_of(kv_unroll * kp + u), False) for u in range(kv_unroll)])
        return carry

    lax.fori_loop(0, qi // kv_unroll, body, 0)
    for rem in range(kv_unroll):
        @pl.when(qi % kv_unroll == rem)
        def _(rem=rem):
            blocks([(start_of(qi - rem + u), False) for u in range(rem)] + [(start_of(qi), True)])
    slot_row = lax.broadcasted_iota(jnp.int32, (HEAD_PAD, tq), 0)

    def normalised(h):
        acc = acc_scr[h]
        r = _denominator_row(h)
        return jnp.where(slot_row == r, 0.0, acc) / acc[r:r + 1, :]

    for j in range(hps // 2):
        pair = normalised(2 * j) + normalised(2 * j + 1)
        o_ref[0, :, j * LANES:(j + 1) * LANES] = pair.T.astype(o_ref.dtype)


def _attn_call(qcat, kcat, vt, k_meta, vt_meta, tq, hps, kv_unroll):
    B, T, _ = qcat.shape
    assert MLA_HEADS % hps == 0 and hps % 2 == 0
    n_groups = MLA_HEADS // hps
    gw = hps * HEAD_PAD
    ow = (hps // 2) * LANES
    kern = functools.partial(_attn_kernel, tq=tq, n_meta=N_META, hps=hps, kv_unroll=kv_unroll)
    return pl.pallas_call(
        kern,
        grid=(B, n_groups, T // tq),
        in_specs=[
            pl.BlockSpec((1, tq, gw), lambda b, g, qi: (b, qi, g)),
            pl.BlockSpec((1, T, gw), lambda b, g, qi: (b, 0, g)),
            pl.BlockSpec((gw, T), lambda b, g, qi: (g, b)),
            pl.BlockSpec((k_meta.shape[0], gw), lambda b, g, qi: (0, g)),
            pl.BlockSpec((gw, vt_meta.shape[1]), lambda b, g, qi: (g, 0)),
        ],
        out_specs=pl.BlockSpec((1, tq, ow), lambda b, g, qi: (b, qi, g)),
        out_shape=jax.ShapeDtypeStruct((B, T, n_groups * ow), BF16),
        scratch_shapes=[pltpu.VMEM((hps, 1, tq), F32), pltpu.VMEM((hps, HEAD_PAD, tq), F32)],
        compiler_params=pltpu.CompilerParams(dimension_semantics=("arbitrary", "arbitrary", "arbitrary"),
                                             vmem_limit_bytes=VMEM_LIMIT),
        name="attn",
    )(qcat, kcat, vt, k_meta, vt_meta)


def _dattn_kernel(*refs, **static):
    _dattn_body(pl.program_id(0), pl.num_programs(0), *refs, **static)


def _dattn_body(b, n_seq, pt_ref, qlat_ref, qpe_ref, latn_ref, kpen_ref, lat_hbm, kpe_hbm, o_ref,
                lat_buf, kpe_buf, sem_lat, sem_kpe, m_scr, l_scr, acc_scr,
                *, group, n_groups, n_slots, n_sub, t_new, page):
    neg = -jnp.inf

    def page_copies(seq, g, k, slot):
        pid = pt_ref[seq, g * group + k]
        rows = pl.ds(pl.multiple_of(k * page, page), page)
        return (pltpu.make_async_copy(lat_hbm.at[0, pid], lat_buf.at[slot, rows, :], sem_lat.at[slot]),
                pltpu.make_async_copy(kpe_hbm.at[0, pid], kpe_buf.at[slot, k], sem_kpe.at[slot]))

    def start_group(seq, g, slot):
        def body(k2, carry):
            for prio in range(2):
                for cp in page_copies(seq, g, 2 * k2 + prio, slot):
                    cp.start(priority=prio)
            return carry
        lax.fori_loop(0, group // 2, body, 0)

    def wait_group(seq, g, slot):
        def body(k, carry):
            for cp in page_copies(seq, g, k, slot):
                cp.wait()
            return carry
        lax.fori_loop(0, group, body, 0)

    ahead = n_slots - 1

    @pl.when(b == 0)
    def _():
        for g0 in range(ahead):
            start_group(0, g0, g0 % n_slots)

    m_scr[...] = jnp.full(m_scr.shape, neg, F32)
    l_scr[...] = jnp.zeros(l_scr.shape, F32)
    acc_scr[...] = jnp.zeros(acc_scr.shape, F32)

    ql = qlat_ref[0]
    qp = qpe_ref[0]

    def update(scores, vals):
        ms = [jnp.max(s, axis=-1, keepdims=True) for s in scores]
        ps = [jnp.exp2(s - m) for s, m in zip(scores, ms)]
        ls = [jnp.sum(p, axis=-1, keepdims=True) for p in ps]
        pvs = [_dot(p.astype(BF16), v) for p, v in zip(ps, vals)]
        m_old = m_scr[...]
        m_new = m_old
        for m in ms:
            m_new = jnp.maximum(m_new, m)
        alpha = jnp.exp2(m_old - m_new)
        l_new = alpha * l_scr[...]
        acc_new = alpha * acc_scr[...]
        for m, l, pv in zip(ms, ls, pvs):
            w = jnp.exp2(m - m_new)
            l_new = l_new + w * l
            acc_new = acc_new + w * pv
        l_scr[...] = l_new
        acc_scr[...] = acc_new
        m_scr[...] = m_new

    sub_pages = group // n_sub
    sub_keys = sub_pages * page
    for g in range(n_groups):
        slot = g % n_slots
        nxt = g + ahead
        if nxt < n_groups:
            start_group(b, nxt, nxt % n_slots)
        else:
            @pl.when(b + 1 < n_seq)
            def _(nxt=nxt):
                start_group(b + 1, nxt - n_groups, nxt % n_slots)
        wait_group(b, g, slot)
        cbs = [lat_buf[slot, i * sub_keys:(i + 1) * sub_keys, :].astype(BF16) for i in range(n_sub)]
        kbts = [jnp.concatenate([kpe_buf[slot, k] for k in range(i * sub_pages, (i + 1) * sub_pages)],
                                axis=1).astype(BF16) for i in range(n_sub)]
        update([_dot_nt(ql, cb) + _dot(qp, kbt) for cb, kbt in zip(cbs, kbts)], cbs)

    pad = 2 * SUBLANES - latn_ref.shape[1]
    cn = jnp.concatenate([latn_ref[0], jnp.zeros((pad, latn_ref.shape[2]), F32)], axis=0).astype(BF16)
    kn = jnp.concatenate([kpen_ref[0], jnp.zeros((pad, kpen_ref.shape[2]), F32)], axis=0).astype(BF16)
    s = _dot_nt(ql, cn) + _dot_nt(qp, kn)
    tok = lax.shift_right_logical(lax.broadcasted_iota(jnp.int32, s.shape, 0), int(math.log2(MLA_HEADS)))
    key = lax.broadcasted_iota(jnp.int32, s.shape, 1)
    s = jnp.where((key <= tok) & (key < t_new), s, neg)
    update([s], [cn])
    o_ref[0] = (acc_scr[...] / l_scr[...]).astype(o_ref.dtype)


def _dattn_parts(page_table, qlat, qpe, lat_new, kpe_new, cache_latent, cache_krope_t, t_new, group, n_slots, n_sub):
    assert group % n_sub == 0 and group % 2 == 0
    Bs, R, kv_lora = qlat.shape
    n_pages = page_table.shape[1]
    assert n_pages % (n_slots * group) == 0, "a group must keep its slot number across sequences"
    n_groups = n_pages // group
    assert 2 <= n_slots <= n_groups
    page = cache_latent.shape[2]
    per_b = lambda shape: pl.BlockSpec((1,) + shape, lambda b, pt: (b, 0, 0))
    return dict(
        n_seq=Bs,
        static=dict(group=group, n_groups=n_groups, n_slots=n_slots, n_sub=n_sub, t_new=t_new, page=page),
        page_table=page_table,
        operands=(qlat, qpe, lat_new, kpe_new, cache_latent, cache_krope_t),
        in_specs=[per_b((R, kv_lora)), per_b((R, MLA_D_ROPE)),
                  per_b(lat_new.shape[1:]), per_b(kpe_new.shape[1:]),
                  pl.BlockSpec(memory_space=pl.ANY), pl.BlockSpec(memory_space=pl.ANY)],
        out_spec=per_b((R, kv_lora)),
        out_shape=jax.ShapeDtypeStruct((Bs, R, kv_lora), BF16),
        scratch=[pltpu.VMEM((n_slots, group * page, kv_lora), F32),
                 pltpu.VMEM((n_slots, group, MLA_D_ROPE, page), F32),
                 pltpu.SemaphoreType.DMA((n_slots,)), pltpu.SemaphoreType.DMA((n_slots,)),
                 pltpu.VMEM((R, 1), F32), pltpu.VMEM((R, 1), F32), pltpu.VMEM((R, kv_lora), F32)],
    )


def _dattn_run(parts):
    grid_spec = pltpu.PrefetchScalarGridSpec(
        num_scalar_prefetch=1,
        grid=(parts["n_seq"],),
        in_specs=parts["in_specs"],
        out_specs=parts["out_spec"],
        scratch_shapes=parts["scratch"],
    )
    return pl.pallas_call(
        functools.partial(_dattn_kernel, **parts["static"]),
        grid_spec=grid_spec,
        out_shape=parts["out_shape"],
        compiler_params=pltpu.CompilerParams(dimension_semantics=("arbitrary",),
                                             vmem_limit_bytes=VMEM_LIMIT),
        name="dattn",
    )(parts["page_table"], *parts["operands"])


def _post_kernel(*refs, absorbed_values):
    if absorbed_values:
        h_ref, og_ref, om_ref, wvbd_ref, wo1_ref, wo2_ref, g2_ref, wup_ref, wdn_ref, gf_ref, y_ref = refs
        om = _dot(om_ref[...], wvbd_ref[...]).astype(BF16)
    else:
        h_ref, og_ref, om_ref, wo1_ref, wo2_ref, g2_ref, wup_ref, wdn_ref, gf_ref, y_ref = refs
        om = om_ref[...]
    h2 = h_ref[...] + _dot(og_ref[...], wo1_ref[...]) + _dot(om, wo2_ref[...])
    hn = _rms(h2, g2_ref[...]).astype(BF16)
    u = jnp.maximum(_dot(hn, wup_ref[...]), 0.0)
    h3 = h2 + _dot((u * u).astype(BF16), wdn_ref[...])
    y_ref[...] = _rms(h3, gf_ref[...])


def _post_call(h2d, og, om, consts, tm, absorbed_values, name):
    n_rows, d_model = h2d.shape
    assert n_rows % tm == 0
    row_spec = lambda w: pl.BlockSpec((tm, w), lambda i: (i, 0))
    kern = functools.partial(_post_kernel, absorbed_values=absorbed_values)
    return pl.pallas_call(
        kern,
        grid=(n_rows // tm,),
        in_specs=[row_spec(d_model), row_spec(og.shape[1]), row_spec(om.shape[1])]
                 + [_const_spec(c.shape) for c in consts],
        out_specs=row_spec(d_model),
        out_shape=jax.ShapeDtypeStruct((n_rows, d_model), F32),
        compiler_params=pltpu.CompilerParams(dimension_semantics=("arbitrary",),
                                             vmem_limit_bytes=VMEM_LIMIT),
        name=name,
    )(h2d, og, om, *consts)


def _rope_parts(pos):
    half = MLA_D_ROPE // 2
    inv = ROPE_THETA ** (-jnp.arange(half, dtype=F32) / half)
    ang = pos.astype(F32)[:, None] * inv[None, :]
    cos = jnp.concatenate([jnp.cos(ang), jnp.cos(ang)], -1)
    sin_signed = jnp.concatenate([-jnp.sin(ang), jnp.sin(ang)], -1)
    return cos, sin_signed


def _pad_lanes(x, width):
    return jnp.pad(x, ((0, 0), (0, width - x.shape[1])))


def _at_lanes(x, lane0, width=LANES):
    return jnp.pad(x, ((0, 0), (lane0, width - lane0 - x.shape[1])))


def _key_tables(cos, sin_signed):
    return _at_lanes(cos, KPE_LANE0), _at_lanes(sin_signed, KPE_LANE0)


def _prompt_tables(pos):
    cos, sin_signed = _rope_parts(pos)
    n = pos.shape[0]
    half = MLA_D_ROPE // 2
    qcos = jnp.concatenate([jnp.full((n, MLA_D_NOPE), Q_SCALE_LOG2, F32), cos * Q_SCALE_LOG2], -1)
    qsin_lo = _at_lanes(sin_signed[:, :half] * Q_SCALE_LOG2, MLA_D_NOPE)
    qsin_hi = _at_lanes(sin_signed[:, half:] * Q_SCALE_LOG2, MLA_D_NOPE + half)
    kcos, ksin = _key_tables(cos, sin_signed)
    return jnp.concatenate([_pad_lanes(qcos, LANES), qsin_lo, qsin_hi, kcos, ksin], -1)


def _sample_tables(pos):
    cos, sin_signed = _rope_parts(pos)
    kcos, ksin = _key_tables(cos, sin_signed)
    return jnp.concatenate([jnp.tile(cos * Q_SCALE_LOG2, (1, MLA_HEADS)),
                            jnp.tile(sin_signed * Q_SCALE_LOG2, (1, MLA_HEADS)), kcos, ksin], -1)


def _swap_halves(w):
    half = MLA_D_ROPE // 2
    return jnp.concatenate([w[..., half:], w[..., :half]], -1)


def kernel(x_prompt, x_sample, cache_latent, cache_krope, state_conv, state_ssm, page_table,
           meta_tokens, norm_mix_g, w_in, conv_w, a_log, dt_bias, gdn_norm_g, q_norm_g, w_q_b,
           kv_norm_g, w_kv_b, w_out, norm_mlp_g, w_up, w_down, final_norm_g):
    assert w_in.shape[0] == 1, "single-layer problem"
    B, T, D = x_prompt.shape
    Bs, Ts, _ = x_sample.shape
    assert CONV_W - 1 <= Ts <= SAMPLE_T_PAD
    past_len = page_table.shape[1] * cache_latent.shape[2]
    q_lora = q_norm_g.shape[1]
    kv_lora = kv_norm_g.shape[1]
    H = MLA_HEADS

    wi = w_in[0]
    c0 = CONV_CH + GDN_V_W
    c1 = c0 + 2 * GDN_HEADS
    wkpe_raw = wi[:, c1 + q_lora + kv_lora:]
    blk_a = jnp.concatenate([_pad_lanes(wi[:, c0:c1], KPE_LANE0), _pad_lanes(wkpe_raw, LANES - KPE_LANE0)], 1)
    blk_b = _at_lanes(_swap_halves(wkpe_raw), KPE_LANE0)
    win = jnp.concatenate([wi[:, :c0], wi[:, c1:c1 + q_lora], blk_a,
                           wi[:, c1 + q_lora:c1 + q_lora + kv_lora], blk_b], axis=1).astype(BF16)
    gmix = norm_mix_g[0][None, :]
    qg = q_norm_g[0][None, :]
    kvg = kv_norm_g[0][None, :]
    wq = w_q_b[0]
    wq_nope, wq_pe = wq[..., :MLA_D_NOPE], wq[..., MLA_D_NOPE:]
    wq_pesw = _swap_halves(wq_pe)
    zq = lambda n: jnp.zeros((q_lora, H, n), F32)
    wqcat = jnp.concatenate([wq_nope, wq_pe, zq(HEAD_PAD - MLA_D_NOPE - MLA_D_ROPE)], -1)
    wqcat = wqcat.reshape(q_lora, H * HEAD_PAD).astype(BF16)
    wkv = w_kv_b[0]
    wkb, wvb = wkv[..., :MLA_D_NOPE], wkv[..., MLA_D_NOPE:]
    zk = lambda n: jnp.zeros((kv_lora, H, n), F32)
    wkb_sp = jnp.concatenate([wkb, zk(HEAD_PAD - MLA_D_NOPE)], -1).reshape(kv_lora, H * HEAD_PAD).astype(BF16)
    odd = (jnp.arange(H) % 2 == 1)[None, :, None]
    wvb_sp = jnp.where(odd, jnp.concatenate([zk(HEAD_PAD - MLA_D_V), wvb], -1),
                       jnp.concatenate([wvb, zk(HEAD_PAD - MLA_D_V)], -1))
    wvb_sp_t = wvb_sp.reshape(kv_lora, H * HEAD_PAD).T.astype(BF16)
    eye_h = jnp.eye(H, dtype=bool)
    wabs = jnp.where(eye_h[:, None, :, None], wkb.transpose(1, 2, 0)[:, :, None, :], 0.0)
    wabs = wabs.reshape(H * MLA_D_NOPE, H * kv_lora).astype(BF16)
    wvbd = jnp.where(eye_h[:, None, :, None], wvb.transpose(1, 0, 2)[:, :, None, :], 0.0)
    wvbd = wvbd.reshape(H * kv_lora, H * MLA_D_V).astype(BF16)
    wo1 = w_out[0][:GDN_V_W].astype(BF16)
    wo2 = w_out[0][GDN_V_W:].astype(BF16)
    g2 = norm_mlp_g[0][None, :]
    wup = w_up[0].astype(BF16)
    wdn = w_down[0].astype(BF16)
    gf = final_norm_g[None, :]
    cw = jnp.pad(conv_w[0], ((0, SUBLANES - CONV_W), (0, 0)))
    lane_row = lambda v: jnp.pad(v[None, :], ((0, 0), (GDN_HEADS, LANES - 2 * GDN_HEADS)))
    alog = lane_row(a_log[0])
    dtb = lane_row(dt_bias[0])
    ng = gdn_norm_g[0][None, :]

    common = (gmix, win, qg, kvg)
    slot_row = jnp.arange(H * HEAD_PAD) % HEAD_PAD
    vones = (slot_row == _denominator_row(jnp.arange(H * HEAD_PAD) // HEAD_PAD)).astype(F32)[:, None]
    prompt_consts = common + (wqcat, wkb_sp, wvb_sp_t, vones)
    proj_prompt = functools.partial(_proj_prompt_kernel, q_lora=q_lora, kv_lora=kv_lora)
    proj_sample = functools.partial(_proj_sample_kernel, q_lora=q_lora, kv_lora=kv_lora)
    sample_consts = common + (wq_nope.reshape(q_lora, -1).astype(BF16), wq_pe.reshape(q_lora, -1).astype(BF16),
                              wq_pesw.reshape(q_lora, -1).astype(BF16), wabs)
    pw = [CONV_CH, GDN_V_W, LANES, kv_lora, MLA_D_ROPE, H * HEAD_PAD, H * HEAD_PAD, H * HEAD_PAD]
    pd = [F32, F32, F32, F32, F32, BF16, BF16, BF16]

    tab_m = _prompt_tables(jnp.arange(N_META))
    qkv_m, z_m, ba_m, lat_m, kpe_m, _, kcat_m, vt_m = _proj_call(
        proj_prompt, meta_tokens, tab_m, prompt_consts, pw, pd, N_META, "proj_meta", transposed_outs=(7,))
    zero_conv = jnp.zeros((1, SUBLANES, CONV_CH), F32)
    zero_s = jnp.zeros((1, GDN_HEADS, GDN_DK, GDN_DV), F32)
    _, s1, _ = _gdn_call(qkv_m[None], z_m[None], ba_m[None], zero_conv, zero_s, cw, alog, dtb, ng,
                         N_META, N_META, "gdn_meta")
    conv1 = jnp.pad(qkv_m[N_META - (CONV_W - 1):], ((SUBLANES - (CONV_W - 1), 0), (0, 0)))[None]
    k_meta = jnp.pad(kcat_m, ((0, LANES - N_META), (0, 0)))
    vt_meta = jnp.pad(vt_m, ((0, 0), (0, LANES - N_META)))

    tp = SAMPLE_T_PAD
    xs = jnp.pad(x_sample, ((0, 0), (0, tp - Ts), (0, 0))).reshape(Bs * tp, D)
    tab_s = jnp.tile(_sample_tables(past_len + jnp.arange(tp)), (Bs, 1))
    sw = [CONV_CH, GDN_V_W, LANES, kv_lora, MLA_D_ROPE, H * kv_lora, H * MLA_D_ROPE]
    sd = [F32, F32, F32, F32, F32, BF16, BF16]
    qkv_s, z_s, ba_s, lat_s, kpe_s, qlat_s, qpe_s = _proj_call(
        proj_sample, xs, tab_s, sample_consts, sw, sd, Bs * tp, "proj_sample")
    s3 = lambda a: a.reshape(Bs, tp, a.shape[-1])
    dattn = _dattn_parts(page_table, qlat_s.reshape(Bs, tp * H, kv_lora), qpe_s.reshape(Bs, tp * H, MLA_D_ROPE),
                         s3(lat_s), s3(kpe_s), cache_latent, jnp.swapaxes(cache_krope, 2, 3), Ts,
                         DATTN_GROUP, DATTN_SLOTS, DATTN_SUB)

    tab_p = _prompt_tables(N_META + jnp.arange(T))
    xp2d = x_prompt.reshape(B * T, D)
    qkv_p, z_p, ba_p, lat_p, kpe_p, qcat_p, kcat_p, vt_p = _proj_call(
        functools.partial(proj_prompt, n_split=PROJ_SPLIT), xp2d, tab_p, prompt_consts, pw, pd, PROJ_TM,
        "proj_prompt", transposed_outs=(7,))
    r3 = lambda a: a.reshape(B, T, a.shape[-1])
    qkv_p3 = r3(qkv_p)
    og_p, s2, out_lat = _gdn_call(qkv_p3, r3(z_p), r3(ba_p), conv1, s1, cw, alog, dtb, ng, GDN_CHUNK, T,
                                  "gdn_prompt", nb_intra=GDN_NB_INTRA, nb_scan=GDN_NB_SCAN, dattn=dattn)
    om_p = _attn_call(r3(qcat_p), r3(kcat_p), vt_p, k_meta, vt_meta, ATTN_TQ, ATTN_HPS, ATTN_KV_UNROLL)
    post_consts = (wo1, wo2, g2, wup, wdn, gf)
    y_p = _post_call(xp2d, og_p.reshape(B * T, -1), om_p.reshape(B * T, -1), post_consts, POST_TM, False, "post_prompt")

    qkv_s3 = s3(qkv_s)
    conv_in_s = jnp.pad(state_conv[0], ((0, 0), (SUBLANES - (CONV_W - 1), 0), (0, 0)))
    og_s, s_new, _ = _gdn_call(qkv_s3, s3(z_s), s3(ba_s), conv_in_s, state_ssm[0], cw, alog, dtb, ng, tp, Ts,
                               "gdn_sample", nb_intra=GDN_NB_SAMPLE, nb_scan=GDN_NB_SAMPLE)
    sample_post_consts = (wvbd,) + post_consts
    y_s = _post_call(xs, og_s.reshape(Bs * tp, -1), out_lat.reshape(Bs * tp, H * kv_lora),
                     sample_post_consts, Bs * tp, True, "post_sample")

    bc = lambda a: jnp.broadcast_to(a[None], (B,) + a.shape)
    lat_po = jnp.concatenate([bc(lat_m), r3(lat_p)], axis=1)[None]
    kpe_po = jnp.concatenate([bc(kpe_m), r3(kpe_p)], axis=1)[None]
    conv_po = qkv_p3[:, T - (CONV_W - 1):][None]
    conv_so = qkv_s3[:, Ts - (CONV_W - 1):Ts][None]
    return (y_p.reshape(B, T, D), y_s.reshape(Bs, tp, D)[:, :Ts],
            lat_po, kpe_po, conv_po, s2[None],
            s3(lat_s)[:, :Ts][None], s3(kpe_s)[:, :Ts][None], conv_so, s_new[None])
```
